```python
import math
import jax, jax.numpy as jnp
from jax import lax
import numpy as np

D_MODEL = 2048
BATCH = 8
SEQ = 8192
DEPTH = 4

N_A = DEPTH // 2
N_B = DEPTH - N_A
NORM_EPS = 1e-6

SSD_EXPAND = 2
D_INNER = SSD_EXPAND * D_MODEL
SSD_HEAD_DIM = 64
SSD_HEADS = D_INNER // SSD_HEAD_DIM
SSD_GROUPS = 8
SSD_STATE = 128
SSD_CONV = 4
SSD_CHUNK = 128
SSD_BC_DIM = SSD_GROUPS * SSD_STATE
SSD_CONV_DIM = D_INNER + 2 * SSD_BC_DIM
SSD_IN_DIM = D_INNER + SSD_CONV_DIM + SSD_HEADS
DT_MIN = 1e-3
DT_MAX = 1e-1

DIL_PATTERNS = ((128, 1), (512, 4), (2048, 16))
N_DIL = len(DIL_PATTERNS)
DIL_HEADS = 16
DIL_HEAD_DIM = D_MODEL // DIL_HEADS
DIL_WIDTH = DIL_HEADS * DIL_HEAD_DIM
DIL_BLOCK = 128
ROPE_THETA = 10000.0

MEM_LEN = 256
MEM_HEADS = 4
MEM_HEAD_DIM = 128
MEM_WIDTH = MEM_HEADS * MEM_HEAD_DIM

D_FF = -(-(8 * D_MODEL) // (3 * 256)) * 256

kernel_name = 'yoco_ssd_dilated_hybrid'


def rmsnorm(x, g):
    xf = x.astype(jnp.float32)
    y = xf * lax.rsqrt(jnp.mean(xf * xf, axis=-1, keepdims=True) + NORM_EPS)
    return (y * g.astype(jnp.float32)).astype(x.dtype)


def rope(t, positions):
    half = t.shape[-1] // 2
    inv_freq = ROPE_THETA ** (-jnp.arange(half, dtype=jnp.float32) / half)
    ang = positions.astype(jnp.float32)[:, :, None] * inv_freq
    cos = jnp.cos(ang)[:, :, None, :]
    sin = jnp.sin(ang)[:, :, None, :]
    t1 = t[..., :half].astype(jnp.float32)
    t2 = t[..., half:].astype(jnp.float32)
    return jnp.concatenate([t1 * cos - t2 * sin, t2 * cos + t1 * sin], axis=-1).astype(t.dtype)


def causal_depthwise_conv(u, w, bias):
    out = lax.conv_general_dilated(
        u, w[:, None, :].astype(u.dtype), window_strides=(1,),
        padding=[(w.shape[0] - 1, 0)], dimension_numbers=('NWC', 'WIO', 'NWC'),
        feature_group_count=u.shape[-1])
    return out + bias.astype(u.dtype)


def ssd_chunked(xh, dt, a_neg, bm, cm):
    b, s = xh.shape[0], xh.shape[1]
    c = s // SSD_CHUNK
    L = SSD_CHUNK
    G = SSD_GROUPS
    J = SSD_HEADS // G
    dtype = xh.dtype
    x = xh.reshape(b, c, L, G, J, SSD_HEAD_DIM)
    dtc = dt.reshape(b, c, L, G, J)
    bc = bm.reshape(b, c, L, G, SSD_STATE)
    cc = cm.reshape(b, c, L, G, SSD_STATE)
    a_cum = jnp.cumsum(dtc * a_neg.reshape(G, J), axis=2)
    causal = jnp.tril(jnp.ones((L, L), dtype=bool))[None, None, :, :, None, None]
    seg = a_cum[:, :, :, None] - a_cum[:, :, None, :]
    decay = jnp.exp(jnp.where(causal, seg, -jnp.inf))
    cb = jnp.einsum('bctgn,bcsgn->bctsg', cc, bc).astype(jnp.float32)
    m = cb[..., None] * decay * dtc[:, :, None]
    y_diag = jnp.einsum('bctsgj,bcsgjp->bctgjp', m.astype(dtype), x)
    w_end = jnp.exp(a_cum[:, :, -1:] - a_cum) * dtc
    states = jnp.einsum('bclgn,bclgj,bclgjp->bcgjpn', bc, w_end.astype(dtype), x)
    chunk_decay = jnp.exp(a_cum[:, :, -1])

    def step(h, inp):
        st, dec = inp
        return h * dec[..., None, None] + st, h

    h0 = jnp.zeros((b, G, J, SSD_HEAD_DIM, SSD_STATE), jnp.float32)
    _, h_in = lax.scan(step, h0, (jnp.moveaxis(states.astype(jnp.float32), 1, 0),
                                  jnp.moveaxis(chunk_decay, 1, 0)))
    h_in = jnp.moveaxis(h_in, 0, 1)
    y_off = jnp.einsum('bclgn,bcgjpn,bclgj->bclgjp', cc, h_in.astype(dtype),
                       jnp.exp(a_cum).astype(dtype))
    return (y_diag + y_off).reshape(b, s, SSD_HEADS, SSD_HEAD_DIM)


def ssd_mixer(u, w_in, conv_w, conv_b, dt_bias, a_log, d_skip, norm_w, w_out):
    b, s, _ = u.shape
    proj = u @ w_in
    z = proj[..., :D_INNER]
    xbc = proj[..., D_INNER:D_INNER + SSD_CONV_DIM]
    dt_raw = proj[..., D_INNER + SSD_CONV_DIM:]
    xbc = jax.nn.silu(causal_depthwise_conv(xbc, conv_w, conv_b))
    xs = xbc[..., :D_INNER].reshape(b, s, SSD_HEADS, SSD_HEAD_DIM)
    bm = xbc[..., D_INNER:D_INNER + SSD_BC_DIM].reshape(b, s, SSD_GROUPS, SSD_STATE)
    cm = xbc[..., D_INNER + SSD_BC_DIM:].reshape(b, s, SSD_GROUPS, SSD_STATE)
    dt = jax.nn.softplus(dt_raw.astype(jnp.float32) + dt_bias.astype(jnp.float32))
    a_neg = -jnp.exp(a_log.astype(jnp.float32))
    y = ssd_chunked(xs, dt, a_neg, bm, cm) + xs * d_skip[:, None].astype(xs.dtype)
    y = y.reshape(b, s, D_INNER) * jax.nn.silu(z)
    yg = y.reshape(b, s, SSD_GROUPS, D_INNER // SSD_GROUPS).astype(jnp.float32)
    yg = yg * lax.rsqrt(jnp.mean(yg * yg, axis=-1, keepdims=True) + NORM_EPS)
    y = (yg.reshape(b, s, D_INNER) * norm_w.astype(jnp.float32)).astype(u.dtype)
    return y @ w_out


def banded_window_attn(q, k, v, steps):
    bb, n, h, dh = q.shape
    nb = -(-n // DIL_BLOCK)
    pad = nb * DIL_BLOCK - n
    padw = ((0, 0), (0, pad), (0, 0), (0, 0))
    q = jnp.pad(q, padw)
    k = jnp.pad(k, padw)
    v = jnp.pad(v, padw)
    qb = q.reshape(bb, nb, DIL_BLOCK, h, dh)

    def with_prev(t):
        tb = t.reshape(bb, nb, DIL_BLOCK, h, dh)
        prev = jnp.concatenate([jnp.zeros_like(tb[:, :1]), tb[:, :-1]], axis=1)
        return jnp.concatenate([prev, tb], axis=2)

    kw = with_prev(k)
    vw = with_prev(v)
    sc = jnp.einsum('bnqhd,bnkhd->bnhqk', qb, kw).astype(jnp.float32) * (dh ** -0.5)
    qi = jnp.arange(DIL_BLOCK)[:, None] + DIL_BLOCK
    ki = jnp.arange(2 * DIL_BLOCK)[None, :]
    dist = qi - ki
    key_idx = jnp.arange(nb)[:, None, None] * DIL_BLOCK + ki[None] - DIL_BLOCK
    valid = (dist >= 0) & (dist <= steps) & (key_idx >= 0)
    sc = jnp.where(valid[None, :, None], sc, -jnp.inf)
    mx = jnp.max(sc, axis=-1, keepdims=True)
    p = jnp.exp(sc - mx)
    den = jnp.sum(p, axis=-1)
    o = jnp.einsum('bnhqk,bnkhd->bnqhd', p.astype(v.dtype), vw)
    o = o / jnp.swapaxes(den, 2, 3)[..., None].astype(o.dtype)
    lse = jnp.swapaxes(mx[..., 0] + jnp.log(den), 2, 3)
    o = o.reshape(bb, nb * DIL_BLOCK, h, dh)[:, :n]
    lse = lse.reshape(bb, nb * DIL_BLOCK, h)[:, :n]
    return o, lse


def dilated_group_attn(q, k, v, window, dilation):
    b, s, h, dh = q.shape
    n = s // dilation

    def to_sub(t):
        return t.reshape(b, n, dilation, h, dh).transpose(0, 2, 1, 3, 4).reshape(b * dilation, n, h, dh)

    o, lse = banded_window_attn(to_sub(q), to_sub(k), to_sub(v), window // dilation)
    o = o.reshape(b, dilation, n, h, dh).transpose(0, 2, 1, 3, 4).reshape(b, s, h, dh)
    lse = lse.reshape(b, dilation, n, h).transpose(0, 2, 1, 3).reshape(b, s, h)
    return o, lse


def dilated_mixer(u, k_all, v_all, positions, w_q, w_o):
    b, s, _ = u.shape
    q_all = rope((u @ w_q).reshape(b, s, N_DIL * DIL_HEADS, DIL_HEAD_DIM), positions)
    outs = []
    lses = []
    for g, (window, dilation) in enumerate(DIL_PATTERNS):
        sl = slice(g * DIL_HEADS, (g + 1) * DIL_HEADS)
        o, lse = dilated_group_attn(q_all[:, :, sl], k_all[:, :, sl], v_all[:, :, sl], window, dilation)
        outs.append(o)
        lses.append(lse)
    wts = jax.nn.softmax(jnp.stack(lses, axis=0), axis=0)
    o = jnp.einsum('gbsh,gbshd->bshd', wts.astype(q_all.dtype), jnp.stack(outs, axis=0))
    return o.reshape(b, s, DIL_WIDTH) @ w_o


def memory_cross_attn(u, mem_n, w_q, w_kv, w_o):
    b, s, _ = u.shape
    q = (u @ w_q).reshape(b, s, MEM_HEADS, MEM_HEAD_DIM)
    kv = mem_n @ w_kv
    k = kv[..., :MEM_WIDTH].reshape(b, MEM_LEN, MEM_HEADS, MEM_HEAD_DIM)
    v = kv[..., MEM_WIDTH:].reshape(b, MEM_LEN, MEM_HEADS, MEM_HEAD_DIM)
    sc = jnp.einsum('bshd,bmhd->bhsm', q, k).astype(jnp.float32) * (MEM_HEAD_DIM ** -0.5)
    p = jax.nn.softmax(sc, axis=-1).astype(v.dtype)
    o = jnp.einsum('bhsm,bmhd->bshd', p, v).reshape(b, s, MEM_WIDTH)
    return o @ w_o


def swiglu(u, w_in, w_out):
    gu = u @ w_in
    return (jax.nn.silu(gu[..., :D_FF]) * gu[..., D_FF:]) @ w_out


def _fwd_setup_inputs(seed: int = 0) -> dict:
    key = jax.random.key(seed)
    ks = jax.random.split(key, 26)
    f32 = jnp.float32

    def dense(k, shape, fan_in):
        return jax.random.normal(k, shape, f32) * (fan_in ** -0.5)

    def gain(k, shape):
        return 1.0 + 0.02 * jax.random.normal(k, shape, f32)

    x = jax.random.normal(ks[0], (BATCH, SEQ, D_MODEL), f32)
    mem = jax.random.normal(ks[1], (BATCH, MEM_LEN, D_MODEL), f32)
    start = jax.random.randint(ks[2], (BATCH, 1), 0, 1024, dtype=jnp.int32)
    positions = start + jnp.arange(SEQ, dtype=jnp.int32)[None, :]
    norm_mix = gain(ks[3], (DEPTH, D_MODEL))
    norm_mem = gain(ks[4], (DEPTH, D_MODEL))
    norm_ffn = gain(ks[5], (DEPTH, D_MODEL))
    norm_final = gain(ks[6], (D_MODEL,))
    ssd_w_in = dense(ks[7], (N_A, D_MODEL, SSD_IN_DIM), D_MODEL)
    ssd_conv_w = dense(ks[8], (N_A, SSD_CONV, SSD_CONV_DIM), SSD_CONV)
    ssd_conv_b = 0.02 * jax.random.normal(ks[9], (N_A, SSD_CONV_DIM), f32)
    u = jax.random.uniform(ks[10], (N_A, SSD_HEADS), f32)
    dt0 = jnp.exp(u * (math.log(DT_MAX) - math.log(DT_MIN)) + math.log(DT_MIN))
    ssd_dt_bias = dt0 + jnp.log(-jnp.expm1(-dt0))
    ssd_a_log = jnp.log(jax.random.uniform(ks[11], (N_A, SSD_HEADS), f32, 1.0, 16.0))
    ssd_d = gain(ks[12], (N_A, SSD_HEADS))
    ssd_norm = gain(ks[13], (N_A, D_INNER))
    ssd_w_out = dense(ks[14], (N_A, D_INNER, D_MODEL), D_INNER)
    kv_norm = gain(ks[15], (D_MODEL,))
    w_kv_shared = dense(ks[16], (D_MODEL, 2 * N_DIL * DIL_WIDTH), D_MODEL)
    dil_w_q = dense(ks[17], (N_B, D_MODEL, N_DIL * DIL_WIDTH), D_MODEL)
    dil_w_o = dense(ks[18], (N_B, DIL_WIDTH, D_MODEL), DIL_WIDTH)
    mem_src_norm = gain(ks[19], (D_MODEL,))
    mem_w_q = dense(ks[20], (DEPTH, D_MODEL, MEM_WIDTH), D_MODEL)
    mem_w_kv = dense(ks[21], (DEPTH, D_MODEL, 2 * MEM_WIDTH), D_MODEL)
    mem_w_o = dense(ks[22], (DEPTH, MEM_WIDTH, D_MODEL), MEM_WIDTH)
    ffn_w_in = dense(ks[23], (DEPTH, D_MODEL, 2 * D_FF), D_MODEL)
    ffn_w_out = dense(ks[24], (DEPTH, D_FF, D_MODEL), D_FF)
    return {'x': x, 'mem': mem, 'positions': positions,
            'norm_mix': norm_mix, 'norm_mem': norm_mem, 'norm_ffn': norm_ffn, 'norm_final': norm_final,
            'ssd_w_in': ssd_w_in, 'ssd_conv_w': ssd_conv_w, 'ssd_conv_b': ssd_conv_b,
            'ssd_dt_bias': ssd_dt_bias, 'ssd_a_log': ssd_a_log, 'ssd_d': ssd_d,
            'ssd_norm': ssd_norm, 'ssd_w_out': ssd_w_out,
            'kv_norm': kv_norm, 'w_kv_shared': w_kv_shared, 'dil_w_q': dil_w_q, 'dil_w_o': dil_w_o,
            'mem_src_norm': mem_src_norm, 'mem_w_q': mem_w_q, 'mem_w_kv': mem_w_kv, 'mem_w_o': mem_w_o,
            'ffn_w_in': ffn_w_in, 'ffn_w_out': ffn_w_out}


def _fwd_reference(x, mem, positions, norm_mix, norm_mem, norm_ffn, norm_final,
              ssd_w_in, ssd_conv_w, ssd_conv_b, ssd_dt_bias, ssd_a_log, ssd_d, ssd_norm, ssd_w_out,
              kv_norm, w_kv_shared, dil_w_q, dil_w_o,
              mem_src_norm, mem_w_q, mem_w_kv, mem_w_o, ffn_w_in, ffn_w_out):
    b, s, _ = x.shape
    mem_n = rmsnorm(mem, mem_src_norm)
    h = x
    k_sh = None
    v_sh = None
    for i in range(DEPTH):
        if i < N_A:
            h = h + ssd_mixer(rmsnorm(h, norm_mix[i]), ssd_w_in[i], ssd_conv_w[i], ssd_conv_b[i],
                              ssd_dt_bias[i], ssd_a_log[i], ssd_d[i], ssd_norm[i], ssd_w_out[i])
        else:
            if i == N_A:
                kv = rmsnorm(h, kv_norm) @ w_kv_shared
                k_sh = rope(kv[..., :N_DIL * DIL_WIDTH].reshape(b, s, N_DIL * DIL_HEADS, DIL_HEAD_DIM), positions)
                v_sh = kv[..., N_DIL * DIL_WIDTH:].reshape(b, s, N_DIL * DIL_HEADS, DIL_HEAD_DIM)
            j = i - N_A
            h = h + dilated_mixer(rmsnorm(h, norm_mix[i]), k_sh, v_sh, positions, dil_w_q[j], dil_w_o[j])
        h = h + memory_cross_attn(rmsnorm(h, norm_mem[i]), mem_n, mem_w_q[i], mem_w_kv[i], mem_w_o[i])
        h = h + swiglu(rmsnorm(h, norm_ffn[i]), ffn_w_in[i], ffn_w_out[i])
    return rmsnorm(h, norm_final)


import jax as _jax
import jax.numpy as _jnp

TWIN_FORMAT = 'train_step'
FWD_PARAMS = ['x', 'mem', 'positions', 'norm_mix', 'norm_mem', 'norm_ffn', 'norm_final', 'ssd_w_in', 'ssd_conv_w', 'ssd_conv_b', 'ssd_dt_bias', 'ssd_a_log', 'ssd_d', 'ssd_norm', 'ssd_w_out', 'kv_norm', 'w_kv_shared', 'dil_w_q', 'dil_w_o', 'mem_src_norm', 'mem_w_q', 'mem_w_kv', 'mem_w_o', 'ffn_w_in', 'ffn_w_out']
TWIN_WEIGHTS = ['norm_mix', 'norm_mem', 'norm_ffn', 'norm_final', 'ssd_w_in', 'ssd_conv_w', 'ssd_conv_b', 'ssd_dt_bias', 'ssd_a_log', 'ssd_d', 'ssd_norm', 'ssd_w_out', 'kv_norm', 'w_kv_shared', 'dil_w_q', 'dil_w_o', 'mem_src_norm', 'mem_w_q', 'mem_w_kv', 'mem_w_o', 'ffn_w_in', 'ffn_w_out']
TWIN_DIFF_INPUT = 'x'
TWIN_INPUTS = ['x', 'mem', 'positions', 'norm_mix', 'norm_mem', 'norm_ffn', 'norm_final', 'ssd_w_in', 'ssd_conv_w', 'ssd_conv_b', 'ssd_dt_bias', 'ssd_a_log', 'ssd_d', 'ssd_norm', 'ssd_w_out', 'kv_norm', 'w_kv_shared', 'dil_w_q', 'dil_w_o', 'mem_src_norm', 'mem_w_q', 'mem_w_kv', 'mem_w_o', 'ffn_w_in', 'ffn_w_out', 'loss_target', 'm_norm_mix', 'm_norm_mem', 'm_norm_ffn', 'm_norm_final', 'm_ssd_w_in', 'm_ssd_conv_w', 'm_ssd_conv_b', 'm_ssd_dt_bias', 'm_ssd_a_log', 'm_ssd_d', 'm_ssd_norm', 'm_ssd_w_out', 'm_kv_norm', 'm_w_kv_shared', 'm_dil_w_q', 'm_dil_w_o', 'm_mem_src_norm', 'm_mem_w_q', 'm_mem_w_kv', 'm_mem_w_o', 'm_ffn_w_in', 'm_ffn_w_out', 'v_norm_mix', 'v_norm_mem', 'v_norm_ffn', 'v_norm_final', 'v_ssd_w_in', 'v_ssd_conv_w', 'v_ssd_conv_b', 'v_ssd_dt_bias', 'v_ssd_a_log', 'v_ssd_d', 'v_ssd_norm', 'v_ssd_w_out', 'v_kv_norm', 'v_w_kv_shared', 'v_dil_w_q', 'v_dil_w_o', 'v_mem_src_norm', 'v_mem_w_q', 'v_mem_w_kv', 'v_mem_w_o', 'v_ffn_w_in', 'v_ffn_w_out']
TWIN_OUTPUTS = ['loss', 'grad_x', 'grad_norm_mix', 'grad_norm_mem', 'grad_norm_ffn', 'grad_norm_final', 'grad_ssd_w_in', 'grad_ssd_conv_w', 'grad_ssd_conv_b', 'grad_ssd_dt_bias', 'grad_ssd_a_log', 'grad_ssd_d', 'grad_ssd_norm', 'grad_ssd_w_out', 'grad_kv_norm', 'grad_w_kv_shared', 'grad_dil_w_q', 'grad_dil_w_o', 'grad_mem_src_norm', 'grad_mem_w_q', 'grad_mem_w_kv', 'grad_mem_w_o', 'grad_ffn_w_in', 'grad_ffn_w_out', 'delta_norm_mix', 'delta_norm_mem', 'delta_norm_ffn', 'delta_norm_final', 'delta_ssd_w_in', 'delta_ssd_conv_w', 'delta_ssd_conv_b', 'delta_ssd_dt_bias', 'delta_ssd_a_log', 'delta_ssd_d', 'delta_ssd_norm', 'delta_ssd_w_out', 'delta_kv_norm', 'delta_w_kv_shared', 'delta_dil_w_q', 'delta_dil_w_o', 'delta_mem_src_norm', 'delta_mem_w_q', 'delta_mem_w_kv', 'delta_mem_w_o', 'delta_ffn_w_in', 'delta_ffn_w_out', 'new_m_norm_mix', 'new_m_norm_mem', 'new_m_norm_ffn', 'new_m_norm_final', 'new_m_ssd_w_in', 'new_m_ssd_conv_w', 'new_m_ssd_conv_b', 'new_m_ssd_dt_bias', 'new_m_ssd_a_log', 'new_m_ssd_d', 'new_m_ssd_norm', 'new_m_ssd_w_out', 'new_m_kv_norm', 'new_m_w_kv_shared', 'new_m_dil_w_q', 'new_m_dil_w_o', 'new_m_mem_src_norm', 'new_m_mem_w_q', 'new_m_mem_w_kv', 'new_m_mem_w_o', 'new_m_ffn_w_in', 'new_m_ffn_w_out', 'new_v_norm_mix', 'new_v_norm_mem', 'new_v_norm_ffn', 'new_v_norm_final', 'new_v_ssd_w_in', 'new_v_ssd_conv_w', 'new_v_ssd_conv_b', 'new_v_ssd_dt_bias', 'new_v_ssd_a_log', 'new_v_ssd_d', 'new_v_ssd_norm', 'new_v_ssd_w_out', 'new_v_kv_norm', 'new_v_w_kv_shared', 'new_v_dil_w_q', 'new_v_dil_w_o', 'new_v_mem_src_norm', 'new_v_mem_w_q', 'new_v_mem_w_kv', 'new_v_mem_w_o', 'new_v_ffn_w_in', 'new_v_ffn_w_out']
TWIN_LEAF_KINDS = {'loss': 'loss', 'grad_x': 'grad_x', 'grad_norm_mix': 'grad_w', 'grad_norm_mem': 'grad_w', 'grad_norm_ffn': 'grad_w', 'grad_norm_final': 'grad_w', 'grad_ssd_w_in': 'grad_w', 'grad_ssd_conv_w': 'grad_w', 'grad_ssd_conv_b': 'grad_w', 'grad_ssd_dt_bias': 'grad_w', 'grad_ssd_a_log': 'grad_w', 'grad_ssd_d': 'grad_w', 'grad_ssd_norm': 'grad_w', 'grad_ssd_w_out': 'grad_w', 'grad_kv_norm': 'grad_w', 'grad_w_kv_shared': 'grad_w', 'grad_dil_w_q': 'grad_w', 'grad_dil_w_o': 'grad_w', 'grad_mem_src_norm': 'grad_w', 'grad_mem_w_q': 'grad_w', 'grad_mem_w_kv': 'grad_w', 'grad_mem_w_o': 'grad_w', 'grad_ffn_w_in': 'grad_w', 'grad_ffn_w_out': 'grad_w', 'delta_norm_mix': 'delta_w', 'delta_norm_mem': 'delta_w', 'delta_norm_ffn': 'delta_w', 'delta_norm_final': 'delta_w', 'delta_ssd_w_in': 'delta_w', 'delta_ssd_conv_w': 'delta_w', 'delta_ssd_conv_b': 'delta_w', 'delta_ssd_dt_bias': 'delta_w', 'delta_ssd_a_log': 'delta_w', 'delta_ssd_d': 'delta_w', 'delta_ssd_norm': 'delta_w', 'delta_ssd_w_out': 'delta_w', 'delta_kv_norm': 'delta_w', 'delta_w_kv_shared': 'delta_w', 'delta_dil_w_q': 'delta_w', 'delta_dil_w_o': 'delta_w', 'delta_mem_src_norm': 'delta_w', 'delta_mem_w_q': 'delta_w', 'delta_mem_w_kv': 'delta_w', 'delta_mem_w_o': 'delta_w', 'delta_ffn_w_in': 'delta_w', 'delta_ffn_w_out': 'delta_w', 'new_m_norm_mix': 'new_m', 'new_m_norm_mem': 'new_m', 'new_m_norm_ffn': 'new_m', 'new_m_norm_final': 'new_m', 'new_m_ssd_w_in': 'new_m', 'new_m_ssd_conv_w': 'new_m', 'new_m_ssd_conv_b': 'new_m', 'new_m_ssd_dt_bias': 'new_m', 'new_m_ssd_a_log': 'new_m', 'new_m_ssd_d': 'new_m', 'new_m_ssd_norm': 'new_m', 'new_m_ssd_w_out': 'new_m', 'new_m_kv_norm': 'new_m', 'new_m_w_kv_shared': 'new_m', 'new_m_dil_w_q': 'new_m', 'new_m_dil_w_o': 'new_m', 'new_m_mem_src_norm': 'new_m', 'new_m_mem_w_q': 'new_m', 'new_m_mem_w_kv': 'new_m', 'new_m_mem_w_o': 'new_m', 'new_m_ffn_w_in': 'new_m', 'new_m_ffn_w_out': 'new_m', 'new_v_norm_mix': 'new_v', 'new_v_norm_mem': 'new_v', 'new_v_norm_ffn': 'new_v', 'new_v_norm_final': 'new_v', 'new_v_ssd_w_in': 'new_v', 'new_v_ssd_conv_w': 'new_v', 'new_v_ssd_conv_b': 'new_v', 'new_v_ssd_dt_bias': 'new_v', 'new_v_ssd_a_log': 'new_v', 'new_v_ssd_d': 'new_v', 'new_v_ssd_norm': 'new_v', 'new_v_ssd_w_out': 'new_v', 'new_v_kv_norm': 'new_v', 'new_v_w_kv_shared': 'new_v', 'new_v_dil_w_q': 'new_v', 'new_v_dil_w_o': 'new_v', 'new_v_mem_src_norm': 'new_v', 'new_v_mem_w_q': 'new_v', 'new_v_mem_w_kv': 'new_v', 'new_v_mem_w_o': 'new_v', 'new_v_ffn_w_in': 'new_v', 'new_v_ffn_w_out': 'new_v'}


def _forward(args):
    return _fwd_reference(*[args[k] for k in FWD_PARAMS])


def _output_shape():
    def fwd():
        inp = _fwd_setup_inputs(0)
        return _fwd_reference(*[inp[k] for k in FWD_PARAMS])
    out = _jax.eval_shape(fwd)
    return out.shape, out.dtype

N_MICROBATCH = 1
ADAM_LR = 0.001
ADAM_B1 = 0.9
ADAM_B2 = 0.999
ADAM_EPS = 1e-08
ADAM_WD = 0.01
ADAM_STEP = 10
PER_EXAMPLE_BATCH_AXIS = {'x': 0, 'mem': 0, 'positions': 0, 'loss_target': 0}
SHARED_INPUTS = []
_WEIGHT_DTYPES = {'norm_mix': _jnp.float32, 'norm_mem': _jnp.float32, 'norm_ffn': _jnp.float32, 'norm_final': _jnp.float32, 'ssd_w_in': _jnp.float32, 'ssd_conv_w': _jnp.float32, 'ssd_conv_b': _jnp.float32, 'ssd_dt_bias': _jnp.float32, 'ssd_a_log': _jnp.float32, 'ssd_d': _jnp.float32, 'ssd_norm': _jnp.float32, 'ssd_w_out': _jnp.float32, 'kv_norm': _jnp.float32, 'w_kv_shared': _jnp.float32, 'dil_w_q': _jnp.float32, 'dil_w_o': _jnp.float32, 'mem_src_norm': _jnp.float32, 'mem_w_q': _jnp.float32, 'mem_w_kv': _jnp.float32, 'mem_w_o': _jnp.float32, 'ffn_w_in': _jnp.float32, 'ffn_w_out': _jnp.float32}
MOMENT_SCALE = {'norm_mix': 1.135861e-01, 'norm_mem': 9.175004e-03, 'norm_ffn': 6.714444e-02, 'norm_final': 3.200561e+01, 'ssd_w_in': 7.133649e-02, 'ssd_conv_w': 6.566754e-02, 'ssd_conv_b': 9.280729e-02, 'ssd_dt_bias': 1.301556e-01, 'ssd_a_log': 2.378403e-01, 'ssd_d': 3.999549e-01, 'ssd_norm': 7.791349e-02, 'ssd_w_out': 1.068868e-01, 'kv_norm': 2.528227e-02, 'w_kv_shared': 1.031348e-02, 'dil_w_q': 6.234140e-03, 'dil_w_o': 1.418777e-02, 'mem_src_norm': 2.759804e-02, 'mem_w_q': 1.815477e-02, 'mem_w_kv': 1.862897e-02, 'mem_w_o': 9.509187e-03, 'ffn_w_in': 2.868184e-02, 'ffn_w_out': 4.679881e-02}


def _to_microbatches(a, axis):
    t = _jnp.moveaxis(a, axis, 0)
    t = t.reshape((N_MICROBATCH, t.shape[0] // N_MICROBATCH) + t.shape[1:])
    return _jnp.moveaxis(t, 1, axis + 1)


def setup_inputs(seed: int = 0) -> dict:
    inp = _fwd_setup_inputs(seed)
    key = _jax.random.fold_in(_jax.random.key(seed), 7919)
    shape, _ = _output_shape()
    out = dict(inp)
    out["loss_target"] = _jax.random.normal(_jax.random.fold_in(key, 0), shape, _jnp.float32)
    for i, name in enumerate(TWIN_WEIGHTS):
        w = inp[name].astype(_jnp.float32)
        if MOMENT_SCALE is None:
            s = _jnp.sqrt(_jnp.mean(_jnp.square(w)) + 1e-30)
        else:
            s = MOMENT_SCALE[name]
        km, kv = _jax.random.split(_jax.random.fold_in(key, i + 1))
        out[name] = w
        out["m_" + name] = s * _jax.random.normal(km, w.shape, _jnp.float32)
        out["v_" + name] = (s * s) * _jax.random.uniform(kv, w.shape, _jnp.float32, 0.5, 1.5)
    if N_MICROBATCH > 1:
        for name, axis in PER_EXAMPLE_BATCH_AXIS.items():
            out[name] = _to_microbatches(out[name], axis)
    return {'x': out['x'], 'mem': out['mem'], 'positions': out['positions'], 'norm_mix': out['norm_mix'], 'norm_mem': out['norm_mem'], 'norm_ffn': out['norm_ffn'], 'norm_final': out['norm_final'], 'ssd_w_in': out['ssd_w_in'], 'ssd_conv_w': out['ssd_conv_w'], 'ssd_conv_b': out['ssd_conv_b'], 'ssd_dt_bias': out['ssd_dt_bias'], 'ssd_a_log': out['ssd_a_log'], 'ssd_d': out['ssd_d'], 'ssd_norm': out['ssd_norm'], 'ssd_w_out': out['ssd_w_out'], 'kv_norm': out['kv_norm'], 'w_kv_shared': out['w_kv_shared'], 'dil_w_q': out['dil_w_q'], 'dil_w_o': out['dil_w_o'], 'mem_src_norm': out['mem_src_norm'], 'mem_w_q': out['mem_w_q'], 'mem_w_kv': out['mem_w_kv'], 'mem_w_o': out['mem_w_o'], 'ffn_w_in': out['ffn_w_in'], 'ffn_w_out': out['ffn_w_out'], 'loss_target': out['loss_target'], 'm_norm_mix': out['m_norm_mix'], 'm_norm_mem': out['m_norm_mem'], 'm_norm_ffn': out['m_norm_ffn'], 'm_norm_final': out['m_norm_final'], 'm_ssd_w_in': out['m_ssd_w_in'], 'm_ssd_conv_w': out['m_ssd_conv_w'], 'm_ssd_conv_b': out['m_ssd_conv_b'], 'm_ssd_dt_bias': out['m_ssd_dt_bias'], 'm_ssd_a_log': out['m_ssd_a_log'], 'm_ssd_d': out['m_ssd_d'], 'm_ssd_norm': out['m_ssd_norm'], 'm_ssd_w_out': out['m_ssd_w_out'], 'm_kv_norm': out['m_kv_norm'], 'm_w_kv_shared': out['m_w_kv_shared'], 'm_dil_w_q': out['m_dil_w_q'], 'm_dil_w_o': out['m_dil_w_o'], 'm_mem_src_norm': out['m_mem_src_norm'], 'm_mem_w_q': out['m_mem_w_q'], 'm_mem_w_kv': out['m_mem_w_kv'], 'm_mem_w_o': out['m_mem_w_o'], 'm_ffn_w_in': out['m_ffn_w_in'], 'm_ffn_w_out': out['m_ffn_w_out'], 'v_norm_mix': out['v_norm_mix'], 'v_norm_mem': out['v_norm_mem'], 'v_norm_ffn': out['v_norm_ffn'], 'v_norm_final': out['v_norm_final'], 'v_ssd_w_in': out['v_ssd_w_in'], 'v_ssd_conv_w': out['v_ssd_conv_w'], 'v_ssd_conv_b': out['v_ssd_conv_b'], 'v_ssd_dt_bias': out['v_ssd_dt_bias'], 'v_ssd_a_log': out['v_ssd_a_log'], 'v_ssd_d': out['v_ssd_d'], 'v_ssd_norm': out['v_ssd_norm'], 'v_ssd_w_out': out['v_ssd_w_out'], 'v_kv_norm': out['v_kv_norm'], 'v_w_kv_shared': out['v_w_kv_shared'], 'v_dil_w_q': out['v_dil_w_q'], 'v_dil_w_o': out['v_dil_w_o'], 'v_mem_src_norm': out['v_mem_src_norm'], 'v_mem_w_q': out['v_mem_w_q'], 'v_mem_w_kv': out['v_mem_w_kv'], 'v_mem_w_o': out['v_mem_w_o'], 'v_ffn_w_in': out['v_ffn_w_in'], 'v_ffn_w_out': out['v_ffn_w_out']}


def _loss(weights, diff, rest, loss_target):
    with _jax.named_scope("forward"):
        args = {**rest, TWIN_DIFF_INPUT: diff, **{k: w.astype(_WEIGHT_DTYPES[k]) for k, w in weights.items()}}
        y = _forward(args)
    with _jax.named_scope("loss_head"):
        err = _jnp.square(y.astype(_jnp.float32) - loss_target)
        return 0.5 * _jnp.sum(_jnp.mean(err, axis=-1)) if err.ndim else 0.5 * err


def _adamw(w, g, m, v):
    m = ADAM_B1 * m + (1.0 - ADAM_B1) * g
    v = ADAM_B2 * v + (1.0 - ADAM_B2) * _jnp.square(g)
    m_hat = m / (1.0 - ADAM_B1 ** ADAM_STEP)
    v_hat = v / (1.0 - ADAM_B2 ** ADAM_STEP)
    delta = -ADAM_LR * (m_hat / (_jnp.sqrt(v_hat) + ADAM_EPS) + ADAM_WD * w)
    return delta, m, v


def reference(x, mem, positions, norm_mix, norm_mem, norm_ffn, norm_final, ssd_w_in, ssd_conv_w, ssd_conv_b, ssd_dt_bias, ssd_a_log, ssd_d, ssd_norm, ssd_w_out, kv_norm, w_kv_shared, dil_w_q, dil_w_o, mem_src_norm, mem_w_q, mem_w_kv, mem_w_o, ffn_w_in, ffn_w_out, loss_target, m_norm_mix, m_norm_mem, m_norm_ffn, m_norm_final, m_ssd_w_in, m_ssd_conv_w, m_ssd_conv_b, m_ssd_dt_bias, m_ssd_a_log, m_ssd_d, m_ssd_norm, m_ssd_w_out, m_kv_norm, m_w_kv_shared, m_dil_w_q, m_dil_w_o, m_mem_src_norm, m_mem_w_q, m_mem_w_kv, m_mem_w_o, m_ffn_w_in, m_ffn_w_out, v_norm_mix, v_norm_mem, v_norm_ffn, v_norm_final, v_ssd_w_in, v_ssd_conv_w, v_ssd_conv_b, v_ssd_dt_bias, v_ssd_a_log, v_ssd_d, v_ssd_norm, v_ssd_w_out, v_kv_norm, v_w_kv_shared, v_dil_w_q, v_dil_w_o, v_mem_src_norm, v_mem_w_q, v_mem_w_kv, v_mem_w_o, v_ffn_w_in, v_ffn_w_out):
    given = dict(x=x, mem=mem, positions=positions, norm_mix=norm_mix, norm_mem=norm_mem, norm_ffn=norm_ffn, norm_final=norm_final, ssd_w_in=ssd_w_in, ssd_conv_w=ssd_conv_w, ssd_conv_b=ssd_conv_b, ssd_dt_bias=ssd_dt_bias, ssd_a_log=ssd_a_log, ssd_d=ssd_d, ssd_norm=ssd_norm, ssd_w_out=ssd_w_out, kv_norm=kv_norm, w_kv_shared=w_kv_shared, dil_w_q=dil_w_q, dil_w_o=dil_w_o, mem_src_norm=mem_src_norm, mem_w_q=mem_w_q, mem_w_kv=mem_w_kv, mem_w_o=mem_w_o, ffn_w_in=ffn_w_in, ffn_w_out=ffn_w_out, loss_target=loss_target, m_norm_mix=m_norm_mix, m_norm_mem=m_norm_mem, m_norm_ffn=m_norm_ffn, m_norm_final=m_norm_final, m_ssd_w_in=m_ssd_w_in, m_ssd_conv_w=m_ssd_conv_w, m_ssd_conv_b=m_ssd_conv_b, m_ssd_dt_bias=m_ssd_dt_bias, m_ssd_a_log=m_ssd_a_log, m_ssd_d=m_ssd_d, m_ssd_norm=m_ssd_norm, m_ssd_w_out=m_ssd_w_out, m_kv_norm=m_kv_norm, m_w_kv_shared=m_w_kv_shared, m_dil_w_q=m_dil_w_q, m_dil_w_o=m_dil_w_o, m_mem_src_norm=m_mem_src_norm, m_mem_w_q=m_mem_w_q, m_mem_w_kv=m_mem_w_kv, m_mem_w_o=m_mem_w_o, m_ffn_w_in=m_ffn_w_in, m_ffn_w_out=m_ffn_w_out, v_norm_mix=v_norm_mix, v_norm_mem=v_norm_mem, v_norm_ffn=v_norm_ffn, v_norm_final=v_norm_final, v_ssd_w_in=v_ssd_w_in, v_ssd_conv_w=v_ssd_conv_w, v_ssd_conv_b=v_ssd_conv_b, v_ssd_dt_bias=v_ssd_dt_bias, v_ssd_a_log=v_ssd_a_log, v_ssd_d=v_ssd_d, v_ssd_norm=v_ssd_norm, v_ssd_w_out=v_ssd_w_out, v_kv_norm=v_kv_norm, v_w_kv_shared=v_w_kv_shared, v_dil_w_q=v_dil_w_q, v_dil_w_o=v_dil_w_o, v_mem_src_norm=v_mem_src_norm, v_mem_w_q=v_mem_w_q, v_mem_w_kv=v_mem_w_kv, v_mem_w_o=v_mem_w_o, v_ffn_w_in=v_ffn_w_in, v_ffn_w_out=v_ffn_w_out)
    weights = {n: given[n] for n in TWIN_WEIGHTS}
    shared = {n: given[n] for n in SHARED_INPUTS}
    per_example = {n: given[n] for n in ['x', 'mem', 'positions']}
    grad_fn = _jax.value_and_grad(_loss, argnums=(0, 1))

    def one_microbatch(ex, loss_target):
        ex = dict(ex)
        diff = ex.pop(TWIN_DIFF_INPUT)
        return grad_fn(weights, diff, {**shared, **ex}, loss_target)

    if N_MICROBATCH == 1:
        loss, (grad_w, grad_x) = one_microbatch(per_example, given["loss_target"])
    else:
        def body(carry, xs):
            loss_sum, grad_sum = carry
            l_k, (gw_k, gx_k) = one_microbatch(xs[0], xs[1])
            with _jax.named_scope("update"):
                return (loss_sum + l_k, _jax.tree.map(_jnp.add, grad_sum, gw_k)), gx_k

        init = (_jnp.zeros((), _jnp.float32), _jax.tree.map(_jnp.zeros_like, weights))
        (loss, grad_w), grad_x = _jax.lax.scan(body, init, (per_example, given["loss_target"]))
    with _jax.named_scope("update"):
        delta_w, new_m, new_v = {}, {}, {}
        for n in TWIN_WEIGHTS:
            delta_w[n], new_m[n], new_v[n] = _adamw(weights[n], grad_w[n], given["m_" + n], given["v_" + n])
    return (loss, grad_x, *[grad_w[n] for n in TWIN_WEIGHTS], *[delta_w[n] for n in TWIN_WEIGHTS],
            *[new_m[n] for n in TWIN_WEIGHTS], *[new_v[n] for n in TWIN_WEIGHTS])
```

```python
import functools
import math

import jax
import jax.numpy as jnp
from jax import lax
from jax.experimental import pallas as pl
from jax.experimental.pallas import tpu as pltpu

F32 = jnp.float32
BF16 = jnp.bfloat16
MESH = pl.DeviceIdType.MESH
HIGHEST = lax.Precision.HIGHEST

NORM_EPS = 1e-6
SSD_GROUPS = 8
SSD_STATE = 128
SSD_CHUNK = 128
SSD_CONV = 4
DIL_DILATIONS = (1, 4, 16)
DIL_HEADS = 16
DIL_BLOCK = 128
ROPE_THETA = 10000.0
MEM_HEADS = 4
ADAM_LR, ADAM_B1, ADAM_B2, ADAM_EPS, ADAM_WD, ADAM_STEP = 0.001, 0.9, 0.999, 1e-08, 0.01, 10

V7X_VMEM_LIMIT = 48 * 1024 * 1024
LANES = 128
SUBLANES_BF16 = 16

WEIGHTS = ['norm_mix', 'norm_mem', 'norm_ffn', 'norm_final', 'ssd_w_in', 'ssd_conv_w', 'ssd_conv_b',
           'ssd_dt_bias', 'ssd_a_log', 'ssd_d', 'ssd_norm', 'ssd_w_out', 'kv_norm', 'w_kv_shared', 'dil_w_q',
           'dil_w_o', 'mem_src_norm', 'mem_w_q', 'mem_w_kv', 'mem_w_o', 'ffn_w_in', 'ffn_w_out']
COL_SHARDED = ('ssd_w_in', 'w_kv_shared', 'dil_w_q', 'mem_w_o', 'ffn_w_in')
ROW_SHARDED = ('ssd_w_out', 'dil_w_o', 'mem_w_q', 'mem_w_kv', 'ffn_w_out')


def _dot(a, b, ca, cb, prec=None):
    return lax.dot_general(a, b, (((ca,), (cb,)), ((), ())), preferred_element_type=F32, precision=prec)


def _nn(a, b, prec=None):
    return _dot(a, b, 1, 0, prec)


def _nt(a, b, prec=None):
    return _dot(a, b, 1, 1, prec)


def _tn(a, b, prec=None):
    return _dot(a, b, 0, 0, prec)


def _tile(n, pref, unit=LANES):
    if n <= pref:
        return n
    t = (pref // unit) * unit
    while t >= unit:
        if n % t == 0:
            return t
        t -= unit
    return n


def _call(body, name, grid, in_specs, out_specs, out_shape, scratch=(), sem=None, aliases=None):
    return pl.pallas_call(
        body, name=name, grid=grid, in_specs=in_specs, out_specs=out_specs, out_shape=out_shape,
        scratch_shapes=list(scratch), input_output_aliases=aliases or {},
        compiler_params=pltpu.CompilerParams(dimension_semantics=sem, vmem_limit_bytes=V7X_VMEM_LIMIT))


def _sds(shape, dtype):
    return jax.ShapeDtypeStruct(tuple(shape), dtype)


def _silu(x):
    return x * jax.nn.sigmoid(x)


def _dsilu(x):
    s = jax.nn.sigmoid(x)
    return s * (1.0 + x * (1.0 - s))


def _mm(a, b, mode, name, *, m, n, k, out_dtype=F32, res=None, b_noff=0, b_koff=0, pref=(1024, 1024, 512)):
    tm = _tile(m, pref[0], LANES if mode == 'tn' else 8)
    tn = _tile(math.gcd(n, b_noff) if b_noff else n, pref[1])
    tk = _tile(math.gcd(k, b_koff) if b_koff else k, pref[2])
    nk = k // tk
    jo, ko = b_noff // tn, b_koff // tk
    if mode == 'nn':
        a_spec = pl.BlockSpec((tm, tk), lambda i, j, kk: (i, kk))
        b_spec = pl.BlockSpec((tk, tn), lambda i, j, kk: (kk + ko, j + jo))
        ca, cb = 1, 0
    elif mode == 'nt':
        a_spec = pl.BlockSpec((tm, tk), lambda i, j, kk: (i, kk))
        b_spec = pl.BlockSpec((tn, tk), lambda i, j, kk: (j + jo, kk + ko))
        ca, cb = 1, 1
    else:
        a_spec = pl.BlockSpec((tk, tm), lambda i, j, kk: (kk, i))
        b_spec = pl.BlockSpec((tk, tn), lambda i, j, kk: (kk + ko, j + jo))
        ca, cb = 0, 0
    o_spec = pl.BlockSpec((tm, tn), lambda i, j, kk: (i, j))
    has_res = res is not None

    def body(*refs):
        if has_res:
            a_ref, b_ref, r_ref, o_ref, acc = refs
        else:
            a_ref, b_ref, o_ref, acc = refs
        kk = pl.program_id(2)

        @pl.when(kk == 0)
        def _():
            acc[...] = jnp.zeros_like(acc)

        acc[...] += _dot(a_ref[...].astype(BF16), b_ref[...].astype(BF16), ca, cb)

        @pl.when(kk == nk - 1)
        def _():
            r = acc[...]
            if has_res:
                r = r + r_ref[...].astype(F32)
            o_ref[...] = r.astype(o_ref.dtype)

    ins = [a, b] + ([res] if has_res else [])
    specs = [a_spec, b_spec] + ([o_spec] if has_res else [])
    return _call(body, name, (m // tm, n // tn, nk), specs, o_spec, _sds((m, n), out_dtype),
                 scratch=[pltpu.VMEM((tm, tn), F32)], sem=("parallel", "parallel", "arbitrary"))(*ins)


def _rowwise(fn, name, rows, consts, outs, tb, n_rows):
    n_r, n_c = len(rows), len(consts)
    in_specs = [pl.BlockSpec((tb, w), functools.partial(lambda i, cb: (i, cb), cb=cb)) for _, w, cb in rows]
    in_specs += [pl.BlockSpec(c.shape, functools.partial(lambda i, nd: (0,) * nd, nd=c.ndim)) for c in consts]
    out_specs, out_shape = [], []
    for kind, w, dt in outs:
        if kind == 'row':
            out_specs.append(pl.BlockSpec((tb, w), lambda i: (i, 0)))
            out_shape.append(_sds((n_rows, w), dt))
        else:
            out_specs.append(pl.BlockSpec(w, lambda i: (0, 0)))
            out_shape.append(_sds(w, dt))

    def body(*refs):
        ins = [r[...] for r in refs[:n_r + n_c]]
        orefs = refs[n_r + n_c:]
        vals = fn(*ins)
        i = pl.program_id(0)
        for (kind, _, _), o_ref, v in zip(outs, orefs, vals):
            if kind == 'row':
                o_ref[...] = v.astype(o_ref.dtype)
            else:
                @pl.when(i == 0)
                def _(o_ref=o_ref):
                    o_ref[...] = jnp.zeros_like(o_ref)

                o_ref[...] += v.astype(o_ref.dtype)

    res = _call(body, name, (n_rows // tb,), in_specs, out_specs, out_shape, sem=("arbitrary",))(
        *[r[0] for r in rows], *consts)
    return res


def _rms_fwd_fn(h, g):
    r = lax.rsqrt(jnp.mean(h * h, axis=-1, keepdims=True) + NORM_EPS)
    return (h * r * g,)


def _rms_bwd_vals(h, g, dy):
    r = lax.rsqrt(jnp.mean(h * h, axis=-1, keepdims=True) + NORM_EPS)
    t = dy * g
    dh = r * t - h * (r * r * r) * jnp.mean(h * t, axis=-1, keepdims=True)
    dg = jnp.sum(dy * h * r, axis=0, keepdims=True)
    return dh, dg


def _rmsnorm(h, g, name):
    n, d = h.shape
    return _rowwise(_rms_fwd_fn, name, [(h, d, 0)], [g.reshape(1, d)], [('row', d, BF16)], _tile(n, 512, 8), n)[0]


def _rmsnorm_bwd(h, g, du, dres, name):
    n, d = h.shape

    def fn(hv, duv, drv, gv):
        dh, dg = _rms_bwd_vals(hv, gv, duv.astype(F32))
        return dh + drv, dg

    return _rowwise(fn, name, [(h, d, 0), (du, d, 0), (dres, d, 0)], [g.reshape(1, d)],
                    [('row', d, F32), ('acc', (1, d), F32)], _tile(n, 256, 8), n)


def _rmsnorm_bwd_noacc(h, g, du, name):
    n, d = h.shape

    def fn(hv, duv, gv):
        return _rms_bwd_vals(hv, gv, duv.astype(F32))

    return _rowwise(fn, name, [(h, d, 0), (du, d, 0)], [g.reshape(1, d)],
                    [('row', d, F32), ('acc', (1, d), F32)], _tile(n, 256, 8), n)


def _swiglu_fwd(gu, name):
    n, f2 = gu.shape
    f = f2 // 2

    def fn(v):
        return (_silu(v[:, :f]) * v[:, f:],)

    return _rowwise(fn, name, [(gu, f2, 0)], [], [('row', f, BF16)], _tile(n, 128, 8), n)[0]


def _swiglu_bwd(gu, dact, name):
    n, f2 = gu.shape
    f = f2 // 2

    def fn(v, da):
        g, up = v[:, :f], v[:, f:]
        da = da.astype(F32)
        return (jnp.concatenate([da * up * _dsilu(g), da * _silu(g)], axis=1),)

    return _rowwise(fn, name, [(gu, f2, 0), (dact, f, 0)], [], [('row', f2, BF16)], _tile(n, 128, 8), n)[0]


def _group_sum(v, ng):
    w = v.shape[1] // ng
    return [jnp.sum(v[:, i * w:(i + 1) * w], axis=1, keepdims=True) for i in range(ng)]


def _gated_norm_fwd(y, z, nw, name):
    n, di = y.shape
    gw = di // SSD_GROUPS

    def fn(yv, zv, nwv):
        a = yv * _silu(zv)
        ms = _group_sum(a * a, SSD_GROUPS)
        out = jnp.concatenate([a[:, i * gw:(i + 1) * gw] * lax.rsqrt(ms[i] / gw + NORM_EPS)
                               for i in range(SSD_GROUPS)], axis=1)
        return (out * nwv,)

    return _rowwise(fn, name, [(y, di, 0), (z, di, 0)], [nw.reshape(1, di)], [('row', di, BF16)], _tile(n, 256, 8), n)[0]


def _gated_norm_bwd(y, z, nw, dout, name):
    n, di = y.shape
    gw = di // SSD_GROUPS

    def fn(yv, zv, dov, nwv):
        sz = _silu(zv)
        a = yv * sz
        t = dov * nwv
        ms = _group_sum(a * a, SSD_GROUPS)
        at = _group_sum(a * t, SSD_GROUPS)
        das, ars = [], []
        for i in range(SSD_GROUPS):
            r = lax.rsqrt(ms[i] / gw + NORM_EPS)
            sl = slice(i * gw, (i + 1) * gw)
            das.append(r * t[:, sl] - a[:, sl] * (r * r * r) * (at[i] / gw))
            ars.append(a[:, sl] * r)
        da = jnp.concatenate(das, axis=1)
        ar = jnp.concatenate(ars, axis=1)
        return da * sz, da * yv * _dsilu(zv), jnp.sum(dov * ar, axis=0, keepdims=True)

    return _rowwise(fn, name, [(y, di, 0), (z, di, 0), (dout, di, 0)], [nw.reshape(1, di)],
                    [('row', di, F32), ('row', di, BF16), ('acc', (1, di), F32)], _tile(n, 128, 8), n)


def _softplus_fwd(raw, bias, name):
    n, h = raw.shape

    def fn(v, b):
        t = v + b
        return (jnp.maximum(t, 0.0) + jnp.log(1.0 + jnp.exp(-jnp.abs(t))),)

    return _rowwise(fn, name, [(raw, h, 0)], [bias.reshape(1, h)], [('row', h, F32)], _tile(n, 1024, 8), n)[0]


def _softplus_bwd(raw, bias, ddt, name):
    n, h = raw.shape

    def fn(v, d, b):
        g = d * jax.nn.sigmoid(v + b)
        return g, jnp.sum(g, axis=0, keepdims=True)

    return _rowwise(fn, name, [(raw, h, 0), (ddt, h, 0)], [bias.reshape(1, h)],
                    [('row', h, BF16), ('acc', (1, h), F32)], _tile(n, 1024, 8), n)


def _loss_and_grad(h, g, tgt, name):
    n, d = h.shape

    def fn(hv, tv, gv):
        y = _rms_fwd_fn(hv, gv)[0]
        err = y - tv
        part = 0.5 * jnp.sum(jnp.sum(err * err, axis=1, keepdims=True), axis=0, keepdims=True) / d
        dh, dg = _rms_bwd_vals(hv, gv, err / d)
        return jnp.broadcast_to(part, (8, LANES)), dh, dg

    return _rowwise(fn, name, [(h, d, 0), (tgt, d, 0)], [g.reshape(1, d)],
                    [('acc', (8, LANES), F32), ('row', d, F32), ('acc', (1, d), F32)], _tile(n, 256, 8), n)


def _rope_tables(pos_col, dh, name):
    n = pos_col.shape[0]
    half = dh // 2
    inv = ROPE_THETA ** (-jnp.arange(half, dtype=F32) / half)
    inv2 = jnp.concatenate([inv, inv]).reshape(1, dh)
    sign = jnp.concatenate([-jnp.ones((half,), F32), jnp.ones((half,), F32)]).reshape(1, dh)

    def fn(p, iv, sg):
        ang = p.astype(F32) * iv
        return jnp.cos(ang), jnp.sin(ang) * sg

    return _rowwise(fn, name, [(pos_col, 1, 0)], [inv2, sign], [('row', dh, F32), ('row', dh, F32)], _tile(n, 1024, 8), n)


def _rope(t, cos2, sin2, name, *, width, sign, out_dtype):
    n = t.shape[0]
    dh = cos2.shape[1]
    tw = _tile(width, 2048)
    tb = _tile(n, 512, 8)

    def body(t_ref, c_ref, s_ref, o_ref):
        c, s = c_ref[...], s_ref[...]
        for hh in range(tw // dh):
            v = t_ref[:, hh * dh:(hh + 1) * dh].astype(F32)
            o_ref[:, hh * dh:(hh + 1) * dh] = (v * c + sign * pltpu.roll(v, dh // 2, 1) * s).astype(o_ref.dtype)

    return _call(body, name, (n // tb, width // tw),
                 [pl.BlockSpec((tb, tw), lambda i, j: (i, j)), pl.BlockSpec((tb, dh), lambda i, j: (i, 0)),
                  pl.BlockSpec((tb, dh), lambda i, j: (i, 0))],
                 pl.BlockSpec((tb, tw), lambda i, j: (i, j)), _sds((n, width), out_dtype),
                 sem=("parallel", "parallel"))(t, cos2, sin2)


def _conv_taps(ext, w, tb):
    shifted = [ext[8:] if k == SSD_CONV - 1 else pltpu.roll(ext, SSD_CONV - 1 - k, 0)[8:] for k in range(SSD_CONV)]
    pre = shifted[0] * w[0:1]
    for k in range(1, SSD_CONV):
        pre = pre + shifted[k] * w[k:k + 1]
    return pre, shifted


def _conv_specs(n, c, tb, tc):
    hb = tb // 8
    blk = pl.BlockSpec((tb, tc), lambda j, i: (i, j))
    halo = pl.BlockSpec((8, tc), lambda j, i: (jnp.maximum(i * hb - 1, 0), j))
    wsp = pl.BlockSpec((SSD_CONV, tc), lambda j, i: (0, j))
    bsp = pl.BlockSpec((1, tc), lambda j, i: (0, j))
    return blk, halo, wsp, bsp


def _conv_fwd(u, w, b, name):
    n, c = u.shape
    tb, tc = _tile(n, 512, 8), _tile(c, 1536)
    blk, halo, wsp, bsp = _conv_specs(n, c, tb, tc)

    def body(u_ref, h_ref, w_ref, b_ref, o_ref):
        halo_v = jnp.where(pl.program_id(1) > 0, h_ref[...], 0.0)
        pre, _ = _conv_taps(jnp.concatenate([halo_v, u_ref[...]], axis=0), w_ref[...], tb)
        o_ref[...] = _silu(pre + b_ref[...])

    return _call(body, name, (c // tc, n // tb), [blk, halo, wsp, bsp], blk, _sds((n, c), F32),
                 sem=("parallel", "arbitrary"))(u, u, w, b.reshape(1, c))


def _conv_bwd_pre(u, w, b, dout, name):
    n, c = u.shape
    tb, tc = _tile(n, 512, 8), _tile(c, 1536)
    blk, halo, wsp, bsp = _conv_specs(n, c, tb, tc)

    def body(u_ref, h_ref, w_ref, b_ref, d_ref, dp_ref, dw_ref, db_ref):
        i = pl.program_id(1)
        halo_v = jnp.where(i > 0, h_ref[...], 0.0)
        pre, shifted = _conv_taps(jnp.concatenate([halo_v, u_ref[...]], axis=0), w_ref[...], tb)
        dp = d_ref[...] * _dsilu(pre + b_ref[...])
        dp_ref[...] = dp

        @pl.when(i == 0)
        def _():
            dw_ref[...] = jnp.zeros_like(dw_ref)
            db_ref[...] = jnp.zeros_like(db_ref)

        dw_ref[...] += jnp.concatenate([jnp.sum(dp * s, axis=0, keepdims=True) for s in shifted], axis=0)
        db_ref[...] += jnp.sum(dp, axis=0, keepdims=True)

    return _call(body, name, (c // tc, n // tb), [blk, halo, wsp, bsp, blk], [blk, wsp, bsp],
                 [_sds((n, c), F32), _sds((SSD_CONV, c), F32), _sds((1, c), F32)],
                 sem=("parallel", "arbitrary"))(u, u, w, b.reshape(1, c), dout)


def _conv_bwd_in(dpre, w, name):
    n, c = dpre.shape
    tb, tc = _tile(n, 512, 8), _tile(c, 1536)
    hb = tb // 8
    nb = n // tb
    blk = pl.BlockSpec((tb, tc), lambda j, i: (i, j))
    nxt = pl.BlockSpec((8, tc), lambda j, i: (jnp.minimum((i + 1) * hb, n // 8 - 1), j))
    wsp = pl.BlockSpec((SSD_CONV, tc), lambda j, i: (0, j))

    def body(d_ref, n_ref, w_ref, o_ref):
        nxt_v = jnp.where(pl.program_id(1) < nb - 1, n_ref[...], 0.0)
        ext = jnp.concatenate([d_ref[...], nxt_v], axis=0)
        wv = w_ref[...]
        acc = ext[:tb] * wv[SSD_CONV - 1:SSD_CONV]
        for k in range(SSD_CONV - 1):
            s = SSD_CONV - 1 - k
            acc = acc + pltpu.roll(ext, tb + 8 - s, 0)[:tb] * wv[k:k + 1]
        o_ref[...] = acc.astype(o_ref.dtype)

    return _call(body, name, (c // tc, nb), [blk, nxt, wsp], blk, _sds((n, c), BF16),
                 sem=("parallel", "arbitrary"))(dpre, dpre, w)


def _col(v, j):
    lane = lax.broadcasted_iota(jnp.int32, v.shape, 1)
    return jnp.sum(jnp.where(lane == j, v, 0.0), axis=1, keepdims=True)


def _row(v, j):
    sub = lax.broadcasted_iota(jnp.int32, v.shape, 0)
    return jnp.sum(jnp.where(sub == j, v, 0.0), axis=0, keepdims=True)


def _ssd_head_terms(dtc, dtr, a_row, a_col):
    ll = dtc.shape[0]
    r = lax.broadcasted_iota(jnp.int32, (ll, ll), 0)
    c = lax.broadcasted_iota(jnp.int32, (ll, ll), 1)
    tril = (r >= c)
    trilf = tril.astype(F32)
    triuf = (r <= c).astype(F32)
    cumc = _nn(trilf, dtc * a_row, HIGHEST)
    cumr = _nn(dtr * a_col, triuf, HIGHEST)
    return cumc, cumr, tril, trilf, triuf


def _ssd_specs(n, di, gn, jh, p, nc, rev):
    ll = SSD_CHUNK
    jp = jh * p

    def ci(c):
        return (nc - 1 - c) if rev else c

    xs = pl.BlockSpec((ll, jp), lambda g, c: (ci(c), g))
    bs = pl.BlockSpec((ll, SSD_STATE), lambda g, c: (ci(c), di // SSD_STATE + g))
    cs = pl.BlockSpec((ll, SSD_STATE), lambda g, c: (ci(c), (di + gn * SSD_STATE) // SSD_STATE + g))
    dtc = pl.BlockSpec((None, ll, jh), lambda g, c: (g, ci(c), 0))
    dtr = pl.BlockSpec((None, jh, ll), lambda g, c: (g, 0, ci(c)))
    arow = pl.BlockSpec((None, 1, jh), lambda g, c: (g, 0, 0))
    acol = pl.BlockSpec((None, jh, 1), lambda g, c: (g, 0, 0))
    dsk = pl.BlockSpec((None, 1, jp), lambda g, c: (g, 0, 0))
    hin = pl.BlockSpec((None, None, SSD_STATE, jp), lambda g, c: (g, ci(c), 0, 0))
    ys = pl.BlockSpec((ll, jp), lambda g, c: (ci(c), g))
    return xs, bs, cs, dtc, dtr, arow, acol, dsk, hin, ys


def _ssd_fwd(xbc, dtc, dtr, a_row, a_col, dskip, name, *, di, p):
    n = xbc.shape[0]
    gn = SSD_GROUPS
    jh = dtc.shape[2]
    jp = jh * p
    ll = SSD_CHUNK
    nc = n // ll
    xs, bs, cs, dtcs, dtrs, arow, acol, dsk, hin, ys = _ssd_specs(n, di, gn, jh, p, nc, False)
    pair = 2 * p

    def body(x_ref, b_ref, c_ref, dtc_ref, dtr_ref, ar_ref, ac_ref, ds_ref, y_ref, hin_ref, h_scr):
        @pl.when(pl.program_id(1) == 0)
        def _():
            h_scr[...] = jnp.zeros_like(h_scr)

        dtcv, dtrv = dtc_ref[...], dtr_ref[...]
        cumc, cumr, tril, _, _ = _ssd_head_terms(dtcv, dtrv, ar_ref[...], ac_ref[...])
        tot = jnp.sum(dtcv * ar_ref[...], axis=0, keepdims=True)
        bb, cb_ = b_ref[...].astype(BF16), c_ref[...].astype(BF16)
        cbm = _nt(cb_, bb)
        hin_ref[...] = h_scr[...]
        lane = lax.broadcasted_iota(jnp.int32, (ll, pair), 1)
        lane1 = lax.broadcasted_iota(jnp.int32, (1, pair), 1)
        for pr in range(jh // 2):
            sl = slice(pr * pair, (pr + 1) * pair)
            xp = x_ref[:, sl]
            xpb = xp.astype(BF16)
            hp = h_scr[:, sl]
            ydiag = jnp.zeros((ll, pair), F32)
            e_p = jnp.zeros((ll, pair), F32)
            w_p = jnp.zeros((ll, pair), F32)
            cd_p = jnp.zeros((1, pair), F32)
            for q in range(2):
                j = 2 * pr + q
                mj = (lane >= p) if q else (lane < p)
                cc, cr = _col(cumc, j), _row(cumr, j)
                decay = jnp.exp(jnp.where(tril, cc - cr, -1e30))
                mm = cbm * decay * _row(dtrv, j)
                ydiag = ydiag + _nn(mm.astype(BF16), jnp.where(mj, xpb, jnp.zeros_like(xpb)))
                cl = _col(tot, j)
                e_p = jnp.where(mj, jnp.exp(cc), e_p)
                w_p = jnp.where(mj, jnp.exp(cl - cc) * _col(dtcv, j), w_p)
                cd_p = jnp.where((lane1 >= p) if q else (lane1 < p), jnp.exp(cl), cd_p)
            yoff = _nn(cb_, hp.astype(BF16)) * e_p
            y_ref[:, sl] = ydiag + yoff + xp * ds_ref[:, sl]
            h_scr[:, sl] = hp * cd_p + _tn(bb, (xp * w_p).astype(BF16))

    y, hins = _call(body, name, (gn, nc), [xs, bs, cs, dtcs, dtrs, arow, acol, dsk], [ys, hin],
                    [_sds((n, di), F32), _sds((gn, nc, SSD_STATE, jp), F32)],
                    scratch=[pltpu.VMEM((SSD_STATE, jp), F32)], sem=("parallel", "arbitrary"))(
        xbc, xbc, xbc, dtc, dtr, a_row, a_col, dskip)
    return y, hins


def _ssd_bwd(xbc, dtc, dtr, a_row, a_col, dskip, hins, dy, name, *, di, p):
    n = xbc.shape[0]
    gn = SSD_GROUPS
    jh = dtc.shape[2]
    jp = jh * p
    ll = SSD_CHUNK
    nc = n // ll
    xs, bs, cs, dtcs, dtrs, arow, acol, dsk, hin, ys = _ssd_specs(n, di, gn, jh, p, nc, True)
    pair = 2 * p
    bc_out = pl.BlockSpec((ll, SSD_STATE), lambda g, c: (nc - 1 - c, g))

    def body(x_ref, b_ref, c_ref, dtc_ref, dtr_ref, ar_ref, ac_ref, ds_ref, hin_ref, dy_ref,
             dx_ref, db_ref, dc_ref, ddtc_ref, ddtr_ref, dac_ref, dar_ref, dd_ref, dh_scr):
        first = pl.program_id(1) == 0

        @pl.when(first)
        def _():
            dh_scr[...] = jnp.zeros_like(dh_scr)
            dac_ref[...] = jnp.zeros_like(dac_ref)
            dar_ref[...] = jnp.zeros_like(dar_ref)
            dd_ref[...] = jnp.zeros_like(dd_ref)

        dtcv, dtrv = dtc_ref[...], dtr_ref[...]
        a_r, a_c = ar_ref[...], ac_ref[...]
        cumc, cumr, tril, trilf, triuf = _ssd_head_terms(dtcv, dtrv, a_r, a_c)
        tot = jnp.sum(dtcv * a_r, axis=0, keepdims=True)
        bb, cb_ = b_ref[...].astype(BF16), c_ref[...].astype(BF16)
        cbm = _nt(cb_, bb)
        lane = lax.broadcasted_iota(jnp.int32, (ll, pair), 1)
        lane1 = lax.broadcasted_iota(jnp.int32, (1, pair), 1)
        lane_j = lax.broadcasted_iota(jnp.int32, (ll, jh), 1)
        sub_l = lax.broadcasted_iota(jnp.int32, (ll, jh), 0)
        sub_j = lax.broadcasted_iota(jnp.int32, (jh, ll), 0)
        dcb = jnp.zeros((ll, ll), F32)
        db_acc = jnp.zeros((ll, SSD_STATE), F32)
        dc_acc = jnp.zeros((ll, SSD_STATE), F32)
        dcum_c = jnp.zeros((ll, jh), F32)
        dcum_r = jnp.zeros((jh, ll), F32)
        ddt_c = jnp.zeros((ll, jh), F32)
        ddt_r = jnp.zeros((jh, ll), F32)
        for pr in range(jh // 2):
            sl = slice(pr * pair, (pr + 1) * pair)
            xp = x_ref[:, sl]
            xpb = xp.astype(BF16)
            dyp = dy_ref[:, sl]
            hp = hin_ref[:, sl]
            hpb = hp.astype(BF16)
            dhp = dh_scr[:, sl]
            dhpb = dhp.astype(BF16)
            ch = _nn(cb_, hpb)
            gp = _nn(bb, dhpb)
            e_p = jnp.zeros((ll, pair), F32)
            w_p = jnp.zeros((ll, pair), F32)
            cd_p = jnp.zeros((1, pair), F32)
            dxp = dyp * ds_ref[:, sl]
            heads = []
            for q in range(2):
                j = 2 * pr + q
                mj = (lane >= p) if q else (lane < p)
                cc, cr = _col(cumc, j), _row(cumr, j)
                cl = _col(tot, j)
                ej = jnp.exp(cc)
                wdec = jnp.exp(cl - cc)
                wj = wdec * _col(dtcv, j)
                e_p = jnp.where(mj, ej, e_p)
                w_p = jnp.where(mj, wj, w_p)
                cdj = jnp.exp(cl)
                cd_p = jnp.where((lane1 >= p) if q else (lane1 < p), cdj, cd_p)
                heads.append((j, mj, cc, cr, cl, wdec, wj, cdj))
            dye = dyp * e_p
            dyeb = dye.astype(BF16)
            dc_acc = dc_acc + _nt(dyeb, hpb)
            dh_new = dhp * cd_p + _tn(cb_, dyeb)
            dxp = dxp + gp * w_p
            db_acc = db_acc + _nt((xp * w_p).astype(BF16), dhpb)
            t_off = dye * ch
            t_w = gp * xp
            t_cd = jnp.sum(dhp * hp, axis=0, keepdims=True)
            for (j, mj, cc, cr, cl, wdec, wj, cdj) in heads:
                dyj = jnp.where(mj, dyp, 0.0).astype(BF16)
                decay = jnp.exp(jnp.where(tril, cc - cr, -1e30))
                dtrow = _row(dtrv, j)
                mm = cbm * decay * dtrow
                dm = _nt(dyj, xpb)
                dxp = dxp + _tn(mm.astype(BF16), dyj)
                dcb = dcb + dm * decay * dtrow
                ddt_rj = jnp.sum(dm * cbm * decay, axis=0, keepdims=True)
                dseg = dm * mm
                dcum_cj = jnp.sum(dseg, axis=1, keepdims=True) + jnp.sum(jnp.where(mj, t_off, 0.0), axis=1, keepdims=True)
                dcum_rj = -jnp.sum(dseg, axis=0, keepdims=True)
                dwj = jnp.sum(jnp.where(mj, t_w, 0.0), axis=1, keepdims=True)
                ddt_cj = dwj * wdec
                qj = dwj * wj
                dcum_cj = dcum_cj - qj
                m1 = (lane1 >= p) if (j % 2) else (lane1 < p)
                dcl = jnp.sum(jnp.where(m1, t_cd, 0.0), axis=1, keepdims=True) * cdj + jnp.sum(qj, axis=0, keepdims=True)
                dcum_c = dcum_c + jnp.where(lane_j == j, dcum_cj, 0.0) + jnp.where((lane_j == j) & (sub_l == ll - 1), dcl, 0.0)
                dcum_r = dcum_r + jnp.where(sub_j == j, dcum_rj, 0.0)
                ddt_c = ddt_c + jnp.where(lane_j == j, ddt_cj, 0.0)
                ddt_r = ddt_r + jnp.where(sub_j == j, ddt_rj, 0.0)
            dx_ref[:, sl] = dxp
            dd_ref[:, sl] += jnp.sum(dyp * xp, axis=0, keepdims=True)
            dh_scr[:, sl] = dh_new
        dcbb = dcb.astype(BF16)
        dc_ref[...] = dc_acc + _nn(dcbb, bb)
        db_ref[...] = db_acc + _tn(dcbb, cb_)
        dda_c = _nn(triuf, dcum_c, HIGHEST)
        dda_r = _nn(dcum_r, trilf, HIGHEST)
        ddtc_ref[...] = ddt_c + dda_c * a_r
        ddtr_ref[...] = ddt_r + dda_r * a_c
        dac_ref[...] += jnp.sum(dda_c * dtcv, axis=0, keepdims=True)
        dar_ref[...] += jnp.sum(dda_r * dtrv, axis=1, keepdims=True)

    gs = SSD_GROUPS * SSD_STATE
    return _call(body, name, (gn, nc), [xs, bs, cs, dtcs, dtrs, arow, acol, dsk, hin, ys],
                 [ys, bc_out, bc_out, dtcs, dtrs, arow, acol, dsk],
                 [_sds((n, di), F32), _sds((n, gs), F32), _sds((n, gs), F32), _sds(dtc.shape, F32), _sds(dtr.shape, F32),
                  _sds((gn, 1, jh), F32), _sds((gn, jh, 1), F32), _sds((gn, 1, jp), F32)],
                 scratch=[pltpu.VMEM((SSD_STATE, jp), F32)], sem=("parallel", "arbitrary"))(
        xbc, xbc, xbc, dtc, dtr, a_row, a_col, dskip, hins, dy)


def _band_masks():
    r = lax.broadcasted_iota(jnp.int32, (DIL_BLOCK, DIL_BLOCK), 0)
    c = lax.broadcasted_iota(jnp.int32, (DIL_BLOCK, DIL_BLOCK), 1)
    return c >= r, c <= r


def _to_lanes(cols, width):
    tb = cols[0].shape[0]
    lane = lax.broadcasted_iota(jnp.int32, (tb, width), 1)
    out = jnp.zeros((tb, width), F32)
    for h, cv in enumerate(cols):
        out = jnp.where(lane == h, cv, out)
    return out


def _dil_attn_fwd(q2, k2, v2, g, dil, name, *, dw, dh):
    n = q2.shape[0]
    nb = n // DIL_BLOCK
    scale = dh ** -0.5
    tb = DIL_BLOCK
    cur = pl.BlockSpec((tb, dw), lambda r, jb: (jb, r * 3 + g))
    prev = pl.BlockSpec((tb, dw), lambda r, jb: (jnp.maximum(jb - 1, 0), r * 3 + g))
    o_spec = pl.BlockSpec((tb, dw), lambda r, jb: (jb, r))
    l_spec = pl.BlockSpec((tb, LANES), lambda r, jb: (jb, r))

    def body(q_ref, kc_ref, kp_ref, vc_ref, vp_ref, o_ref, l_ref):
        mp, mc = _band_masks()
        mp = mp & (pl.program_id(1) > 0)
        lses = []
        for h in range(DIL_HEADS):
            sl = slice(h * dh, (h + 1) * dh)
            qh = q_ref[:, sl]
            sp = jnp.where(mp, _nt(qh, kp_ref[:, sl]) * scale, -jnp.inf)
            sc = jnp.where(mc, _nt(qh, kc_ref[:, sl]) * scale, -jnp.inf)
            mx = jnp.maximum(jnp.max(sp, axis=1, keepdims=True), jnp.max(sc, axis=1, keepdims=True))
            pp, pc = jnp.exp(sp - mx), jnp.exp(sc - mx)
            den = jnp.sum(pp, axis=1, keepdims=True) + jnp.sum(pc, axis=1, keepdims=True)
            o = _nn(pp.astype(BF16), vp_ref[:, sl]) + _nn(pc.astype(BF16), vc_ref[:, sl])
            o_ref[:, sl] = o / den
            lses.append(mx + jnp.log(den))
        l_ref[...] = _to_lanes(lses, LANES)

    return _call(body, name, (dil, nb), [cur, cur, prev, cur, prev], [o_spec, l_spec],
                 [_sds((n, dil * dw), F32), _sds((n, dil * LANES), F32)], sem=("parallel", "arbitrary"))(
        q2, k2, k2, v2, v2)


def _dil_attn_bwd(q2, k2, v2, do2, lse2, dl2, g, dil, name, *, dw, dh):
    n = q2.shape[0]
    nb = n // DIL_BLOCK
    scale = dh ** -0.5
    tb = DIL_BLOCK

    def nx(jb):
        return jnp.minimum(jb + 1, nb - 1)

    q_c = pl.BlockSpec((tb, dw), lambda r, jb: (jb, r * 3 + g))
    q_n = pl.BlockSpec((tb, dw), lambda r, jb: (nx(jb), r * 3 + g))
    k_p = pl.BlockSpec((tb, dw), lambda r, jb: (jnp.maximum(jb - 1, 0), r * 3 + g))
    o_c = pl.BlockSpec((tb, dw), lambda r, jb: (jb, r))
    o_n = pl.BlockSpec((tb, dw), lambda r, jb: (nx(jb), r))
    l_c = pl.BlockSpec((tb, LANES), lambda r, jb: (jb, r))
    l_n = pl.BlockSpec((tb, LANES), lambda r, jb: (nx(jb), r))

    def body(qc_ref, qn_ref, kc_ref, kp_ref, vc_ref, vp_ref, doc_ref, don_ref, lc_ref, ln_ref, dlc_ref, dln_ref,
             dq_ref, dk_ref, dv_ref):
        jb = pl.program_id(1)
        mp, mc = _band_masks()
        has_prev = jb > 0
        has_next = jb < nb - 1
        lc, ln, dlc, dln = lc_ref[...], ln_ref[...], dlc_ref[...], dln_ref[...]

        def pair(qh, kh, vh, doh, lse_h, dl_h, mask):
            s = _nt(qh, kh) * scale
            pm = jnp.where(mask, jnp.exp(s - lse_h), 0.0)
            dp = _nt(doh, vh)
            ds = (pm * (dp - dl_h) * scale).astype(BF16)
            return pm.astype(BF16), ds

        for h in range(DIL_HEADS):
            sl = slice(h * dh, (h + 1) * dh)
            qc, qn, kc, kp = qc_ref[:, sl], qn_ref[:, sl], kc_ref[:, sl], kp_ref[:, sl]
            vc, vp, doc, don = vc_ref[:, sl], vp_ref[:, sl], doc_ref[:, sl], don_ref[:, sl]
            lch, lnh, dlch, dlnh = _col(lc, h), _col(ln, h), _col(dlc, h), _col(dln, h)
            _, ds_cp = pair(qc, kp, vp, doc, lch, dlch, mp & has_prev)
            p_cc, ds_cc = pair(qc, kc, vc, doc, lch, dlch, mc)
            p_nc, ds_nc = pair(qn, kc, vc, don, lnh, dlnh, mp & has_next)
            dq_ref[:, sl] = _nn(ds_cp, kp) + _nn(ds_cc, kc)
            dk_ref[:, sl] = _tn(ds_cc, qc) + _tn(ds_nc, qn)
            dv_ref[:, sl] = _tn(p_cc, doc) + _tn(p_nc, don)

    out = _sds((n, dil * dw), F32)
    return _call(body, name, (dil, nb), [q_c, q_n, q_c, k_p, q_c, k_p, o_c, o_n, l_c, l_n, l_c, l_n],
                 [o_c, o_c, o_c], [out, out, out], sem=("parallel", "arbitrary"))(
        q2, q2, k2, k2, v2, v2, do2, do2, lse2, lse2, dl2, dl2)


def _head_expand(dw, dh):
    r = lax.broadcasted_iota(jnp.int32, (LANES, dw), 0)
    c = lax.broadcasted_iota(jnp.int32, (LANES, dw), 1)
    return ((c // dh) == r).astype(F32)


def _mix_weights(l0, l1, l2):
    mx = jnp.maximum(jnp.maximum(l0, l1), l2)
    e = [jnp.exp(l0 - mx), jnp.exp(l1 - mx), jnp.exp(l2 - mx)]
    den = e[0] + e[1] + e[2]
    return [v / den for v in e]


def _mix_fwd(os_, ls_, name, *, dh):
    n, dw = os_[0].shape

    def fn(o0, o1, o2, l0, l1, l2):
        ex = _head_expand(dw, dh)
        ws = _mix_weights(l0, l1, l2)
        return (sum(_nn(w, ex, HIGHEST) * o for w, o in zip(ws, (o0, o1, o2))),)

    return _rowwise(fn, name, [(o, dw, 0) for o in os_] + [(l, LANES, 0) for l in ls_], [],
                    [('row', dw, BF16)], _tile(n, 256, 8), n)[0]


def _mix_bwd(os_, ls_, do, name, *, dh):
    n, dw = os_[0].shape

    def fn(o0, o1, o2, l0, l1, l2, dov):
        ex = _head_expand(dw, dh)
        ws = _mix_weights(l0, l1, l2)
        dws = [_nt(dov * o, ex, HIGHEST) for o in (o0, o1, o2)]
        sdw = ws[0] * dws[0] + ws[1] * dws[1] + ws[2] * dws[2]
        return tuple(_nn(w, ex, HIGHEST) * dov for w in ws) + tuple(w * sdw for w in ws)

    return _rowwise(fn, name, [(o, dw, 0) for o in os_] + [(l, LANES, 0) for l in ls_] + [(do, dw, 0)], [],
                    [('row', dw, BF16)] * 3 + [('row', LANES, F32)] * 3, _tile(n, 256, 8), n)


def _mem_attn_fwd(q, kv, name):
    n, mw = q.shape
    ml = kv.shape[0]
    dh = mw // MEM_HEADS
    scale = dh ** -0.5
    tb = _tile(n, 512, 8)

    def body(q_ref, kv_ref, o_ref):
        for h in range(MEM_HEADS):
            sl = slice(h * dh, (h + 1) * dh)
            s = _nt(q_ref[:, sl].astype(BF16), kv_ref[:, sl].astype(BF16)) * scale
            pm = jnp.exp(s - jnp.max(s, axis=1, keepdims=True))
            pm = pm / jnp.sum(pm, axis=1, keepdims=True)
            o_ref[:, sl] = _nn(pm.astype(BF16), kv_ref[:, mw + h * dh:mw + (h + 1) * dh].astype(BF16)).astype(o_ref.dtype)

    return _call(body, name, (n // tb,), [pl.BlockSpec((tb, mw), lambda i: (i, 0)), pl.BlockSpec((ml, 2 * mw), lambda i: (0, 0))],
                 pl.BlockSpec((tb, mw), lambda i: (i, 0)), _sds((n, mw), BF16), sem=("parallel",))(q, kv)


def _mem_attn_bwd(q, kv, do, name):
    n, mw = q.shape
    ml = kv.shape[0]
    dh = mw // MEM_HEADS
    scale = dh ** -0.5
    tb = _tile(n, 512, 8)

    def body(q_ref, kv_ref, do_ref, dq_ref, dkv_ref):
        @pl.when(pl.program_id(0) == 0)
        def _():
            dkv_ref[...] = jnp.zeros_like(dkv_ref)

        for h in range(MEM_HEADS):
            sl = slice(h * dh, (h + 1) * dh)
            vsl = slice(mw + h * dh, mw + (h + 1) * dh)
            qh, kh, vh = q_ref[:, sl].astype(BF16), kv_ref[:, sl].astype(BF16), kv_ref[:, vsl].astype(BF16)
            doh = do_ref[:, sl].astype(BF16)
            s = _nt(qh, kh) * scale
            pm = jnp.exp(s - jnp.max(s, axis=1, keepdims=True))
            pm = pm / jnp.sum(pm, axis=1, keepdims=True)
            dp = _nt(doh, vh)
            ds = (pm * (dp - jnp.sum(pm * dp, axis=1, keepdims=True)) * scale).astype(BF16)
            dq_ref[:, sl] = _nn(ds, kh).astype(dq_ref.dtype)
            dkv_ref[:, sl] += _tn(ds, qh)
            dkv_ref[:, vsl] += _tn(pm.astype(BF16), doh)

    row = pl.BlockSpec((tb, mw), lambda i: (i, 0))
    kvs = pl.BlockSpec((ml, 2 * mw), lambda i: (0, 0))
    return _call(body, name, (n // tb,), [row, kvs, row], [row, kvs], [_sds((n, mw), BF16), _sds((ml, 2 * mw), F32)],
                 sem=("arbitrary",))(q, kv, do)


def _adamw(w, g, m, v, name):
    shape = w.shape
    c = shape[-1]
    r = max(1, math.prod(shape[:-1]))
    w2, g2, m2, v2 = (t.reshape(r, c) for t in (w, g, m, v))
    tb = _tile(r, max(8, (1 << 19) // max(c, 1) // 8 * 8), 8)
    bc1 = 1.0 - ADAM_B1 ** ADAM_STEP
    bc2 = 1.0 - ADAM_B2 ** ADAM_STEP

    def body(w_ref, g_ref, m_ref, v_ref, d_ref, nm_ref, nv_ref):
        gv = g_ref[...]
        nm = ADAM_B1 * m_ref[...] + (1.0 - ADAM_B1) * gv
        nv = ADAM_B2 * v_ref[...] + (1.0 - ADAM_B2) * (gv * gv)
        d_ref[...] = -ADAM_LR * ((nm / bc1) / (jnp.sqrt(nv / bc2) + ADAM_EPS) + ADAM_WD * w_ref[...])
        nm_ref[...] = nm
        nv_ref[...] = nv

    spec = pl.BlockSpec((tb, c), lambda i: (i, 0))
    o = _sds((r, c), F32)
    d, nm, nv = _call(body, name, (r // tb,), [spec] * 4, [spec] * 3, [o, o, o], sem=("parallel",))(w2, g2, m2, v2)
    return d.reshape(shape), nm.reshape(shape), nv.reshape(shape)


def _ew(fn, name, ins, out_dtype):
    shape = ins[0].shape
    c = shape[-1]
    r = math.prod(shape[:-1])
    tb = _tile(r, max(16, (1 << 19) // c // 16 * 16), 16)

    def body(*refs):
        refs[-1][...] = fn(*[t[...] for t in refs[:-1]]).astype(out_dtype)

    spec = pl.BlockSpec((tb, c), lambda i: (i, 0))
    out = _call(body, name, (r // tb,), [spec] * len(ins), spec, _sds((r, c), out_dtype), sem=("parallel",))(
        *[t.reshape(r, c) for t in ins])
    return out.reshape(shape)


def _place():
    x, y, c = lax.axis_index("x"), lax.axis_index("y"), lax.axis_index("c")
    chips = [(1 - x, y), (x, 1 - y), (1 - x, 1 - y)]
    return x, y, c, chips


def _half(ref, c, lead):
    nl = ref.shape[lead]
    idx = [slice(None)] * lead
    if nl % 2 == 0:
        return ref.at[tuple(idx + [pl.ds(c * (nl // 2), nl // 2)])]
    nr = ref.shape[-2]
    idx = [slice(None)] * (len(ref.shape) - 2)
    return ref.at[tuple(idx + [pl.ds(c * (nr // 2), nr // 2)])]


def _slot(ref, s):
    return ref.at[:, s]


def _gather_weights(shards, name):
    na = len(shards)

    def body(*refs):
        srcs, outs = refs[:na], refs[na:2 * na]
        send, recv, loc = refs[2 * na:]
        x, y, c, chips = _place()
        me = 2 * x + y
        sib = (x, y, 1 - c)
        for a in range(na):
            pltpu.make_async_copy(srcs[a], _slot(outs[a], me), loc.at[a]).start()

        def rc(a, k, src, dst, to):
            return pltpu.make_async_remote_copy(src_ref=src, dst_ref=dst, send_sem=send.at[6 * a + k],
                                                recv_sem=recv.at[6 * a + k], device_id=to, device_id_type=MESH)

        firsts, passes = [], []
        for a in range(na):
            for j, (px, py) in enumerate(chips):
                cp = rc(a, j, _half(srcs[a], c, 0), _half(_slot(outs[a], me), c, 0), (px, py, c))
                cp.start()
                firsts.append(cp)
        for a in range(na):
            for j, (px, py) in enumerate(chips):
                blk = _half(_slot(outs[a], 2 * px + py), c, 0)
                rc(a, j, blk, blk, (px, py, c)).wait_recv()
                cp = rc(a, 3 + j, blk, blk, sib)
                cp.start()
                passes.append(cp)
        for a in range(na):
            for j, (px, py) in enumerate(chips):
                blk = _half(_slot(outs[a], 2 * px + py), 1 - c, 0)
                rc(a, 3 + j, blk, blk, sib).wait_recv()
        for cp in firsts + passes:
            cp.wait_send()
        for a in range(na):
            pltpu.make_async_copy(srcs[a], _slot(outs[a], me), loc.at[a]).wait()

    anyspec = pl.BlockSpec(memory_space=pl.ANY)
    return pl.pallas_call(
        body, name=name, in_specs=[anyspec] * na, out_specs=[anyspec] * na,
        out_shape=[_sds((s.shape[0], 4) + s.shape[1:], s.dtype) for s in shards],
        scratch_shapes=[pltpu.SemaphoreType.DMA((6 * na,)), pltpu.SemaphoreType.DMA((6 * na,)),
                        pltpu.SemaphoreType.DMA((na,))],
        compiler_params=pltpu.CompilerParams(has_side_effects=True))(*shards)


def _half_shape(shape, lead):
    shape = list(shape)
    if shape[lead] % 2 == 0:
        shape[lead] //= 2
    else:
        shape[-2] //= 2
    return tuple(shape)


def _swap_halves(grads, name):
    na = len(grads)

    def body(*refs):
        srcs, outs = refs[:na], refs[na:2 * na]
        send, recv = refs[2 * na:]
        x, y, c, _ = _place()
        sib = (x, y, 1 - c)
        cps = [pltpu.make_async_remote_copy(src_ref=_half(srcs[a], 1 - c, 0), dst_ref=outs[a], send_sem=send.at[a],
                                            recv_sem=recv.at[a], device_id=sib, device_id_type=MESH) for a in range(na)]
        for cp in cps:
            cp.start()
        for cp in cps:
            cp.wait()

    anyspec = pl.BlockSpec(memory_space=pl.ANY)
    return pl.pallas_call(
        body, name=name, in_specs=[anyspec] * na, out_specs=[anyspec] * na,
        out_shape=[_sds(_half_shape(g.shape, 0), g.dtype) for g in grads],
        scratch_shapes=[pltpu.SemaphoreType.DMA((na,)), pltpu.SemaphoreType.DMA((na,))],
        compiler_params=pltpu.CompilerParams(has_side_effects=True))(*grads)


def _scatter_partials(parts, name):
    na = len(parts)

    def body(*refs):
        srcs, outs = refs[:na], refs[na:2 * na]
        send, recv, loc = refs[2 * na:]
        x, y, c, chips = _place()
        me = 2 * x + y
        cps = []
        for a in range(na):
            pltpu.make_async_copy(_slot(srcs[a], me), outs[a].at[me], loc.at[a]).start()
            for j, (px, py) in enumerate(chips):
                cp = pltpu.make_async_remote_copy(src_ref=_slot(srcs[a], 2 * px + py), dst_ref=outs[a].at[me],
                                                  send_sem=send.at[3 * a + j], recv_sem=recv.at[3 * a + j],
                                                  device_id=(px, py, c), device_id_type=MESH)
                cp.start()
                cps.append(cp)
        for a in range(na):
            for j, (px, py) in enumerate(chips):
                got = outs[a].at[2 * px + py]
                pltpu.make_async_remote_copy(src_ref=got, dst_ref=got, send_sem=send.at[3 * a + j],
                                             recv_sem=recv.at[3 * a + j], device_id=(px, py, c),
                                             device_id_type=MESH).wait_recv()
        for cp in cps:
            cp.wait_send()
        for a in range(na):
            pltpu.make_async_copy(_slot(srcs[a], me), outs[a].at[me], loc.at[a]).wait()

    anyspec = pl.BlockSpec(memory_space=pl.ANY)
    return pl.pallas_call(
        body, name=name, in_specs=[anyspec] * na, out_specs=[anyspec] * na,
        out_shape=[_sds((4, p.shape[0]) + p.shape[2:], p.dtype) for p in parts],
        scratch_shapes=[pltpu.SemaphoreType.DMA((3 * na,)), pltpu.SemaphoreType.DMA((3 * na,)),
                        pltpu.SemaphoreType.DMA((na,))],
        compiler_params=pltpu.CompilerParams(has_side_effects=True))(*parts)


def _join_halves(halves, full_shapes, name):
    na = len(halves)

    def body(*refs):
        srcs, outs = refs[:na], refs[na:2 * na]
        send, recv, loc = refs[2 * na:]
        x, y, c, _ = _place()
        sib = (x, y, 1 - c)
        cps = []
        for a in range(na):
            pltpu.make_async_copy(srcs[a], _half(outs[a], c, 0), loc.at[a]).start()
            cp = pltpu.make_async_remote_copy(src_ref=srcs[a], dst_ref=_half(outs[a], c, 0), send_sem=send.at[a],
                                              recv_sem=recv.at[a], device_id=sib, device_id_type=MESH)
            cp.start()
            cps.append(cp)
        for a in range(na):
            oth = _half(outs[a], 1 - c, 0)
            pltpu.make_async_remote_copy(src_ref=oth, dst_ref=oth, send_sem=send.at[a], recv_sem=recv.at[a],
                                         device_id=sib, device_id_type=MESH).wait_recv()
        for cp in cps:
            cp.wait_send()
        for a in range(na):
            pltpu.make_async_copy(srcs[a], _half(outs[a], c, 0), loc.at[a]).wait()

    anyspec = pl.BlockSpec(memory_space=pl.ANY)
    return pl.pallas_call(
        body, name=name, in_specs=[anyspec] * na, out_specs=[anyspec] * na,
        out_shape=[_sds(s, F32) for s in full_shapes],
        scratch_shapes=[pltpu.SemaphoreType.DMA((na,)), pltpu.SemaphoreType.DMA((na,)), pltpu.SemaphoreType.DMA((na,))],
        compiler_params=pltpu.CompilerParams(has_side_effects=True))(*halves)


def _allreduce_small(v, name):
    rows, cols = v.shape

    def body(v_ref, o_ref, buf, send, recv):
        x, y, c, _ = _place()
        me = 4 * x + 2 * y + c
        buf[me] = v_ref[...]
        cps = []
        for d in range(1, 8):
            to = (x ^ (d >> 2), y ^ ((d >> 1) & 1), c ^ (d & 1))
            cp = pltpu.make_async_remote_copy(src_ref=v_ref, dst_ref=buf.at[me], send_sem=send.at[d - 1],
                                              recv_sem=recv.at[d - 1], device_id=to, device_id_type=MESH)
            cp.start()
            cps.append(cp)
        for d in range(1, 8):
            frm = 4 * (x ^ (d >> 2)) + 2 * (y ^ ((d >> 1) & 1)) + (c ^ (d & 1))
            got = buf.at[frm]
            pltpu.make_async_remote_copy(src_ref=got, dst_ref=got, send_sem=send.at[d - 1], recv_sem=recv.at[d - 1],
                                         device_id=(x, y, c), device_id_type=MESH).wait_recv()
        for cp in cps:
            cp.wait_send()
        acc = buf[0]
        for d in range(1, 8):
            acc = acc + buf[d]
        o_ref[...] = acc

    vm = pl.BlockSpec(memory_space=pltpu.VMEM)
    return pl.pallas_call(
        body, name=name, in_specs=[vm], out_specs=vm, out_shape=_sds((rows, cols), F32),
        scratch_shapes=[pltpu.VMEM((8, rows, cols), F32), pltpu.SemaphoreType.DMA((7,)), pltpu.SemaphoreType.DMA((7,))],
        compiler_params=pltpu.CompilerParams(has_side_effects=True, vmem_limit_bytes=V7X_VMEM_LIMIT))(v)


def _sum4(r2, name):
    shape = r2.shape[1:]
    c = shape[-1]
    r = math.prod(shape[:-1])
    tb = _tile(r, max(16, (1 << 19) // c // 16 * 16), 16)

    def body(i_ref, o_ref):
        o_ref[...] = ((i_ref[0].astype(F32) + i_ref[1].astype(F32)) + i_ref[2].astype(F32)) + i_ref[3].astype(F32)

    out = _call(body, name, (r // tb,), [pl.BlockSpec((4, tb, c), lambda i: (0, i, 0))],
                pl.BlockSpec((tb, c), lambda i: (i, 0)), _sds((r, c), F32), sem=("parallel",))(r2.reshape(4, r, c))
    return out.reshape(shape)


def _pack(arrs, pw):
    rows = []
    for a in arrs:
        f = a.astype(F32).reshape(-1)
        pad = (-f.shape[0]) % pw
        rows.append(jnp.pad(f, (0, pad)).reshape(-1, pw))
    p = jnp.concatenate(rows, axis=0)
    return jnp.pad(p, ((0, (-p.shape[0]) % 8), (0, 0)))


def _unpack(p, shapes, pw):
    out, r0 = [], 0
    for s in shapes:
        size = math.prod(s) if s else 1
        nr = -(-size // pw)
        out.append(p[r0:r0 + nr].reshape(-1)[:size].reshape(s))
        r0 += nr
    return out


def _step(x, mem, positions, wts, loss_target, mom, vel):
    w = dict(zip(WEIGHTS, wts))
    s_len, d = x.shape[1], x.shape[2]
    ml = mem.shape[1]
    depth = w['norm_mix'].shape[0]
    n_a = w['ssd_w_in'].shape[0]
    di = 4 * w['ssd_w_out'].shape[1]
    nh = w['ssd_dt_bias'].shape[1]
    p = di // nh
    gn = SSD_GROUPS
    jh = nh // gn
    cd = 4 * w['ssd_conv_w'].shape[2]
    w3 = 4 * w['dil_w_q'].shape[2]
    dw = w3 // 3
    hd = dw // DIL_HEADS
    mw = w['mem_w_q'].shape[2]
    ff = 4 * w['ffn_w_out'].shape[1]
    cx, cy, cc = lax.axis_index("x"), lax.axis_index("y"), lax.axis_index("c")
    chip = 2 * cx + cy

    def travel(name):
        t = w[name]
        if t.ndim == 2:
            t = t[None]
        if name in COL_SHARDED:
            t = jnp.swapaxes(t, 1, 2)
        return t.astype(BF16)

    big = list(COL_SHARDED + ROW_SHARDED)
    shards = {n_: travel(n_) for n_ in big}
    full = {}
    for gi, names in enumerate((big[:5], big[5:])):
        got = _gather_weights([shards[n_] for n_ in names], f"gather_weights_{gi}")
        for n_, t in zip(names, got):
            full[n_] = t.reshape(t.shape[0], 4 * t.shape[2], t.shape[3])

    small_sharded = ('ssd_conv_w', 'ssd_conv_b', 'ssd_norm')

    def placed(name):
        t = w[name]
        wd = t.shape[-1]
        z = jnp.zeros(t.shape[:-1] + (4 * wd,), F32)
        z = lax.dynamic_update_slice_in_dim(z, t, chip * wd, axis=t.ndim - 1)
        return z * (cc == 0).astype(F32)

    sm_shapes = [w[n_].shape[:-1] + (4 * w[n_].shape[-1],) for n_ in small_sharded]
    conv_w, conv_b, ssd_nw = _unpack(_allreduce_small(_pack([placed(n_) for n_ in small_sharded], d), "gather_small"),
                                     sm_shapes, d)

    h = x[0]
    tgt = loss_target[0]
    mem_n = _rmsnorm(mem[0], w['mem_src_norm'], "mem_src_norm_fwd")
    cos2, sin2 = _rope_tables(positions[0].reshape(s_len, 1), hd, "rope_tables")
    a_neg = -jnp.exp(w['ssd_a_log'])
    saved = []
    kv_saved = None
    k2s = v2s = None
    for i in range(depth):
        sv = {'h_mix': h}
        u = _rmsnorm(h, w['norm_mix'][i], f"norm_mix_fwd_{i}")
        sv['u'] = u
        if i < n_a:
            wt = full['ssd_w_in'][i]
            wt_dt = wt[di + cd:]
            z = _mm(u, wt, 'nt', f"ssd_in_z_{i}", m=s_len, n=di, k=d)
            xbc_raw = _mm(u, wt, 'nt', f"ssd_in_xbc_{i}", m=s_len, n=cd, k=d, b_noff=di)
            dt_raw = _mm(u, wt_dt, 'nt', f"ssd_in_dt_{i}", m=s_len, n=nh, k=d)
            xbc = _conv_fwd(xbc_raw, conv_w[i], conv_b[i], f"ssd_conv_fwd_{i}")
            dt = _softplus_fwd(dt_raw, w['ssd_dt_bias'][i], f"ssd_dt_fwd_{i}")
            dtc = dt.reshape(s_len, gn, jh).transpose(1, 0, 2)
            dtr = dt.reshape(s_len, gn, jh).transpose(1, 2, 0)
            a_row = a_neg[i].reshape(gn, 1, jh)
            a_col = a_neg[i].reshape(gn, jh, 1)
            dskip = jnp.repeat(w['ssd_d'][i], p).reshape(gn, 1, jh * p)
            y, hins = _ssd_fwd(xbc, dtc, dtr, a_row, a_col, dskip, f"ssd_scan_fwd_{i}", di=di, p=p)
            yn = _gated_norm_fwd(y, z, ssd_nw[i], f"ssd_norm_fwd_{i}")
            h = _mm(yn, full['ssd_w_out'][i], 'nn', f"ssd_out_{i}", m=s_len, n=d, k=di, res=h)
            sv.update(wt=wt, wt_dt=wt_dt, z=z, xbc_raw=xbc_raw, xbc=xbc, dt_raw=dt_raw, dtc=dtc, dtr=dtr, a_row=a_row,
                      a_col=a_col, dskip=dskip, y=y, hins=hins, yn=yn)
        else:
            j = i - n_a
            if j == 0:
                kvn = _rmsnorm(h, w['kv_norm'], "kv_norm_fwd")
                wkv = full['w_kv_shared'][0]
                k_raw = _mm(kvn, wkv, 'nt', "kv_proj_k", m=s_len, n=w3, k=d)
                v_all = _mm(kvn, wkv, 'nt', "kv_proj_v", m=s_len, n=w3, k=d, b_noff=w3, out_dtype=BF16)
                k_all = _rope(k_raw, cos2, sin2, "rope_k", width=w3, sign=1.0, out_dtype=BF16)
                k2s = [k_all.reshape(s_len // dl, dl * w3) for dl in DIL_DILATIONS]
                v2s = [v_all.reshape(s_len // dl, dl * w3) for dl in DIL_DILATIONS]
                kv_saved = {'h': h, 'kvn': kvn, 'wkv': wkv}
            q_raw = _mm(u, full['dil_w_q'][j], 'nt', f"dil_q_{i}", m=s_len, n=w3, k=d)
            q_all = _rope(q_raw, cos2, sin2, f"rope_q_{i}", width=w3, sign=1.0, out_dtype=BF16)
            q2s = [q_all.reshape(s_len // dl, dl * w3) for dl in DIL_DILATIONS]
            os_, ls_, l2s = [], [], []
            for g, dl in enumerate(DIL_DILATIONS):
                o2, l2 = _dil_attn_fwd(q2s[g], k2s[g], v2s[g], g, dl, f"dil_attn_fwd_{i}_{g}", dw=dw, dh=hd)
                os_.append(o2.reshape(s_len, dw))
                ls_.append(l2.reshape(s_len, LANES))
                l2s.append(l2)
            om = _mix_fwd(os_, ls_, f"dil_mix_fwd_{i}", dh=hd)
            h = _mm(om, full['dil_w_o'][j], 'nn', f"dil_out_{i}", m=s_len, n=d, k=dw, res=h)
            sv.update(q2s=q2s, os=os_, ls=ls_, l2s=l2s, om=om)
        sv['h_mem'] = h
        u2 = _rmsnorm(h, w['norm_mem'][i], f"norm_mem_fwd_{i}")
        qm = _mm(u2, full['mem_w_q'][i], 'nn', f"mem_q_{i}", m=s_len, n=mw, k=d, out_dtype=BF16)
        kvm = _mm(mem_n, full['mem_w_kv'][i], 'nn', f"mem_kv_{i}", m=ml, n=2 * mw, k=d)
        omem = _mem_attn_fwd(qm, kvm, f"mem_attn_fwd_{i}")
        h = _mm(omem, full['mem_w_o'][i], 'nt', f"mem_out_{i}", m=s_len, n=d, k=mw, res=h)
        sv.update(u2=u2, qm=qm, kvm=kvm, omem=omem, h_ffn=h)
        u3 = _rmsnorm(h, w['norm_ffn'][i], f"norm_ffn_fwd_{i}")
        gu = _mm(u3, full['ffn_w_in'][i], 'nt', f"ffn_in_{i}", m=s_len, n=2 * ff, k=d)
        act = _swiglu_fwd(gu, f"ffn_act_fwd_{i}")
        h = _mm(act, full['ffn_w_out'][i], 'nn', f"ffn_out_{i}", m=s_len, n=d, k=ff, res=h)
        sv.update(u3=u3, gu=gu, act=act)
        saved.append(sv)

    loss_part, dh, dg_final = _loss_and_grad(h, w['norm_final'], tgt, "loss_and_final_norm")

    gbig = {n_: [None] * full[n_].shape[0] for n_ in big}
    gsm = {n_: [None] * depth for n_ in ('norm_mix', 'norm_mem', 'norm_ffn')}
    for n_ in ('ssd_conv_w', 'ssd_conv_b', 'ssd_norm', 'ssd_dt_bias', 'ssd_a_log', 'ssd_d'):
        gsm[n_] = [None] * n_a
    dmem_n = None
    dk_acc = dv_acc = None
    for i in reversed(range(depth)):
        sv = saved[i]
        dact = _mm(dh, full['ffn_w_out'][i], 'nt', f"ffn_out_bwd_x_{i}", m=s_len, n=ff, k=d, out_dtype=BF16)
        gbig['ffn_w_out'][i] = _mm(sv['act'], dh, 'tn', f"ffn_out_bwd_w_{i}", m=ff, n=d, k=s_len)
        dgu = _swiglu_bwd(sv['gu'], dact, f"ffn_act_bwd_{i}")
        gbig['ffn_w_in'][i] = _mm(dgu, sv['u3'], 'tn', f"ffn_in_bwd_w_{i}", m=2 * ff, n=d, k=s_len)
        du = _mm(dgu, full['ffn_w_in'][i], 'nn', f"ffn_in_bwd_x_{i}", m=s_len, n=d, k=2 * ff)
        dh, gsm['norm_ffn'][i] = _rmsnorm_bwd(sv['h_ffn'], w['norm_ffn'][i], du, dh, f"norm_ffn_bwd_{i}")
        do = _mm(dh, full['mem_w_o'][i], 'nn', f"mem_out_bwd_x_{i}", m=s_len, n=mw, k=d, out_dtype=BF16)
        gbig['mem_w_o'][i] = _mm(dh, sv['omem'], 'tn', f"mem_out_bwd_w_{i}", m=d, n=mw, k=s_len)
        dqm, dkvm = _mem_attn_bwd(sv['qm'], sv['kvm'], do, f"mem_attn_bwd_{i}")
        gbig['mem_w_q'][i] = _mm(sv['u2'], dqm, 'tn', f"mem_q_bwd_w_{i}", m=d, n=mw, k=s_len)
        du = _mm(dqm, full['mem_w_q'][i], 'nt', f"mem_q_bwd_x_{i}", m=s_len, n=d, k=mw)
        gbig['mem_w_kv'][i] = _mm(mem_n, dkvm, 'tn', f"mem_kv_bwd_w_{i}", m=d, n=2 * mw, k=ml)
        dmem_n = _mm(dkvm, full['mem_w_kv'][i], 'nt', f"mem_kv_bwd_x_{i}", m=ml, n=d, k=2 * mw, res=dmem_n)
        dh, gsm['norm_mem'][i] = _rmsnorm_bwd(sv['h_mem'], w['norm_mem'][i], du, dh, f"norm_mem_bwd_{i}")
        if i >= n_a:
            j = i - n_a
            dom = _mm(dh, full['dil_w_o'][j], 'nt', f"dil_out_bwd_x_{i}", m=s_len, n=dw, k=d)
            gbig['dil_w_o'][j] = _mm(sv['om'], dh, 'tn', f"dil_out_bwd_w_{i}", m=dw, n=d, k=s_len)
            mb = _mix_bwd(sv['os'], sv['ls'], dom, f"dil_mix_bwd_{i}", dh=hd)
            dqs, dks, dvs = [], [], []
            for g, dl in enumerate(DIL_DILATIONS):
                do2 = mb[g].reshape(s_len // dl, dl * dw)
                dl2 = mb[3 + g].reshape(s_len // dl, dl * LANES)
                dq2, dk2, dv2 = _dil_attn_bwd(sv['q2s'][g], k2s[g], v2s[g], do2, sv['l2s'][g], dl2, g, dl,
                                              f"dil_attn_bwd_{i}_{g}", dw=dw, dh=hd)
                dqs.append(dq2.reshape(s_len, dw))
                dks.append(dk2.reshape(s_len, dw))
                dvs.append(dv2.reshape(s_len, dw))
            dq_r = jnp.concatenate(dqs, axis=1)
            dk_r = jnp.concatenate(dks, axis=1)
            dv_r = jnp.concatenate(dvs, axis=1)
            dk_acc = dk_r if dk_acc is None else dk_acc + dk_r
            dv_acc = dv_r if dv_acc is None else dv_acc + dv_r
            dq_raw = _rope(dq_r, cos2, sin2, f"rope_q_bwd_{i}", width=w3, sign=-1.0, out_dtype=BF16)
            gbig['dil_w_q'][j] = _mm(dq_raw, sv['u'], 'tn', f"dil_q_bwd_w_{i}", m=w3, n=d, k=s_len)
            du = _mm(dq_raw, full['dil_w_q'][j], 'nn', f"dil_q_bwd_x_{i}", m=s_len, n=d, k=w3)
            dh, gsm['norm_mix'][i] = _rmsnorm_bwd(sv['h_mix'], w['norm_mix'][i], du, dh, f"norm_mix_bwd_{i}")
            if j == 0:
                dk_raw = _rope(dk_acc, cos2, sin2, "rope_k_bwd", width=w3, sign=-1.0, out_dtype=BF16)
                dkv = jnp.concatenate([dk_raw, dv_acc.astype(BF16)], axis=1)
                gbig['w_kv_shared'][0] = _mm(dkv, kv_saved['kvn'], 'tn', "kv_proj_bwd_w", m=2 * w3, n=d, k=s_len)
                du = _mm(dkv, kv_saved['wkv'], 'nn', "kv_proj_bwd_x", m=s_len, n=d, k=2 * w3)
                dh, dg_kv = _rmsnorm_bwd(kv_saved['h'], w['kv_norm'], du, dh, "kv_norm_bwd")
        else:
            dyn = _mm(dh, full['ssd_w_out'][i], 'nt', f"ssd_out_bwd_x_{i}", m=s_len, n=di, k=d)
            gbig['ssd_w_out'][i] = _mm(sv['yn'], dh, 'tn', f"ssd_out_bwd_w_{i}", m=di, n=d, k=s_len)
            dy, dz, gsm['ssd_norm'][i] = _gated_norm_bwd(sv['y'], sv['z'], ssd_nw[i], dyn, f"ssd_norm_bwd_{i}")
            dx, db_, dc_, ddtc, ddtr, dac, dar, ddl = _ssd_bwd(
                sv['xbc'], sv['dtc'], sv['dtr'], sv['a_row'], sv['a_col'], sv['dskip'], sv['hins'], dy,
                f"ssd_scan_bwd_{i}", di=di, p=p)
            dxbc = jnp.concatenate([dx, db_, dc_], axis=1)
            ddt = ddtc.transpose(1, 0, 2).reshape(s_len, nh) + ddtr.transpose(2, 0, 1).reshape(s_len, nh)
            gsm['ssd_a_log'][i] = (dac.reshape(nh) + dar.reshape(nh)) * a_neg[i]
            gsm['ssd_d'][i] = ddl.reshape(nh, p).sum(axis=1)
            ddt_raw, dbias = _softplus_bwd(sv['dt_raw'], w['ssd_dt_bias'][i], ddt, f"ssd_dt_bwd_{i}")
            gsm['ssd_dt_bias'][i] = dbias.reshape(nh)
            dpre, dcw, dcb = _conv_bwd_pre(sv['xbc_raw'], conv_w[i], conv_b[i], dxbc, f"ssd_conv_bwd_pre_{i}")
            gsm['ssd_conv_w'][i], gsm['ssd_conv_b'][i] = dcw, dcb.reshape(cd)
            dxbc_raw = _conv_bwd_in(dpre, conv_w[i], f"ssd_conv_bwd_in_{i}")
            gbig['ssd_w_in'][i] = jnp.concatenate([
                _mm(dz, sv['u'], 'tn', f"ssd_in_bwd_w_z_{i}", m=di, n=d, k=s_len),
                _mm(dxbc_raw, sv['u'], 'tn', f"ssd_in_bwd_w_xbc_{i}", m=cd, n=d, k=s_len),
                _mm(ddt_raw, sv['u'], 'tn', f"ssd_in_bwd_w_dt_{i}", m=nh, n=d, k=s_len)], axis=0)
            du = _mm(dz, sv['wt'], 'nn', f"ssd_in_bwd_x_z_{i}", m=s_len, n=d, k=di)
            du = _mm(dxbc_raw, sv['wt'], 'nn', f"ssd_in_bwd_x_xbc_{i}", m=s_len, n=d, k=cd, b_koff=di, res=du)
            du = _mm(ddt_raw, sv['wt_dt'], 'nn', f"ssd_in_bwd_x_dt_{i}", m=s_len, n=d, k=nh, res=du)
            dh, gsm['norm_mix'][i] = _rmsnorm_bwd(sv['h_mix'], w['norm_mix'][i], du, dh, f"norm_mix_bwd_{i}")
    grad_x = dh[None]
    _, dg_src = _rmsnorm_bwd_noacc(mem[0], w['mem_src_norm'], dmem_n, "mem_src_norm_bwd")

    gfull = [jnp.stack(gbig[n_]) for n_ in big]
    gfull = [t.reshape(t.shape[0], 4, t.shape[1] // 4, t.shape[2]) for t in gfull]
    grads_big = {}
    for gi, idx in enumerate((range(0, 5), range(5, 10))):
        gl = [gfull[k] for k in idx]
        theirs = _swap_halves(gl, f"grad_swap_halves_{gi}")
        parts = []
        for k, t, o in zip(idx, gl, theirs):
            if t.shape[0] % 2 == 0:
                mine = lax.dynamic_slice_in_dim(t, cc * (t.shape[0] // 2), t.shape[0] // 2, axis=0)
            else:
                mine = lax.dynamic_slice_in_dim(t, cc * (t.shape[2] // 2), t.shape[2] // 2, axis=2)
            parts.append(_ew(lambda a, b: a + b, f"grad_pair_sum_{big[k]}", [mine, o], BF16))
        r2 = _scatter_partials(parts, f"grad_scatter_{gi}")
        halves = [_sum4(t, f"grad_chip_sum_{big[k]}") for k, t in zip(idx, r2)]
        outs = _join_halves(halves, [shards[big[k]].shape for k in idx], f"grad_join_halves_{gi}")
        for k, t in zip(idx, outs):
            n_ = big[k]
            if n_ in COL_SHARDED:
                t = jnp.swapaxes(t, 1, 2)
            grads_big[n_] = t.reshape(w[n_].shape)

    sm_names = ['norm_mix', 'norm_mem', 'norm_ffn', 'ssd_conv_w', 'ssd_conv_b', 'ssd_norm', 'ssd_dt_bias', 'ssd_a_log',
                'ssd_d']
    sm_arrs = [jnp.stack([t.reshape(t.shape[-1]) if n_.startswith('norm') else t for t in gsm[n_]]) for n_ in sm_names]
    sm_names += ['norm_final', 'kv_norm', 'mem_src_norm', 'loss']
    sm_arrs += [dg_final.reshape(d), dg_kv.reshape(d), dg_src.reshape(d), loss_part[0, :1]]
    summed = _unpack(_allreduce_small(_pack(sm_arrs, d), "reduce_small"), [t.shape for t in sm_arrs], d)
    gs = dict(zip(sm_names, summed))
    loss = gs.pop('loss').reshape(())
    grads = dict(grads_big)
    for n_, t in gs.items():
        if n_ in small_sharded:
            wd = w[n_].shape[-1]
            t = lax.dynamic_slice_in_dim(t, chip * wd, wd, axis=t.ndim - 1)
        grads[n_] = t.reshape(w[n_].shape)

    deltas, new_m, new_v = [], [], []
    for n_, m_, v_ in zip(WEIGHTS, mom, vel):
        dlt, nm, nv = _adamw(w[n_], grads[n_], m_, v_, f"adamw_{n_}")
        deltas.append(dlt)
        new_m.append(nm)
        new_v.append(nv)
    return (loss, grad_x, *[grads[n_] for n_ in WEIGHTS], *deltas, *new_m, *new_v)


def kernel(x, mem, positions, norm_mix, norm_mem, norm_ffn, norm_final, ssd_w_in, ssd_conv_w, ssd_conv_b, ssd_dt_bias, ssd_a_log, ssd_d, ssd_norm, ssd_w_out, kv_norm, w_kv_shared, dil_w_q, dil_w_o, mem_src_norm, mem_w_q, mem_w_kv, mem_w_o, ffn_w_in, ffn_w_out, loss_target, m_norm_mix, m_norm_mem, m_norm_ffn, m_norm_final, m_ssd_w_in, m_ssd_conv_w, m_ssd_conv_b, m_ssd_dt_bias, m_ssd_a_log, m_ssd_d, m_ssd_norm, m_ssd_w_out, m_kv_norm, m_w_kv_shared, m_dil_w_q, m_dil_w_o, m_mem_src_norm, m_mem_w_q, m_mem_w_kv, m_mem_w_o, m_ffn_w_in, m_ffn_w_out, v_norm_mix, v_norm_mem, v_norm_ffn, v_norm_final, v_ssd_w_in, v_ssd_conv_w, v_ssd_conv_b, v_ssd_dt_bias, v_ssd_a_log, v_ssd_d, v_ssd_norm, v_ssd_w_out, v_kv_norm, v_w_kv_shared, v_dil_w_q, v_dil_w_o, v_mem_src_norm, v_mem_w_q, v_mem_w_kv, v_mem_w_o, v_ffn_w_in, v_ffn_w_out):
    wts = (norm_mix, norm_mem, norm_ffn, norm_final, ssd_w_in, ssd_conv_w, ssd_conv_b, ssd_dt_bias, ssd_a_log, ssd_d, ssd_norm, ssd_w_out, kv_norm, w_kv_shared, dil_w_q, dil_w_o, mem_src_norm, mem_w_q, mem_w_kv, mem_w_o, ffn_w_in, ffn_w_out)
    mom = (m_norm_mix, m_norm_mem, m_norm_ffn, m_norm_final, m_ssd_w_in, m_ssd_conv_w, m_ssd_conv_b, m_ssd_dt_bias, m_ssd_a_log, m_ssd_d, m_ssd_norm, m_ssd_w_out, m_kv_norm, m_w_kv_shared, m_dil_w_q, m_dil_w_o, m_mem_src_norm, m_mem_w_q, m_mem_w_kv, m_mem_w_o, m_ffn_w_in, m_ffn_w_out)
    vel = (v_norm_mix, v_norm_mem, v_norm_ffn, v_norm_final, v_ssd_w_in, v_ssd_conv_w, v_ssd_conv_b, v_ssd_dt_bias, v_ssd_a_log, v_ssd_d, v_ssd_norm, v_ssd_w_out, v_kv_norm, v_w_kv_shared, v_dil_w_q, v_dil_w_o, v_mem_src_norm, v_mem_w_q, v_mem_w_kv, v_mem_w_o, v_ffn_w_in, v_ffn_w_out)
    return _step(x, mem, positions, wts, loss_target, mom, vel)
```

```python
import functools
import math

import jax
import jax.numpy as jnp
from jax import lax
from jax.experimental import pallas as pl
from jax.experimental.pallas import tpu as pltpu

F32 = jnp.float32
BF16 = jnp.bfloat16
MESH = pl.DeviceIdType.MESH
HIGHEST = lax.Precision.HIGHEST

NORM_EPS = 1e-6
SSD_GROUPS = 8
SSD_STATE = 128
SSD_CHUNK = 128
SSD_CONV = 4
DIL_DILATIONS = (1, 4, 16)
DIL_HEADS = 16
DIL_BLOCK = 128
ROPE_THETA = 10000.0
MEM_HEADS = 4
ADAM_LR, ADAM_B1, ADAM_B2, ADAM_EPS, ADAM_WD, ADAM_STEP = 0.001, 0.9, 0.999, 1e-08, 0.01, 10

V7X_VMEM_LIMIT = 48 * 1024 * 1024
LANES = 128
SUBLANES_BF16 = 16

WEIGHTS = ['norm_mix', 'norm_mem', 'norm_ffn', 'norm_final', 'ssd_w_in', 'ssd_conv_w', 'ssd_conv_b',
           'ssd_dt_bias', 'ssd_a_log', 'ssd_d', 'ssd_norm', 'ssd_w_out', 'kv_norm', 'w_kv_shared', 'dil_w_q',
           'dil_w_o', 'mem_src_norm', 'mem_w_q', 'mem_w_kv', 'mem_w_o', 'ffn_w_in', 'ffn_w_out']
COL_SHARDED = ('ssd_w_in', 'w_kv_shared', 'dil_w_q', 'mem_w_o', 'ffn_w_in')
ROW_SHARDED = ('ssd_w_out', 'dil_w_o', 'mem_w_q', 'mem_w_kv', 'ffn_w_out')


def _dot(a, b, ca, cb, prec=None):
    return lax.dot_general(a, b, (((ca,), (cb,)), ((), ())), preferred_element_type=F32, precision=prec)


def _nn(a, b, prec=None):
    return _dot(a, b, 1, 0, prec)


def _nt(a, b, prec=None):
    return _dot(a, b, 1, 1, prec)


def _tn(a, b, prec=None):
    return _dot(a, b, 0, 0, prec)


def _tile(n, pref, unit=LANES):
    if n <= pref:
        return n
    t = (pref // unit) * unit
    while t >= unit:
        if n % t == 0:
            return t
        t -= unit
    return n


def _call(body, name, grid, in_specs, out_specs, out_shape, scratch=(), sem=None, aliases=None):
    return pl.pallas_call(
        body, name=name, grid=grid, in_specs=in_specs, out_specs=out_specs, out_shape=out_shape,
        scratch_shapes=list(scratch), input_output_aliases=aliases or {},
        compiler_params=pltpu.CompilerParams(dimension_semantics=sem, vmem_limit_bytes=V7X_VMEM_LIMIT))


def _sds(shape, dtype):
    return jax.ShapeDtypeStruct(tuple(shape), dtype)


def _silu(x):
    return x * jax.nn.sigmoid(x)


def _dsilu(x):
    s = jax.nn.sigmoid(x)
    return s * (1.0 + x * (1.0 - s))


def _mm(a, b, mode, name, *, m, n, k, out_dtype=F32, res=None, b_noff=0, b_koff=0):
    has_res = res is not None
    m_unit = LANES if mode == 'tn' else SUBLANES_BF16
    n_lim = math.gcd(n, b_noff) if b_noff else n
    k_lim = math.gcd(k, b_koff) if b_koff else k
    wide = out_dtype == F32
    if k <= 2048:
        tk = k
        tm = _tile(m, 2048, m_unit)
        tn = _tile(n_lim, 512 if (wide or has_res) else 1024)
    else:
        tk = _tile(k_lim, 1024)
        tm = _tile(m, 1024, m_unit)
        tn = _tile(n_lim, 2048 if (wide and not has_res) else 1024)
    nk = k // tk
    use_acc = nk > 1 and not wide
    jo, ko = b_noff // tn, b_koff // tk
    if mode == 'nn':
        a_spec = pl.BlockSpec((tm, tk), lambda i, j, kk: (i, kk))
        b_spec = pl.BlockSpec((tk, tn), lambda i, j, kk: (kk + ko, j + jo))
        ca, cb = 1, 0
    elif mode == 'nt':
        a_spec = pl.BlockSpec((tm, tk), lambda i, j, kk: (i, kk))
        b_spec = pl.BlockSpec((tn, tk), lambda i, j, kk: (j + jo, kk + ko))
        ca, cb = 1, 1
    else:
        a_spec = pl.BlockSpec((tk, tm), lambda i, j, kk: (kk, i))
        b_spec = pl.BlockSpec((tk, tn), lambda i, j, kk: (kk + ko, j + jo))
        ca, cb = 0, 0
    o_spec = pl.BlockSpec((tm, tn), lambda i, j, kk: (i, j))

    def body(*refs):
        a_ref, b_ref = refs[0], refs[1]
        r_ref = refs[2] if has_res else None
        o_ref = refs[3] if has_res else refs[2]
        acc = refs[-1] if use_acc else None

        def prod():
            return _dot(a_ref[...].astype(BF16), b_ref[...].astype(BF16), ca, cb)

        def with_res(v):
            return v + r_ref[...].astype(F32) if has_res else v

        if nk == 1:
            o_ref[...] = with_res(prod()).astype(o_ref.dtype)
            return
        kk = pl.program_id(2)
        if use_acc:
            @pl.when(kk == 0)
            def _():
                acc[...] = prod()

            @pl.when((kk > 0) & (kk < nk - 1))
            def _():
                acc[...] += prod()

            @pl.when(kk == nk - 1)
            def _():
                o_ref[...] = with_res(acc[...] + prod()).astype(o_ref.dtype)
        else:
            @pl.when(kk == 0)
            def _():
                o_ref[...] = with_res(prod())

            @pl.when(kk > 0)
            def _():
                o_ref[...] += prod()

    ins = [a, b] + ([res] if has_res else [])
    specs = [a_spec, b_spec] + ([o_spec] if has_res else [])
    return _call(body, name, (m // tm, n // tn, nk), specs, o_spec, _sds((m, n), out_dtype),
                 scratch=[pltpu.VMEM((tm, tn), F32)] if use_acc else [], sem=("parallel", "parallel", "arbitrary"))(*ins)


def _rowwise(fn, name, rows, consts, outs, tb, n_rows):
    n_r, n_c = len(rows), len(consts)
    in_specs = [pl.BlockSpec((tb, w), functools.partial(lambda i, cb: (i, cb), cb=cb)) for _, w, cb in rows]
    in_specs += [pl.BlockSpec(c.shape, functools.partial(lambda i, nd: (0,) * nd, nd=c.ndim)) for c in consts]
    out_specs, out_shape = [], []
    for kind, w, dt in outs:
        if kind == 'row':
            out_specs.append(pl.BlockSpec((tb, w), lambda i: (i, 0)))
            out_shape.append(_sds((n_rows, w), dt))
        else:
            out_specs.append(pl.BlockSpec(w, lambda i: (0, 0)))
            out_shape.append(_sds(w, dt))

    def body(*refs):
        ins = [r[...] for r in refs[:n_r + n_c]]
        orefs = refs[n_r + n_c:]
        vals = fn(*ins)
        i = pl.program_id(0)
        for (kind, _, _), o_ref, v in zip(outs, orefs, vals):
            if kind == 'row':
                o_ref[...] = v.astype(o_ref.dtype)
            else:
                @pl.when(i == 0)
                def _(o_ref=o_ref):
                    o_ref[...] = jnp.zeros_like(o_ref)

                o_ref[...] += v.astype(o_ref.dtype)

    res = _call(body, name, (n_rows // tb,), in_specs, out_specs, out_shape, sem=("arbitrary",))(
        *[r[0] for r in rows], *consts)
    return res


def _rms_fwd_fn(h, g):
    r = lax.rsqrt(jnp.mean(h * h, axis=-1, keepdims=True) + NORM_EPS)
    return (h * r * g,)


def _rms_bwd_vals(h, g, dy):
    r = lax.rsqrt(jnp.mean(h * h, axis=-1, keepdims=True) + NORM_EPS)
    t = dy * g
    dh = r * t - h * (r * r * r) * jnp.mean(h * t, axis=-1, keepdims=True)
    dg = jnp.sum(dy * h * r, axis=0, keepdims=True)
    return dh, dg


def _rmsnorm(h, g, name):
    n, d = h.shape
    return _rowwise(_rms_fwd_fn, name, [(h, d, 0)], [g.reshape(1, d)], [('row', d, BF16)], _tile(n, 512, 8), n)[0]


def _rmsnorm_bwd(h, g, du, dres, name):
    n, d = h.shape

    def fn(hv, duv, drv, gv):
        dh, dg = _rms_bwd_vals(hv, gv, duv.astype(F32))
        return dh + drv, dh + drv, dg

    return _rowwise(fn, name, [(h, d, 0), (du, d, 0), (dres, d, 0)], [g.reshape(1, d)],
                    [('row', d, F32), ('row', d, BF16), ('acc', (1, d), F32)], _tile(n, 256, 8), n)


def _rmsnorm_bwd_noacc(h, g, du, name):
    n, d = h.shape

    def fn(hv, duv, gv):
        return _rms_bwd_vals(hv, gv, duv.astype(F32))

    return _rowwise(fn, name, [(h, d, 0), (du, d, 0)], [g.reshape(1, d)],
                    [('row', d, F32), ('acc', (1, d), F32)], _tile(n, 256, 8), n)


def _swiglu_fwd(gu, name):
    n, f2 = gu.shape
    f = f2 // 2

    def fn(v):
        v = v.astype(F32)
        return (_silu(v[:, :f]) * v[:, f:],)

    return _rowwise(fn, name, [(gu, f2, 0)], [], [('row', f, BF16)], _tile(n, 256, 16), n)[0]


def _swiglu_bwd(gu, dact, name):
    n, f2 = gu.shape
    f = f2 // 2

    def fn(v, da):
        v = v.astype(F32)
        g, up = v[:, :f], v[:, f:]
        da = da.astype(F32)
        return (jnp.concatenate([da * up * _dsilu(g), da * _silu(g)], axis=1),)

    return _rowwise(fn, name, [(gu, f2, 0), (dact, f, 0)], [], [('row', f2, BF16)], _tile(n, 256, 16), n)[0]


def _group_sum(v, ng):
    w = v.shape[1] // ng
    return [jnp.sum(v[:, i * w:(i + 1) * w], axis=1, keepdims=True) for i in range(ng)]


def _gated_norm_fwd(y, z, nw, name):
    n, di = y.shape
    gw = di // SSD_GROUPS

    def fn(yv, zv, nwv):
        a = yv * _silu(zv)
        ms = _group_sum(a * a, SSD_GROUPS)
        out = jnp.concatenate([a[:, i * gw:(i + 1) * gw] * lax.rsqrt(ms[i] / gw + NORM_EPS)
                               for i in range(SSD_GROUPS)], axis=1)
        return (out * nwv,)

    return _rowwise(fn, name, [(y, di, 0), (z, di, 0)], [nw.reshape(1, di)], [('row', di, BF16)], _tile(n, 256, 8), n)[0]


def _gated_norm_bwd(y, z, nw, dout, name):
    n, di = y.shape
    gw = di // SSD_GROUPS

    def fn(yv, zv, dov, nwv):
        sz = _silu(zv)
        a = yv * sz
        t = dov * nwv
        ms = _group_sum(a * a, SSD_GROUPS)
        at = _group_sum(a * t, SSD_GROUPS)
        das, ars = [], []
        for i in range(SSD_GROUPS):
            r = lax.rsqrt(ms[i] / gw + NORM_EPS)
            sl = slice(i * gw, (i + 1) * gw)
            das.append(r * t[:, sl] - a[:, sl] * (r * r * r) * (at[i] / gw))
            ars.append(a[:, sl] * r)
        da = jnp.concatenate(das, axis=1)
        ar = jnp.concatenate(ars, axis=1)
        return da * sz, da * yv * _dsilu(zv), jnp.sum(dov * ar, axis=0, keepdims=True)

    return _rowwise(fn, name, [(y, di, 0), (z, di, 0), (dout, di, 0)], [nw.reshape(1, di)],
                    [('row', di, F32), ('row', di, BF16), ('acc', (1, di), F32)], _tile(n, 128, 8), n)


def _softplus_fwd(raw, bias, name):
    n, h = raw.shape

    def fn(v, b):
        t = v + b
        return (jnp.maximum(t, 0.0) + jnp.log(1.0 + jnp.exp(-jnp.abs(t))),)

    return _rowwise(fn, name, [(raw, h, 0)], [bias.reshape(1, h)], [('row', h, F32)], _tile(n, 1024, 8), n)[0]


def _softplus_bwd(raw, bias, ddt, name):
    n, h = raw.shape

    def fn(v, d, b):
        g = d * jax.nn.sigmoid(v + b)
        return g, jnp.sum(g, axis=0, keepdims=True)

    return _rowwise(fn, name, [(raw, h, 0), (ddt, h, 0)], [bias.reshape(1, h)],
                    [('row', h, BF16), ('acc', (1, h), F32)], _tile(n, 1024, 8), n)


def _loss_and_grad(h, g, tgt, name):
    n, d = h.shape

    def fn(hv, tv, gv):
        y = _rms_fwd_fn(hv, gv)[0]
        err = y - tv
        part = 0.5 * jnp.sum(jnp.sum(err * err, axis=1, keepdims=True), axis=0, keepdims=True) / d
        dh, dg = _rms_bwd_vals(hv, gv, err / d)
        return jnp.broadcast_to(part, (8, LANES)), dh, dh, dg

    return _rowwise(fn, name, [(h, d, 0), (tgt, d, 0)], [g.reshape(1, d)],
                    [('acc', (8, LANES), F32), ('row', d, F32), ('row', d, BF16), ('acc', (1, d), F32)],
                    _tile(n, 256, 8), n)


def _rope_tables(pos_col, dh, name):
    n = pos_col.shape[0]
    half = dh // 2
    inv = ROPE_THETA ** (-jnp.arange(half, dtype=F32) / half)
    inv2 = jnp.concatenate([inv, inv]).reshape(1, dh)
    sign = jnp.concatenate([-jnp.ones((half,), F32), jnp.ones((half,), F32)]).reshape(1, dh)

    def fn(p, iv, sg):
        ang = p.astype(F32) * iv
        return jnp.cos(ang), jnp.sin(ang) * sg

    return _rowwise(fn, name, [(pos_col, 1, 0)], [inv2, sign], [('row', dh, F32), ('row', dh, F32)], _tile(n, 1024, 8), n)


def _rope(t, cos2, sin2, name, *, width, sign, out_dtype):
    n = t.shape[0]
    dh = cos2.shape[1]
    tw = _tile(width, 2048)
    tb = _tile(n, 512, 8)

    def body(t_ref, c_ref, s_ref, o_ref):
        c, s = c_ref[...], s_ref[...]
        for hh in range(tw // dh):
            v = t_ref[:, hh * dh:(hh + 1) * dh].astype(F32)
            o_ref[:, hh * dh:(hh + 1) * dh] = (v * c + sign * pltpu.roll(v, dh // 2, 1) * s).astype(o_ref.dtype)

    return _call(body, name, (n // tb, width // tw),
                 [pl.BlockSpec((tb, tw), lambda i, j: (i, j)), pl.BlockSpec((tb, dh), lambda i, j: (i, 0)),
                  pl.BlockSpec((tb, dh), lambda i, j: (i, 0))],
                 pl.BlockSpec((tb, tw), lambda i, j: (i, j)), _sds((n, width), out_dtype),
                 sem=("parallel", "parallel"))(t, cos2, sin2)


def _conv_taps(ext, w, tb):
    shifted = [ext[8:] if k == SSD_CONV - 1 else pltpu.roll(ext, SSD_CONV - 1 - k, 0)[8:] for k in range(SSD_CONV)]
    pre = shifted[0] * w[0:1]
    for k in range(1, SSD_CONV):
        pre = pre + shifted[k] * w[k:k + 1]
    return pre, shifted


def _conv_specs(n, c, tb, tc):
    hb = tb // 8
    blk = pl.BlockSpec((tb, tc), lambda j, i: (i, j))
    halo = pl.BlockSpec((8, tc), lambda j, i: (jnp.maximum(i * hb - 1, 0), j))
    wsp = pl.BlockSpec((SSD_CONV, tc), lambda j, i: (0, j))
    bsp = pl.BlockSpec((1, tc), lambda j, i: (0, j))
    return blk, halo, wsp, bsp


def _conv_fwd(u, w, b, name):
    n, c = u.shape
    tb, tc = _tile(n, 512, 8), _tile(c, 1536)
    blk, halo, wsp, bsp = _conv_specs(n, c, tb, tc)

    def body(u_ref, h_ref, w_ref, b_ref, o_ref):
        halo_v = jnp.where(pl.program_id(1) > 0, h_ref[...], 0.0)
        pre, _ = _conv_taps(jnp.concatenate([halo_v, u_ref[...]], axis=0), w_ref[...], tb)
        o_ref[...] = _silu(pre + b_ref[...])

    return _call(body, name, (c // tc, n // tb), [blk, halo, wsp, bsp], blk, _sds((n, c), F32),
                 sem=("parallel", "arbitrary"))(u, u, w, b.reshape(1, c))


def _conv_bwd_pre(u, w, b, dout, name):
    n, c = u.shape
    tb, tc = _tile(n, 512, 8), _tile(c, 1536)
    blk, halo, wsp, bsp = _conv_specs(n, c, tb, tc)

    def body(u_ref, h_ref, w_ref, b_ref, d_ref, dp_ref, dw_ref, db_ref):
        i = pl.program_id(1)
        halo_v = jnp.where(i > 0, h_ref[...], 0.0)
        pre, shifted = _conv_taps(jnp.concatenate([halo_v, u_ref[...]], axis=0), w_ref[...], tb)
        dp = d_ref[...] * _dsilu(pre + b_ref[...])
        dp_ref[...] = dp

        @pl.when(i == 0)
        def _():
            dw_ref[...] = jnp.zeros_like(dw_ref)
            db_ref[...] = jnp.zeros_like(db_ref)

        dw_ref[...] += jnp.concatenate([jnp.sum(dp * s, axis=0, keepdims=True) for s in shifted], axis=0)
        db_ref[...] += jnp.sum(dp, axis=0, keepdims=True)

    return _call(body, name, (c // tc, n // tb), [blk, halo, wsp, bsp, blk], [blk, wsp, bsp],
                 [_sds((n, c), F32), _sds((SSD_CONV, c), F32), _sds((1, c), F32)],
                 sem=("parallel", "arbitrary"))(u, u, w, b.reshape(1, c), dout)


def _conv_bwd_in(dpre, w, name):
    n, c = dpre.shape
    tb, tc = _tile(n, 512, 8), _tile(c, 1536)
    hb = tb // 8
    nb = n // tb
    blk = pl.BlockSpec((tb, tc), lambda j, i: (i, j))
    nxt = pl.BlockSpec((8, tc), lambda j, i: (jnp.minimum((i + 1) * hb, n // 8 - 1), j))
    wsp = pl.BlockSpec((SSD_CONV, tc), lambda j, i: (0, j))

    def body(d_ref, n_ref, w_ref, o_ref):
        nxt_v = jnp.where(pl.program_id(1) < nb - 1, n_ref[...], 0.0)
        ext = jnp.concatenate([d_ref[...], nxt_v], axis=0)
        wv = w_ref[...]
        acc = ext[:tb] * wv[SSD_CONV - 1:SSD_CONV]
        for k in range(SSD_CONV - 1):
            s = SSD_CONV - 1 - k
            acc = acc + pltpu.roll(ext, tb + 8 - s, 0)[:tb] * wv[k:k + 1]
        o_ref[...] = acc.astype(o_ref.dtype)

    return _call(body, name, (c // tc, nb), [blk, nxt, wsp], blk, _sds((n, c), BF16),
                 sem=("parallel", "arbitrary"))(dpre, dpre, w)


def _col(v, j):
    lane = lax.broadcasted_iota(jnp.int32, v.shape, 1)
    return jnp.sum(jnp.where(lane == j, v, 0.0), axis=1, keepdims=True)


def _row(v, j):
    sub = lax.broadcasted_iota(jnp.int32, v.shape, 0)
    return jnp.sum(jnp.where(sub == j, v, 0.0), axis=0, keepdims=True)


def _ssd_head_terms(dtc, dtr, a_row, a_col):
    ll = dtc.shape[0]
    r = lax.broadcasted_iota(jnp.int32, (ll, ll), 0)
    c = lax.broadcasted_iota(jnp.int32, (ll, ll), 1)
    tril = (r >= c)
    trilf = tril.astype(F32)
    triuf = (r <= c).astype(F32)
    cumc = _nn(trilf, dtc * a_row, HIGHEST)
    cumr = _nn(dtr * a_col, triuf, HIGHEST)
    return cumc, cumr, tril, trilf, triuf


def _ssd_specs(n, di, gn, jh, p, nc, rev):
    ll = SSD_CHUNK
    jp = jh * p

    def ci(c):
        return (nc - 1 - c) if rev else c

    xs = pl.BlockSpec((ll, jp), lambda g, c: (ci(c), g))
    bs = pl.BlockSpec((ll, SSD_STATE), lambda g, c: (ci(c), di // SSD_STATE + g))
    cs = pl.BlockSpec((ll, SSD_STATE), lambda g, c: (ci(c), (di + gn * SSD_STATE) // SSD_STATE + g))
    dtc = pl.BlockSpec((None, ll, jh), lambda g, c: (g, ci(c), 0))
    dtr = pl.BlockSpec((None, jh, ll), lambda g, c: (g, 0, ci(c)))
    arow = pl.BlockSpec((None, 1, jh), lambda g, c: (g, 0, 0))
    acol = pl.BlockSpec((None, jh, 1), lambda g, c: (g, 0, 0))
    dsk = pl.BlockSpec((None, 1, jp), lambda g, c: (g, 0, 0))
    hin = pl.BlockSpec((None, None, SSD_STATE, jp), lambda g, c: (g, ci(c), 0, 0))
    ys = pl.BlockSpec((ll, jp), lambda g, c: (ci(c), g))
    return xs, bs, cs, dtc, dtr, arow, acol, dsk, hin, ys


def _ssd_fwd(xbc, dtc, dtr, a_row, a_col, dskip, name, *, di, p):
    n = xbc.shape[0]
    gn = SSD_GROUPS
    jh = dtc.shape[2]
    jp = jh * p
    ll = SSD_CHUNK
    nc = n // ll
    xs, bs, cs, dtcs, dtrs, arow, acol, dsk, hin, ys = _ssd_specs(n, di, gn, jh, p, nc, False)
    pair = 2 * p

    def body(x_ref, b_ref, c_ref, dtc_ref, dtr_ref, ar_ref, ac_ref, ds_ref, y_ref, hin_ref, h_scr):
        @pl.when(pl.program_id(1) == 0)
        def _():
            h_scr[...] = jnp.zeros_like(h_scr)

        dtcv, dtrv = dtc_ref[...], dtr_ref[...]
        cumc, cumr, tril, _, _ = _ssd_head_terms(dtcv, dtrv, ar_ref[...], ac_ref[...])
        tot = jnp.sum(dtcv * ar_ref[...], axis=0, keepdims=True)
        bb, cb_ = b_ref[...].astype(BF16), c_ref[...].astype(BF16)
        cbm = _nt(cb_, bb)
        hin_ref[...] = h_scr[...]
        lane = lax.broadcasted_iota(jnp.int32, (ll, pair), 1)
        lane1 = lax.broadcasted_iota(jnp.int32, (1, pair), 1)
        for pr in range(jh // 2):
            sl = slice(pr * pair, (pr + 1) * pair)
            xp = x_ref[:, sl]
            xpb = xp.astype(BF16)
            hp = h_scr[:, sl]
            ydiag = jnp.zeros((ll, pair), F32)
            e_p = jnp.zeros((ll, pair), F32)
            w_p = jnp.zeros((ll, pair), F32)
            cd_p = jnp.zeros((1, pair), F32)
            for q in range(2):
                j = 2 * pr + q
                mj = (lane >= p) if q else (lane < p)
                cc, cr = _col(cumc, j), _row(cumr, j)
                decay = jnp.exp(jnp.where(tril, cc - cr, -1e30))
                mm = cbm * decay * _row(dtrv, j)
                ydiag = ydiag + _nn(mm.astype(BF16), jnp.where(mj, xpb, jnp.zeros_like(xpb)))
                cl = _col(tot, j)
                e_p = jnp.where(mj, jnp.exp(cc), e_p)
                w_p = jnp.where(mj, jnp.exp(cl - cc) * _col(dtcv, j), w_p)
                cd_p = jnp.where((lane1 >= p) if q else (lane1 < p), jnp.exp(cl), cd_p)
            yoff = _nn(cb_, hp.astype(BF16)) * e_p
            y_ref[:, sl] = ydiag + yoff + xp * ds_ref[:, sl]
            h_scr[:, sl] = hp * cd_p + _tn(bb, (xp * w_p).astype(BF16))

    y, hins = _call(body, name, (gn, nc), [xs, bs, cs, dtcs, dtrs, arow, acol, dsk], [ys, hin],
                    [_sds((n, di), F32), _sds((gn, nc, SSD_STATE, jp), F32)],
                    scratch=[pltpu.VMEM((SSD_STATE, jp), F32)], sem=("parallel", "arbitrary"))(
        xbc, xbc, xbc, dtc, dtr, a_row, a_col, dskip)
    return y, hins


def _ssd_bwd(xbc, dtc, dtr, a_row, a_col, dskip, hins, dy, name, *, di, p):
    n = xbc.shape[0]
    gn = SSD_GROUPS
    jh = dtc.shape[2]
    jp = jh * p
    ll = SSD_CHUNK
    nc = n // ll
    xs, bs, cs, dtcs, dtrs, arow, acol, dsk, hin, ys = _ssd_specs(n, di, gn, jh, p, nc, True)
    pair = 2 * p
    bc_out = pl.BlockSpec((ll, SSD_STATE), lambda g, c: (nc - 1 - c, g))

    def body(x_ref, b_ref, c_ref, dtc_ref, dtr_ref, ar_ref, ac_ref, ds_ref, hin_ref, dy_ref,
             dx_ref, db_ref, dc_ref, ddtc_ref, ddtr_ref, dac_ref, dar_ref, dd_ref, dh_scr):
        first = pl.program_id(1) == 0

        @pl.when(first)
        def _():
            dh_scr[...] = jnp.zeros_like(dh_scr)
            dac_ref[...] = jnp.zeros_like(dac_ref)
            dar_ref[...] = jnp.zeros_like(dar_ref)
            dd_ref[...] = jnp.zeros_like(dd_ref)

        dtcv, dtrv = dtc_ref[...], dtr_ref[...]
        a_r, a_c = ar_ref[...], ac_ref[...]
        cumc, cumr, tril, trilf, triuf = _ssd_head_terms(dtcv, dtrv, a_r, a_c)
        tot = jnp.sum(dtcv * a_r, axis=0, keepdims=True)
        bb, cb_ = b_ref[...].astype(BF16), c_ref[...].astype(BF16)
        cbm = _nt(cb_, bb)
        lane = lax.broadcasted_iota(jnp.int32, (ll, pair), 1)
        lane1 = lax.broadcasted_iota(jnp.int32, (1, pair), 1)
        lane_j = lax.broadcasted_iota(jnp.int32, (ll, jh), 1)
        sub_l = lax.broadcasted_iota(jnp.int32, (ll, jh), 0)
        sub_j = lax.broadcasted_iota(jnp.int32, (jh, ll), 0)
        dcb = jnp.zeros((ll, ll), F32)
        db_acc = jnp.zeros((ll, SSD_STATE), F32)
        dc_acc = jnp.zeros((ll, SSD_STATE), F32)
        dcum_c = jnp.zeros((ll, jh), F32)
        dcum_r = jnp.zeros((jh, ll), F32)
        ddt_c = jnp.zeros((ll, jh), F32)
        ddt_r = jnp.zeros((jh, ll), F32)
        for pr in range(jh // 2):
            sl = slice(pr * pair, (pr + 1) * pair)
            xp = x_ref[:, sl]
            xpb = xp.astype(BF16)
            dyp = dy_ref[:, sl]
            hp = hin_ref[:, sl]
            hpb = hp.astype(BF16)
            dhp = dh_scr[:, sl]
            dhpb = dhp.astype(BF16)
            ch = _nn(cb_, hpb)
            gp = _nn(bb, dhpb)
            e_p = jnp.zeros((ll, pair), F32)
            w_p = jnp.zeros((ll, pair), F32)
            cd_p = jnp.zeros((1, pair), F32)
            dxp = dyp * ds_ref[:, sl]
            heads = []
            for q in range(2):
                j = 2 * pr + q
                mj = (lane >= p) if q else (lane < p)
                cc, cr = _col(cumc, j), _row(cumr, j)
                cl = _col(tot, j)
                ej = jnp.exp(cc)
                wdec = jnp.exp(cl - cc)
                wj = wdec * _col(dtcv, j)
                e_p = jnp.where(mj, ej, e_p)
                w_p = jnp.where(mj, wj, w_p)
                cdj = jnp.exp(cl)
                cd_p = jnp.where((lane1 >= p) if q else (lane1 < p), cdj, cd_p)
                heads.append((j, mj, cc, cr, cl, wdec, wj, cdj))
            dye = dyp * e_p
            dyeb = dye.astype(BF16)
            dc_acc = dc_acc + _nt(dyeb, hpb)
            dh_new = dhp * cd_p + _tn(cb_, dyeb)
            dxp = dxp + gp * w_p
            db_acc = db_acc + _nt((xp * w_p).astype(BF16), dhpb)
            t_off = dye * ch
            t_w = gp * xp
            t_cd = jnp.sum(dhp * hp, axis=0, keepdims=True)
            for (j, mj, cc, cr, cl, wdec, wj, cdj) in heads:
                dyj = jnp.where(mj, dyp, 0.0).astype(BF16)
                decay = jnp.exp(jnp.where(tril, cc - cr, -1e30))
                dtrow = _row(dtrv, j)
                mm = cbm * decay * dtrow
                dm = _nt(dyj, xpb)
                dxp = dxp + _tn(mm.astype(BF16), dyj)
                dcb = dcb + dm * decay * dtrow
                ddt_rj = jnp.sum(dm * cbm * decay, axis=0, keepdims=True)
                dseg = dm * mm
                dcum_cj = jnp.sum(dseg, axis=1, keepdims=True) + jnp.sum(jnp.where(mj, t_off, 0.0), axis=1, keepdims=True)
                dcum_rj = -jnp.sum(dseg, axis=0, keepdims=True)
                dwj = jnp.sum(jnp.where(mj, t_w, 0.0), axis=1, keepdims=True)
                ddt_cj = dwj * wdec
                qj = dwj * wj
                dcum_cj = dcum_cj - qj
                m1 = (lane1 >= p) if (j % 2) else (lane1 < p)
                dcl = jnp.sum(jnp.where(m1, t_cd, 0.0), axis=1, keepdims=True) * cdj + jnp.sum(qj, axis=0, keepdims=True)
                dcum_c = dcum_c + jnp.where(lane_j == j, dcum_cj, 0.0) + jnp.where((lane_j == j) & (sub_l == ll - 1), dcl, 0.0)
                dcum_r = dcum_r + jnp.where(sub_j == j, dcum_rj, 0.0)
                ddt_c = ddt_c + jnp.where(lane_j == j, ddt_cj, 0.0)
                ddt_r = ddt_r + jnp.where(sub_j == j, ddt_rj, 0.0)
            dx_ref[:, sl] = dxp
            dd_ref[:, sl] += jnp.sum(dyp * xp, axis=0, keepdims=True)
            dh_scr[:, sl] = dh_new
        dcbb = dcb.astype(BF16)
        dc_ref[...] = dc_acc + _nn(dcbb, bb)
        db_ref[...] = db_acc + _tn(dcbb, cb_)
        dda_c = _nn(triuf, dcum_c, HIGHEST)
        dda_r = _nn(dcum_r, trilf, HIGHEST)
        ddtc_ref[...] = ddt_c + dda_c * a_r
        ddtr_ref[...] = ddt_r + dda_r * a_c
        dac_ref[...] += jnp.sum(dda_c * dtcv, axis=0, keepdims=True)
        dar_ref[...] += jnp.sum(dda_r * dtrv, axis=1, keepdims=True)

    gs = SSD_GROUPS * SSD_STATE
    return _call(body, name, (gn, nc), [xs, bs, cs, dtcs, dtrs, arow, acol, dsk, hin, ys],
                 [ys, bc_out, bc_out, dtcs, dtrs, arow, acol, dsk],
                 [_sds((n, di), F32), _sds((n, gs), F32), _sds((n, gs), F32), _sds(dtc.shape, F32), _sds(dtr.shape, F32),
                  _sds((gn, 1, jh), F32), _sds((gn, jh, 1), F32), _sds((gn, 1, jp), F32)],
                 scratch=[pltpu.VMEM((SSD_STATE, jp), F32)], sem=("parallel", "arbitrary"))(
        xbc, xbc, xbc, dtc, dtr, a_row, a_col, dskip, hins, dy)


def _band_masks():
    r = lax.broadcasted_iota(jnp.int32, (DIL_BLOCK, DIL_BLOCK), 0)
    c = lax.broadcasted_iota(jnp.int32, (DIL_BLOCK, DIL_BLOCK), 1)
    return c >= r, c <= r


def _to_lanes(cols, width):
    tb = cols[0].shape[0]
    lane = lax.broadcasted_iota(jnp.int32, (tb, width), 1)
    out = jnp.zeros((tb, width), F32)
    for h, cv in enumerate(cols):
        out = jnp.where(lane == h, cv, out)
    return out


def _dil_attn_fwd(q2, k2, v2, g, dil, name, *, dw, dh):
    n = q2.shape[0]
    nb = n // DIL_BLOCK
    scale = dh ** -0.5
    tb = DIL_BLOCK
    cur = pl.BlockSpec((tb, dw), lambda r, jb: (jb, r * 3 + g))
    prev = pl.BlockSpec((tb, dw), lambda r, jb: (jnp.maximum(jb - 1, 0), r * 3 + g))
    o_spec = pl.BlockSpec((tb, dw), lambda r, jb: (jb, r))
    l_spec = pl.BlockSpec((tb, LANES), lambda r, jb: (jb, r))

    def body(q_ref, kc_ref, kp_ref, vc_ref, vp_ref, o_ref, l_ref):
        mp, mc = _band_masks()
        mp = mp & (pl.program_id(1) > 0)
        lses = []
        for h in range(DIL_HEADS):
            sl = slice(h * dh, (h + 1) * dh)
            qh = q_ref[:, sl]
            sp = jnp.where(mp, _nt(qh, kp_ref[:, sl]) * scale, -jnp.inf)
            sc = jnp.where(mc, _nt(qh, kc_ref[:, sl]) * scale, -jnp.inf)
            mx = jnp.maximum(jnp.max(sp, axis=1, keepdims=True), jnp.max(sc, axis=1, keepdims=True))
            pp, pc = jnp.exp(sp - mx), jnp.exp(sc - mx)
            den = jnp.sum(pp, axis=1, keepdims=True) + jnp.sum(pc, axis=1, keepdims=True)
            o = _nn(pp.astype(BF16), vp_ref[:, sl]) + _nn(pc.astype(BF16), vc_ref[:, sl])
            o_ref[:, sl] = o / den
            lses.append(mx + jnp.log(den))
        l_ref[...] = _to_lanes(lses, LANES)

    return _call(body, name, (dil, nb), [cur, cur, prev, cur, prev], [o_spec, l_spec],
                 [_sds((n, dil * dw), F32), _sds((n, dil * LANES), F32)], sem=("parallel", "arbitrary"))(
        q2, k2, k2, v2, v2)


def _dil_attn_bwd(q2, k2, v2, do2, lse2, dl2, g, dil, name, *, dw, dh):
    n = q2.shape[0]
    nb = n // DIL_BLOCK
    scale = dh ** -0.5
    tb = DIL_BLOCK

    def nx(jb):
        return jnp.minimum(jb + 1, nb - 1)

    q_c = pl.BlockSpec((tb, dw), lambda r, jb: (jb, r * 3 + g))
    q_n = pl.BlockSpec((tb, dw), lambda r, jb: (nx(jb), r * 3 + g))
    k_p = pl.BlockSpec((tb, dw), lambda r, jb: (jnp.maximum(jb - 1, 0), r * 3 + g))
    o_c = pl.BlockSpec((tb, dw), lambda r, jb: (jb, r))
    o_n = pl.BlockSpec((tb, dw), lambda r, jb: (nx(jb), r))
    l_c = pl.BlockSpec((tb, LANES), lambda r, jb: (jb, r))
    l_n = pl.BlockSpec((tb, LANES), lambda r, jb: (nx(jb), r))

    def body(qc_ref, qn_ref, kc_ref, kp_ref, vc_ref, vp_ref, doc_ref, don_ref, lc_ref, ln_ref, dlc_ref, dln_ref,
             dq_ref, dk_ref, dv_ref):
        jb = pl.program_id(1)
        mp, mc = _band_masks()
        has_prev = jb > 0
        has_next = jb < nb - 1
        lc, ln, dlc, dln = lc_ref[...], ln_ref[...], dlc_ref[...], dln_ref[...]

        def pair(qh, kh, vh, doh, lse_h, dl_h, mask):
            s = _nt(qh, kh) * scale
            pm = jnp.where(mask, jnp.exp(s - lse_h), 0.0)
            dp = _nt(doh, vh)
            ds = (pm * (dp - dl_h) * scale).astype(BF16)
            return pm.astype(BF16), ds

        for h in range(DIL_HEADS):
            sl = slice(h * dh, (h + 1) * dh)
            qc, qn, kc, kp = qc_ref[:, sl], qn_ref[:, sl], kc_ref[:, sl], kp_ref[:, sl]
            vc, vp, doc, don = vc_ref[:, sl], vp_ref[:, sl], doc_ref[:, sl], don_ref[:, sl]
            lch, lnh, dlch, dlnh = _col(lc, h), _col(ln, h), _col(dlc, h), _col(dln, h)
            _, ds_cp = pair(qc, kp, vp, doc, lch, dlch, mp & has_prev)
            p_cc, ds_cc = pair(qc, kc, vc, doc, lch, dlch, mc)
            p_nc, ds_nc = pair(qn, kc, vc, don, lnh, dlnh, mp & has_next)
            dq_ref[:, sl] = _nn(ds_cp, kp) + _nn(ds_cc, kc)
            dk_ref[:, sl] = _tn(ds_cc, qc) + _tn(ds_nc, qn)
            dv_ref[:, sl] = _tn(p_cc, doc) + _tn(p_nc, don)

    out = _sds((n, dil * dw), F32)
    return _call(body, name, (dil, nb), [q_c, q_n, q_c, k_p, q_c, k_p, o_c, o_n, l_c, l_n, l_c, l_n],
                 [o_c, o_c, o_c], [out, out, out], sem=("parallel", "arbitrary"))(
        q2, q2, k2, k2, v2, v2, do2, do2, lse2, lse2, dl2, dl2)


def _head_expand(dw, dh):
    r = lax.broadcasted_iota(jnp.int32, (LANES, dw), 0)
    c = lax.broadcasted_iota(jnp.int32, (LANES, dw), 1)
    return ((c // dh) == r).astype(F32)


def _mix_weights(l0, l1, l2):
    mx = jnp.maximum(jnp.maximum(l0, l1), l2)
    e = [jnp.exp(l0 - mx), jnp.exp(l1 - mx), jnp.exp(l2 - mx)]
    den = e[0] + e[1] + e[2]
    return [v / den for v in e]


def _mix_fwd(os_, ls_, name, *, dh):
    n, dw = os_[0].shape

    def fn(o0, o1, o2, l0, l1, l2):
        ex = _head_expand(dw, dh)
        ws = _mix_weights(l0, l1, l2)
        return (sum(_nn(w, ex, HIGHEST) * o for w, o in zip(ws, (o0, o1, o2))),)

    return _rowwise(fn, name, [(o, dw, 0) for o in os_] + [(l, LANES, 0) for l in ls_], [],
                    [('row', dw, BF16)], _tile(n, 256, 8), n)[0]


def _mix_bwd(os_, ls_, do, name, *, dh):
    n, dw = os_[0].shape

    def fn(o0, o1, o2, l0, l1, l2, dov):
        ex = _head_expand(dw, dh)
        ws = _mix_weights(l0, l1, l2)
        dws = [_nt(dov * o, ex, HIGHEST) for o in (o0, o1, o2)]
        sdw = ws[0] * dws[0] + ws[1] * dws[1] + ws[2] * dws[2]
        return tuple(_nn(w, ex, HIGHEST) * dov for w in ws) + tuple(w * sdw for w in ws)

    return _rowwise(fn, name, [(o, dw, 0) for o in os_] + [(l, LANES, 0) for l in ls_] + [(do, dw, 0)], [],
                    [('row', dw, BF16)] * 3 + [('row', LANES, F32)] * 3, _tile(n, 256, 8), n)


def _mem_attn_fwd(q, kv, name):
    n, mw = q.shape
    ml = kv.shape[0]
    dh = mw // MEM_HEADS
    scale = dh ** -0.5
    tb = _tile(n, 512, 8)

    def body(q_ref, kv_ref, o_ref):
        for h in range(MEM_HEADS):
            sl = slice(h * dh, (h + 1) * dh)
            s = _nt(q_ref[:, sl].astype(BF16), kv_ref[:, sl].astype(BF16)) * scale
            pm = jnp.exp(s - jnp.max(s, axis=1, keepdims=True))
            pm = pm / jnp.sum(pm, axis=1, keepdims=True)
            o_ref[:, sl] = _nn(pm.astype(BF16), kv_ref[:, mw + h * dh:mw + (h + 1) * dh].astype(BF16)).astype(o_ref.dtype)

    return _call(body, name, (n // tb,), [pl.BlockSpec((tb, mw), lambda i: (i, 0)), pl.BlockSpec((ml, 2 * mw), lambda i: (0, 0))],
                 pl.BlockSpec((tb, mw), lambda i: (i, 0)), _sds((n, mw), BF16), sem=("parallel",))(q, kv)


def _mem_attn_bwd(q, kv, do, name):
    n, mw = q.shape
    ml = kv.shape[0]
    dh = mw // MEM_HEADS
    scale = dh ** -0.5
    tb = _tile(n, 512, 8)

    def body(q_ref, kv_ref, do_ref, dq_ref, dkv_ref):
        @pl.when(pl.program_id(0) == 0)
        def _():
            dkv_ref[...] = jnp.zeros_like(dkv_ref)

        for h in range(MEM_HEADS):
            sl = slice(h * dh, (h + 1) * dh)
            vsl = slice(mw + h * dh, mw + (h + 1) * dh)
            qh, kh, vh = q_ref[:, sl].astype(BF16), kv_ref[:, sl].astype(BF16), kv_ref[:, vsl].astype(BF16)
            doh = do_ref[:, sl].astype(BF16)
            s = _nt(qh, kh) * scale
            pm = jnp.exp(s - jnp.max(s, axis=1, keepdims=True))
            pm = pm / jnp.sum(pm, axis=1, keepdims=True)
            dp = _nt(doh, vh)
            ds = (pm * (dp - jnp.sum(pm * dp, axis=1, keepdims=True)) * scale).astype(BF16)
            dq_ref[:, sl] = _nn(ds, kh).astype(dq_ref.dtype)
            dkv_ref[:, sl] += _tn(ds, qh)
            dkv_ref[:, vsl] += _tn(pm.astype(BF16), doh)

    row = pl.BlockSpec((tb, mw), lambda i: (i, 0))
    kvs = pl.BlockSpec((ml, 2 * mw), lambda i: (0, 0))
    return _call(body, name, (n // tb,), [row, kvs, row], [row, kvs], [_sds((n, mw), BF16), _sds((ml, 2 * mw), F32)],
                 sem=("arbitrary",))(q, kv, do)


def _adamw(w, g, m, v, name):
    shape = w.shape
    c = shape[-1]
    r = max(1, math.prod(shape[:-1]))
    w2, g2, m2, v2 = (t.reshape(r, c) for t in (w, g, m, v))
    tb = _tile(r, max(8, (1 << 19) // max(c, 1) // 8 * 8), 8)
    bc1 = 1.0 - ADAM_B1 ** ADAM_STEP
    bc2 = 1.0 - ADAM_B2 ** ADAM_STEP

    def body(w_ref, g_ref, m_ref, v_ref, d_ref, nm_ref, nv_ref):
        gv = g_ref[...]
        nm = ADAM_B1 * m_ref[...] + (1.0 - ADAM_B1) * gv
        nv = ADAM_B2 * v_ref[...] + (1.0 - ADAM_B2) * (gv * gv)
        d_ref[...] = -ADAM_LR * ((nm / bc1) / (jnp.sqrt(nv / bc2) + ADAM_EPS) + ADAM_WD * w_ref[...])
        nm_ref[...] = nm
        nv_ref[...] = nv

    spec = pl.BlockSpec((tb, c), lambda i: (i, 0))
    o = _sds((r, c), F32)
    d, nm, nv = _call(body, name, (r // tb,), [spec] * 4, [spec] * 3, [o, o, o], sem=("parallel",))(w2, g2, m2, v2)
    return d.reshape(shape), nm.reshape(shape), nv.reshape(shape)


def _place():
    x, y, c = lax.axis_index("x"), lax.axis_index("y"), lax.axis_index("c")
    chips = [(1 - x, y), (x, 1 - y), (1 - x, 1 - y)]
    return x, y, c, chips


def _cols(ref, c):
    hw = ref.shape[-1] // 2
    return ref.at[(slice(None),) * (len(ref.shape) - 1) + (pl.ds(c * hw, hw),)]


def _slot(ref, s):
    return ref.at[:, s]


def _scalar_call(body, name, scalars, grid, in_specs, out_specs, out_shape, ins):
    gs = pltpu.PrefetchScalarGridSpec(num_scalar_prefetch=1, grid=grid, in_specs=in_specs, out_specs=out_specs)
    return pl.pallas_call(
        body, name=name, grid_spec=gs, out_shape=out_shape,
        compiler_params=pltpu.CompilerParams(dimension_semantics=("parallel",) * len(grid),
                                             vmem_limit_bytes=V7X_VMEM_LIMIT))(scalars, *ins)


def _place_shard(t, chip, name):
    nl, r, n = t.shape
    tb = _tile(r, 512, SUBLANES_BF16)

    def body(s_ref, t_ref, o_ref):
        o_ref[...] = t_ref[...].astype(BF16)

    return _scalar_call(body, name, chip.reshape(1), (nl, r // tb),
                        [pl.BlockSpec((None, tb, n), lambda l, i, s: (l, i, 0))],
                        pl.BlockSpec((None, None, tb, n), lambda l, i, s: (l, s[0], i, 0)),
                        _sds((nl, 4, r, n), BF16), [t])


def _aliased_comm_call(body, name, bufs, n_sems):
    na = len(bufs)
    anyspec = pl.BlockSpec(memory_space=pl.ANY)
    return pl.pallas_call(
        body, name=name, in_specs=[anyspec] * na, out_specs=[anyspec] * na,
        out_shape=[_sds(b.shape, b.dtype) for b in bufs], input_output_aliases={a: a for a in range(na)},
        scratch_shapes=[pltpu.SemaphoreType.DMA((n_sems,)), pltpu.SemaphoreType.DMA((n_sems,))],
        compiler_params=pltpu.CompilerParams(has_side_effects=True))(*bufs)


def _gather_weights(bufs, name):
    na = len(bufs)

    def body(*refs):
        outs = refs[na:2 * na]
        send, recv = refs[2 * na:]
        x, y, c, chips = _place()
        me = 2 * x + y
        sib = (x, y, 1 - c)

        def rc(a, k, blk, to):
            return pltpu.make_async_remote_copy(src_ref=blk, dst_ref=blk, send_sem=send.at[6 * a + k],
                                                recv_sem=recv.at[6 * a + k], device_id=to, device_id_type=MESH)

        firsts, passes = [], []
        for a in range(na):
            for j, (px, py) in enumerate(chips):
                cp = rc(a, j, _cols(_slot(outs[a], me), c), (px, py, c))
                cp.start()
                firsts.append(cp)
        for a in range(na):
            for j, (px, py) in enumerate(chips):
                blk = _cols(_slot(outs[a], 2 * px + py), c)
                rc(a, j, blk, (px, py, c)).wait_recv()
                cp = rc(a, 3 + j, blk, sib)
                cp.start()
                passes.append(cp)
        for a in range(na):
            for j, (px, py) in enumerate(chips):
                rc(a, 3 + j, _cols(_slot(outs[a], 2 * px + py), 1 - c), sib).wait_recv()
        for cp in firsts + passes:
            cp.wait_send()

    return _aliased_comm_call(body, name, bufs, 6 * na)


def _swap_halves(grads, name):
    na = len(grads)

    def body(*refs):
        srcs, outs = refs[:na], refs[na:2 * na]
        send, recv = refs[2 * na:]
        x, y, c, _ = _place()
        sib = (x, y, 1 - c)
        cps = [pltpu.make_async_remote_copy(src_ref=_cols(srcs[a], 1 - c), dst_ref=outs[a], send_sem=send.at[a],
                                            recv_sem=recv.at[a], device_id=sib, device_id_type=MESH) for a in range(na)]
        for cp in cps:
            cp.start()
        for cp in cps:
            cp.wait()

    anyspec = pl.BlockSpec(memory_space=pl.ANY)
    return pl.pallas_call(
        body, name=name, in_specs=[anyspec] * na, out_specs=[anyspec] * na,
        out_shape=[_sds(g.shape[:-1] + (g.shape[-1] // 2,), g.dtype) for g in grads],
        scratch_shapes=[pltpu.SemaphoreType.DMA((na,)), pltpu.SemaphoreType.DMA((na,))],
        compiler_params=pltpu.CompilerParams(has_side_effects=True))(*grads)


def _pair_sum(g, theirs, chip, core, name):
    nl, _, r, n = g.shape
    hw = n // 2
    tb = _tile(r, 256, SUBLANES_BF16)

    def body(s_ref, g_ref, t_ref, p_ref, own_ref):
        p_ref[...] = (g_ref[...] + t_ref[...]).astype(BF16)
        own_ref[...] = p_ref[s_ref[0]]

    blk = pl.BlockSpec((None, 4, tb, hw), lambda l, i, s: (l, 0, i, 0))
    out = _sds((nl, 4, r, hw), BF16)
    return _scalar_call(body, name, jnp.stack([chip, core]), (nl, r // tb),
                        [pl.BlockSpec((None, 4, tb, hw), lambda l, i, s: (l, 0, i, s[1])), blk],
                        [blk, pl.BlockSpec((None, None, tb, hw), lambda l, i, s: (l, s[0], i, 0))],
                        [out, out], [g, theirs])


def _scatter_partials(parts, lands, name):
    na = len(parts)

    def body(*refs):
        srcs, outs = refs[:na], refs[2 * na:3 * na]
        send, recv = refs[3 * na:]
        x, y, c, chips = _place()
        me = 2 * x + y
        cps = []
        for a in range(na):
            for j, (px, py) in enumerate(chips):
                cp = pltpu.make_async_remote_copy(src_ref=_slot(srcs[a], 2 * px + py), dst_ref=_slot(outs[a], me),
                                                  send_sem=send.at[3 * a + j], recv_sem=recv.at[3 * a + j],
                                                  device_id=(px, py, c), device_id_type=MESH)
                cp.start()
                cps.append(cp)
        for a in range(na):
            for j, (px, py) in enumerate(chips):
                got = _slot(outs[a], 2 * px + py)
                pltpu.make_async_remote_copy(src_ref=got, dst_ref=got, send_sem=send.at[3 * a + j],
                                             recv_sem=recv.at[3 * a + j], device_id=(px, py, c),
                                             device_id_type=MESH).wait_recv()
        for cp in cps:
            cp.wait_send()

    anyspec = pl.BlockSpec(memory_space=pl.ANY)
    return pl.pallas_call(
        body, name=name, in_specs=[anyspec] * (2 * na), out_specs=[anyspec] * na,
        out_shape=[_sds(b.shape, b.dtype) for b in lands],
        input_output_aliases={na + a: a for a in range(na)},
        scratch_shapes=[pltpu.SemaphoreType.DMA((3 * na,)), pltpu.SemaphoreType.DMA((3 * na,))],
        compiler_params=pltpu.CompilerParams(has_side_effects=True))(*parts, *lands)


def _chip_sum(land, core, n, name):
    nl, _, r, hw = land.shape
    tb = _tile(r, 512, SUBLANES_BF16)

    def body(s_ref, i_ref, o_ref):
        o_ref[...] = ((i_ref[0].astype(F32) + i_ref[1].astype(F32)) + i_ref[2].astype(F32)) + i_ref[3].astype(F32)

    return _scalar_call(body, name, core.reshape(1), (nl, r // tb),
                        [pl.BlockSpec((None, 4, tb, hw), lambda l, i, s: (l, 0, i, 0))],
                        pl.BlockSpec((None, tb, hw), lambda l, i, s: (l, i, s[0])),
                        _sds((nl, r, n), F32), [land])


def _join_halves(bufs, name):
    na = len(bufs)

    def body(*refs):
        outs = refs[na:2 * na]
        send, recv = refs[2 * na:]
        x, y, c, _ = _place()
        sib = (x, y, 1 - c)
        cps = []
        for a in range(na):
            mine = _cols(outs[a], c)
            cp = pltpu.make_async_remote_copy(src_ref=mine, dst_ref=mine, send_sem=send.at[a], recv_sem=recv.at[a],
                                              device_id=sib, device_id_type=MESH)
            cp.start()
            cps.append(cp)
        for a in range(na):
            oth = _cols(outs[a], 1 - c)
            pltpu.make_async_remote_copy(src_ref=oth, dst_ref=oth, send_sem=send.at[a], recv_sem=recv.at[a],
                                         device_id=sib, device_id_type=MESH).wait_recv()
        for cp in cps:
            cp.wait_send()

    return _aliased_comm_call(body, name, bufs, na)


def _allreduce_small(v, name):
    rows, cols = v.shape

    def body(v_ref, o_ref, buf, send, recv):
        x, y, c, _ = _place()
        me = 4 * x + 2 * y + c
        buf[me] = v_ref[...]
        cps = []
        for d in range(1, 8):
            to = (x ^ (d >> 2), y ^ ((d >> 1) & 1), c ^ (d & 1))
            cp = pltpu.make_async_remote_copy(src_ref=v_ref, dst_ref=buf.at[me], send_sem=send.at[d - 1],
                                              recv_sem=recv.at[d - 1], device_id=to, device_id_type=MESH)
            cp.start()
            cps.append(cp)
        for d in range(1, 8):
            frm = 4 * (x ^ (d >> 2)) + 2 * (y ^ ((d >> 1) & 1)) + (c ^ (d & 1))
            got = buf.at[frm]
            pltpu.make_async_remote_copy(src_ref=got, dst_ref=got, send_sem=send.at[d - 1], recv_sem=recv.at[d - 1],
                                         device_id=(x, y, c), device_id_type=MESH).wait_recv()
        for cp in cps:
            cp.wait_send()
        acc = buf[0]
        for d in range(1, 8):
            acc = acc + buf[d]
        o_ref[...] = acc

    vm = pl.BlockSpec(memory_space=pltpu.VMEM)
    return pl.pallas_call(
        body, name=name, in_specs=[vm], out_specs=vm, out_shape=_sds((rows, cols), F32),
        scratch_shapes=[pltpu.VMEM((8, rows, cols), F32), pltpu.SemaphoreType.DMA((7,)), pltpu.SemaphoreType.DMA((7,))],
        compiler_params=pltpu.CompilerParams(has_side_effects=True, vmem_limit_bytes=V7X_VMEM_LIMIT))(v)


def _pack(arrs, pw):
    rows = []
    for a in arrs:
        f = a.astype(F32).reshape(-1)
        pad = (-f.shape[0]) % pw
        rows.append(jnp.pad(f, (0, pad)).reshape(-1, pw))
    p = jnp.concatenate(rows, axis=0)
    return jnp.pad(p, ((0, (-p.shape[0]) % 8), (0, 0)))


def _unpack(p, shapes, pw):
    out, r0 = [], 0
    for s in shapes:
        size = math.prod(s) if s else 1
        nr = -(-size // pw)
        out.append(p[r0:r0 + nr].reshape(-1)[:size].reshape(s))
        r0 += nr
    return out


def _step(x, mem, positions, wts, loss_target, mom, vel):
    w = dict(zip(WEIGHTS, wts))
    s_len, d = x.shape[1], x.shape[2]
    ml = mem.shape[1]
    depth = w['norm_mix'].shape[0]
    n_a = w['ssd_w_in'].shape[0]
    di = 4 * w['ssd_w_out'].shape[1]
    nh = w['ssd_dt_bias'].shape[1]
    p = di // nh
    gn = SSD_GROUPS
    jh = nh // gn
    cd = 4 * w['ssd_conv_w'].shape[2]
    w3 = 4 * w['dil_w_q'].shape[2]
    dw = w3 // 3
    hd = dw // DIL_HEADS
    mw = w['mem_w_q'].shape[2]
    ff = 4 * w['ffn_w_out'].shape[1]
    cx, cy, cc = lax.axis_index("x"), lax.axis_index("y"), lax.axis_index("c")
    chip = 2 * cx + cy

    chip_i = chip.astype(jnp.int32)
    core_i = cc.astype(jnp.int32)

    big = list(COL_SHARDED + ROW_SHARDED)
    stacked = ('mem_w_q', 'mem_w_kv', 'mem_w_o')

    def layer_items(i):
        if i < n_a:
            its = [('ssd_w_in', i), ('ssd_w_out', i)]
        else:
            its = ([('w_kv_shared', 0)] if i == n_a else []) + [('dil_w_q', i - n_a), ('dil_w_o', i - n_a)]
        its += [('ffn_w_in', i), ('ffn_w_out', i)]
        return its + ([(n_, None) for n_ in stacked] if i == 0 else [])

    def shard_of(name, l):
        t = w[name] if w[name].ndim == 3 else w[name][None]
        if l is not None:
            t = t[l:l + 1]
        return jnp.swapaxes(t, 1, 2).astype(BF16) if name in COL_SHARDED else t

    full = {n_: [None] * (w[n_].shape[0] if w[n_].ndim == 3 else 1) for n_ in big}
    for i in range(depth):
        its = layer_items(i)
        bufs = [_place_shard(shard_of(n_, l), chip_i, f"place_{n_}_{l}") for n_, l in its]
        for (n_, l), t in zip(its, _gather_weights(bufs, f"gather_weights_{i}")):
            fl = t.reshape(t.shape[0], 4 * t.shape[2], t.shape[3])
            if l is None:
                full[n_] = [fl[k] for k in range(fl.shape[0])]
            else:
                full[n_][l] = fl[0]

    small_sharded = ('ssd_conv_w', 'ssd_conv_b', 'ssd_norm')

    def placed(name):
        t = w[name]
        wd = t.shape[-1]
        z = jnp.zeros(t.shape[:-1] + (4 * wd,), F32)
        z = lax.dynamic_update_slice_in_dim(z, t, chip * wd, axis=t.ndim - 1)
        return z * (cc == 0).astype(F32)

    sm_shapes = [w[n_].shape[:-1] + (4 * w[n_].shape[-1],) for n_ in small_sharded]
    conv_w, conv_b, ssd_nw = _unpack(_allreduce_small(_pack([placed(n_) for n_ in small_sharded], d), "gather_small"),
                                     sm_shapes, d)

    h = x[0]
    tgt = loss_target[0]
    mem_n = _rmsnorm(mem[0], w['mem_src_norm'], "mem_src_norm_fwd")
    cos2, sin2 = _rope_tables(positions[0].reshape(s_len, 1), hd, "rope_tables")
    a_neg = -jnp.exp(w['ssd_a_log'])
    saved = []
    kv_saved = None
    k2s = v2s = None
    for i in range(depth):
        sv = {'h_mix': h}
        u = _rmsnorm(h, w['norm_mix'][i], f"norm_mix_fwd_{i}")
        sv['u'] = u
        if i < n_a:
            wt = full['ssd_w_in'][i]
            wt_dt = wt[di + cd:]
            z = _mm(u, wt, 'nt', f"ssd_in_z_{i}", m=s_len, n=di, k=d)
            xbc_raw = _mm(u, wt, 'nt', f"ssd_in_xbc_{i}", m=s_len, n=cd, k=d, b_noff=di)
            dt_raw = _mm(u, wt_dt, 'nt', f"ssd_in_dt_{i}", m=s_len, n=nh, k=d)
            xbc = _conv_fwd(xbc_raw, conv_w[i], conv_b[i], f"ssd_conv_fwd_{i}")
            dt = _softplus_fwd(dt_raw, w['ssd_dt_bias'][i], f"ssd_dt_fwd_{i}")
            dtc = dt.reshape(s_len, gn, jh).transpose(1, 0, 2)
            dtr = dt.reshape(s_len, gn, jh).transpose(1, 2, 0)
            a_row = a_neg[i].reshape(gn, 1, jh)
            a_col = a_neg[i].reshape(gn, jh, 1)
            dskip = jnp.repeat(w['ssd_d'][i], p).reshape(gn, 1, jh * p)
            y, hins = _ssd_fwd(xbc, dtc, dtr, a_row, a_col, dskip, f"ssd_scan_fwd_{i}", di=di, p=p)
            yn = _gated_norm_fwd(y, z, ssd_nw[i], f"ssd_norm_fwd_{i}")
            h = _mm(yn, full['ssd_w_out'][i], 'nn', f"ssd_out_{i}", m=s_len, n=d, k=di, res=h)
            sv.update(wt=wt, wt_dt=wt_dt, z=z, xbc_raw=xbc_raw, xbc=xbc, dt_raw=dt_raw, dtc=dtc, dtr=dtr, a_row=a_row,
                      a_col=a_col, dskip=dskip, y=y, hins=hins, yn=yn)
        else:
            j = i - n_a
            if j == 0:
                kvn = _rmsnorm(h, w['kv_norm'], "kv_norm_fwd")
                wkv = full['w_kv_shared'][0]
                k_raw = _mm(kvn, wkv, 'nt', "kv_proj_k", m=s_len, n=w3, k=d)
                v_all = _mm(kvn, wkv, 'nt', "kv_proj_v", m=s_len, n=w3, k=d, b_noff=w3, out_dtype=BF16)
                k_all = _rope(k_raw, cos2, sin2, "rope_k", width=w3, sign=1.0, out_dtype=BF16)
                k2s = [k_all.reshape(s_len // dl, dl * w3) for dl in DIL_DILATIONS]
                v2s = [v_all.reshape(s_len // dl, dl * w3) for dl in DIL_DILATIONS]
                kv_saved = {'h': h, 'kvn': kvn, 'wkv': wkv}
            q_raw = _mm(u, full['dil_w_q'][j], 'nt', f"dil_q_{i}", m=s_len, n=w3, k=d)
            q_all = _rope(q_raw, cos2, sin2, f"rope_q_{i}", width=w3, sign=1.0, out_dtype=BF16)
            q2s = [q_all.reshape(s_len // dl, dl * w3) for dl in DIL_DILATIONS]
            os_, ls_, l2s = [], [], []
            for g, dl in enumerate(DIL_DILATIONS):
                o2, l2 = _dil_attn_fwd(q2s[g], k2s[g], v2s[g], g, dl, f"dil_attn_fwd_{i}_{g}", dw=dw, dh=hd)
                os_.append(o2.reshape(s_len, dw))
                ls_.append(l2.reshape(s_len, LANES))
                l2s.append(l2)
            om = _mix_fwd(os_, ls_, f"dil_mix_fwd_{i}", dh=hd)
            h = _mm(om, full['dil_w_o'][j], 'nn', f"dil_out_{i}", m=s_len, n=d, k=dw, res=h)
            sv.update(q2s=q2s, os=os_, ls=ls_, l2s=l2s, om=om)
        sv['h_mem'] = h
        u2 = _rmsnorm(h, w['norm_mem'][i], f"norm_mem_fwd_{i}")
        qm = _mm(u2, full['mem_w_q'][i], 'nn', f"mem_q_{i}", m=s_len, n=mw, k=d, out_dtype=BF16)
        kvm = _mm(mem_n, full['mem_w_kv'][i], 'nn', f"mem_kv_{i}", m=ml, n=2 * mw, k=d)
        omem = _mem_attn_fwd(qm, kvm, f"mem_attn_fwd_{i}")
        h = _mm(omem, full['mem_w_o'][i], 'nt', f"mem_out_{i}", m=s_len, n=d, k=mw, res=h)
        sv.update(u2=u2, qm=qm, kvm=kvm, omem=omem, h_ffn=h)
        u3 = _rmsnorm(h, w['norm_ffn'][i], f"norm_ffn_fwd_{i}")
        gu = _mm(u3, full['ffn_w_in'][i], 'nt', f"ffn_in_{i}", m=s_len, n=2 * ff, k=d, out_dtype=BF16)
        act = _swiglu_fwd(gu, f"ffn_act_fwd_{i}")
        h = _mm(act, full['ffn_w_out'][i], 'nn', f"ffn_out_{i}", m=s_len, n=d, k=ff, res=h)
        sv.update(u3=u3, gu=gu, act=act)
        saved.append(sv)

    loss_part, dh, dhb, dg_final = _loss_and_grad(h, w['norm_final'], tgt, "loss_and_final_norm")

    gbig = {n_: [None] * len(full[n_]) for n_ in big}
    gsm = {n_: [None] * depth for n_ in ('norm_mix', 'norm_mem', 'norm_ffn')}
    for n_ in ('ssd_conv_w', 'ssd_conv_b', 'ssd_norm', 'ssd_dt_bias', 'ssd_a_log', 'ssd_d'):
        gsm[n_] = [None] * n_a
    dmem_n = None
    dk_acc = dv_acc = None
    for i in reversed(range(depth)):
        sv = saved[i]
        dact = _mm(dhb, full['ffn_w_out'][i], 'nt', f"ffn_out_bwd_x_{i}", m=s_len, n=ff, k=d, out_dtype=BF16)
        gbig['ffn_w_out'][i] = _mm(sv['act'], dhb, 'tn', f"ffn_out_bwd_w_{i}", m=ff, n=d, k=s_len)
        dgu = _swiglu_bwd(sv['gu'], dact, f"ffn_act_bwd_{i}")
        gbig['ffn_w_in'][i] = _mm(dgu, sv['u3'], 'tn', f"ffn_in_bwd_w_{i}", m=2 * ff, n=d, k=s_len)
        du = _mm(dgu, full['ffn_w_in'][i], 'nn', f"ffn_in_bwd_x_{i}", m=s_len, n=d, k=2 * ff)
        dh, dhb, gsm['norm_ffn'][i] = _rmsnorm_bwd(sv['h_ffn'], w['norm_ffn'][i], du, dh, f"norm_ffn_bwd_{i}")
        do = _mm(dhb, full['mem_w_o'][i], 'nn', f"mem_out_bwd_x_{i}", m=s_len, n=mw, k=d, out_dtype=BF16)
        gbig['mem_w_o'][i] = _mm(dhb, sv['omem'], 'tn', f"mem_out_bwd_w_{i}", m=d, n=mw, k=s_len)
        dqm, dkvm = _mem_attn_bwd(sv['qm'], sv['kvm'], do, f"mem_attn_bwd_{i}")
        gbig['mem_w_q'][i] = _mm(sv['u2'], dqm, 'tn', f"mem_q_bwd_w_{i}", m=d, n=mw, k=s_len)
        du = _mm(dqm, full['mem_w_q'][i], 'nt', f"mem_q_bwd_x_{i}", m=s_len, n=d, k=mw)
        gbig['mem_w_kv'][i] = _mm(mem_n, dkvm, 'tn', f"mem_kv_bwd_w_{i}", m=d, n=2 * mw, k=ml)
        dmem_n = _mm(dkvm, full['mem_w_kv'][i], 'nt', f"mem_kv_bwd_x_{i}", m=ml, n=d, k=2 * mw, res=dmem_n)
        dh, dhb, gsm['norm_mem'][i] = _rmsnorm_bwd(sv['h_mem'], w['norm_mem'][i], du, dh, f"norm_mem_bwd_{i}")
        if i >= n_a:
            j = i - n_a
            dom = _mm(dhb, full['dil_w_o'][j], 'nt', f"dil_out_bwd_x_{i}", m=s_len, n=dw, k=d)
            gbig['dil_w_o'][j] = _mm(sv['om'], dhb, 'tn', f"dil_out_bwd_w_{i}", m=dw, n=d, k=s_len)
            mb = _mix_bwd(sv['os'], sv['ls'], dom, f"dil_mix_bwd_{i}", dh=hd)
            dqs, dks, dvs = [], [], []
            for g, dl in enumerate(DIL_DILATIONS):
                do2 = mb[g].reshape(s_len // dl, dl * dw)
                dl2 = mb[3 + g].reshape(s_len // dl, dl * LANES)
                dq2, dk2, dv2 = _dil_attn_bwd(sv['q2s'][g], k2s[g], v2s[g], do2, sv['l2s'][g], dl2, g, dl,
                                              f"dil_attn_bwd_{i}_{g}", dw=dw, dh=hd)
                dqs.append(dq2.reshape(s_len, dw))
                dks.append(dk2.reshape(s_len, dw))
                dvs.append(dv2.reshape(s_len, dw))
            dq_r = jnp.concatenate(dqs, axis=1)
            dk_r = jnp.concatenate(dks, axis=1)
            dv_r = jnp.concatenate(dvs, axis=1)
            dk_acc = dk_r if dk_acc is None else dk_acc + dk_r
            dv_acc = dv_r if dv_acc is None else dv_acc + dv_r
            dq_raw = _rope(dq_r, cos2, sin2, f"rope_q_bwd_{i}", width=w3, sign=-1.0, out_dtype=BF16)
            gbig['dil_w_q'][j] = _mm(dq_raw, sv['u'], 'tn', f"dil_q_bwd_w_{i}", m=w3, n=d, k=s_len)
            du = _mm(dq_raw, full['dil_w_q'][j], 'nn', f"dil_q_bwd_x_{i}", m=s_len, n=d, k=w3)
            dh, dhb, gsm['norm_mix'][i] = _rmsnorm_bwd(sv['h_mix'], w['norm_mix'][i], du, dh, f"norm_mix_bwd_{i}")
            if j == 0:
                dk_raw = _rope(dk_acc, cos2, sin2, "rope_k_bwd", width=w3, sign=-1.0, out_dtype=BF16)
                dkv = jnp.concatenate([dk_raw, dv_acc.astype(BF16)], axis=1)
                gbig['w_kv_shared'][0] = _mm(dkv, kv_saved['kvn'], 'tn', "kv_proj_bwd_w", m=2 * w3, n=d, k=s_len)
                du = _mm(dkv, kv_saved['wkv'], 'nn', "kv_proj_bwd_x", m=s_len, n=d, k=2 * w3)
                dh, dhb, dg_kv = _rmsnorm_bwd(kv_saved['h'], w['kv_norm'], du, dh, "kv_norm_bwd")
        else:
            dyn = _mm(dhb, full['ssd_w_out'][i], 'nt', f"ssd_out_bwd_x_{i}", m=s_len, n=di, k=d)
            gbig['ssd_w_out'][i] = _mm(sv['yn'], dhb, 'tn', f"ssd_out_bwd_w_{i}", m=di, n=d, k=s_len)
            dy, dz, gsm['ssd_norm'][i] = _gated_norm_bwd(sv['y'], sv['z'], ssd_nw[i], dyn, f"ssd_norm_bwd_{i}")
            dx, db_, dc_, ddtc, ddtr, dac, dar, ddl = _ssd_bwd(
                sv['xbc'], sv['dtc'], sv['dtr'], sv['a_row'], sv['a_col'], sv['dskip'], sv['hins'], dy,
                f"ssd_scan_bwd_{i}", di=di, p=p)
            dxbc = jnp.concatenate([dx, db_, dc_], axis=1)
            ddt = ddtc.transpose(1, 0, 2).reshape(s_len, nh) + ddtr.transpose(2, 0, 1).reshape(s_len, nh)
            gsm['ssd_a_log'][i] = (dac.reshape(nh) + dar.reshape(nh)) * a_neg[i]
            gsm['ssd_d'][i] = ddl.reshape(nh, p).sum(axis=1)
            ddt_raw, dbias = _softplus_bwd(sv['dt_raw'], w['ssd_dt_bias'][i], ddt, f"ssd_dt_bwd_{i}")
            gsm['ssd_dt_bias'][i] = dbias.reshape(nh)
            dpre, dcw, dcb = _conv_bwd_pre(sv['xbc_raw'], conv_w[i], conv_b[i], dxbc, f"ssd_conv_bwd_pre_{i}")
            gsm['ssd_conv_w'][i], gsm['ssd_conv_b'][i] = dcw, dcb.reshape(cd)
            dxbc_raw = _conv_bwd_in(dpre, conv_w[i], f"ssd_conv_bwd_in_{i}")
            gbig['ssd_w_in'][i] = jnp.concatenate([
                _mm(dz, sv['u'], 'tn', f"ssd_in_bwd_w_z_{i}", m=di, n=d, k=s_len),
                _mm(dxbc_raw, sv['u'], 'tn', f"ssd_in_bwd_w_xbc_{i}", m=cd, n=d, k=s_len),
                _mm(ddt_raw, sv['u'], 'tn', f"ssd_in_bwd_w_dt_{i}", m=nh, n=d, k=s_len)], axis=0)
            du = _mm(dz, sv['wt'], 'nn', f"ssd_in_bwd_x_z_{i}", m=s_len, n=d, k=di)
            du = _mm(dxbc_raw, sv['wt'], 'nn', f"ssd_in_bwd_x_xbc_{i}", m=s_len, n=d, k=cd, b_koff=di, res=du)
            du = _mm(ddt_raw, sv['wt_dt'], 'nn', f"ssd_in_bwd_x_dt_{i}", m=s_len, n=d, k=nh, res=du)
            dh, dhb, gsm['norm_mix'][i] = _rmsnorm_bwd(sv['h_mix'], w['norm_mix'][i], du, dh, f"norm_mix_bwd_{i}")
    grad_x = dh[None]
    _, dg_src = _rmsnorm_bwd_noacc(mem[0], w['mem_src_norm'], dmem_n, "mem_src_norm_bwd")

    gshard = {}
    for i in reversed(range(depth)):
        its = layer_items(i)
        gl = []
        for n_, l in its:
            t = jnp.stack(gbig[n_]) if l is None else gbig[n_][l][None]
            gl.append(t.reshape(t.shape[0], 4, t.shape[1] // 4, t.shape[2]))
        theirs = _swap_halves(gl, f"grad_swap_halves_{i}")
        parts, lands = [], []
        for (n_, l), t, o in zip(its, gl, theirs):
            pt, ld = _pair_sum(t, o, chip_i, core_i, f"grad_pair_sum_{n_}_{l}")
            parts.append(pt)
            lands.append(ld)
        lands = _scatter_partials(parts, lands, f"grad_scatter_{i}")
        fins = [_chip_sum(ld, core_i, t.shape[3], f"grad_chip_sum_{n_}_{l}") for (n_, l), ld, t in zip(its, lands, gl)]
        for it, t in zip(its, _join_halves(fins, f"grad_join_halves_{i}")):
            gshard[it] = t
    grads_big = {}
    for n_ in big:
        t = gshard[(n_, None)] if n_ in stacked else jnp.concatenate([gshard[(n_, l)] for l in range(len(full[n_]))])
        if n_ in COL_SHARDED:
            t = jnp.swapaxes(t, 1, 2)
        grads_big[n_] = t.reshape(w[n_].shape)

    sm_names = ['norm_mix', 'norm_mem', 'norm_ffn', 'ssd_conv_w', 'ssd_conv_b', 'ssd_norm', 'ssd_dt_bias', 'ssd_a_log',
                'ssd_d']
    sm_arrs = [jnp.stack([t.reshape(t.shape[-1]) if n_.startswith('norm') else t for t in gsm[n_]]) for n_ in sm_names]
    sm_names += ['norm_final', 'kv_norm', 'mem_src_norm', 'loss']
    sm_arrs += [dg_final.reshape(d), dg_kv.reshape(d), dg_src.reshape(d), loss_part[0, :1]]
    summed = _unpack(_allreduce_small(_pack(sm_arrs, d), "reduce_small"), [t.shape for t in sm_arrs], d)
    gs = dict(zip(sm_names, summed))
    loss = gs.pop('loss').reshape(())
    grads = dict(grads_big)
    for n_, t in gs.items():
        if n_ in small_sharded:
            wd = w[n_].shape[-1]
            t = lax.dynamic_slice_in_dim(t, chip * wd, wd, axis=t.ndim - 1)
        grads[n_] = t.reshape(w[n_].shape)

    deltas, new_m, new_v = [], [], []
    for n_, m_, v_ in zip(WEIGHTS, mom, vel):
        dlt, nm, nv = _adamw(w[n_], grads[n_], m_, v_, f"adamw_{n_}")
        deltas.append(dlt)
        new_m.append(nm)
        new_v.append(nv)
    return (loss, grad_x, *[grads[n_] for n_ in WEIGHTS], *deltas, *new_m, *new_v)


def kernel(x, mem, positions, norm_mix, norm_mem, norm_ffn, norm_final, ssd_w_in, ssd_conv_w, ssd_conv_b, ssd_dt_bias, ssd_a_log, ssd_d, ssd_norm, ssd_w_out, kv_norm, w_kv_shared, dil_w_q, dil_w_o, mem_src_norm, mem_w_q, mem_w_kv, mem_w_o, ffn_w_in, ffn_w_out, loss_target, m_norm_mix, m_norm_mem, m_norm_ffn, m_norm_final, m_ssd_w_in, m_ssd_conv_w, m_ssd_conv_b, m_ssd_dt_bias, m_ssd_a_log, m_ssd_d, m_ssd_norm, m_ssd_w_out, m_kv_norm, m_w_kv_shared, m_dil_w_q, m_dil_w_o, m_mem_src_norm, m_mem_w_q, m_mem_w_kv, m_mem_w_o, m_ffn_w_in, m_ffn_w_out, v_norm_mix, v_norm_mem, v_norm_ffn, v_norm_final, v_ssd_w_in, v_ssd_conv_w, v_ssd_conv_b, v_ssd_dt_bias, v_ssd_a_log, v_ssd_d, v_ssd_norm, v_ssd_w_out, v_kv_norm, v_w_kv_shared, v_dil_w_q, v_dil_w_o, v_mem_src_norm, v_mem_w_q, v_mem_w_kv, v_mem_w_o, v_ffn_w_in, v_ffn_w_out):
    wts = (norm_mix, norm_mem, norm_ffn, norm_final, ssd_w_in, ssd_conv_w, ssd_conv_b, ssd_dt_bias, ssd_a_log, ssd_d, ssd_norm, ssd_w_out, kv_norm, w_kv_shared, dil_w_q, dil_w_o, mem_src_norm, mem_w_q, mem_w_kv, mem_w_o, ffn_w_in, ffn_w_out)
    mom = (m_norm_mix, m_norm_mem, m_norm_ffn, m_norm_final, m_ssd_w_in, m_ssd_conv_w, m_ssd_conv_b, m_ssd_dt_bias, m_ssd_a_log, m_ssd_d, m_ssd_norm, m_ssd_w_out, m_kv_norm, m_w_kv_shared, m_dil_w_q, m_dil_w_o, m_mem_src_norm, m_mem_w_q, m_mem_w_kv, m_mem_w_o, m_ffn_w_in, m_ffn_w_out)
    vel = (v_norm_mix, v_norm_mem, v_norm_ffn, v_norm_final, v_ssd_w_in, v_ssd_conv_w, v_ssd_conv_b, v_ssd_dt_bias, v_ssd_a_log, v_ssd_d, v_ssd_norm, v_ssd_w_out, v_kv_norm, v_w_kv_shared, v_dil_w_q, v_dil_w_o, v_mem_src_norm, v_mem_w_q, v_mem_w_kv, v_mem_w_o, v_ffn_w_in, v_ffn_w_out)
    return _step(x, mem, positions, wts, loss_target, mom, vel)
```

```python
import functools
import math

import jax
import jax.numpy as jnp
from jax import lax
from jax.experimental import pallas as pl
from jax.experimental.pallas import tpu as pltpu

F32 = jnp.float32
BF16 = jnp.bfloat16
MESH = pl.DeviceIdType.MESH
HIGHEST = lax.Precision.HIGHEST

NORM_EPS = 1e-6
SSD_GROUPS = 8
SSD_STATE = 128
SSD_CHUNK = 128
SSD_CONV = 4
DIL_DILATIONS = (1, 4, 16)
DIL_HEADS = 16
DIL_BLOCK = 128
ROPE_THETA = 10000.0
MEM_HEADS = 4
ADAM_LR, ADAM_B1, ADAM_B2, ADAM_EPS, ADAM_WD, ADAM_STEP = 0.001, 0.9, 0.999, 1e-08, 0.01, 10

V7X_VMEM_LIMIT = 48 * 1024 * 1024
LANES = 128
SUBLANES_BF16 = 16

WEIGHTS = ['norm_mix', 'norm_mem', 'norm_ffn', 'norm_final', 'ssd_w_in', 'ssd_conv_w', 'ssd_conv_b',
           'ssd_dt_bias', 'ssd_a_log', 'ssd_d', 'ssd_norm', 'ssd_w_out', 'kv_norm', 'w_kv_shared', 'dil_w_q',
           'dil_w_o', 'mem_src_norm', 'mem_w_q', 'mem_w_kv', 'mem_w_o', 'ffn_w_in', 'ffn_w_out']
COL_SHARDED = ('ssd_w_in', 'w_kv_shared', 'dil_w_q', 'mem_w_o', 'ffn_w_in')
ROW_SHARDED = ('ssd_w_out', 'dil_w_o', 'mem_w_q', 'mem_w_kv', 'ffn_w_out')


def _dot(a, b, ca, cb, prec=None):
    return lax.dot_general(a, b, (((ca,), (cb,)), ((), ())), preferred_element_type=F32, precision=prec)


def _nn(a, b, prec=None):
    return _dot(a, b, 1, 0, prec)


def _nt(a, b, prec=None):
    return _dot(a, b, 1, 1, prec)


def _tn(a, b, prec=None):
    return _dot(a, b, 0, 0, prec)


def _tile(n, pref, unit=LANES):
    if n <= pref:
        return n
    t = (pref // unit) * unit
    while t >= unit:
        if n % t == 0:
            return t
        t -= unit
    return n


class _Carry:
    def __init__(self, bufs, outs, n_sems, start, finish):
        self.bufs, self.outs, self.n_sems, self.start, self.finish = list(bufs), list(outs), n_sems, start, finish


def _call(body, name, grid, in_specs, out_specs, out_shape, scratch=(), sem=None, carry=None):
    if carry is None:
        return pl.pallas_call(
            body, name=name, grid=grid, in_specs=in_specs, out_specs=out_specs, out_shape=out_shape,
            scratch_shapes=list(scratch),
            compiler_params=pltpu.CompilerParams(dimension_semantics=sem, vmem_limit_bytes=V7X_VMEM_LIMIT))
    single = not isinstance(out_specs, (list, tuple))
    o_specs = [out_specs] if single else list(out_specs)
    o_shape = [out_shape] if single else list(out_shape)
    n_in, n_out, n_cb, n_co, n_scr = len(in_specs), len(o_specs), len(carry.bufs), len(carry.outs), len(scratch)
    anyspec = pl.BlockSpec(memory_space=pl.ANY)

    def riding(*refs):
        ins, cbufs = refs[:n_in], list(refs[n_in:n_in + n_cb])
        outs = refs[n_in + n_cb:n_in + n_cb + n_out]
        for k, b in enumerate(carry.outs):
            cbufs[b] = refs[n_in + n_cb + n_out + k]
        rest = refs[n_in + n_cb + n_out + n_co:]
        send, recv = rest[n_scr], rest[n_scr + 1]
        ids = [pl.program_id(a) for a in range(len(grid))]
        first = functools.reduce(lambda p, q: p & q, [i == 0 for i in ids])
        last = functools.reduce(lambda p, q: p & q, [i == g - 1 for i, g in zip(ids, grid)])

        @pl.when(first)
        def _():
            carry.start(cbufs, send, recv)

        body(*ins, *outs, *rest[:n_scr])

        @pl.when(last)
        def _():
            carry.finish(cbufs, send, recv)

    call = pl.pallas_call(
        riding, name=name, grid=grid, in_specs=list(in_specs) + [anyspec] * n_cb, out_specs=o_specs + [anyspec] * n_co,
        out_shape=o_shape + [_sds(carry.bufs[b].shape, carry.bufs[b].dtype) for b in carry.outs],
        scratch_shapes=list(scratch) + [pltpu.SemaphoreType.DMA((carry.n_sems,)), pltpu.SemaphoreType.DMA((carry.n_sems,))],
        input_output_aliases={n_in + b: n_out + k for k, b in enumerate(carry.outs)},
        compiler_params=pltpu.CompilerParams(dimension_semantics=("arbitrary",) * len(grid), has_side_effects=True,
                                             vmem_limit_bytes=V7X_VMEM_LIMIT))

    def run(*operands):
        res = call(*operands, *carry.bufs)
        own = res[0] if single else list(res[:n_out])
        return own, list(res[n_out:])

    return run


def _sds(shape, dtype):
    return jax.ShapeDtypeStruct(tuple(shape), dtype)


def _silu(x):
    return x * jax.nn.sigmoid(x)


def _dsilu(x):
    s = jax.nn.sigmoid(x)
    return s * (1.0 + x * (1.0 - s))


def _mm(a, b, mode, name, *, m, n, k, out_dtype=F32, res=None, b_noff=0, b_koff=0, carry=None):
    has_res = res is not None
    m_unit = LANES if mode == 'tn' else SUBLANES_BF16
    n_lim = math.gcd(n, b_noff) if b_noff else n
    k_lim = math.gcd(k, b_koff) if b_koff else k
    wide = out_dtype == F32
    if k <= 2048:
        tk = k
        tm = _tile(m, 2048, m_unit)
        tn = _tile(n_lim, 512 if (wide or has_res) else 1024)
    else:
        tk = _tile(k_lim, 1024)
        tm = _tile(m, 1024, m_unit)
        tn = _tile(n_lim, 2048 if (wide and not has_res) else 1024)
    nk = k // tk
    use_acc = nk > 1 and not wide
    jo, ko = b_noff // tn, b_koff // tk
    if mode == 'nn':
        a_spec = pl.BlockSpec((tm, tk), lambda i, j, kk: (i, kk))
        b_spec = pl.BlockSpec((tk, tn), lambda i, j, kk: (kk + ko, j + jo))
        ca, cb = 1, 0
    elif mode == 'nt':
        a_spec = pl.BlockSpec((tm, tk), lambda i, j, kk: (i, kk))
        b_spec = pl.BlockSpec((tn, tk), lambda i, j, kk: (j + jo, kk + ko))
        ca, cb = 1, 1
    else:
        a_spec = pl.BlockSpec((tk, tm), lambda i, j, kk: (kk, i))
        b_spec = pl.BlockSpec((tk, tn), lambda i, j, kk: (kk + ko, j + jo))
        ca, cb = 0, 0
    o_spec = pl.BlockSpec((tm, tn), lambda i, j, kk: (i, j))

    def body(*refs):
        a_ref, b_ref = refs[0], refs[1]
        r_ref = refs[2] if has_res else None
        o_ref = refs[3] if has_res else refs[2]
        acc = refs[-1] if use_acc else None

        def prod():
            return _dot(a_ref[...].astype(BF16), b_ref[...].astype(BF16), ca, cb)

        def with_res(v):
            return v + r_ref[...].astype(F32) if has_res else v

        if nk == 1:
            o_ref[...] = with_res(prod()).astype(o_ref.dtype)
            return
        kk = pl.program_id(2)
        if use_acc:
            @pl.when(kk == 0)
            def _():
                acc[...] = prod()

            @pl.when((kk > 0) & (kk < nk - 1))
            def _():
                acc[...] += prod()

            @pl.when(kk == nk - 1)
            def _():
                o_ref[...] = with_res(acc[...] + prod()).astype(o_ref.dtype)
        else:
            @pl.when(kk == 0)
            def _():
                o_ref[...] = with_res(prod())

            @pl.when(kk > 0)
            def _():
                o_ref[...] += prod()

    ins = [a, b] + ([res] if has_res else [])
    specs = [a_spec, b_spec] + ([o_spec] if has_res else [])
    return _call(body, name, (m // tm, n // tn, nk), specs, o_spec, _sds((m, n), out_dtype),
                 scratch=[pltpu.VMEM((tm, tn), F32)] if use_acc else [], sem=("parallel", "parallel", "arbitrary"),
                 carry=carry)(*ins)


def _rowwise(fn, name, rows, consts, outs, tb, n_rows):
    n_r, n_c = len(rows), len(consts)
    in_specs = [pl.BlockSpec((tb, w), functools.partial(lambda i, cb: (i, cb), cb=cb)) for _, w, cb in rows]
    in_specs += [pl.BlockSpec(c.shape, functools.partial(lambda i, nd: (0,) * nd, nd=c.ndim)) for c in consts]
    out_specs, out_shape = [], []
    for kind, w, dt in outs:
        if kind == 'row':
            out_specs.append(pl.BlockSpec((tb, w), lambda i: (i, 0)))
            out_shape.append(_sds((n_rows, w), dt))
        else:
            out_specs.append(pl.BlockSpec(w, lambda i: (0, 0)))
            out_shape.append(_sds(w, dt))

    def body(*refs):
        ins = [r[...] for r in refs[:n_r + n_c]]
        orefs = refs[n_r + n_c:]
        vals = fn(*ins)
        i = pl.program_id(0)
        for (kind, _, _), o_ref, v in zip(outs, orefs, vals):
            if kind == 'row':
                o_ref[...] = v.astype(o_ref.dtype)
            else:
                @pl.when(i == 0)
                def _(o_ref=o_ref):
                    o_ref[...] = jnp.zeros_like(o_ref)

                o_ref[...] += v.astype(o_ref.dtype)

    res = _call(body, name, (n_rows // tb,), in_specs, out_specs, out_shape, sem=("arbitrary",))(
        *[r[0] for r in rows], *consts)
    return res


def _rms_fwd_fn(h, g):
    r = lax.rsqrt(jnp.mean(h * h, axis=-1, keepdims=True) + NORM_EPS)
    return (h * r * g,)


def _rms_bwd_vals(h, g, dy):
    r = lax.rsqrt(jnp.mean(h * h, axis=-1, keepdims=True) + NORM_EPS)
    t = dy * g
    dh = r * t - h * (r * r * r) * jnp.mean(h * t, axis=-1, keepdims=True)
    dg = jnp.sum(dy * h * r, axis=0, keepdims=True)
    return dh, dg


def _rmsnorm(h, g, name):
    n, d = h.shape
    return _rowwise(_rms_fwd_fn, name, [(h, d, 0)], [g.reshape(1, d)], [('row', d, BF16)], _tile(n, 512, 8), n)[0]


def _rmsnorm_bwd(h, g, du, dres, name):
    n, d = h.shape

    def fn(hv, duv, drv, gv):
        dh, dg = _rms_bwd_vals(hv, gv, duv.astype(F32))
        return dh + drv, dh + drv, dg

    return _rowwise(fn, name, [(h, d, 0), (du, d, 0), (dres, d, 0)], [g.reshape(1, d)],
                    [('row', d, F32), ('row', d, BF16), ('acc', (1, d), F32)], _tile(n, 256, 8), n)


def _rmsnorm_bwd_noacc(h, g, du, name):
    n, d = h.shape

    def fn(hv, duv, gv):
        return _rms_bwd_vals(hv, gv, duv.astype(F32))

    return _rowwise(fn, name, [(h, d, 0), (du, d, 0)], [g.reshape(1, d)],
                    [('row', d, F32), ('acc', (1, d), F32)], _tile(n, 256, 8), n)


def _swiglu_fwd(gu, name):
    n, f2 = gu.shape
    f = f2 // 2

    def fn(v):
        v = v.astype(F32)
        return (_silu(v[:, :f]) * v[:, f:],)

    return _rowwise(fn, name, [(gu, f2, 0)], [], [('row', f, BF16)], _tile(n, 256, 16), n)[0]


def _swiglu_bwd(gu, dact, name):
    n, f2 = gu.shape
    f = f2 // 2

    def fn(v, da):
        v = v.astype(F32)
        g, up = v[:, :f], v[:, f:]
        da = da.astype(F32)
        return (jnp.concatenate([da * up * _dsilu(g), da * _silu(g)], axis=1),)

    return _rowwise(fn, name, [(gu, f2, 0), (dact, f, 0)], [], [('row', f2, BF16)], _tile(n, 256, 16), n)[0]


def _group_sum(v, ng):
    w = v.shape[1] // ng
    return [jnp.sum(v[:, i * w:(i + 1) * w], axis=1, keepdims=True) for i in range(ng)]


def _gated_norm_fwd(y, z, nw, name):
    n, di = y.shape
    gw = di // SSD_GROUPS

    def fn(yv, zv, nwv):
        a = yv * _silu(zv)
        ms = _group_sum(a * a, SSD_GROUPS)
        out = jnp.concatenate([a[:, i * gw:(i + 1) * gw] * lax.rsqrt(ms[i] / gw + NORM_EPS)
                               for i in range(SSD_GROUPS)], axis=1)
        return (out * nwv,)

    return _rowwise(fn, name, [(y, di, 0), (z, di, 0)], [nw.reshape(1, di)], [('row', di, BF16)], _tile(n, 256, 8), n)[0]


def _gated_norm_bwd(y, z, nw, dout, name):
    n, di = y.shape
    gw = di // SSD_GROUPS

    def fn(yv, zv, dov, nwv):
        sz = _silu(zv)
        a = yv * sz
        t = dov * nwv
        ms = _group_sum(a * a, SSD_GROUPS)
        at = _group_sum(a * t, SSD_GROUPS)
        das, ars = [], []
        for i in range(SSD_GROUPS):
            r = lax.rsqrt(ms[i] / gw + NORM_EPS)
            sl = slice(i * gw, (i + 1) * gw)
            das.append(r * t[:, sl] - a[:, sl] * (r * r * r) * (at[i] / gw))
            ars.append(a[:, sl] * r)
        da = jnp.concatenate(das, axis=1)
        ar = jnp.concatenate(ars, axis=1)
        return da * sz, da * yv * _dsilu(zv), jnp.sum(dov * ar, axis=0, keepdims=True)

    return _rowwise(fn, name, [(y, di, 0), (z, di, 0), (dout, di, 0)], [nw.reshape(1, di)],
                    [('row', di, F32), ('row', di, BF16), ('acc', (1, di), F32)], _tile(n, 128, 8), n)


def _softplus_fwd(raw, bias, name):
    n, h = raw.shape

    def fn(v, b):
        t = v + b
        return (jnp.maximum(t, 0.0) + jnp.log(1.0 + jnp.exp(-jnp.abs(t))),)

    return _rowwise(fn, name, [(raw, h, 0)], [bias.reshape(1, h)], [('row', h, F32)], _tile(n, 1024, 8), n)[0]


def _softplus_bwd(raw, bias, ddt, name):
    n, h = raw.shape

    def fn(v, d, b):
        g = d * jax.nn.sigmoid(v + b)
        return g, jnp.sum(g, axis=0, keepdims=True)

    return _rowwise(fn, name, [(raw, h, 0), (ddt, h, 0)], [bias.reshape(1, h)],
                    [('row', h, BF16), ('acc', (1, h), F32)], _tile(n, 1024, 8), n)


def _loss_and_grad(h, g, tgt, name):
    n, d = h.shape

    def fn(hv, tv, gv):
        y = _rms_fwd_fn(hv, gv)[0]
        err = y - tv
        part = 0.5 * jnp.sum(jnp.sum(err * err, axis=1, keepdims=True), axis=0, keepdims=True) / d
        dh, dg = _rms_bwd_vals(hv, gv, err / d)
        return jnp.broadcast_to(part, (8, LANES)), dh, dh, dg

    return _rowwise(fn, name, [(h, d, 0), (tgt, d, 0)], [g.reshape(1, d)],
                    [('acc', (8, LANES), F32), ('row', d, F32), ('row', d, BF16), ('acc', (1, d), F32)],
                    _tile(n, 256, 8), n)


def _rope_tables(pos_col, dh, name):
    n = pos_col.shape[0]
    half = dh // 2
    inv = ROPE_THETA ** (-jnp.arange(half, dtype=F32) / half)
    inv2 = jnp.concatenate([inv, inv]).reshape(1, dh)
    sign = jnp.concatenate([-jnp.ones((half,), F32), jnp.ones((half,), F32)]).reshape(1, dh)

    def fn(p, iv, sg):
        ang = p.astype(F32) * iv
        return jnp.cos(ang), jnp.sin(ang) * sg

    return _rowwise(fn, name, [(pos_col, 1, 0)], [inv2, sign], [('row', dh, F32), ('row', dh, F32)], _tile(n, 1024, 8), n)


def _rope(t, cos2, sin2, name, *, width, sign, out_dtype):
    n = t.shape[0]
    dh = cos2.shape[1]
    tw = _tile(width, 2048)
    tb = _tile(n, 512, 8)

    def body(t_ref, c_ref, s_ref, o_ref):
        c, s = c_ref[...], s_ref[...]
        for hh in range(tw // dh):
            v = t_ref[:, hh * dh:(hh + 1) * dh].astype(F32)
            o_ref[:, hh * dh:(hh + 1) * dh] = (v * c + sign * pltpu.roll(v, dh // 2, 1) * s).astype(o_ref.dtype)

    return _call(body, name, (n // tb, width // tw),
                 [pl.BlockSpec((tb, tw), lambda i, j: (i, j)), pl.BlockSpec((tb, dh), lambda i, j: (i, 0)),
                  pl.BlockSpec((tb, dh), lambda i, j: (i, 0))],
                 pl.BlockSpec((tb, tw), lambda i, j: (i, j)), _sds((n, width), out_dtype),
                 sem=("parallel", "parallel"))(t, cos2, sin2)


def _conv_taps(ext, w, tb):
    shifted = [ext[8:] if k == SSD_CONV - 1 else pltpu.roll(ext, SSD_CONV - 1 - k, 0)[8:] for k in range(SSD_CONV)]
    pre = shifted[0] * w[0:1]
    for k in range(1, SSD_CONV):
        pre = pre + shifted[k] * w[k:k + 1]
    return pre, shifted


def _conv_specs(n, c, tb, tc):
    hb = tb // 8
    blk = pl.BlockSpec((tb, tc), lambda j, i: (i, j))
    halo = pl.BlockSpec((8, tc), lambda j, i: (jnp.maximum(i * hb - 1, 0), j))
    wsp = pl.BlockSpec((SSD_CONV, tc), lambda j, i: (0, j))
    bsp = pl.BlockSpec((1, tc), lambda j, i: (0, j))
    return blk, halo, wsp, bsp


def _conv_fwd(u, w, b, name):
    n, c = u.shape
    tb, tc = _tile(n, 512, 8), _tile(c, 1536)
    blk, halo, wsp, bsp = _conv_specs(n, c, tb, tc)

    def body(u_ref, h_ref, w_ref, b_ref, o_ref):
        halo_v = jnp.where(pl.program_id(1) > 0, h_ref[...], 0.0)
        pre, _ = _conv_taps(jnp.concatenate([halo_v, u_ref[...]], axis=0), w_ref[...], tb)
        o_ref[...] = _silu(pre + b_ref[...])

    return _call(body, name, (c // tc, n // tb), [blk, halo, wsp, bsp], blk, _sds((n, c), F32),
                 sem=("parallel", "arbitrary"))(u, u, w, b.reshape(1, c))


def _conv_bwd_pre(u, w, b, dout, name):
    n, c = u.shape
    tb, tc = _tile(n, 512, 8), _tile(c, 1536)
    blk, halo, wsp, bsp = _conv_specs(n, c, tb, tc)

    def body(u_ref, h_ref, w_ref, b_ref, d_ref, dp_ref, dw_ref, db_ref):
        i = pl.program_id(1)
        halo_v = jnp.where(i > 0, h_ref[...], 0.0)
        pre, shifted = _conv_taps(jnp.concatenate([halo_v, u_ref[...]], axis=0), w_ref[...], tb)
        dp = d_ref[...] * _dsilu(pre + b_ref[...])
        dp_ref[...] = dp

        @pl.when(i == 0)
        def _():
            dw_ref[...] = jnp.zeros_like(dw_ref)
            db_ref[...] = jnp.zeros_like(db_ref)

        dw_ref[...] += jnp.concatenate([jnp.sum(dp * s, axis=0, keepdims=True) for s in shifted], axis=0)
        db_ref[...] += jnp.sum(dp, axis=0, keepdims=True)

    return _call(body, name, (c // tc, n // tb), [blk, halo, wsp, bsp, blk], [blk, wsp, bsp],
                 [_sds((n, c), F32), _sds((SSD_CONV, c), F32), _sds((1, c), F32)],
                 sem=("parallel", "arbitrary"))(u, u, w, b.reshape(1, c), dout)


def _conv_bwd_in(dpre, w, name):
    n, c = dpre.shape
    tb, tc = _tile(n, 512, 8), _tile(c, 1536)
    hb = tb // 8
    nb = n // tb
    blk = pl.BlockSpec((tb, tc), lambda j, i: (i, j))
    nxt = pl.BlockSpec((8, tc), lambda j, i: (jnp.minimum((i + 1) * hb, n // 8 - 1), j))
    wsp = pl.BlockSpec((SSD_CONV, tc), lambda j, i: (0, j))

    def body(d_ref, n_ref, w_ref, o_ref):
        nxt_v = jnp.where(pl.program_id(1) < nb - 1, n_ref[...], 0.0)
        ext = jnp.concatenate([d_ref[...], nxt_v], axis=0)
        wv = w_ref[...]
        acc = ext[:tb] * wv[SSD_CONV - 1:SSD_CONV]
        for k in range(SSD_CONV - 1):
            s = SSD_CONV - 1 - k
            acc = acc + pltpu.roll(ext, tb + 8 - s, 0)[:tb] * wv[k:k + 1]
        o_ref[...] = acc.astype(o_ref.dtype)

    return _call(body, name, (c // tc, nb), [blk, nxt, wsp], blk, _sds((n, c), BF16),
                 sem=("parallel", "arbitrary"))(dpre, dpre, w)


def _col(v, j):
    lane = lax.broadcasted_iota(jnp.int32, v.shape, 1)
    return jnp.sum(jnp.where(lane == j, v, 0.0), axis=1, keepdims=True)


def _row(v, j):
    sub = lax.broadcasted_iota(jnp.int32, v.shape, 0)
    return jnp.sum(jnp.where(sub == j, v, 0.0), axis=0, keepdims=True)


def _ssd_head_terms(dtc, dtr, a_row, a_col):
    ll = dtc.shape[0]
    r = lax.broadcasted_iota(jnp.int32, (ll, ll), 0)
    c = lax.broadcasted_iota(jnp.int32, (ll, ll), 1)
    tril = (r >= c)
    trilf = tril.astype(F32)
    triuf = (r <= c).astype(F32)
    cumc = _nn(trilf, dtc * a_row, HIGHEST)
    cumr = _nn(dtr * a_col, triuf, HIGHEST)
    return cumc, cumr, tril, trilf, triuf


def _ssd_specs(n, di, gn, jh, p, nc, rev):
    ll = SSD_CHUNK
    jp = jh * p

    def ci(c):
        return (nc - 1 - c) if rev else c

    xs = pl.BlockSpec((ll, jp), lambda g, c: (ci(c), g))
    bs = pl.BlockSpec((ll, SSD_STATE), lambda g, c: (ci(c), di // SSD_STATE + g))
    cs = pl.BlockSpec((ll, SSD_STATE), lambda g, c: (ci(c), (di + gn * SSD_STATE) // SSD_STATE + g))
    dtc = pl.BlockSpec((None, ll, jh), lambda g, c: (g, ci(c), 0))
    dtr = pl.BlockSpec((None, jh, ll), lambda g, c: (g, 0, ci(c)))
    arow = pl.BlockSpec((None, 1, jh), lambda g, c: (g, 0, 0))
    acol = pl.BlockSpec((None, jh, 1), lambda g, c: (g, 0, 0))
    dsk = pl.BlockSpec((None, 1, jp), lambda g, c: (g, 0, 0))
    hin = pl.BlockSpec((None, None, SSD_STATE, jp), lambda g, c: (g, ci(c), 0, 0))
    ys = pl.BlockSpec((ll, jp), lambda g, c: (ci(c), g))
    return xs, bs, cs, dtc, dtr, arow, acol, dsk, hin, ys


def _ssd_fwd(xbc, dtc, dtr, a_row, a_col, dskip, name, *, di, p, carry=None):
    n = xbc.shape[0]
    gn = SSD_GROUPS
    jh = dtc.shape[2]
    jp = jh * p
    ll = SSD_CHUNK
    nc = n // ll
    xs, bs, cs, dtcs, dtrs, arow, acol, dsk, hin, ys = _ssd_specs(n, di, gn, jh, p, nc, False)
    pair = 2 * p

    def body(x_ref, b_ref, c_ref, dtc_ref, dtr_ref, ar_ref, ac_ref, ds_ref, y_ref, hin_ref, h_scr):
        @pl.when(pl.program_id(1) == 0)
        def _():
            h_scr[...] = jnp.zeros_like(h_scr)

        dtcv, dtrv = dtc_ref[...], dtr_ref[...]
        cumc, cumr, tril, _, _ = _ssd_head_terms(dtcv, dtrv, ar_ref[...], ac_ref[...])
        tot = jnp.sum(dtcv * ar_ref[...], axis=0, keepdims=True)
        bb, cb_ = b_ref[...].astype(BF16), c_ref[...].astype(BF16)
        cbm = _nt(cb_, bb)
        hin_ref[...] = h_scr[...]
        lane = lax.broadcasted_iota(jnp.int32, (ll, pair), 1)
        lane1 = lax.broadcasted_iota(jnp.int32, (1, pair), 1)
        for pr in range(jh // 2):
            sl = slice(pr * pair, (pr + 1) * pair)
            xp = x_ref[:, sl]
            xpb = xp.astype(BF16)
            hp = h_scr[:, sl]
            ydiag = jnp.zeros((ll, pair), F32)
            e_p = jnp.zeros((ll, pair), F32)
            w_p = jnp.zeros((ll, pair), F32)
            cd_p = jnp.zeros((1, pair), F32)
            for q in range(2):
                j = 2 * pr + q
                mj = (lane >= p) if q else (lane < p)
                cc, cr = _col(cumc, j), _row(cumr, j)
                decay = jnp.exp(jnp.where(tril, cc - cr, -1e30))
                mm = cbm * decay * _row(dtrv, j)
                ydiag = ydiag + _nn(mm.astype(BF16), jnp.where(mj, xpb, jnp.zeros_like(xpb)))
                cl = _col(tot, j)
                e_p = jnp.where(mj, jnp.exp(cc), e_p)
                w_p = jnp.where(mj, jnp.exp(cl - cc) * _col(dtcv, j), w_p)
                cd_p = jnp.where((lane1 >= p) if q else (lane1 < p), jnp.exp(cl), cd_p)
            yoff = _nn(cb_, hp.astype(BF16)) * e_p
            y_ref[:, sl] = ydiag + yoff + xp * ds_ref[:, sl]
            h_scr[:, sl] = hp * cd_p + _tn(bb, (xp * w_p).astype(BF16))

    return _call(body, name, (gn, nc), [xs, bs, cs, dtcs, dtrs, arow, acol, dsk], [ys, hin],
                 [_sds((n, di), F32), _sds((gn, nc, SSD_STATE, jp), F32)],
                 scratch=[pltpu.VMEM((SSD_STATE, jp), F32)], sem=("parallel", "arbitrary"), carry=carry)(
        xbc, xbc, xbc, dtc, dtr, a_row, a_col, dskip)


def _ssd_bwd(xbc, dtc, dtr, a_row, a_col, dskip, hins, dy, name, *, di, p, carry=None):
    n = xbc.shape[0]
    gn = SSD_GROUPS
    jh = dtc.shape[2]
    jp = jh * p
    ll = SSD_CHUNK
    nc = n // ll
    xs, bs, cs, dtcs, dtrs, arow, acol, dsk, hin, ys = _ssd_specs(n, di, gn, jh, p, nc, True)
    pair = 2 * p
    bc_out = pl.BlockSpec((ll, SSD_STATE), lambda g, c: (nc - 1 - c, g))

    def body(x_ref, b_ref, c_ref, dtc_ref, dtr_ref, ar_ref, ac_ref, ds_ref, hin_ref, dy_ref,
             dx_ref, db_ref, dc_ref, ddtc_ref, ddtr_ref, dac_ref, dar_ref, dd_ref, dh_scr):
        first = pl.program_id(1) == 0

        @pl.when(first)
        def _():
            dh_scr[...] = jnp.zeros_like(dh_scr)
            dac_ref[...] = jnp.zeros_like(dac_ref)
            dar_ref[...] = jnp.zeros_like(dar_ref)
            dd_ref[...] = jnp.zeros_like(dd_ref)

        dtcv, dtrv = dtc_ref[...], dtr_ref[...]
        a_r, a_c = ar_ref[...], ac_ref[...]
        cumc, cumr, tril, trilf, triuf = _ssd_head_terms(dtcv, dtrv, a_r, a_c)
        tot = jnp.sum(dtcv * a_r, axis=0, keepdims=True)
        bb, cb_ = b_ref[...].astype(BF16), c_ref[...].astype(BF16)
        cbm = _nt(cb_, bb)
        lane = lax.broadcasted_iota(jnp.int32, (ll, pair), 1)
        lane1 = lax.broadcasted_iota(jnp.int32, (1, pair), 1)
        lane_j = lax.broadcasted_iota(jnp.int32, (ll, jh), 1)
        sub_l = lax.broadcasted_iota(jnp.int32, (ll, jh), 0)
        sub_j = lax.broadcasted_iota(jnp.int32, (jh, ll), 0)
        dcb = jnp.zeros((ll, ll), F32)
        db_acc = jnp.zeros((ll, SSD_STATE), F32)
        dc_acc = jnp.zeros((ll, SSD_STATE), F32)
        dcum_c = jnp.zeros((ll, jh), F32)
        dcum_r = jnp.zeros((jh, ll), F32)
        ddt_c = jnp.zeros((ll, jh), F32)
        ddt_r = jnp.zeros((jh, ll), F32)
        for pr in range(jh // 2):
            sl = slice(pr * pair, (pr + 1) * pair)
            xp = x_ref[:, sl]
            xpb = xp.astype(BF16)
            dyp = dy_ref[:, sl]
            hp = hin_ref[:, sl]
            hpb = hp.astype(BF16)
            dhp = dh_scr[:, sl]
            dhpb = dhp.astype(BF16)
            ch = _nn(cb_, hpb)
            gp = _nn(bb, dhpb)
            e_p = jnp.zeros((ll, pair), F32)
            w_p = jnp.zeros((ll, pair), F32)
            cd_p = jnp.zeros((1, pair), F32)
            dxp = dyp * ds_ref[:, sl]
            heads = []
            for q in range(2):
                j = 2 * pr + q
                mj = (lane >= p) if q else (lane < p)
                cc, cr = _col(cumc, j), _row(cumr, j)
                cl = _col(tot, j)
                ej = jnp.exp(cc)
                wdec = jnp.exp(cl - cc)
                wj = wdec * _col(dtcv, j)
                e_p = jnp.where(mj, ej, e_p)
                w_p = jnp.where(mj, wj, w_p)
                cdj = jnp.exp(cl)
                cd_p = jnp.where((lane1 >= p) if q else (lane1 < p), cdj, cd_p)
                heads.append((j, mj, cc, cr, cl, wdec, wj, cdj))
            dye = dyp * e_p
            dyeb = dye.astype(BF16)
            dc_acc = dc_acc + _nt(dyeb, hpb)
            dh_new = dhp * cd_p + _tn(cb_, dyeb)
            dxp = dxp + gp * w_p
            db_acc = db_acc + _nt((xp * w_p).astype(BF16), dhpb)
            t_off = dye * ch
            t_w = gp * xp
            t_cd = jnp.sum(dhp * hp, axis=0, keepdims=True)
            for (j, mj, cc, cr, cl, wdec, wj, cdj) in heads:
                dyj = jnp.where(mj, dyp, 0.0).astype(BF16)
                decay = jnp.exp(jnp.where(tril, cc - cr, -1e30))
                dtrow = _row(dtrv, j)
                mm = cbm * decay * dtrow
                dm = _nt(dyj, xpb)
                dxp = dxp + _tn(mm.astype(BF16), dyj)
                dcb = dcb + dm * decay * dtrow
                ddt_rj = jnp.sum(dm * cbm * decay, axis=0, keepdims=True)
                dseg = dm * mm
                dcum_cj = jnp.sum(dseg, axis=1, keepdims=True) + jnp.sum(jnp.where(mj, t_off, 0.0), axis=1, keepdims=True)
                dcum_rj = -jnp.sum(dseg, axis=0, keepdims=True)
                dwj = jnp.sum(jnp.where(mj, t_w, 0.0), axis=1, keepdims=True)
                ddt_cj = dwj * wdec
                qj = dwj * wj
                dcum_cj = dcum_cj - qj
                m1 = (lane1 >= p) if (j % 2) else (lane1 < p)
                dcl = jnp.sum(jnp.where(m1, t_cd, 0.0), axis=1, keepdims=True) * cdj + jnp.sum(qj, axis=0, keepdims=True)
                dcum_c = dcum_c + jnp.where(lane_j == j, dcum_cj, 0.0) + jnp.where((lane_j == j) & (sub_l == ll - 1), dcl, 0.0)
                dcum_r = dcum_r + jnp.where(sub_j == j, dcum_rj, 0.0)
                ddt_c = ddt_c + jnp.where(lane_j == j, ddt_cj, 0.0)
                ddt_r = ddt_r + jnp.where(sub_j == j, ddt_rj, 0.0)
            dx_ref[:, sl] = dxp
            dd_ref[:, sl] += jnp.sum(dyp * xp, axis=0, keepdims=True)
            dh_scr[:, sl] = dh_new
        dcbb = dcb.astype(BF16)
        dc_ref[...] = dc_acc + _nn(dcbb, bb)
        db_ref[...] = db_acc + _tn(dcbb, cb_)
        dda_c = _nn(triuf, dcum_c, HIGHEST)
        dda_r = _nn(dcum_r, trilf, HIGHEST)
        ddtc_ref[...] = ddt_c + dda_c * a_r
        ddtr_ref[...] = ddt_r + dda_r * a_c
        dac_ref[...] += jnp.sum(dda_c * dtcv, axis=0, keepdims=True)
        dar_ref[...] += jnp.sum(dda_r * dtrv, axis=1, keepdims=True)

    gs = SSD_GROUPS * SSD_STATE
    return _call(body, name, (gn, nc), [xs, bs, cs, dtcs, dtrs, arow, acol, dsk, hin, ys],
                 [ys, bc_out, bc_out, dtcs, dtrs, arow, acol, dsk],
                 [_sds((n, di), F32), _sds((n, gs), F32), _sds((n, gs), F32), _sds(dtc.shape, F32), _sds(dtr.shape, F32),
                  _sds((gn, 1, jh), F32), _sds((gn, jh, 1), F32), _sds((gn, 1, jp), F32)],
                 scratch=[pltpu.VMEM((SSD_STATE, jp), F32)], sem=("parallel", "arbitrary"), carry=carry)(
        xbc, xbc, xbc, dtc, dtr, a_row, a_col, dskip, hins, dy)


def _band_masks():
    r = lax.broadcasted_iota(jnp.int32, (DIL_BLOCK, DIL_BLOCK), 0)
    c = lax.broadcasted_iota(jnp.int32, (DIL_BLOCK, DIL_BLOCK), 1)
    return c >= r, c <= r


def _to_lanes(cols, width):
    tb = cols[0].shape[0]
    lane = lax.broadcasted_iota(jnp.int32, (tb, width), 1)
    out = jnp.zeros((tb, width), F32)
    for h, cv in enumerate(cols):
        out = jnp.where(lane == h, cv, out)
    return out


def _dil_attn_fwd(q2, k2, v2, g, dil, name, *, dw, dh, carry=None):
    n = q2.shape[0]
    nb = n // DIL_BLOCK
    scale = dh ** -0.5
    tb = DIL_BLOCK
    cur = pl.BlockSpec((tb, dw), lambda r, jb: (jb, r * 3 + g))
    prev = pl.BlockSpec((tb, dw), lambda r, jb: (jnp.maximum(jb - 1, 0), r * 3 + g))
    o_spec = pl.BlockSpec((tb, dw), lambda r, jb: (jb, r))
    l_spec = pl.BlockSpec((tb, LANES), lambda r, jb: (jb, r))

    def body(q_ref, kc_ref, kp_ref, vc_ref, vp_ref, o_ref, l_ref):
        mp, mc = _band_masks()
        mp = mp & (pl.program_id(1) > 0)
        lses = []
        for h in range(DIL_HEADS):
            sl = slice(h * dh, (h + 1) * dh)
            qh = q_ref[:, sl]
            sp = jnp.where(mp, _nt(qh, kp_ref[:, sl]) * scale, -jnp.inf)
            sc = jnp.where(mc, _nt(qh, kc_ref[:, sl]) * scale, -jnp.inf)
            mx = jnp.maximum(jnp.max(sp, axis=1, keepdims=True), jnp.max(sc, axis=1, keepdims=True))
            pp, pc = jnp.exp(sp - mx), jnp.exp(sc - mx)
            den = jnp.sum(pp, axis=1, keepdims=True) + jnp.sum(pc, axis=1, keepdims=True)
            o = _nn(pp.astype(BF16), vp_ref[:, sl]) + _nn(pc.astype(BF16), vc_ref[:, sl])
            o_ref[:, sl] = o / den
            lses.append(mx + jnp.log(den))
        l_ref[...] = _to_lanes(lses, LANES)

    return _call(body, name, (dil, nb), [cur, cur, prev, cur, prev], [o_spec, l_spec],
                 [_sds((n, dil * dw), F32), _sds((n, dil * LANES), F32)], sem=("parallel", "arbitrary"), carry=carry)(
        q2, k2, k2, v2, v2)


def _dil_attn_bwd(q2, k2, v2, do2, lse2, dl2, g, dil, name, *, dw, dh, carry=None):
    n = q2.shape[0]
    nb = n // DIL_BLOCK
    scale = dh ** -0.5
    tb = DIL_BLOCK

    def nx(jb):
        return jnp.minimum(jb + 1, nb - 1)

    q_c = pl.BlockSpec((tb, dw), lambda r, jb: (jb, r * 3 + g))
    q_n = pl.BlockSpec((tb, dw), lambda r, jb: (nx(jb), r * 3 + g))
    k_p = pl.BlockSpec((tb, dw), lambda r, jb: (jnp.maximum(jb - 1, 0), r * 3 + g))
    o_c = pl.BlockSpec((tb, dw), lambda r, jb: (jb, r))
    o_n = pl.BlockSpec((tb, dw), lambda r, jb: (nx(jb), r))
    l_c = pl.BlockSpec((tb, LANES), lambda r, jb: (jb, r))
    l_n = pl.BlockSpec((tb, LANES), lambda r, jb: (nx(jb), r))

    def body(qc_ref, qn_ref, kc_ref, kp_ref, vc_ref, vp_ref, doc_ref, don_ref, lc_ref, ln_ref, dlc_ref, dln_ref,
             dq_ref, dk_ref, dv_ref):
        jb = pl.program_id(1)
        mp, mc = _band_masks()
        has_prev = jb > 0
        has_next = jb < nb - 1
        lc, ln, dlc, dln = lc_ref[...], ln_ref[...], dlc_ref[...], dln_ref[...]

        def pair(qh, kh, vh, doh, lse_h, dl_h, mask):
            s = _nt(qh, kh) * scale
            pm = jnp.where(mask, jnp.exp(s - lse_h), 0.0)
            dp = _nt(doh, vh)
            ds = (pm * (dp - dl_h) * scale).astype(BF16)
            return pm.astype(BF16), ds

        for h in range(DIL_HEADS):
            sl = slice(h * dh, (h + 1) * dh)
            qc, qn, kc, kp = qc_ref[:, sl], qn_ref[:, sl], kc_ref[:, sl], kp_ref[:, sl]
            vc, vp, doc, don = vc_ref[:, sl], vp_ref[:, sl], doc_ref[:, sl], don_ref[:, sl]
            lch, lnh, dlch, dlnh = _col(lc, h), _col(ln, h), _col(dlc, h), _col(dln, h)
            _, ds_cp = pair(qc, kp, vp, doc, lch, dlch, mp & has_prev)
            p_cc, ds_cc = pair(qc, kc, vc, doc, lch, dlch, mc)
            p_nc, ds_nc = pair(qn, kc, vc, don, lnh, dlnh, mp & has_next)
            dq_ref[:, sl] = _nn(ds_cp, kp) + _nn(ds_cc, kc)
            dk_ref[:, sl] = _tn(ds_cc, qc) + _tn(ds_nc, qn)
            dv_ref[:, sl] = _tn(p_cc, doc) + _tn(p_nc, don)

    out = _sds((n, dil * dw), F32)
    return _call(body, name, (dil, nb), [q_c, q_n, q_c, k_p, q_c, k_p, o_c, o_n, l_c, l_n, l_c, l_n],
                 [o_c, o_c, o_c], [out, out, out], sem=("parallel", "arbitrary"), carry=carry)(
        q2, q2, k2, k2, v2, v2, do2, do2, lse2, lse2, dl2, dl2)


def _head_expand(dw, dh):
    r = lax.broadcasted_iota(jnp.int32, (LANES, dw), 0)
    c = lax.broadcasted_iota(jnp.int32, (LANES, dw), 1)
    return ((c // dh) == r).astype(F32)


def _mix_weights(l0, l1, l2):
    mx = jnp.maximum(jnp.maximum(l0, l1), l2)
    e = [jnp.exp(l0 - mx), jnp.exp(l1 - mx), jnp.exp(l2 - mx)]
    den = e[0] + e[1] + e[2]
    return [v / den for v in e]


def _mix_fwd(os_, ls_, name, *, dh):
    n, dw = os_[0].shape

    def fn(o0, o1, o2, l0, l1, l2):
        ex = _head_expand(dw, dh)
        ws = _mix_weights(l0, l1, l2)
        return (sum(_nn(w, ex, HIGHEST) * o for w, o in zip(ws, (o0, o1, o2))),)

    return _rowwise(fn, name, [(o, dw, 0) for o in os_] + [(l, LANES, 0) for l in ls_], [],
                    [('row', dw, BF16)], _tile(n, 256, 8), n)[0]


def _mix_bwd(os_, ls_, do, name, *, dh):
    n, dw = os_[0].shape

    def fn(o0, o1, o2, l0, l1, l2, dov):
        ex = _head_expand(dw, dh)
        ws = _mix_weights(l0, l1, l2)
        dws = [_nt(dov * o, ex, HIGHEST) for o in (o0, o1, o2)]
        sdw = ws[0] * dws[0] + ws[1] * dws[1] + ws[2] * dws[2]
        return tuple(_nn(w, ex, HIGHEST) * dov for w in ws) + tuple(w * sdw for w in ws)

    return _rowwise(fn, name, [(o, dw, 0) for o in os_] + [(l, LANES, 0) for l in ls_] + [(do, dw, 0)], [],
                    [('row', dw, BF16)] * 3 + [('row', LANES, F32)] * 3, _tile(n, 256, 8), n)


def _mem_attn_fwd(q, kv, name):
    n, mw = q.shape
    ml = kv.shape[0]
    dh = mw // MEM_HEADS
    scale = dh ** -0.5
    tb = _tile(n, 512, 8)

    def body(q_ref, kv_ref, o_ref):
        for h in range(MEM_HEADS):
            sl = slice(h * dh, (h + 1) * dh)
            s = _nt(q_ref[:, sl].astype(BF16), kv_ref[:, sl].astype(BF16)) * scale
            pm = jnp.exp(s - jnp.max(s, axis=1, keepdims=True))
            pm = pm / jnp.sum(pm, axis=1, keepdims=True)
            o_ref[:, sl] = _nn(pm.astype(BF16), kv_ref[:, mw + h * dh:mw + (h + 1) * dh].astype(BF16)).astype(o_ref.dtype)

    return _call(body, name, (n // tb,), [pl.BlockSpec((tb, mw), lambda i: (i, 0)), pl.BlockSpec((ml, 2 * mw), lambda i: (0, 0))],
                 pl.BlockSpec((tb, mw), lambda i: (i, 0)), _sds((n, mw), BF16), sem=("parallel",))(q, kv)


def _mem_attn_bwd(q, kv, do, name):
    n, mw = q.shape
    ml = kv.shape[0]
    dh = mw // MEM_HEADS
    scale = dh ** -0.5
    tb = _tile(n, 512, 8)

    def body(q_ref, kv_ref, do_ref, dq_ref, dkv_ref):
        @pl.when(pl.program_id(0) == 0)
        def _():
            dkv_ref[...] = jnp.zeros_like(dkv_ref)

        for h in range(MEM_HEADS):
            sl = slice(h * dh, (h + 1) * dh)
            vsl = slice(mw + h * dh, mw + (h + 1) * dh)
            qh, kh, vh = q_ref[:, sl].astype(BF16), kv_ref[:, sl].astype(BF16), kv_ref[:, vsl].astype(BF16)
            doh = do_ref[:, sl].astype(BF16)
            s = _nt(qh, kh) * scale
            pm = jnp.exp(s - jnp.max(s, axis=1, keepdims=True))
            pm = pm / jnp.sum(pm, axis=1, keepdims=True)
            dp = _nt(doh, vh)
            ds = (pm * (dp - jnp.sum(pm * dp, axis=1, keepdims=True)) * scale).astype(BF16)
            dq_ref[:, sl] = _nn(ds, kh).astype(dq_ref.dtype)
            dkv_ref[:, sl] += _tn(ds, qh)
            dkv_ref[:, vsl] += _tn(pm.astype(BF16), doh)

    row = pl.BlockSpec((tb, mw), lambda i: (i, 0))
    kvs = pl.BlockSpec((ml, 2 * mw), lambda i: (0, 0))
    return _call(body, name, (n // tb,), [row, kvs, row], [row, kvs], [_sds((n, mw), BF16), _sds((ml, 2 * mw), F32)],
                 sem=("arbitrary",))(q, kv, do)


def _adamw(w, g, m, v, name):
    shape = w.shape
    c = shape[-1]
    r = max(1, math.prod(shape[:-1]))
    w2, g2, m2, v2 = (t.reshape(r, c) for t in (w, g, m, v))
    tb = _tile(r, max(8, (1 << 19) // max(c, 1) // 8 * 8), 8)
    bc1 = 1.0 - ADAM_B1 ** ADAM_STEP
    bc2 = 1.0 - ADAM_B2 ** ADAM_STEP

    def body(w_ref, g_ref, m_ref, v_ref, d_ref, nm_ref, nv_ref):
        gv = g_ref[...]
        nm = ADAM_B1 * m_ref[...] + (1.0 - ADAM_B1) * gv
        nv = ADAM_B2 * v_ref[...] + (1.0 - ADAM_B2) * (gv * gv)
        d_ref[...] = -ADAM_LR * ((nm / bc1) / (jnp.sqrt(nv / bc2) + ADAM_EPS) + ADAM_WD * w_ref[...])
        nm_ref[...] = nm
        nv_ref[...] = nv

    spec = pl.BlockSpec((tb, c), lambda i: (i, 0))
    o = _sds((r, c), F32)
    d, nm, nv = _call(body, name, (r // tb,), [spec] * 4, [spec] * 3, [o, o, o], sem=("parallel",))(w2, g2, m2, v2)
    return d.reshape(shape), nm.reshape(shape), nv.reshape(shape)


def _place():
    x, y, c = lax.axis_index("x"), lax.axis_index("y"), lax.axis_index("c")
    chips = [(1 - x, y), (x, 1 - y), (1 - x, 1 - y)]
    return x, y, c, chips


def _cols(ref, c):
    hw = ref.shape[-1] // 2
    return ref.at[(slice(None),) * (len(ref.shape) - 1) + (pl.ds(c * hw, hw),)]


def _slot(ref, s):
    return ref.at[:, s]


def _my_chip():
    return 2 * lax.axis_index("x") + lax.axis_index("y")


def _place_shard(t, name):
    nl, r, n = t.shape
    tb = _tile(r, 512, SUBLANES_BF16)

    def body(t_ref, o_ref):
        o_ref[...] = t_ref[...].astype(BF16)

    return _call(body, name, (nl, r // tb), [pl.BlockSpec((None, tb, n), lambda l, i: (l, i, 0))],
                 pl.BlockSpec((None, None, tb, n), lambda l, i: (l, _my_chip(), i, 0)),
                 _sds((nl, 4, r, n), BF16), sem=("parallel", "parallel"))(t)


def _aliased_comm_call(body, name, bufs, n_sems):
    na = len(bufs)
    anyspec = pl.BlockSpec(memory_space=pl.ANY)
    return pl.pallas_call(
        body, name=name, in_specs=[anyspec] * na, out_specs=[anyspec] * na,
        out_shape=[_sds(b.shape, b.dtype) for b in bufs], input_output_aliases={a: a for a in range(na)},
        scratch_shapes=[pltpu.SemaphoreType.DMA((n_sems,)), pltpu.SemaphoreType.DMA((n_sems,))],
        compiler_params=pltpu.CompilerParams(has_side_effects=True))(*bufs)


def _same_block_copy(blk, send, recv, k, to):
    return pltpu.make_async_remote_copy(src_ref=blk, dst_ref=blk, send_sem=send.at[k], recv_sem=recv.at[k],
                                        device_id=to, device_id_type=MESH)


def _gather_start(bufs, send, recv):
    x, y, c, chips = _place()
    for a, buf in enumerate(bufs):
        for j, (px, py) in enumerate(chips):
            _same_block_copy(_cols(_slot(buf, 2 * x + y), c), send, recv, 6 * a + j, (px, py, c)).start()


def _gather_finish(bufs, send, recv):
    x, y, c, chips = _place()
    sib = (x, y, 1 - c)
    for a, buf in enumerate(bufs):
        for j, (px, py) in enumerate(chips):
            blk = _cols(_slot(buf, 2 * px + py), c)
            _same_block_copy(blk, send, recv, 6 * a + j, (px, py, c)).wait_recv()
            _same_block_copy(blk, send, recv, 6 * a + 3 + j, sib).start()
    for a, buf in enumerate(bufs):
        for j, (px, py) in enumerate(chips):
            _same_block_copy(_cols(_slot(buf, 2 * px + py), 1 - c), send, recv, 6 * a + 3 + j, sib).wait_recv()
    for a, buf in enumerate(bufs):
        for j, (px, py) in enumerate(chips):
            _same_block_copy(_cols(_slot(buf, 2 * x + y), c), send, recv, 6 * a + j, (px, py, c)).wait_send()
            _same_block_copy(_cols(_slot(buf, 2 * px + py), c), send, recv, 6 * a + 3 + j, sib).wait_send()


def _gather_carry(bufs):
    return _Carry(bufs, range(len(bufs)), 6 * len(bufs), _gather_start, _gather_finish)


def _gather_weights(bufs, name):
    na = len(bufs)

    def body(*refs):
        outs = refs[na:2 * na]
        send, recv = refs[2 * na:]
        _gather_start(outs, send, recv)
        _gather_finish(outs, send, recv)

    return _aliased_comm_call(body, name, bufs, 6 * na)


def _swap_halves(grads, name):
    na = len(grads)

    def body(*refs):
        srcs, outs = refs[:na], refs[na:2 * na]
        send, recv = refs[2 * na:]
        x, y, c, _ = _place()
        sib = (x, y, 1 - c)
        cps = [pltpu.make_async_remote_copy(src_ref=_cols(srcs[a], 1 - c), dst_ref=outs[a], send_sem=send.at[a],
                                            recv_sem=recv.at[a], device_id=sib, device_id_type=MESH) for a in range(na)]
        for cp in cps:
            cp.start()
        for cp in cps:
            cp.wait()

    anyspec = pl.BlockSpec(memory_space=pl.ANY)
    return pl.pallas_call(
        body, name=name, in_specs=[anyspec] * na, out_specs=[anyspec] * na,
        out_shape=[_sds(g.shape[:-1] + (g.shape[-1] // 2,), g.dtype) for g in grads],
        scratch_shapes=[pltpu.SemaphoreType.DMA((na,)), pltpu.SemaphoreType.DMA((na,))],
        compiler_params=pltpu.CompilerParams(has_side_effects=True))(*grads)


def _pair_sum(g, theirs, name):
    nl, _, r, n = g.shape
    hw = n // 2
    tb = _tile(r, 256, SUBLANES_BF16)

    def body(g_ref, t_ref, p_ref, own_ref):
        p_ref[...] = (g_ref[...] + t_ref[...]).astype(BF16)
        own_ref[...] = p_ref[_my_chip()]

    blk = pl.BlockSpec((None, 4, tb, hw), lambda l, i: (l, 0, i, 0))
    out = _sds((nl, 4, r, hw), BF16)
    return _call(body, name, (nl, r // tb),
                 [pl.BlockSpec((None, 4, tb, hw), lambda l, i: (l, 0, i, lax.axis_index("c"))), blk],
                 [blk, pl.BlockSpec((None, None, tb, hw), lambda l, i: (l, _my_chip(), i, 0))],
                 [out, out], sem=("parallel", "parallel"))(g, theirs)


def _scatter_start(bufs, send, recv):
    na = len(bufs) // 2
    x, y, c, chips = _place()
    for a in range(na):
        for j, (px, py) in enumerate(chips):
            pltpu.make_async_remote_copy(src_ref=_slot(bufs[a], 2 * px + py), dst_ref=_slot(bufs[na + a], 2 * x + y),
                                         send_sem=send.at[3 * a + j], recv_sem=recv.at[3 * a + j],
                                         device_id=(px, py, c), device_id_type=MESH).start()


def _scatter_finish(bufs, send, recv):
    na = len(bufs) // 2
    x, y, c, chips = _place()
    for a in range(na):
        for j, (px, py) in enumerate(chips):
            _same_block_copy(_slot(bufs[na + a], 2 * px + py), send, recv, 3 * a + j, (px, py, c)).wait_recv()
    for a in range(na):
        for j, (px, py) in enumerate(chips):
            _same_block_copy(_slot(bufs[a], 2 * px + py), send, recv, 3 * a + j, (px, py, c)).wait_send()


def _scatter_carry(parts, lands):
    na = len(parts)
    return _Carry(list(parts) + list(lands), range(na, 2 * na), 3 * na, _scatter_start, _scatter_finish)


def _scatter_partials(parts, lands, name):
    na = len(parts)

    def body(*refs):
        bufs = list(refs[:na]) + list(refs[2 * na:3 * na])
        send, recv = refs[3 * na:]
        _scatter_start(bufs, send, recv)
        _scatter_finish(bufs, send, recv)

    anyspec = pl.BlockSpec(memory_space=pl.ANY)
    return pl.pallas_call(
        body, name=name, in_specs=[anyspec] * (2 * na), out_specs=[anyspec] * na,
        out_shape=[_sds(b.shape, b.dtype) for b in lands],
        input_output_aliases={na + a: a for a in range(na)},
        scratch_shapes=[pltpu.SemaphoreType.DMA((3 * na,)), pltpu.SemaphoreType.DMA((3 * na,))],
        compiler_params=pltpu.CompilerParams(has_side_effects=True))(*parts, *lands)


def _chip_sum(land, n, name):
    nl, _, r, hw = land.shape
    tb = _tile(r, 512, SUBLANES_BF16)

    def body(i_ref, o_ref):
        o_ref[...] = ((i_ref[0].astype(F32) + i_ref[1].astype(F32)) + i_ref[2].astype(F32)) + i_ref[3].astype(F32)

    return _call(body, name, (nl, r // tb), [pl.BlockSpec((None, 4, tb, hw), lambda l, i: (l, 0, i, 0))],
                 pl.BlockSpec((None, tb, hw), lambda l, i: (l, i, lax.axis_index("c"))),
                 _sds((nl, r, n), F32), sem=("parallel", "parallel"))(land)


def _join_halves(bufs, name):
    na = len(bufs)

    def body(*refs):
        outs = refs[na:2 * na]
        send, recv = refs[2 * na:]
        x, y, c, _ = _place()
        sib = (x, y, 1 - c)
        cps = []
        for a in range(na):
            mine = _cols(outs[a], c)
            cp = pltpu.make_async_remote_copy(src_ref=mine, dst_ref=mine, send_sem=send.at[a], recv_sem=recv.at[a],
                                              device_id=sib, device_id_type=MESH)
            cp.start()
            cps.append(cp)
        for a in range(na):
            oth = _cols(outs[a], 1 - c)
            pltpu.make_async_remote_copy(src_ref=oth, dst_ref=oth, send_sem=send.at[a], recv_sem=recv.at[a],
                                         device_id=sib, device_id_type=MESH).wait_recv()
        for cp in cps:
            cp.wait_send()

    return _aliased_comm_call(body, name, bufs, na)


def _allreduce_small(v, name):
    rows, cols = v.shape

    def body(v_ref, o_ref, buf, send, recv):
        x, y, c, _ = _place()
        me = 4 * x + 2 * y + c
        buf[me] = v_ref[...]
        cps = []
        for d in range(1, 8):
            to = (x ^ (d >> 2), y ^ ((d >> 1) & 1), c ^ (d & 1))
            cp = pltpu.make_async_remote_copy(src_ref=v_ref, dst_ref=buf.at[me], send_sem=send.at[d - 1],
                                              recv_sem=recv.at[d - 1], device_id=to, device_id_type=MESH)
            cp.start()
            cps.append(cp)
        for d in range(1, 8):
            frm = 4 * (x ^ (d >> 2)) + 2 * (y ^ ((d >> 1) & 1)) + (c ^ (d & 1))
            got = buf.at[frm]
            pltpu.make_async_remote_copy(src_ref=got, dst_ref=got, send_sem=send.at[d - 1], recv_sem=recv.at[d - 1],
                                         device_id=(x, y, c), device_id_type=MESH).wait_recv()
        for cp in cps:
            cp.wait_send()
        acc = buf[0]
        for d in range(1, 8):
            acc = acc + buf[d]
        o_ref[...] = acc

    vm = pl.BlockSpec(memory_space=pltpu.VMEM)
    return pl.pallas_call(
        body, name=name, in_specs=[vm], out_specs=vm, out_shape=_sds((rows, cols), F32),
        scratch_shapes=[pltpu.VMEM((8, rows, cols), F32), pltpu.SemaphoreType.DMA((7,)), pltpu.SemaphoreType.DMA((7,))],
        compiler_params=pltpu.CompilerParams(has_side_effects=True, vmem_limit_bytes=V7X_VMEM_LIMIT))(v)


def _pack(arrs, pw):
    rows = []
    for a in arrs:
        f = a.astype(F32).reshape(-1)
        pad = (-f.shape[0]) % pw
        rows.append(jnp.pad(f, (0, pad)).reshape(-1, pw))
    p = jnp.concatenate(rows, axis=0)
    return jnp.pad(p, ((0, (-p.shape[0]) % 8), (0, 0)))


def _unpack(p, shapes, pw):
    out, r0 = [], 0
    for s in shapes:
        size = math.prod(s) if s else 1
        nr = -(-size // pw)
        out.append(p[r0:r0 + nr].reshape(-1)[:size].reshape(s))
        r0 += nr
    return out


def _step(x, mem, positions, wts, loss_target, mom, vel):
    w = dict(zip(WEIGHTS, wts))
    s_len, d = x.shape[1], x.shape[2]
    ml = mem.shape[1]
    depth = w['norm_mix'].shape[0]
    n_a = w['ssd_w_in'].shape[0]
    di = 4 * w['ssd_w_out'].shape[1]
    nh = w['ssd_dt_bias'].shape[1]
    p = di // nh
    gn = SSD_GROUPS
    jh = nh // gn
    cd = 4 * w['ssd_conv_w'].shape[2]
    w3 = 4 * w['dil_w_q'].shape[2]
    dw = w3 // 3
    hd = dw // DIL_HEADS
    mw = w['mem_w_q'].shape[2]
    ff = 4 * w['ffn_w_out'].shape[1]
    cx, cy, cc = lax.axis_index("x"), lax.axis_index("y"), lax.axis_index("c")
    chip = 2 * cx + cy


    big = list(COL_SHARDED + ROW_SHARDED)
    stacked = ('mem_w_q', 'mem_w_kv', 'mem_w_o')

    def mixer_items(i):
        if i < n_a:
            return [('ssd_w_in', i), ('ssd_w_out', i)]
        return ([('w_kv_shared', 0)] if i == n_a else []) + [('dil_w_q', i - n_a), ('dil_w_o', i - n_a)]

    def ffn_items(i):
        return [('ffn_w_in', i), ('ffn_w_out', i)] + ([(n_, None) for n_ in stacked] if i == 0 else [])

    def shard_of(name, l):
        t = w[name] if w[name].ndim == 3 else w[name][None]
        if l is not None:
            t = t[l:l + 1]
        return jnp.swapaxes(t, 1, 2).astype(BF16) if name in COL_SHARDED else t

    full = {n_: [None] * (w[n_].shape[0] if w[n_].ndim == 3 else 1) for n_ in big}

    def placed_bufs(its):
        return [_place_shard(shard_of(n_, l), f"place_{n_}_{l}") for n_, l in its]

    def record(its, bufs):
        for (n_, l), t in zip(its, bufs):
            fl = t.reshape(t.shape[0], 4 * t.shape[2], t.shape[3])
            if l is None:
                full[n_] = [fl[k] for k in range(fl.shape[0])]
            else:
                full[n_][l] = fl[0]

    record(mixer_items(0), _gather_weights(placed_bufs(mixer_items(0)), "gather_weights_first"))

    small_sharded = ('ssd_conv_w', 'ssd_conv_b', 'ssd_norm')

    def placed(name):
        t = w[name]
        wd = t.shape[-1]
        z = jnp.zeros(t.shape[:-1] + (4 * wd,), F32)
        z = lax.dynamic_update_slice_in_dim(z, t, chip * wd, axis=t.ndim - 1)
        return z * (cc == 0).astype(F32)

    sm_shapes = [w[n_].shape[:-1] + (4 * w[n_].shape[-1],) for n_ in small_sharded]
    conv_w, conv_b, ssd_nw = _unpack(_allreduce_small(_pack([placed(n_) for n_ in small_sharded], d), "gather_small"),
                                     sm_shapes, d)

    h = x[0]
    tgt = loss_target[0]
    mem_n = _rmsnorm(mem[0], w['mem_src_norm'], "mem_src_norm_fwd")
    cos2, sin2 = _rope_tables(positions[0].reshape(s_len, 1), hd, "rope_tables")
    a_neg = -jnp.exp(w['ssd_a_log'])
    saved = []
    kv_saved = None
    k2s = v2s = None
    for i in range(depth):
        sv = {'h_mix': h}
        u = _rmsnorm(h, w['norm_mix'][i], f"norm_mix_fwd_{i}")
        sv['u'] = u
        if i < n_a:
            wt = full['ssd_w_in'][i]
            wt_dt = wt[di + cd:]
            z = _mm(u, wt, 'nt', f"ssd_in_z_{i}", m=s_len, n=di, k=d)
            xbc_raw = _mm(u, wt, 'nt', f"ssd_in_xbc_{i}", m=s_len, n=cd, k=d, b_noff=di)
            dt_raw = _mm(u, wt_dt, 'nt', f"ssd_in_dt_{i}", m=s_len, n=nh, k=d)
            xbc = _conv_fwd(xbc_raw, conv_w[i], conv_b[i], f"ssd_conv_fwd_{i}")
            dt = _softplus_fwd(dt_raw, w['ssd_dt_bias'][i], f"ssd_dt_fwd_{i}")
            dtc = dt.reshape(s_len, gn, jh).transpose(1, 0, 2)
            dtr = dt.reshape(s_len, gn, jh).transpose(1, 2, 0)
            a_row = a_neg[i].reshape(gn, 1, jh)
            a_col = a_neg[i].reshape(gn, jh, 1)
            dskip = jnp.repeat(w['ssd_d'][i], p).reshape(gn, 1, jh * p)
            (y, hins), got = _ssd_fwd(xbc, dtc, dtr, a_row, a_col, dskip, f"ssd_scan_fwd_{i}", di=di, p=p,
                                      carry=_gather_carry(placed_bufs(ffn_items(i))))
            record(ffn_items(i), got)
            yn = _gated_norm_fwd(y, z, ssd_nw[i], f"ssd_norm_fwd_{i}")
            h = _mm(yn, full['ssd_w_out'][i], 'nn', f"ssd_out_{i}", m=s_len, n=d, k=di, res=h)
            sv.update(wt=wt, wt_dt=wt_dt, z=z, xbc_raw=xbc_raw, xbc=xbc, dt_raw=dt_raw, dtc=dtc, dtr=dtr, a_row=a_row,
                      a_col=a_col, dskip=dskip, y=y, hins=hins, yn=yn)
        else:
            j = i - n_a
            if j == 0:
                kvn = _rmsnorm(h, w['kv_norm'], "kv_norm_fwd")
                wkv = full['w_kv_shared'][0]
                k_raw = _mm(kvn, wkv, 'nt', "kv_proj_k", m=s_len, n=w3, k=d)
                v_all = _mm(kvn, wkv, 'nt', "kv_proj_v", m=s_len, n=w3, k=d, b_noff=w3, out_dtype=BF16)
                k_all = _rope(k_raw, cos2, sin2, "rope_k", width=w3, sign=1.0, out_dtype=BF16)
                k2s = [k_all.reshape(s_len // dl, dl * w3) for dl in DIL_DILATIONS]
                v2s = [v_all.reshape(s_len // dl, dl * w3) for dl in DIL_DILATIONS]
                kv_saved = {'h': h, 'kvn': kvn, 'wkv': wkv}
            q_raw = _mm(u, full['dil_w_q'][j], 'nt', f"dil_q_{i}", m=s_len, n=w3, k=d)
            q_all = _rope(q_raw, cos2, sin2, f"rope_q_{i}", width=w3, sign=1.0, out_dtype=BF16)
            q2s = [q_all.reshape(s_len // dl, dl * w3) for dl in DIL_DILATIONS]
            os_, ls_, l2s = [], [], []
            for g, dl in enumerate(DIL_DILATIONS):
                if g == 0:
                    (o2, l2), got = _dil_attn_fwd(q2s[g], k2s[g], v2s[g], g, dl, f"dil_attn_fwd_{i}_{g}", dw=dw, dh=hd,
                                                  carry=_gather_carry(placed_bufs(ffn_items(i))))
                    record(ffn_items(i), got)
                else:
                    o2, l2 = _dil_attn_fwd(q2s[g], k2s[g], v2s[g], g, dl, f"dil_attn_fwd_{i}_{g}", dw=dw, dh=hd)
                os_.append(o2.reshape(s_len, dw))
                ls_.append(l2.reshape(s_len, LANES))
                l2s.append(l2)
            om = _mix_fwd(os_, ls_, f"dil_mix_fwd_{i}", dh=hd)
            h = _mm(om, full['dil_w_o'][j], 'nn', f"dil_out_{i}", m=s_len, n=d, k=dw, res=h)
            sv.update(q2s=q2s, os=os_, ls=ls_, l2s=l2s, om=om)
        sv['h_mem'] = h
        u2 = _rmsnorm(h, w['norm_mem'][i], f"norm_mem_fwd_{i}")
        qm = _mm(u2, full['mem_w_q'][i], 'nn', f"mem_q_{i}", m=s_len, n=mw, k=d, out_dtype=BF16)
        kvm = _mm(mem_n, full['mem_w_kv'][i], 'nn', f"mem_kv_{i}", m=ml, n=2 * mw, k=d)
        omem = _mem_attn_fwd(qm, kvm, f"mem_attn_fwd_{i}")
        h = _mm(omem, full['mem_w_o'][i], 'nt', f"mem_out_{i}", m=s_len, n=d, k=mw, res=h)
        sv.update(u2=u2, qm=qm, kvm=kvm, omem=omem, h_ffn=h)
        u3 = _rmsnorm(h, w['norm_ffn'][i], f"norm_ffn_fwd_{i}")
        if i + 1 < depth:
            gu, got = _mm(u3, full['ffn_w_in'][i], 'nt', f"ffn_in_{i}", m=s_len, n=2 * ff, k=d, out_dtype=BF16,
                          carry=_gather_carry(placed_bufs(mixer_items(i + 1))))
            record(mixer_items(i + 1), got)
        else:
            gu = _mm(u3, full['ffn_w_in'][i], 'nt', f"ffn_in_{i}", m=s_len, n=2 * ff, k=d, out_dtype=BF16)
        act = _swiglu_fwd(gu, f"ffn_act_fwd_{i}")
        h = _mm(act, full['ffn_w_out'][i], 'nn', f"ffn_out_{i}", m=s_len, n=d, k=ff, res=h)
        sv.update(u3=u3, gu=gu, act=act)
        saved.append(sv)

    loss_part, dh, dhb, dg_final = _loss_and_grad(h, w['norm_final'], tgt, "loss_and_final_norm")

    gbig = {n_: [None] * len(full[n_]) for n_ in big}
    gsm = {n_: [None] * depth for n_ in ('norm_mix', 'norm_mem', 'norm_ffn')}
    for n_ in ('ssd_conv_w', 'ssd_conv_b', 'ssd_norm', 'ssd_dt_bias', 'ssd_a_log', 'ssd_d'):
        gsm[n_] = [None] * n_a
    dmem_n = None
    dk_acc = dv_acc = None
    gshard = {}

    def rs_prepare(its, tag):
        gl = []
        for n_, l in its:
            t = jnp.stack(gbig[n_]) if l is None else gbig[n_][l][None]
            gl.append(t.reshape(t.shape[0], 4, t.shape[1] // 4, t.shape[2]))
        theirs = _swap_halves(gl, f"grad_swap_halves_{tag}")
        pairs = [_pair_sum(t, o, f"grad_pair_sum_{n_}_{l}") for (n_, l), t, o in zip(its, gl, theirs)]
        return its, [pr[0] for pr in pairs], [pr[1] for pr in pairs], [t.shape[3] for t in gl], tag

    def rs_finish(pend, lands):
        its, _, _, widths, tag = pend
        fins = [_chip_sum(ld, nw_, f"grad_chip_sum_{n_}_{l}") for (n_, l), ld, nw_ in zip(its, lands, widths)]
        for it, t in zip(its, _join_halves(fins, f"grad_join_halves_{tag}")):
            gshard[it] = t

    pend_mixer = None
    for i in reversed(range(depth)):
        sv = saved[i]
        dact = _mm(dhb, full['ffn_w_out'][i], 'nt', f"ffn_out_bwd_x_{i}", m=s_len, n=ff, k=d, out_dtype=BF16)
        gbig['ffn_w_out'][i] = _mm(sv['act'], dhb, 'tn', f"ffn_out_bwd_w_{i}", m=ff, n=d, k=s_len)
        dgu = _swiglu_bwd(sv['gu'], dact, f"ffn_act_bwd_{i}")
        if pend_mixer is None:
            gbig['ffn_w_in'][i] = _mm(dgu, sv['u3'], 'tn', f"ffn_in_bwd_w_{i}", m=2 * ff, n=d, k=s_len)
        else:
            gbig['ffn_w_in'][i], got = _mm(dgu, sv['u3'], 'tn', f"ffn_in_bwd_w_{i}", m=2 * ff, n=d, k=s_len,
                                           carry=_scatter_carry(pend_mixer[1], pend_mixer[2]))
            rs_finish(pend_mixer, got)
        du = _mm(dgu, full['ffn_w_in'][i], 'nn', f"ffn_in_bwd_x_{i}", m=s_len, n=d, k=2 * ff)
        dh, dhb, gsm['norm_ffn'][i] = _rmsnorm_bwd(sv['h_ffn'], w['norm_ffn'][i], du, dh, f"norm_ffn_bwd_{i}")
        do = _mm(dhb, full['mem_w_o'][i], 'nn', f"mem_out_bwd_x_{i}", m=s_len, n=mw, k=d, out_dtype=BF16)
        gbig['mem_w_o'][i] = _mm(dhb, sv['omem'], 'tn', f"mem_out_bwd_w_{i}", m=d, n=mw, k=s_len)
        dqm, dkvm = _mem_attn_bwd(sv['qm'], sv['kvm'], do, f"mem_attn_bwd_{i}")
        gbig['mem_w_q'][i] = _mm(sv['u2'], dqm, 'tn', f"mem_q_bwd_w_{i}", m=d, n=mw, k=s_len)
        du = _mm(dqm, full['mem_w_q'][i], 'nt', f"mem_q_bwd_x_{i}", m=s_len, n=d, k=mw)
        gbig['mem_w_kv'][i] = _mm(mem_n, dkvm, 'tn', f"mem_kv_bwd_w_{i}", m=d, n=2 * mw, k=ml)
        dmem_n = _mm(dkvm, full['mem_w_kv'][i], 'nt', f"mem_kv_bwd_x_{i}", m=ml, n=d, k=2 * mw, res=dmem_n)
        dh, dhb, gsm['norm_mem'][i] = _rmsnorm_bwd(sv['h_mem'], w['norm_mem'][i], du, dh, f"norm_mem_bwd_{i}")
        pend_ffn = rs_prepare(ffn_items(i), f"ffn_{i}")
        ffn_scatter = _scatter_carry(pend_ffn[1], pend_ffn[2])
        if i >= n_a:
            j = i - n_a
            dom = _mm(dhb, full['dil_w_o'][j], 'nt', f"dil_out_bwd_x_{i}", m=s_len, n=dw, k=d)
            gbig['dil_w_o'][j] = _mm(sv['om'], dhb, 'tn', f"dil_out_bwd_w_{i}", m=dw, n=d, k=s_len)
            mb = _mix_bwd(sv['os'], sv['ls'], dom, f"dil_mix_bwd_{i}", dh=hd)
            dqs, dks, dvs = [], [], []
            for g, dl in enumerate(DIL_DILATIONS):
                do2 = mb[g].reshape(s_len // dl, dl * dw)
                dl2 = mb[3 + g].reshape(s_len // dl, dl * LANES)
                if g == 0:
                    (dq2, dk2, dv2), got = _dil_attn_bwd(sv['q2s'][g], k2s[g], v2s[g], do2, sv['l2s'][g], dl2, g, dl,
                                                         f"dil_attn_bwd_{i}_{g}", dw=dw, dh=hd, carry=ffn_scatter)
                    rs_finish(pend_ffn, got)
                else:
                    dq2, dk2, dv2 = _dil_attn_bwd(sv['q2s'][g], k2s[g], v2s[g], do2, sv['l2s'][g], dl2, g, dl,
                                                  f"dil_attn_bwd_{i}_{g}", dw=dw, dh=hd)
                dqs.append(dq2.reshape(s_len, dw))
                dks.append(dk2.reshape(s_len, dw))
                dvs.append(dv2.reshape(s_len, dw))
            dq_r = jnp.concatenate(dqs, axis=1)
            dk_r = jnp.concatenate(dks, axis=1)
            dv_r = jnp.concatenate(dvs, axis=1)
            dk_acc = dk_r if dk_acc is None else dk_acc + dk_r
            dv_acc = dv_r if dv_acc is None else dv_acc + dv_r
            dq_raw = _rope(dq_r, cos2, sin2, f"rope_q_bwd_{i}", width=w3, sign=-1.0, out_dtype=BF16)
            gbig['dil_w_q'][j] = _mm(dq_raw, sv['u'], 'tn', f"dil_q_bwd_w_{i}", m=w3, n=d, k=s_len)
            du = _mm(dq_raw, full['dil_w_q'][j], 'nn', f"dil_q_bwd_x_{i}", m=s_len, n=d, k=w3)
            dh, dhb, gsm['norm_mix'][i] = _rmsnorm_bwd(sv['h_mix'], w['norm_mix'][i], du, dh, f"norm_mix_bwd_{i}")
            if j == 0:
                dk_raw = _rope(dk_acc, cos2, sin2, "rope_k_bwd", width=w3, sign=-1.0, out_dtype=BF16)
                dkv = jnp.concatenate([dk_raw, dv_acc.astype(BF16)], axis=1)
                gbig['w_kv_shared'][0] = _mm(dkv, kv_saved['kvn'], 'tn', "kv_proj_bwd_w", m=2 * w3, n=d, k=s_len)
                du = _mm(dkv, kv_saved['wkv'], 'nn', "kv_proj_bwd_x", m=s_len, n=d, k=2 * w3)
                dh, dhb, dg_kv = _rmsnorm_bwd(kv_saved['h'], w['kv_norm'], du, dh, "kv_norm_bwd")
        else:
            dyn = _mm(dhb, full['ssd_w_out'][i], 'nt', f"ssd_out_bwd_x_{i}", m=s_len, n=di, k=d)
            gbig['ssd_w_out'][i] = _mm(sv['yn'], dhb, 'tn', f"ssd_out_bwd_w_{i}", m=di, n=d, k=s_len)
            dy, dz, gsm['ssd_norm'][i] = _gated_norm_bwd(sv['y'], sv['z'], ssd_nw[i], dyn, f"ssd_norm_bwd_{i}")
            (dx, db_, dc_, ddtc, ddtr, dac, dar, ddl), got = _ssd_bwd(
                sv['xbc'], sv['dtc'], sv['dtr'], sv['a_row'], sv['a_col'], sv['dskip'], sv['hins'], dy,
                f"ssd_scan_bwd_{i}", di=di, p=p, carry=ffn_scatter)
            rs_finish(pend_ffn, got)
            dxbc = jnp.concatenate([dx, db_, dc_], axis=1)
            ddt = ddtc.transpose(1, 0, 2).reshape(s_len, nh) + ddtr.transpose(2, 0, 1).reshape(s_len, nh)
            gsm['ssd_a_log'][i] = (dac.reshape(nh) + dar.reshape(nh)) * a_neg[i]
            gsm['ssd_d'][i] = ddl.reshape(nh, p).sum(axis=1)
            ddt_raw, dbias = _softplus_bwd(sv['dt_raw'], w['ssd_dt_bias'][i], ddt, f"ssd_dt_bwd_{i}")
            gsm['ssd_dt_bias'][i] = dbias.reshape(nh)
            dpre, dcw, dcb = _conv_bwd_pre(sv['xbc_raw'], conv_w[i], conv_b[i], dxbc, f"ssd_conv_bwd_pre_{i}")
            gsm['ssd_conv_w'][i], gsm['ssd_conv_b'][i] = dcw, dcb.reshape(cd)
            dxbc_raw = _conv_bwd_in(dpre, conv_w[i], f"ssd_conv_bwd_in_{i}")
            gbig['ssd_w_in'][i] = jnp.concatenate([
                _mm(dz, sv['u'], 'tn', f"ssd_in_bwd_w_z_{i}", m=di, n=d, k=s_len),
                _mm(dxbc_raw, sv['u'], 'tn', f"ssd_in_bwd_w_xbc_{i}", m=cd, n=d, k=s_len),
                _mm(ddt_raw, sv['u'], 'tn', f"ssd_in_bwd_w_dt_{i}", m=nh, n=d, k=s_len)], axis=0)
            du = _mm(dz, sv['wt'], 'nn', f"ssd_in_bwd_x_z_{i}", m=s_len, n=d, k=di)
            du = _mm(dxbc_raw, sv['wt'], 'nn', f"ssd_in_bwd_x_xbc_{i}", m=s_len, n=d, k=cd, b_koff=di, res=du)
            du = _mm(ddt_raw, sv['wt_dt'], 'nn', f"ssd_in_bwd_x_dt_{i}", m=s_len, n=d, k=nh, res=du)
            dh, dhb, gsm['norm_mix'][i] = _rmsnorm_bwd(sv['h_mix'], w['norm_mix'][i], du, dh, f"norm_mix_bwd_{i}")
        pend_mixer = rs_prepare(mixer_items(i), f"mixer_{i}")
    rs_finish(pend_mixer, _scatter_partials(pend_mixer[1], pend_mixer[2], "grad_scatter_last"))
    grad_x = dh[None]
    _, dg_src = _rmsnorm_bwd_noacc(mem[0], w['mem_src_norm'], dmem_n, "mem_src_norm_bwd")

    grads_big = {}
    for n_ in big:
        t = gshard[(n_, None)] if n_ in stacked else jnp.concatenate([gshard[(n_, l)] for l in range(len(full[n_]))])
        if n_ in COL_SHARDED:
            t = jnp.swapaxes(t, 1, 2)
        grads_big[n_] = t.reshape(w[n_].shape)

    sm_names = ['norm_mix', 'norm_mem', 'norm_ffn', 'ssd_conv_w', 'ssd_conv_b', 'ssd_norm', 'ssd_dt_bias', 'ssd_a_log',
                'ssd_d']
    sm_arrs = [jnp.stack([t.reshape(t.shape[-1]) if n_.startswith('norm') else t for t in gsm[n_]]) for n_ in sm_names]
    sm_names += ['norm_final', 'kv_norm', 'mem_src_norm', 'loss']
    sm_arrs += [dg_final.reshape(d), dg_kv.reshape(d), dg_src.reshape(d), loss_part[0, :1]]
    summed = _unpack(_allreduce_small(_pack(sm_arrs, d), "reduce_small"), [t.shape for t in sm_arrs], d)
    gs = dict(zip(sm_names, summed))
    loss = gs.pop('loss').reshape(())
    grads = dict(grads_big)
    for n_, t in gs.items():
        if n_ in small_sharded:
            wd = w[n_].shape[-1]
            t = lax.dynamic_slice_in_dim(t, chip * wd, wd, axis=t.ndim - 1)
        grads[n_] = t.reshape(w[n_].shape)

    deltas, new_m, new_v = [], [], []
    for n_, m_, v_ in zip(WEIGHTS, mom, vel):
        dlt, nm, nv = _adamw(w[n_], grads[n_], m_, v_, f"adamw_{n_}")
        deltas.append(dlt)
        new_m.append(nm)
        new_v.append(nv)
    return (loss, grad_x, *[grads[n_] for n_ in WEIGHTS], *deltas, *new_m, *new_v)


def kernel(x, mem, positions, norm_mix, norm_mem, norm_ffn, norm_final, ssd_w_in, ssd_conv_w, ssd_conv_b, ssd_dt_bias, ssd_a_log, ssd_d, ssd_norm, ssd_w_out, kv_norm, w_kv_shared, dil_w_q, dil_w_o, mem_src_norm, mem_w_q, mem_w_kv, mem_w_o, ffn_w_in, ffn_w_out, loss_target, m_norm_mix, m_norm_mem, m_norm_ffn, m_norm_final, m_ssd_w_in, m_ssd_conv_w, m_ssd_conv_b, m_ssd_dt_bias, m_ssd_a_log, m_ssd_d, m_ssd_norm, m_ssd_w_out, m_kv_norm, m_w_kv_shared, m_dil_w_q, m_dil_w_o, m_mem_src_norm, m_mem_w_q, m_mem_w_kv, m_mem_w_o, m_ffn_w_in, m_ffn_w_out, v_norm_mix, v_norm_mem, v_norm_ffn, v_norm_final, v_ssd_w_in, v_ssd_conv_w, v_ssd_conv_b, v_ssd_dt_bias, v_ssd_a_log, v_ssd_d, v_ssd_norm, v_ssd_w_out, v_kv_norm, v_w_kv_shared, v_dil_w_q, v_dil_w_o, v_mem_src_norm, v_mem_w_q, v_mem_w_kv, v_mem_w_o, v_ffn_w_in, v_ffn_w_out):
    wts = (norm_mix, norm_mem, norm_ffn, norm_final, ssd_w_in, ssd_conv_w, ssd_conv_b, ssd_dt_bias, ssd_a_log, ssd_d, ssd_norm, ssd_w_out, kv_norm, w_kv_shared, dil_w_q, dil_w_o, mem_src_norm, mem_w_q, mem_w_kv, mem_w_o, ffn_w_in, ffn_w_out)
    mom = (m_norm_mix, m_norm_mem, m_norm_ffn, m_norm_final, m_ssd_w_in, m_ssd_conv_w, m_ssd_conv_b, m_ssd_dt_bias, m_ssd_a_log, m_ssd_d, m_ssd_norm, m_ssd_w_out, m_kv_norm, m_w_kv_shared, m_dil_w_q, m_dil_w_o, m_mem_src_norm, m_mem_w_q, m_mem_w_kv, m_mem_w_o, m_ffn_w_in, m_ffn_w_out)
    vel = (v_norm_mix, v_norm_mem, v_norm_ffn, v_norm_final, v_ssd_w_in, v_ssd_conv_w, v_ssd_conv_b, v_ssd_dt_bias, v_ssd_a_log, v_ssd_d, v_ssd_norm, v_ssd_w_out, v_kv_norm, v_w_kv_shared, v_dil_w_q, v_dil_w_o, v_mem_src_norm, v_mem_w_q, v_mem_w_kv, v_mem_w_o, v_ffn_w_in, v_ffn_w_out)
    return _step(x, mem, positions, wts, loss_target, mom, vel)
```

```python
import functools
import math

import jax
import jax.numpy as jnp
from jax import lax
from jax.experimental import pallas as pl
from jax.experimental.pallas import tpu as pltpu

F32 = jnp.float32
BF16 = jnp.bfloat16
MESH = pl.DeviceIdType.MESH
HIGHEST = lax.Precision.HIGHEST

NORM_EPS = 1e-6
SSD_GROUPS = 8
SSD_STATE = 128
SSD_CHUNK = 128
SSD_CONV = 4
DIL_DILATIONS = (1, 4, 16)
DIL_HEADS = 16
DIL_BLOCK = 128
ROPE_THETA = 10000.0
MEM_HEADS = 4
ADAM_LR, ADAM_B1, ADAM_B2, ADAM_EPS, ADAM_WD, ADAM_STEP = 0.001, 0.9, 0.999, 1e-08, 0.01, 10

V7X_VMEM_LIMIT = 48 * 1024 * 1024
LANES = 128
SUBLANES_BF16 = 16

WEIGHTS = ['norm_mix', 'norm_mem', 'norm_ffn', 'norm_final', 'ssd_w_in', 'ssd_conv_w', 'ssd_conv_b',
           'ssd_dt_bias', 'ssd_a_log', 'ssd_d', 'ssd_norm', 'ssd_w_out', 'kv_norm', 'w_kv_shared', 'dil_w_q',
           'dil_w_o', 'mem_src_norm', 'mem_w_q', 'mem_w_kv', 'mem_w_o', 'ffn_w_in', 'ffn_w_out']
COL_SHARDED = ('ssd_w_in', 'w_kv_shared', 'dil_w_q', 'mem_w_o', 'ffn_w_in')
ROW_SHARDED = ('ssd_w_out', 'dil_w_o', 'mem_w_q', 'mem_w_kv', 'ffn_w_out')


def _dot(a, b, ca, cb, prec=None):
    return lax.dot_general(a, b, (((ca,), (cb,)), ((), ())), preferred_element_type=F32, precision=prec)


def _nn(a, b, prec=None):
    return _dot(a, b, 1, 0, prec)


def _nt(a, b, prec=None):
    return _dot(a, b, 1, 1, prec)


def _tn(a, b, prec=None):
    return _dot(a, b, 0, 0, prec)


def _tile(n, pref, unit=LANES):
    if n <= pref:
        return n
    t = (pref // unit) * unit
    while t >= unit:
        if n % t == 0:
            return t
        t -= unit
    return n


class _Carry:
    def __init__(self, bufs, outs, n_sems, start, finish):
        self.bufs, self.outs, self.n_sems, self.start, self.finish = list(bufs), list(outs), n_sems, start, finish


def _call(body, name, grid, in_specs, out_specs, out_shape, scratch=(), sem=None, carry=None):
    if carry is None:
        return pl.pallas_call(
            body, name=name, grid=grid, in_specs=in_specs, out_specs=out_specs, out_shape=out_shape,
            scratch_shapes=list(scratch),
            compiler_params=pltpu.CompilerParams(dimension_semantics=sem, vmem_limit_bytes=V7X_VMEM_LIMIT))
    single = not isinstance(out_specs, (list, tuple))
    o_specs = [out_specs] if single else list(out_specs)
    o_shape = [out_shape] if single else list(out_shape)
    n_in, n_out, n_cb, n_co, n_scr = len(in_specs), len(o_specs), len(carry.bufs), len(carry.outs), len(scratch)
    anyspec = pl.BlockSpec(memory_space=pl.ANY)

    def riding(*refs):
        ins, cbufs = refs[:n_in], list(refs[n_in:n_in + n_cb])
        outs = refs[n_in + n_cb:n_in + n_cb + n_out]
        for k, b in enumerate(carry.outs):
            cbufs[b] = refs[n_in + n_cb + n_out + k]
        rest = refs[n_in + n_cb + n_out + n_co:]
        send, recv = rest[n_scr], rest[n_scr + 1]
        ids = [pl.program_id(a) for a in range(len(grid))]
        first = functools.reduce(lambda p, q: p & q, [i == 0 for i in ids])
        last = functools.reduce(lambda p, q: p & q, [i == g - 1 for i, g in zip(ids, grid)])

        @pl.when(first)
        def _():
            carry.start(cbufs, send, recv)

        body(*ins, *outs, *rest[:n_scr])

        @pl.when(last)
        def _():
            carry.finish(cbufs, send, recv)

    call = pl.pallas_call(
        riding, name=name, grid=grid, in_specs=list(in_specs) + [anyspec] * n_cb, out_specs=o_specs + [anyspec] * n_co,
        out_shape=o_shape + [_sds(carry.bufs[b].shape, carry.bufs[b].dtype) for b in carry.outs],
        scratch_shapes=list(scratch) + [pltpu.SemaphoreType.DMA((carry.n_sems,)), pltpu.SemaphoreType.DMA((carry.n_sems,))],
        input_output_aliases={n_in + b: n_out + k for k, b in enumerate(carry.outs)},
        compiler_params=pltpu.CompilerParams(dimension_semantics=("arbitrary",) * len(grid), has_side_effects=True,
                                             vmem_limit_bytes=V7X_VMEM_LIMIT))

    def run(*operands):
        res = call(*operands, *carry.bufs)
        own = res[0] if single else list(res[:n_out])
        return own, list(res[n_out:])

    return run


def _sds(shape, dtype):
    return jax.ShapeDtypeStruct(tuple(shape), dtype)


def _silu(x):
    return x * jax.nn.sigmoid(x)


def _dsilu(x):
    s = jax.nn.sigmoid(x)
    return s * (1.0 + x * (1.0 - s))


def _mm(a, b, mode, name, *, m, n, k, out_dtype=F32, res=None, b_noff=0, b_koff=0, carry=None):
    has_res = res is not None
    m_unit = LANES if mode == 'tn' else SUBLANES_BF16
    n_lim = math.gcd(n, b_noff) if b_noff else n
    k_lim = math.gcd(k, b_koff) if b_koff else k
    wide = out_dtype == F32
    if k <= 2048:
        tk = k
        tm = _tile(m, 2048, m_unit)
        tn = _tile(n_lim, 512 if (wide or has_res) else 1024)
    else:
        tk = _tile(k_lim, 1024)
        tm = _tile(m, 1024, m_unit)
        tn = _tile(n_lim, 2048 if (wide and not has_res) else 1024)
    nk = k // tk
    use_acc = nk > 1 and not wide
    jo, ko = b_noff // tn, b_koff // tk
    if mode == 'nn':
        a_spec = pl.BlockSpec((tm, tk), lambda i, j, kk: (i, kk))
        b_spec = pl.BlockSpec((tk, tn), lambda i, j, kk: (kk + ko, j + jo))
        ca, cb = 1, 0
    elif mode == 'nt':
        a_spec = pl.BlockSpec((tm, tk), lambda i, j, kk: (i, kk))
        b_spec = pl.BlockSpec((tn, tk), lambda i, j, kk: (j + jo, kk + ko))
        ca, cb = 1, 1
    else:
        a_spec = pl.BlockSpec((tk, tm), lambda i, j, kk: (kk, i))
        b_spec = pl.BlockSpec((tk, tn), lambda i, j, kk: (kk + ko, j + jo))
        ca, cb = 0, 0
    o_spec = pl.BlockSpec((tm, tn), lambda i, j, kk: (i, j))

    def body(*refs):
        a_ref, b_ref = refs[0], refs[1]
        r_ref = refs[2] if has_res else None
        o_ref = refs[3] if has_res else refs[2]
        acc = refs[-1] if use_acc else None

        def prod():
            return _dot(a_ref[...].astype(BF16), b_ref[...].astype(BF16), ca, cb)

        def with_res(v):
            return v + r_ref[...].astype(F32) if has_res else v

        if nk == 1:
            o_ref[...] = with_res(prod()).astype(o_ref.dtype)
            return
        kk = pl.program_id(2)
        if use_acc:
            @pl.when(kk == 0)
            def _():
                acc[...] = prod()

            @pl.when((kk > 0) & (kk < nk - 1))
            def _():
                acc[...] += prod()

            @pl.when(kk == nk - 1)
            def _():
                o_ref[...] = with_res(acc[...] + prod()).astype(o_ref.dtype)
        else:
            @pl.when(kk == 0)
            def _():
                o_ref[...] = with_res(prod())

            @pl.when(kk > 0)
            def _():
                o_ref[...] += prod()

    ins = [a, b] + ([res] if has_res else [])
    specs = [a_spec, b_spec] + ([o_spec] if has_res else [])
    return _call(body, name, (m // tm, n // tn, nk), specs, o_spec, _sds((m, n), out_dtype),
                 scratch=[pltpu.VMEM((tm, tn), F32)] if use_acc else [], sem=("parallel", "parallel", "arbitrary"),
                 carry=carry)(*ins)


def _rowwise(fn, name, rows, consts, outs, tb, n_rows):
    n_r, n_c = len(rows), len(consts)
    in_specs = [pl.BlockSpec((tb, w), functools.partial(lambda i, cb: (i, cb), cb=cb)) for _, w, cb in rows]
    in_specs += [pl.BlockSpec(c.shape, functools.partial(lambda i, nd: (0,) * nd, nd=c.ndim)) for c in consts]
    out_specs, out_shape = [], []
    for kind, w, dt in outs:
        if kind == 'row':
            out_specs.append(pl.BlockSpec((tb, w), lambda i: (i, 0)))
            out_shape.append(_sds((n_rows, w), dt))
        else:
            out_specs.append(pl.BlockSpec(w, lambda i: (0, 0)))
            out_shape.append(_sds(w, dt))

    def body(*refs):
        ins = [r[...] for r in refs[:n_r + n_c]]
        orefs = refs[n_r + n_c:]
        vals = fn(*ins)
        i = pl.program_id(0)
        for (kind, _, _), o_ref, v in zip(outs, orefs, vals):
            if kind == 'row':
                o_ref[...] = v.astype(o_ref.dtype)
            else:
                @pl.when(i == 0)
                def _(o_ref=o_ref):
                    o_ref[...] = jnp.zeros_like(o_ref)

                o_ref[...] += v.astype(o_ref.dtype)

    res = _call(body, name, (n_rows // tb,), in_specs, out_specs, out_shape, sem=("arbitrary",))(
        *[r[0] for r in rows], *consts)
    return res


def _rms_fwd_fn(h, g):
    r = lax.rsqrt(jnp.mean(h * h, axis=-1, keepdims=True) + NORM_EPS)
    return (h * r * g,)


def _rms_bwd_vals(h, g, dy):
    r = lax.rsqrt(jnp.mean(h * h, axis=-1, keepdims=True) + NORM_EPS)
    t = dy * g
    dh = r * t - h * (r * r * r) * jnp.mean(h * t, axis=-1, keepdims=True)
    dg = jnp.sum(dy * h * r, axis=0, keepdims=True)
    return dh, dg


def _rmsnorm(h, g, name):
    n, d = h.shape
    return _rowwise(_rms_fwd_fn, name, [(h, d, 0)], [g.reshape(1, d)], [('row', d, BF16)], _tile(n, 512, 8), n)[0]


def _rmsnorm_bwd(h, g, du, dres, name):
    n, d = h.shape

    def fn(hv, duv, drv, gv):
        dh, dg = _rms_bwd_vals(hv, gv, duv.astype(F32))
        return dh + drv, dh + drv, dg

    return _rowwise(fn, name, [(h, d, 0), (du, d, 0), (dres, d, 0)], [g.reshape(1, d)],
                    [('row', d, F32), ('row', d, BF16), ('acc', (1, d), F32)], _tile(n, 256, 8), n)


def _rmsnorm_bwd_noacc(h, g, du, name):
    n, d = h.shape

    def fn(hv, duv, gv):
        return _rms_bwd_vals(hv, gv, duv.astype(F32))

    return _rowwise(fn, name, [(h, d, 0), (du, d, 0)], [g.reshape(1, d)],
                    [('row', d, F32), ('acc', (1, d), F32)], _tile(n, 256, 8), n)


def _swiglu_fwd(gu, name):
    n, f2 = gu.shape
    f = f2 // 2

    def fn(v):
        v = v.astype(F32)
        return (_silu(v[:, :f]) * v[:, f:],)

    return _rowwise(fn, name, [(gu, f2, 0)], [], [('row', f, BF16)], _tile(n, 256, 16), n)[0]


def _swiglu_bwd(gu, dact, name):
    n, f2 = gu.shape
    f = f2 // 2

    def fn(v, da):
        v = v.astype(F32)
        g, up = v[:, :f], v[:, f:]
        da = da.astype(F32)
        return (jnp.concatenate([da * up * _dsilu(g), da * _silu(g)], axis=1),)

    return _rowwise(fn, name, [(gu, f2, 0), (dact, f, 0)], [], [('row', f2, BF16)], _tile(n, 256, 16), n)[0]


def _group_sum(v, ng):
    w = v.shape[1] // ng
    return [jnp.sum(v[:, i * w:(i + 1) * w], axis=1, keepdims=True) for i in range(ng)]


def _gated_norm_fwd(y, z, nw, name):
    n, di = y.shape
    gw = di // SSD_GROUPS

    def fn(yv, zv, nwv):
        a = yv * _silu(zv)
        ms = _group_sum(a * a, SSD_GROUPS)
        out = jnp.concatenate([a[:, i * gw:(i + 1) * gw] * lax.rsqrt(ms[i] / gw + NORM_EPS)
                               for i in range(SSD_GROUPS)], axis=1)
        return (out * nwv,)

    return _rowwise(fn, name, [(y, di, 0), (z, di, 0)], [nw.reshape(1, di)], [('row', di, BF16)], _tile(n, 256, 8), n)[0]


def _gated_norm_bwd(y, z, nw, dout, name):
    n, di = y.shape
    gw = di // SSD_GROUPS

    def fn(yv, zv, dov, nwv):
        sz = _silu(zv)
        a = yv * sz
        t = dov * nwv
        ms = _group_sum(a * a, SSD_GROUPS)
        at = _group_sum(a * t, SSD_GROUPS)
        das, ars = [], []
        for i in range(SSD_GROUPS):
            r = lax.rsqrt(ms[i] / gw + NORM_EPS)
            sl = slice(i * gw, (i + 1) * gw)
            das.append(r * t[:, sl] - a[:, sl] * (r * r * r) * (at[i] / gw))
            ars.append(a[:, sl] * r)
        da = jnp.concatenate(das, axis=1)
        ar = jnp.concatenate(ars, axis=1)
        return da * sz, da * yv * _dsilu(zv), jnp.sum(dov * ar, axis=0, keepdims=True)

    return _rowwise(fn, name, [(y, di, 0), (z, di, 0), (dout, di, 0)], [nw.reshape(1, di)],
                    [('row', di, F32), ('row', di, BF16), ('acc', (1, di), F32)], _tile(n, 128, 8), n)


def _softplus_fwd(raw, bias, name):
    n, h = raw.shape

    def fn(v, b):
        t = v + b
        return (jnp.maximum(t, 0.0) + jnp.log(1.0 + jnp.exp(-jnp.abs(t))),)

    return _rowwise(fn, name, [(raw, h, 0)], [bias.reshape(1, h)], [('row', h, F32)], _tile(n, 1024, 8), n)[0]


def _softplus_bwd(raw, bias, ddt, name):
    n, h = raw.shape

    def fn(v, d, b):
        g = d * jax.nn.sigmoid(v + b)
        return g, jnp.sum(g, axis=0, keepdims=True)

    return _rowwise(fn, name, [(raw, h, 0), (ddt, h, 0)], [bias.reshape(1, h)],
                    [('row', h, BF16), ('acc', (1, h), F32)], _tile(n, 1024, 8), n)


def _loss_and_grad(h, g, tgt, name):
    n, d = h.shape

    def fn(hv, tv, gv):
        y = _rms_fwd_fn(hv, gv)[0]
        err = y - tv
        part = 0.5 * jnp.sum(jnp.sum(err * err, axis=1, keepdims=True), axis=0, keepdims=True) / d
        dh, dg = _rms_bwd_vals(hv, gv, err / d)
        return jnp.broadcast_to(part, (8, LANES)), dh, dh, dg

    return _rowwise(fn, name, [(h, d, 0), (tgt, d, 0)], [g.reshape(1, d)],
                    [('acc', (8, LANES), F32), ('row', d, F32), ('row', d, BF16), ('acc', (1, d), F32)],
                    _tile(n, 256, 8), n)


def _rope_tables(pos_col, dh, name):
    n = pos_col.shape[0]
    half = dh // 2
    inv = ROPE_THETA ** (-jnp.arange(half, dtype=F32) / half)
    inv2 = jnp.concatenate([inv, inv]).reshape(1, dh)
    sign = jnp.concatenate([-jnp.ones((half,), F32), jnp.ones((half,), F32)]).reshape(1, dh)

    def fn(p, iv, sg):
        ang = p.astype(F32) * iv
        return jnp.cos(ang), jnp.sin(ang) * sg

    return _rowwise(fn, name, [(pos_col, 1, 0)], [inv2, sign], [('row', dh, F32), ('row', dh, F32)], _tile(n, 1024, 8), n)


def _conv_taps(ext, w, tb):
    shifted = [ext[8:] if k == SSD_CONV - 1 else pltpu.roll(ext, SSD_CONV - 1 - k, 0)[8:] for k in range(SSD_CONV)]
    pre = shifted[0] * w[0:1]
    for k in range(1, SSD_CONV):
        pre = pre + shifted[k] * w[k:k + 1]
    return pre, shifted


def _conv_specs(n, c, tb, tc):
    hb = tb // 8
    blk = pl.BlockSpec((tb, tc), lambda j, i: (i, j))
    halo = pl.BlockSpec((8, tc), lambda j, i: (jnp.maximum(i * hb - 1, 0), j))
    wsp = pl.BlockSpec((SSD_CONV, tc), lambda j, i: (0, j))
    bsp = pl.BlockSpec((1, tc), lambda j, i: (0, j))
    return blk, halo, wsp, bsp


def _conv_fwd(u, w, b, name):
    n, c = u.shape
    tb, tc = _tile(n, 512, 8), _tile(c, 1536)
    blk, halo, wsp, bsp = _conv_specs(n, c, tb, tc)

    def body(u_ref, h_ref, w_ref, b_ref, o_ref):
        halo_v = jnp.where(pl.program_id(1) > 0, h_ref[...], 0.0)
        pre, _ = _conv_taps(jnp.concatenate([halo_v, u_ref[...]], axis=0), w_ref[...], tb)
        o_ref[...] = _silu(pre + b_ref[...])

    return _call(body, name, (c // tc, n // tb), [blk, halo, wsp, bsp], blk, _sds((n, c), F32),
                 sem=("parallel", "arbitrary"))(u, u, w, b.reshape(1, c))


def _conv_bwd_pre(u, w, b, dout, name):
    n, c = u.shape
    tb, tc = _tile(n, 512, 8), _tile(c, 1536)
    blk, halo, wsp, bsp = _conv_specs(n, c, tb, tc)

    def body(u_ref, h_ref, w_ref, b_ref, d_ref, dp_ref, dw_ref, db_ref):
        i = pl.program_id(1)
        halo_v = jnp.where(i > 0, h_ref[...], 0.0)
        pre, shifted = _conv_taps(jnp.concatenate([halo_v, u_ref[...]], axis=0), w_ref[...], tb)
        dp = d_ref[...] * _dsilu(pre + b_ref[...])
        dp_ref[...] = dp

        @pl.when(i == 0)
        def _():
            dw_ref[...] = jnp.zeros_like(dw_ref)
            db_ref[...] = jnp.zeros_like(db_ref)

        dw_ref[...] += jnp.concatenate([jnp.sum(dp * s, axis=0, keepdims=True) for s in shifted], axis=0)
        db_ref[...] += jnp.sum(dp, axis=0, keepdims=True)

    return _call(body, name, (c // tc, n // tb), [blk, halo, wsp, bsp, blk], [blk, wsp, bsp],
                 [_sds((n, c), F32), _sds((SSD_CONV, c), F32), _sds((1, c), F32)],
                 sem=("parallel", "arbitrary"))(u, u, w, b.reshape(1, c), dout)


def _conv_bwd_in(dpre, w, name):
    n, c = dpre.shape
    tb, tc = _tile(n, 512, 8), _tile(c, 1536)
    hb = tb // 8
    nb = n // tb
    blk = pl.BlockSpec((tb, tc), lambda j, i: (i, j))
    nxt = pl.BlockSpec((8, tc), lambda j, i: (jnp.minimum((i + 1) * hb, n // 8 - 1), j))
    wsp = pl.BlockSpec((SSD_CONV, tc), lambda j, i: (0, j))

    def body(d_ref, n_ref, w_ref, o_ref):
        nxt_v = jnp.where(pl.program_id(1) < nb - 1, n_ref[...], 0.0)
        ext = jnp.concatenate([d_ref[...], nxt_v], axis=0)
        wv = w_ref[...]
        acc = ext[:tb] * wv[SSD_CONV - 1:SSD_CONV]
        for k in range(SSD_CONV - 1):
            s = SSD_CONV - 1 - k
            acc = acc + pltpu.roll(ext, tb + 8 - s, 0)[:tb] * wv[k:k + 1]
        o_ref[...] = acc.astype(o_ref.dtype)

    return _call(body, name, (c // tc, nb), [blk, nxt, wsp], blk, _sds((n, c), BF16),
                 sem=("parallel", "arbitrary"))(dpre, dpre, w)


def _col(v, j):
    lane = lax.broadcasted_iota(jnp.int32, v.shape, 1)
    return jnp.sum(jnp.where(lane == j, v, 0.0), axis=1, keepdims=True)


def _row(v, j):
    sub = lax.broadcasted_iota(jnp.int32, v.shape, 0)
    return jnp.sum(jnp.where(sub == j, v, 0.0), axis=0, keepdims=True)


def _ssd_head_terms(dtc, dtr, a_row, a_col):
    ll = dtc.shape[0]
    r = lax.broadcasted_iota(jnp.int32, (ll, ll), 0)
    c = lax.broadcasted_iota(jnp.int32, (ll, ll), 1)
    tril = (r >= c)
    trilf = tril.astype(F32)
    triuf = (r <= c).astype(F32)
    cumc = _nn(trilf, dtc * a_row, HIGHEST)
    cumr = _nn(dtr * a_col, triuf, HIGHEST)
    return cumc, cumr, tril, trilf, triuf


def _ssd_specs(n, di, gn, jh, p, nc, rev):
    ll = SSD_CHUNK
    jp = jh * p

    def ci(c):
        return (nc - 1 - c) if rev else c

    xs = pl.BlockSpec((ll, jp), lambda g, c: (ci(c), g))
    bs = pl.BlockSpec((ll, SSD_STATE), lambda g, c: (ci(c), di // SSD_STATE + g))
    cs = pl.BlockSpec((ll, SSD_STATE), lambda g, c: (ci(c), (di + gn * SSD_STATE) // SSD_STATE + g))
    dtc = pl.BlockSpec((None, ll, jh), lambda g, c: (g, ci(c), 0))
    dtr = pl.BlockSpec((None, jh, ll), lambda g, c: (g, 0, ci(c)))
    arow = pl.BlockSpec((None, 1, jh), lambda g, c: (g, 0, 0))
    acol = pl.BlockSpec((None, jh, 1), lambda g, c: (g, 0, 0))
    dsk = pl.BlockSpec((None, 1, jp), lambda g, c: (g, 0, 0))
    hin = pl.BlockSpec((None, None, SSD_STATE, jp), lambda g, c: (g, ci(c), 0, 0))
    ys = pl.BlockSpec((ll, jp), lambda g, c: (ci(c), g))
    return xs, bs, cs, dtc, dtr, arow, acol, dsk, hin, ys


def _ssd_fwd(xbc, dtc, dtr, a_row, a_col, dskip, name, *, di, p, carry=None):
    n = xbc.shape[0]
    gn = SSD_GROUPS
    jh = dtc.shape[2]
    jp = jh * p
    ll = SSD_CHUNK
    nc = n // ll
    xs, bs, cs, dtcs, dtrs, arow, acol, dsk, hin, ys = _ssd_specs(n, di, gn, jh, p, nc, False)
    pair = 2 * p

    def body(x_ref, b_ref, c_ref, dtc_ref, dtr_ref, ar_ref, ac_ref, ds_ref, y_ref, hin_ref, h_scr):
        @pl.when(pl.program_id(1) == 0)
        def _():
            h_scr[...] = jnp.zeros_like(h_scr)

        dtcv, dtrv = dtc_ref[...], dtr_ref[...]
        cumc, cumr, tril, _, _ = _ssd_head_terms(dtcv, dtrv, ar_ref[...], ac_ref[...])
        tot = jnp.sum(dtcv * ar_ref[...], axis=0, keepdims=True)
        bb, cb_ = b_ref[...].astype(BF16), c_ref[...].astype(BF16)
        cbm = _nt(cb_, bb)
        hin_ref[...] = h_scr[...]
        lane = lax.broadcasted_iota(jnp.int32, (ll, pair), 1)
        lane1 = lax.broadcasted_iota(jnp.int32, (1, pair), 1)
        for pr in range(jh // 2):
            sl = slice(pr * pair, (pr + 1) * pair)
            xp = x_ref[:, sl]
            xpb = xp.astype(BF16)
            hp = h_scr[:, sl]
            ydiag = jnp.zeros((ll, pair), F32)
            e_p = jnp.zeros((ll, pair), F32)
            w_p = jnp.zeros((ll, pair), F32)
            cd_p = jnp.zeros((1, pair), F32)
            for q in range(2):
                j = 2 * pr + q
                mj = (lane >= p) if q else (lane < p)
                cc, cr = _col(cumc, j), _row(cumr, j)
                decay = jnp.exp(jnp.where(tril, cc - cr, -1e30))
                mm = cbm * decay * _row(dtrv, j)
                ydiag = ydiag + _nn(mm.astype(BF16), jnp.where(mj, xpb, jnp.zeros_like(xpb)))
                cl = _col(tot, j)
                e_p = jnp.where(mj, jnp.exp(cc), e_p)
                w_p = jnp.where(mj, jnp.exp(cl - cc) * _col(dtcv, j), w_p)
                cd_p = jnp.where((lane1 >= p) if q else (lane1 < p), jnp.exp(cl), cd_p)
            yoff = _nn(cb_, hp.astype(BF16)) * e_p
            y_ref[:, sl] = ydiag + yoff + xp * ds_ref[:, sl]
            h_scr[:, sl] = hp * cd_p + _tn(bb, (xp * w_p).astype(BF16))

    return _call(body, name, (gn, nc), [xs, bs, cs, dtcs, dtrs, arow, acol, dsk], [ys, hin],
                 [_sds((n, di), F32), _sds((gn, nc, SSD_STATE, jp), F32)],
                 scratch=[pltpu.VMEM((SSD_STATE, jp), F32)], sem=("parallel", "arbitrary"), carry=carry)(
        xbc, xbc, xbc, dtc, dtr, a_row, a_col, dskip)


def _ssd_bwd(xbc, dtc, dtr, a_row, a_col, dskip, hins, dy, name, *, di, p, carry=None):
    n = xbc.shape[0]
    gn = SSD_GROUPS
    jh = dtc.shape[2]
    jp = jh * p
    ll = SSD_CHUNK
    nc = n // ll
    xs, bs, cs, dtcs, dtrs, arow, acol, dsk, hin, ys = _ssd_specs(n, di, gn, jh, p, nc, True)
    pair = 2 * p
    bc_out = pl.BlockSpec((ll, SSD_STATE), lambda g, c: (nc - 1 - c, g))

    def body(x_ref, b_ref, c_ref, dtc_ref, dtr_ref, ar_ref, ac_ref, ds_ref, hin_ref, dy_ref,
             dx_ref, db_ref, dc_ref, ddtc_ref, ddtr_ref, dac_ref, dar_ref, dd_ref, dh_scr):
        first = pl.program_id(1) == 0

        @pl.when(first)
        def _():
            dh_scr[...] = jnp.zeros_like(dh_scr)
            dac_ref[...] = jnp.zeros_like(dac_ref)
            dar_ref[...] = jnp.zeros_like(dar_ref)
            dd_ref[...] = jnp.zeros_like(dd_ref)

        dtcv, dtrv = dtc_ref[...], dtr_ref[...]
        a_r, a_c = ar_ref[...], ac_ref[...]
        cumc, cumr, tril, trilf, triuf = _ssd_head_terms(dtcv, dtrv, a_r, a_c)
        tot = jnp.sum(dtcv * a_r, axis=0, keepdims=True)
        bb, cb_ = b_ref[...].astype(BF16), c_ref[...].astype(BF16)
        cbm = _nt(cb_, bb)
        lane = lax.broadcasted_iota(jnp.int32, (ll, pair), 1)
        lane1 = lax.broadcasted_iota(jnp.int32, (1, pair), 1)
        lane_j = lax.broadcasted_iota(jnp.int32, (ll, jh), 1)
        sub_l = lax.broadcasted_iota(jnp.int32, (ll, jh), 0)
        sub_j = lax.broadcasted_iota(jnp.int32, (jh, ll), 0)
        dcb = jnp.zeros((ll, ll), F32)
        db_acc = jnp.zeros((ll, SSD_STATE), F32)
        dc_acc = jnp.zeros((ll, SSD_STATE), F32)
        dcum_c = jnp.zeros((ll, jh), F32)
        dcum_r = jnp.zeros((jh, ll), F32)
        ddt_c = jnp.zeros((ll, jh), F32)
        ddt_r = jnp.zeros((jh, ll), F32)
        for pr in range(jh // 2):
            sl = slice(pr * pair, (pr + 1) * pair)
            xp = x_ref[:, sl]
            xpb = xp.astype(BF16)
            dyp = dy_ref[:, sl]
            hp = hin_ref[:, sl]
            hpb = hp.astype(BF16)
            dhp = dh_scr[:, sl]
            dhpb = dhp.astype(BF16)
            ch = _nn(cb_, hpb)
            gp = _nn(bb, dhpb)
            e_p = jnp.zeros((ll, pair), F32)
            w_p = jnp.zeros((ll, pair), F32)
            cd_p = jnp.zeros((1, pair), F32)
            dxp = dyp * ds_ref[:, sl]
            heads = []
            for q in range(2):
                j = 2 * pr + q
                mj = (lane >= p) if q else (lane < p)
                cc, cr = _col(cumc, j), _row(cumr, j)
                cl = _col(tot, j)
                ej = jnp.exp(cc)
                wdec = jnp.exp(cl - cc)
                wj = wdec * _col(dtcv, j)
                e_p = jnp.where(mj, ej, e_p)
                w_p = jnp.where(mj, wj, w_p)
                cdj = jnp.exp(cl)
                cd_p = jnp.where((lane1 >= p) if q else (lane1 < p), cdj, cd_p)
                heads.append((j, mj, cc, cr, cl, wdec, wj, cdj))
            dye = dyp * e_p
            dyeb = dye.astype(BF16)
            dc_acc = dc_acc + _nt(dyeb, hpb)
            dh_new = dhp * cd_p + _tn(cb_, dyeb)
            dxp = dxp + gp * w_p
            db_acc = db_acc + _nt((xp * w_p).astype(BF16), dhpb)
            t_off = dye * ch
            t_w = gp * xp
            t_cd = jnp.sum(dhp * hp, axis=0, keepdims=True)
            for (j, mj, cc, cr, cl, wdec, wj, cdj) in heads:
                dyj = jnp.where(mj, dyp, 0.0).astype(BF16)
                decay = jnp.exp(jnp.where(tril, cc - cr, -1e30))
                dtrow = _row(dtrv, j)
                mm = cbm * decay * dtrow
                dm = _nt(dyj, xpb)
                dxp = dxp + _tn(mm.astype(BF16), dyj)
                dcb = dcb + dm * decay * dtrow
                ddt_rj = jnp.sum(dm * cbm * decay, axis=0, keepdims=True)
                dseg = dm * mm
                dcum_cj = jnp.sum(dseg, axis=1, keepdims=True) + jnp.sum(jnp.where(mj, t_off, 0.0), axis=1, keepdims=True)
                dcum_rj = -jnp.sum(dseg, axis=0, keepdims=True)
                dwj = jnp.sum(jnp.where(mj, t_w, 0.0), axis=1, keepdims=True)
                ddt_cj = dwj * wdec
                qj = dwj * wj
                dcum_cj = dcum_cj - qj
                m1 = (lane1 >= p) if (j % 2) else (lane1 < p)
                dcl = jnp.sum(jnp.where(m1, t_cd, 0.0), axis=1, keepdims=True) * cdj + jnp.sum(qj, axis=0, keepdims=True)
                dcum_c = dcum_c + jnp.where(lane_j == j, dcum_cj, 0.0) + jnp.where((lane_j == j) & (sub_l == ll - 1), dcl, 0.0)
                dcum_r = dcum_r + jnp.where(sub_j == j, dcum_rj, 0.0)
                ddt_c = ddt_c + jnp.where(lane_j == j, ddt_cj, 0.0)
                ddt_r = ddt_r + jnp.where(sub_j == j, ddt_rj, 0.0)
            dx_ref[:, sl] = dxp
            dd_ref[:, sl] += jnp.sum(dyp * xp, axis=0, keepdims=True)
            dh_scr[:, sl] = dh_new
        dcbb = dcb.astype(BF16)
        dc_ref[...] = dc_acc + _nn(dcbb, bb)
        db_ref[...] = db_acc + _tn(dcbb, cb_)
        dda_c = _nn(triuf, dcum_c, HIGHEST)
        dda_r = _nn(dcum_r, trilf, HIGHEST)
        ddtc_ref[...] = ddt_c + dda_c * a_r
        ddtr_ref[...] = ddt_r + dda_r * a_c
        dac_ref[...] += jnp.sum(dda_c * dtcv, axis=0, keepdims=True)
        dar_ref[...] += jnp.sum(dda_r * dtrv, axis=1, keepdims=True)

    gs = SSD_GROUPS * SSD_STATE
    return _call(body, name, (gn, nc), [xs, bs, cs, dtcs, dtrs, arow, acol, dsk, hin, ys],
                 [ys, bc_out, bc_out, dtcs, dtrs, arow, acol, dsk],
                 [_sds((n, di), F32), _sds((n, gs), F32), _sds((n, gs), F32), _sds(dtc.shape, F32), _sds(dtr.shape, F32),
                  _sds((gn, 1, jh), F32), _sds((gn, jh, 1), F32), _sds((gn, 1, jp), F32)],
                 scratch=[pltpu.VMEM((SSD_STATE, jp), F32)], sem=("parallel", "arbitrary"), carry=carry)(
        xbc, xbc, xbc, dtc, dtr, a_row, a_col, dskip, hins, dy)


def _band_masks():
    r = lax.broadcasted_iota(jnp.int32, (DIL_BLOCK, DIL_BLOCK), 0)
    c = lax.broadcasted_iota(jnp.int32, (DIL_BLOCK, DIL_BLOCK), 1)
    return c >= r, c <= r


def _to_lanes(cols, width):
    tb = cols[0].shape[0]
    lane = lax.broadcasted_iota(jnp.int32, (tb, width), 1)
    out = jnp.zeros((tb, width), F32)
    for h, cv in enumerate(cols):
        out = jnp.where(lane == h, cv, out)
    return out


def _dil_attn_fwd(q3, k3, v3, name, *, dh, carry=None):
    dil, n, dw = q3.shape
    nb = n // DIL_BLOCK
    scale = dh ** -0.5
    tb = DIL_BLOCK
    cur = pl.BlockSpec((None, tb, dw), lambda r, jb: (r, jb, 0))
    prev = pl.BlockSpec((None, tb, dw), lambda r, jb: (r, jnp.maximum(jb - 1, 0), 0))
    o_spec = cur
    l_spec = pl.BlockSpec((None, tb, LANES), lambda r, jb: (r, jb, 0))

    def body(q_ref, kc_ref, kp_ref, vc_ref, vp_ref, o_ref, l_ref):
        mp, mc = _band_masks()
        mp = mp & (pl.program_id(1) > 0)
        lses = []
        for h in range(DIL_HEADS):
            sl = slice(h * dh, (h + 1) * dh)
            qh = q_ref[:, sl]
            sp = jnp.where(mp, _nt(qh, kp_ref[:, sl]) * scale, -jnp.inf)
            sc = jnp.where(mc, _nt(qh, kc_ref[:, sl]) * scale, -jnp.inf)
            mx = jnp.maximum(jnp.max(sp, axis=1, keepdims=True), jnp.max(sc, axis=1, keepdims=True))
            pp, pc = jnp.exp(sp - mx), jnp.exp(sc - mx)
            den = jnp.sum(pp, axis=1, keepdims=True) + jnp.sum(pc, axis=1, keepdims=True)
            o = _nn(pp.astype(BF16), vp_ref[:, sl]) + _nn(pc.astype(BF16), vc_ref[:, sl])
            o_ref[:, sl] = o / den
            lses.append(mx + jnp.log(den))
        l_ref[...] = _to_lanes(lses, LANES)

    return _call(body, name, (dil, nb), [cur, cur, prev, cur, prev], [o_spec, l_spec],
                 [_sds((dil, n, dw), F32), _sds((dil, n, LANES), F32)], sem=("parallel", "arbitrary"), carry=carry)(
        q3, k3, k3, v3, v3)


def _dil_attn_bwd(q3, k3, v3, do3, lse3, dl3, name, *, dh, carry=None):
    dil, n, dw = q3.shape
    nb = n // DIL_BLOCK
    scale = dh ** -0.5
    tb = DIL_BLOCK

    def nx(jb):
        return jnp.minimum(jb + 1, nb - 1)

    q_c = pl.BlockSpec((None, tb, dw), lambda r, jb: (r, jb, 0))
    q_n = pl.BlockSpec((None, tb, dw), lambda r, jb: (r, nx(jb), 0))
    k_p = pl.BlockSpec((None, tb, dw), lambda r, jb: (r, jnp.maximum(jb - 1, 0), 0))
    o_c, o_n = q_c, q_n
    l_c = pl.BlockSpec((None, tb, LANES), lambda r, jb: (r, jb, 0))
    l_n = pl.BlockSpec((None, tb, LANES), lambda r, jb: (r, nx(jb), 0))

    def body(qc_ref, qn_ref, kc_ref, kp_ref, vc_ref, vp_ref, doc_ref, don_ref, lc_ref, ln_ref, dlc_ref, dln_ref,
             dq_ref, dk_ref, dv_ref):
        jb = pl.program_id(1)
        mp, mc = _band_masks()
        has_prev = jb > 0
        has_next = jb < nb - 1
        lc, ln, dlc, dln = lc_ref[...], ln_ref[...], dlc_ref[...], dln_ref[...]

        def pair(qh, kh, vh, doh, lse_h, dl_h, mask):
            s = _nt(qh, kh) * scale
            pm = jnp.where(mask, jnp.exp(s - lse_h), 0.0)
            dp = _nt(doh, vh)
            ds = (pm * (dp - dl_h) * scale).astype(BF16)
            return pm.astype(BF16), ds

        for h in range(DIL_HEADS):
            sl = slice(h * dh, (h + 1) * dh)
            qc, qn, kc, kp = qc_ref[:, sl], qn_ref[:, sl], kc_ref[:, sl], kp_ref[:, sl]
            vc, vp, doc, don = vc_ref[:, sl], vp_ref[:, sl], doc_ref[:, sl], don_ref[:, sl]
            lch, lnh, dlch, dlnh = _col(lc, h), _col(ln, h), _col(dlc, h), _col(dln, h)
            _, ds_cp = pair(qc, kp, vp, doc, lch, dlch, mp & has_prev)
            p_cc, ds_cc = pair(qc, kc, vc, doc, lch, dlch, mc)
            p_nc, ds_nc = pair(qn, kc, vc, don, lnh, dlnh, mp & has_next)
            dq_ref[:, sl] = _nn(ds_cp, kp) + _nn(ds_cc, kc)
            dk_ref[:, sl] = _tn(ds_cc, qc) + _tn(ds_nc, qn)
            dv_ref[:, sl] = _tn(p_cc, doc) + _tn(p_nc, don)

    out = _sds((dil, n, dw), F32)
    return _call(body, name, (dil, nb), [q_c, q_n, q_c, k_p, q_c, k_p, o_c, o_n, l_c, l_n, l_c, l_n],
                 [o_c, o_c, o_c], [out, out, out], sem=("parallel", "arbitrary"), carry=carry)(
        q3, q3, k3, k3, v3, v3, do3, do3, lse3, lse3, dl3, dl3)


def _head_expand(dw, dh):
    r = lax.broadcasted_iota(jnp.int32, (LANES, dw), 0)
    c = lax.broadcasted_iota(jnp.int32, (LANES, dw), 1)
    return ((c // dh) == r).astype(F32)


def _mix_weights(l0, l1, l2):
    mx = jnp.maximum(jnp.maximum(l0, l1), l2)
    e = [jnp.exp(l0 - mx), jnp.exp(l1 - mx), jnp.exp(l2 - mx)]
    den = e[0] + e[1] + e[2]
    return [v / den for v in e]


MIX_ROWS = 256


def _tile_scratch(tb, w):
    return pltpu.VMEM((w // LANES, tb, LANES), F32)


def _read_tiles(scr):
    return jnp.concatenate([scr[ct] for ct in range(scr.shape[0])], axis=1)


def _write_tiles(scr, val):
    for ct in range(scr.shape[0]):
        scr[ct] = val[:, ct * LANES:(ct + 1) * LANES]


def _to_token_order(scr, blk_ref, dil):
    rows = blk_ref.shape[1]
    for ct in range(scr.shape[0]):
        cs = slice(ct * LANES, (ct + 1) * LANES)
        if dil == 1:
            scr[ct] = blk_ref[0, :, cs].astype(F32)
        else:
            for r in range(dil):
                scr.at[ct][pl.ds(r, rows, stride=dil), :] = blk_ref[r, :, cs].astype(F32)


def _to_residue_order(out_ref, scr, dil):
    rows = out_ref.shape[1]
    for ct in range(scr.shape[0]):
        cs = slice(ct * LANES, (ct + 1) * LANES)
        if dil == 1:
            out_ref[0, :, cs] = scr[ct].astype(out_ref.dtype)
        else:
            for r in range(dil):
                out_ref[r, :, cs] = scr.at[ct][pl.ds(r, rows, stride=dil), :].astype(out_ref.dtype)


def _residue_spec(dil, tb, w):
    return pl.BlockSpec((dil, tb // dil, w), lambda i: (0, i, 0))


def _mix_fwd(os_, ls_, name, *, dh):
    dw = os_[0].shape[2]
    n = os_[0].shape[0] * os_[0].shape[1]
    tb = MIX_ROWS

    def body(o0, o1, o2, l0, l1, l2, out_ref, so0, so1, so2, sl0, sl1, sl2):
        for src, scr, dil in zip((o0, o1, o2, l0, l1, l2), (so0, so1, so2, sl0, sl1, sl2), DIL_DILATIONS * 2):
            _to_token_order(scr, src, dil)
        ex = _head_expand(dw, dh)
        ws = _mix_weights(_read_tiles(sl0), _read_tiles(sl1), _read_tiles(sl2))
        out_ref[...] = sum(_nn(wv, ex, HIGHEST) * _read_tiles(s) for wv, s in zip(ws, (so0, so1, so2))).astype(
            out_ref.dtype)

    specs = [_residue_spec(dl, tb, dw) for dl in DIL_DILATIONS] + [_residue_spec(dl, tb, LANES) for dl in DIL_DILATIONS]
    return _call(body, name, (n // tb,), specs, pl.BlockSpec((tb, dw), lambda i: (i, 0)), _sds((n, dw), BF16),
                 scratch=[_tile_scratch(tb, dw)] * 3 + [_tile_scratch(tb, LANES)] * 3, sem=("parallel",))(
        *os_, *ls_)


def _mix_bwd(os_, ls_, do, name, *, dh):
    dw = os_[0].shape[2]
    n = os_[0].shape[0] * os_[0].shape[1]
    tb = MIX_ROWS

    def body(o0, o1, o2, l0, l1, l2, do_ref, d0, d1, d2, r0, r1, r2, so0, so1, so2, sl0, sl1, sl2):
        for src, scr, dil in zip((o0, o1, o2, l0, l1, l2), (so0, so1, so2, sl0, sl1, sl2), DIL_DILATIONS * 2):
            _to_token_order(scr, src, dil)
        ex = _head_expand(dw, dh)
        dov = do_ref[...]
        ws = _mix_weights(_read_tiles(sl0), _read_tiles(sl1), _read_tiles(sl2))
        dws = [_nt(dov * _read_tiles(s), ex, HIGHEST) for s in (so0, so1, so2)]
        sdw = ws[0] * dws[0] + ws[1] * dws[1] + ws[2] * dws[2]
        for wv, so, sl_, d_ref, r_ref, dil in zip(ws, (so0, so1, so2), (sl0, sl1, sl2), (d0, d1, d2), (r0, r1, r2),
                                                  DIL_DILATIONS):
            _write_tiles(so, _nn(wv, ex, HIGHEST) * dov)
            _write_tiles(sl_, wv * sdw)
            _to_residue_order(d_ref, so, dil)
            _to_residue_order(r_ref, sl_, dil)

    specs = [_residue_spec(dl, tb, dw) for dl in DIL_DILATIONS] + [_residue_spec(dl, tb, LANES) for dl in DIL_DILATIONS]
    return _call(body, name, (n // tb,), specs + [pl.BlockSpec((tb, dw), lambda i: (i, 0))], specs,
                 [_sds(o.shape, BF16) for o in os_] + [_sds(l.shape, F32) for l in ls_],
                 scratch=[_tile_scratch(tb, dw)] * 3 + [_tile_scratch(tb, LANES)] * 3, sem=("parallel",))(
        *os_, *ls_, do)


def _to_residue(x, cos2, sin2, name, *, dw, rope):
    n = x.shape[0]
    dh = cos2.shape[1]
    tb = MIX_ROWS

    def body(x_ref, c_ref, s_ref, o0, o1, o2, scr):
        c, s = c_ref[...], s_ref[...]
        for g, (o_ref, dil) in enumerate(zip((o0, o1, o2), DIL_DILATIONS)):
            for hh in range(dw // dh):
                v = x_ref[:, g * dw + hh * dh:g * dw + (hh + 1) * dh]
                if rope:
                    v = v * c + pltpu.roll(v, dh // 2, 1) * s
                lo = (hh * dh) % LANES
                scr[(hh * dh) // LANES, :, lo:lo + dh] = v
            _to_residue_order(o_ref, scr, dil)

    row = pl.BlockSpec((tb, dh), lambda i: (i, 0))
    return _call(body, name, (n // tb,), [pl.BlockSpec((tb, 3 * dw), lambda i: (i, 0)), row, row],
                 [_residue_spec(dl, tb, dw) for dl in DIL_DILATIONS],
                 [_sds((dl, n // dl, dw), BF16) for dl in DIL_DILATIONS],
                 scratch=[_tile_scratch(tb, dw)], sem=("parallel",))(x, cos2, sin2)


FROM_RESIDUE_ROWS = 128


def _from_residue(cols, cos2, sin2, name, *, dw, rope_cols):
    dils = DIL_DILATIONS * (len(cols) // len(DIL_DILATIONS))
    n = cols[0][0].shape[0] * cols[0][0].shape[1]
    dh = cos2.shape[1]
    tb = FROM_RESIDUE_ROWS
    counts = [len(cl) for cl in cols]

    def body(*refs):
        n_in = sum(counts)
        srcs, (c_ref, s_ref, o_ref, scr) = refs[:n_in], refs[n_in:]
        c, s = c_ref[...], s_ref[...]
        k = 0
        for cb, (cnt, dil) in enumerate(zip(counts, dils)):
            rows = tb // dil
            for ct in range(dw // LANES):
                cs = slice(ct * LANES, (ct + 1) * LANES)
                for r in range(dil):
                    acc = srcs[k][r, :, cs]
                    for extra in srcs[k + 1:k + cnt]:
                        acc = acc + extra[r, :, cs]
                    if dil == 1:
                        scr[ct] = acc
                    else:
                        scr.at[ct][pl.ds(r, rows, stride=dil), :] = acc
            k += cnt
            for hh in range(dw // dh):
                lo = (hh * dh) % LANES
                v = scr[(hh * dh) // LANES, :, lo:lo + dh]
                if cb < rope_cols:
                    v = v * c - pltpu.roll(v, dh // 2, 1) * s
                o_ref[:, cb * dw + hh * dh:cb * dw + (hh + 1) * dh] = v.astype(o_ref.dtype)

    row = pl.BlockSpec((tb, dh), lambda i: (i, 0))
    specs = [_residue_spec(dil, tb, dw) for cnt, dil in zip(counts, dils) for _ in range(cnt)]
    return _call(body, name, (n // tb,), specs + [row, row], pl.BlockSpec((tb, len(cols) * dw), lambda i: (i, 0)),
                 _sds((n, len(cols) * dw), BF16), scratch=[_tile_scratch(tb, dw)], sem=("parallel",))(
        *[a for cl in cols for a in cl], cos2, sin2)


def _mem_attn_fwd(q, kv, name):
    n, mw = q.shape
    ml = kv.shape[0]
    dh = mw // MEM_HEADS
    scale = dh ** -0.5
    tb = _tile(n, 512, 8)

    def body(q_ref, kv_ref, o_ref):
        for h in range(MEM_HEADS):
            sl = slice(h * dh, (h + 1) * dh)
            s = _nt(q_ref[:, sl].astype(BF16), kv_ref[:, sl].astype(BF16)) * scale
            pm = jnp.exp(s - jnp.max(s, axis=1, keepdims=True))
            pm = pm / jnp.sum(pm, axis=1, keepdims=True)
            o_ref[:, sl] = _nn(pm.astype(BF16), kv_ref[:, mw + h * dh:mw + (h + 1) * dh].astype(BF16)).astype(o_ref.dtype)

    return _call(body, name, (n // tb,), [pl.BlockSpec((tb, mw), lambda i: (i, 0)), pl.BlockSpec((ml, 2 * mw), lambda i: (0, 0))],
                 pl.BlockSpec((tb, mw), lambda i: (i, 0)), _sds((n, mw), BF16), sem=("parallel",))(q, kv)


def _mem_attn_bwd(q, kv, do, name):
    n, mw = q.shape
    ml = kv.shape[0]
    dh = mw // MEM_HEADS
    scale = dh ** -0.5
    tb = _tile(n, 512, 8)

    def body(q_ref, kv_ref, do_ref, dq_ref, dkv_ref):
        @pl.when(pl.program_id(0) == 0)
        def _():
            dkv_ref[...] = jnp.zeros_like(dkv_ref)

        for h in range(MEM_HEADS):
            sl = slice(h * dh, (h + 1) * dh)
            vsl = slice(mw + h * dh, mw + (h + 1) * dh)
            qh, kh, vh = q_ref[:, sl].astype(BF16), kv_ref[:, sl].astype(BF16), kv_ref[:, vsl].astype(BF16)
            doh = do_ref[:, sl].astype(BF16)
            s = _nt(qh, kh) * scale
            pm = jnp.exp(s - jnp.max(s, axis=1, keepdims=True))
            pm = pm / jnp.sum(pm, axis=1, keepdims=True)
            dp = _nt(doh, vh)
            ds = (pm * (dp - jnp.sum(pm * dp, axis=1, keepdims=True)) * scale).astype(BF16)
            dq_ref[:, sl] = _nn(ds, kh).astype(dq_ref.dtype)
            dkv_ref[:, sl] += _tn(ds, qh)
            dkv_ref[:, vsl] += _tn(pm.astype(BF16), doh)

    row = pl.BlockSpec((tb, mw), lambda i: (i, 0))
    kvs = pl.BlockSpec((ml, 2 * mw), lambda i: (0, 0))
    return _call(body, name, (n // tb,), [row, kvs, row], [row, kvs], [_sds((n, mw), BF16), _sds((ml, 2 * mw), F32)],
                 sem=("arbitrary",))(q, kv, do)


def _adamw(w, g, m, v, name):
    shape = w.shape
    c = shape[-1]
    r = max(1, math.prod(shape[:-1]))
    w2, g2, m2, v2 = (t.reshape(r, c) for t in (w, g, m, v))
    tb = _tile(r, max(8, (1 << 19) // max(c, 1) // 8 * 8), 8)
    bc1 = 1.0 - ADAM_B1 ** ADAM_STEP
    bc2 = 1.0 - ADAM_B2 ** ADAM_STEP

    def body(w_ref, g_ref, m_ref, v_ref, d_ref, nm_ref, nv_ref):
        gv = g_ref[...]
        nm = ADAM_B1 * m_ref[...] + (1.0 - ADAM_B1) * gv
        nv = ADAM_B2 * v_ref[...] + (1.0 - ADAM_B2) * (gv * gv)
        d_ref[...] = -ADAM_LR * ((nm / bc1) / (jnp.sqrt(nv / bc2) + ADAM_EPS) + ADAM_WD * w_ref[...])
        nm_ref[...] = nm
        nv_ref[...] = nv

    spec = pl.BlockSpec((tb, c), lambda i: (i, 0))
    o = _sds((r, c), F32)
    d, nm, nv = _call(body, name, (r // tb,), [spec] * 4, [spec] * 3, [o, o, o], sem=("parallel",))(w2, g2, m2, v2)
    return d.reshape(shape), nm.reshape(shape), nv.reshape(shape)


def _place():
    x, y, c = lax.axis_index("x"), lax.axis_index("y"), lax.axis_index("c")
    chips = [(1 - x, y), (x, 1 - y), (1 - x, 1 - y)]
    return x, y, c, chips


def _cols(ref, c):
    hw = ref.shape[-1] // 2
    return ref.at[(slice(None),) * (len(ref.shape) - 1) + (pl.ds(c * hw, hw),)]


def _slot(ref, s):
    return ref.at[:, s]


def _my_chip():
    return 2 * lax.axis_index("x") + lax.axis_index("y")


def _place_shard(t, name):
    nl, r, n = t.shape
    tb = _tile(r, 512, SUBLANES_BF16)

    def body(t_ref, o_ref):
        o_ref[...] = t_ref[...].astype(BF16)

    return _call(body, name, (nl, r // tb), [pl.BlockSpec((None, tb, n), lambda l, i: (l, i, 0))],
                 pl.BlockSpec((None, None, tb, n), lambda l, i: (l, _my_chip(), i, 0)),
                 _sds((nl, 4, r, n), BF16), sem=("parallel", "parallel"))(t)


def _aliased_comm_call(body, name, bufs, n_sems):
    na = len(bufs)
    anyspec = pl.BlockSpec(memory_space=pl.ANY)
    return pl.pallas_call(
        body, name=name, in_specs=[anyspec] * na, out_specs=[anyspec] * na,
        out_shape=[_sds(b.shape, b.dtype) for b in bufs], input_output_aliases={a: a for a in range(na)},
        scratch_shapes=[pltpu.SemaphoreType.DMA((n_sems,)), pltpu.SemaphoreType.DMA((n_sems,))],
        compiler_params=pltpu.CompilerParams(has_side_effects=True))(*bufs)


def _same_block_copy(blk, send, recv, k, to):
    return pltpu.make_async_remote_copy(src_ref=blk, dst_ref=blk, send_sem=send.at[k], recv_sem=recv.at[k],
                                        device_id=to, device_id_type=MESH)


def _gather_start(bufs, send, recv):
    x, y, c, chips = _place()
    for a, buf in enumerate(bufs):
        for j, (px, py) in enumerate(chips):
            _same_block_copy(_cols(_slot(buf, 2 * x + y), c), send, recv, 6 * a + j, (px, py, c)).start()


def _gather_finish(bufs, send, recv):
    x, y, c, chips = _place()
    sib = (x, y, 1 - c)
    for a, buf in enumerate(bufs):
        for j, (px, py) in enumerate(chips):
            blk = _cols(_slot(buf, 2 * px + py), c)
            _same_block_copy(blk, send, recv, 6 * a + j, (px, py, c)).wait_recv()
            _same_block_copy(blk, send, recv, 6 * a + 3 + j, sib).start()
    for a, buf in enumerate(bufs):
        for j, (px, py) in enumerate(chips):
            _same_block_copy(_cols(_slot(buf, 2 * px + py), 1 - c), send, recv, 6 * a + 3 + j, sib).wait_recv()
    for a, buf in enumerate(bufs):
        for j, (px, py) in enumerate(chips):
            _same_block_copy(_cols(_slot(buf, 2 * x + y), c), send, recv, 6 * a + j, (px, py, c)).wait_send()
            _same_block_copy(_cols(_slot(buf, 2 * px + py), c), send, recv, 6 * a + 3 + j, sib).wait_send()


def _gather_carry(bufs):
    return _Carry(bufs, range(len(bufs)), 6 * len(bufs), _gather_start, _gather_finish)


def _gather_weights(bufs, name):
    na = len(bufs)

    def body(*refs):
        outs = refs[na:2 * na]
        send, recv = refs[2 * na:]
        _gather_start(outs, send, recv)
        _gather_finish(outs, send, recv)

    return _aliased_comm_call(body, name, bufs, 6 * na)


def _swap_halves(grads, name):
    na = len(grads)

    def body(*refs):
        srcs, outs = refs[:na], refs[na:2 * na]
        send, recv = refs[2 * na:]
        x, y, c, _ = _place()
        sib = (x, y, 1 - c)
        cps = [pltpu.make_async_remote_copy(src_ref=_cols(srcs[a], 1 - c), dst_ref=outs[a], send_sem=send.at[a],
                                            recv_sem=recv.at[a], device_id=sib, device_id_type=MESH) for a in range(na)]
        for cp in cps:
            cp.start()
        for cp in cps:
            cp.wait()

    anyspec = pl.BlockSpec(memory_space=pl.ANY)
    return pl.pallas_call(
        body, name=name, in_specs=[anyspec] * na, out_specs=[anyspec] * na,
        out_shape=[_sds(g.shape[:-1] + (g.shape[-1] // 2,), g.dtype) for g in grads],
        scratch_shapes=[pltpu.SemaphoreType.DMA((na,)), pltpu.SemaphoreType.DMA((na,))],
        compiler_params=pltpu.CompilerParams(has_side_effects=True))(*grads)


def _pair_sum(g, theirs, name):
    nl, _, r, n = g.shape
    hw = n // 2
    tb = _tile(r, 256, SUBLANES_BF16)

    def body(g_ref, t_ref, p_ref, own_ref):
        p_ref[...] = (g_ref[...] + t_ref[...]).astype(BF16)
        own_ref[...] = p_ref[_my_chip()]

    blk = pl.BlockSpec((None, 4, tb, hw), lambda l, i: (l, 0, i, 0))
    out = _sds((nl, 4, r, hw), BF16)
    return _call(body, name, (nl, r // tb),
                 [pl.BlockSpec((None, 4, tb, hw), lambda l, i: (l, 0, i, lax.axis_index("c"))), blk],
                 [blk, pl.BlockSpec((None, None, tb, hw), lambda l, i: (l, _my_chip(), i, 0))],
                 [out, out], sem=("parallel", "parallel"))(g, theirs)


def _scatter_start(bufs, send, recv):
    na = len(bufs) // 2
    x, y, c, chips = _place()
    for a in range(na):
        for j, (px, py) in enumerate(chips):
            pltpu.make_async_remote_copy(src_ref=_slot(bufs[a], 2 * px + py), dst_ref=_slot(bufs[na + a], 2 * x + y),
                                         send_sem=send.at[3 * a + j], recv_sem=recv.at[3 * a + j],
                                         device_id=(px, py, c), device_id_type=MESH).start()


def _scatter_finish(bufs, send, recv):
    na = len(bufs) // 2
    x, y, c, chips = _place()
    for a in range(na):
        for j, (px, py) in enumerate(chips):
            _same_block_copy(_slot(bufs[na + a], 2 * px + py), send, recv, 3 * a + j, (px, py, c)).wait_recv()
    for a in range(na):
        for j, (px, py) in enumerate(chips):
            _same_block_copy(_slot(bufs[a], 2 * px + py), send, recv, 3 * a + j, (px, py, c)).wait_send()


def _scatter_carry(parts, lands):
    na = len(parts)
    return _Carry(list(parts) + list(lands), range(na, 2 * na), 3 * na, _scatter_start, _scatter_finish)


def _scatter_partials(parts, lands, name):
    na = len(parts)

    def body(*refs):
        bufs = list(refs[:na]) + list(refs[2 * na:3 * na])
        send, recv = refs[3 * na:]
        _scatter_start(bufs, send, recv)
        _scatter_finish(bufs, send, recv)

    anyspec = pl.BlockSpec(memory_space=pl.ANY)
    return pl.pallas_call(
        body, name=name, in_specs=[anyspec] * (2 * na), out_specs=[anyspec] * na,
        out_shape=[_sds(b.shape, b.dtype) for b in lands],
        input_output_aliases={na + a: a for a in range(na)},
        scratch_shapes=[pltpu.SemaphoreType.DMA((3 * na,)), pltpu.SemaphoreType.DMA((3 * na,))],
        compiler_params=pltpu.CompilerParams(has_side_effects=True))(*parts, *lands)


def _chip_sum(land, n, name):
    nl, _, r, hw = land.shape
    tb = _tile(r, 512, SUBLANES_BF16)

    def body(i_ref, o_ref):
        o_ref[...] = ((i_ref[0].astype(F32) + i_ref[1].astype(F32)) + i_ref[2].astype(F32)) + i_ref[3].astype(F32)

    return _call(body, name, (nl, r // tb), [pl.BlockSpec((None, 4, tb, hw), lambda l, i: (l, 0, i, 0))],
                 pl.BlockSpec((None, tb, hw), lambda l, i: (l, i, lax.axis_index("c"))),
                 _sds((nl, r, n), F32), sem=("parallel", "parallel"))(land)


def _join_halves(bufs, name):
    na = len(bufs)

    def body(*refs):
        outs = refs[na:2 * na]
        send, recv = refs[2 * na:]
        x, y, c, _ = _place()
        sib = (x, y, 1 - c)
        cps = []
        for a in range(na):
            mine = _cols(outs[a], c)
            cp = pltpu.make_async_remote_copy(src_ref=mine, dst_ref=mine, send_sem=send.at[a], recv_sem=recv.at[a],
                                              device_id=sib, device_id_type=MESH)
            cp.start()
            cps.append(cp)
        for a in range(na):
            oth = _cols(outs[a], 1 - c)
            pltpu.make_async_remote_copy(src_ref=oth, dst_ref=oth, send_sem=send.at[a], recv_sem=recv.at[a],
                                         device_id=sib, device_id_type=MESH).wait_recv()
        for cp in cps:
            cp.wait_send()

    return _aliased_comm_call(body, name, bufs, na)


def _allreduce_small(v, name):
    rows, cols = v.shape

    def body(v_ref, o_ref, buf, send, recv):
        x, y, c, _ = _place()
        me = 4 * x + 2 * y + c
        buf[me] = v_ref[...]
        cps = []
        for d in range(1, 8):
            to = (x ^ (d >> 2), y ^ ((d >> 1) & 1), c ^ (d & 1))
            cp = pltpu.make_async_remote_copy(src_ref=v_ref, dst_ref=buf.at[me], send_sem=send.at[d - 1],
                                              recv_sem=recv.at[d - 1], device_id=to, device_id_type=MESH)
            cp.start()
            cps.append(cp)
        for d in range(1, 8):
            frm = 4 * (x ^ (d >> 2)) + 2 * (y ^ ((d >> 1) & 1)) + (c ^ (d & 1))
            got = buf.at[frm]
            pltpu.make_async_remote_copy(src_ref=got, dst_ref=got, send_sem=send.at[d - 1], recv_sem=recv.at[d - 1],
                                         device_id=(x, y, c), device_id_type=MESH).wait_recv()
        for cp in cps:
            cp.wait_send()
        acc = buf[0]
        for d in range(1, 8):
            acc = acc + buf[d]
        o_ref[...] = acc

    vm = pl.BlockSpec(memory_space=pltpu.VMEM)
    return pl.pallas_call(
        body, name=name, in_specs=[vm], out_specs=vm, out_shape=_sds((rows, cols), F32),
        scratch_shapes=[pltpu.VMEM((8, rows, cols), F32), pltpu.SemaphoreType.DMA((7,)), pltpu.SemaphoreType.DMA((7,))],
        compiler_params=pltpu.CompilerParams(has_side_effects=True, vmem_limit_bytes=V7X_VMEM_LIMIT))(v)


def _pack(arrs, pw):
    rows = []
    for a in arrs:
        f = a.astype(F32).reshape(-1)
        pad = (-f.shape[0]) % pw
        rows.append(jnp.pad(f, (0, pad)).reshape(-1, pw))
    p = jnp.concatenate(rows, axis=0)
    return jnp.pad(p, ((0, (-p.shape[0]) % 8), (0, 0)))


def _unpack(p, shapes, pw):
    out, r0 = [], 0
    for s in shapes:
        size = math.prod(s) if s else 1
        nr = -(-size // pw)
        out.append(p[r0:r0 + nr].reshape(-1)[:size].reshape(s))
        r0 += nr
    return out


def _step(x, mem, positions, wts, loss_target, mom, vel):
    w = dict(zip(WEIGHTS, wts))
    s_len, d = x.shape[1], x.shape[2]
    ml = mem.shape[1]
    depth = w['norm_mix'].shape[0]
    n_a = w['ssd_w_in'].shape[0]
    di = 4 * w['ssd_w_out'].shape[1]
    nh = w['ssd_dt_bias'].shape[1]
    p = di // nh
    gn = SSD_GROUPS
    jh = nh // gn
    cd = 4 * w['ssd_conv_w'].shape[2]
    w3 = 4 * w['dil_w_q'].shape[2]
    dw = w3 // 3
    hd = dw // DIL_HEADS
    mw = w['mem_w_q'].shape[2]
    ff = 4 * w['ffn_w_out'].shape[1]
    cx, cy, cc = lax.axis_index("x"), lax.axis_index("y"), lax.axis_index("c")
    chip = 2 * cx + cy


    big = list(COL_SHARDED + ROW_SHARDED)
    stacked = ('mem_w_q', 'mem_w_kv', 'mem_w_o')

    def mixer_items(i):
        if i < n_a:
            return [('ssd_w_in', i), ('ssd_w_out', i)]
        return ([('w_kv_shared', 0)] if i == n_a else []) + [('dil_w_q', i - n_a), ('dil_w_o', i - n_a)]

    def ffn_items(i):
        return [('ffn_w_in', i), ('ffn_w_out', i)] + ([(n_, None) for n_ in stacked] if i == 0 else [])

    def shard_of(name, l):
        t = w[name] if w[name].ndim == 3 else w[name][None]
        if l is not None:
            t = t[l:l + 1]
        return jnp.swapaxes(t, 1, 2).astype(BF16) if name in COL_SHARDED else t

    full = {n_: [None] * (w[n_].shape[0] if w[n_].ndim == 3 else 1) for n_ in big}

    def placed_bufs(its):
        return [_place_shard(shard_of(n_, l), f"place_{n_}_{l}") for n_, l in its]

    def record(its, bufs):
        for (n_, l), t in zip(its, bufs):
            fl = t.reshape(t.shape[0], 4 * t.shape[2], t.shape[3])
            if l is None:
                full[n_] = [fl[k] for k in range(fl.shape[0])]
            else:
                full[n_][l] = fl[0]

    record(mixer_items(0), _gather_weights(placed_bufs(mixer_items(0)), "gather_weights_first"))

    small_sharded = ('ssd_conv_w', 'ssd_conv_b', 'ssd_norm')

    def placed(name):
        t = w[name]
        wd = t.shape[-1]
        z = jnp.zeros(t.shape[:-1] + (4 * wd,), F32)
        z = lax.dynamic_update_slice_in_dim(z, t, chip * wd, axis=t.ndim - 1)
        return z * (cc == 0).astype(F32)

    sm_shapes = [w[n_].shape[:-1] + (4 * w[n_].shape[-1],) for n_ in small_sharded]
    conv_w, conv_b, ssd_nw = _unpack(_allreduce_small(_pack([placed(n_) for n_ in small_sharded], d), "gather_small"),
                                     sm_shapes, d)

    h = x[0]
    tgt = loss_target[0]
    mem_n = _rmsnorm(mem[0], w['mem_src_norm'], "mem_src_norm_fwd")
    cos2, sin2 = _rope_tables(positions[0].reshape(s_len, 1), hd, "rope_tables")
    a_neg = -jnp.exp(w['ssd_a_log'])
    saved = []
    kv_saved = None
    kgs = vgs = None
    for i in range(depth):
        sv = {'h_mix': h}
        u = _rmsnorm(h, w['norm_mix'][i], f"norm_mix_fwd_{i}")
        sv['u'] = u
        if i < n_a:
            wt = full['ssd_w_in'][i]
            wt_dt = wt[di + cd:]
            z = _mm(u, wt, 'nt', f"ssd_in_z_{i}", m=s_len, n=di, k=d)
            xbc_raw = _mm(u, wt, 'nt', f"ssd_in_xbc_{i}", m=s_len, n=cd, k=d, b_noff=di)
            dt_raw = _mm(u, wt_dt, 'nt', f"ssd_in_dt_{i}", m=s_len, n=nh, k=d)
            xbc = _conv_fwd(xbc_raw, conv_w[i], conv_b[i], f"ssd_conv_fwd_{i}")
            dt = _softplus_fwd(dt_raw, w['ssd_dt_bias'][i], f"ssd_dt_fwd_{i}")
            dtc = dt.reshape(s_len, gn, jh).transpose(1, 0, 2)
            dtr = dt.reshape(s_len, gn, jh).transpose(1, 2, 0)
            a_row = a_neg[i].reshape(gn, 1, jh)
            a_col = a_neg[i].reshape(gn, jh, 1)
            dskip = jnp.repeat(w['ssd_d'][i], p).reshape(gn, 1, jh * p)
            (y, hins), got = _ssd_fwd(xbc, dtc, dtr, a_row, a_col, dskip, f"ssd_scan_fwd_{i}", di=di, p=p,
                                      carry=_gather_carry(placed_bufs(ffn_items(i))))
            record(ffn_items(i), got)
            yn = _gated_norm_fwd(y, z, ssd_nw[i], f"ssd_norm_fwd_{i}")
            h = _mm(yn, full['ssd_w_out'][i], 'nn', f"ssd_out_{i}", m=s_len, n=d, k=di, res=h)
            sv.update(wt=wt, wt_dt=wt_dt, z=z, xbc_raw=xbc_raw, xbc=xbc, dt_raw=dt_raw, dtc=dtc, dtr=dtr, a_row=a_row,
                      a_col=a_col, dskip=dskip, y=y, hins=hins, yn=yn)
        else:
            j = i - n_a
            if j == 0:
                kvn = _rmsnorm(h, w['kv_norm'], "kv_norm_fwd")
                wkv = full['w_kv_shared'][0]
                k_raw = _mm(kvn, wkv, 'nt', "kv_proj_k", m=s_len, n=w3, k=d)
                v_raw = _mm(kvn, wkv, 'nt', "kv_proj_v", m=s_len, n=w3, k=d, b_noff=w3)
                kgs = _to_residue(k_raw, cos2, sin2, "rope_k", dw=dw, rope=True)
                vgs = _to_residue(v_raw, cos2, sin2, "residue_v", dw=dw, rope=False)
                kv_saved = {'h': h, 'kvn': kvn, 'wkv': wkv}
            q_raw = _mm(u, full['dil_w_q'][j], 'nt', f"dil_q_{i}", m=s_len, n=w3, k=d)
            qgs = _to_residue(q_raw, cos2, sin2, f"rope_q_{i}", dw=dw, rope=True)
            os_, ls_ = [], []
            for g in range(len(DIL_DILATIONS)):
                if g == 0:
                    (o3, l3), got = _dil_attn_fwd(qgs[g], kgs[g], vgs[g], f"dil_attn_fwd_{i}_{g}", dh=hd,
                                                  carry=_gather_carry(placed_bufs(ffn_items(i))))
                    record(ffn_items(i), got)
                else:
                    o3, l3 = _dil_attn_fwd(qgs[g], kgs[g], vgs[g], f"dil_attn_fwd_{i}_{g}", dh=hd)
                os_.append(o3)
                ls_.append(l3)
            om = _mix_fwd(os_, ls_, f"dil_mix_fwd_{i}", dh=hd)
            h = _mm(om, full['dil_w_o'][j], 'nn', f"dil_out_{i}", m=s_len, n=d, k=dw, res=h)
            sv.update(qgs=qgs, os=os_, ls=ls_, om=om)
        sv['h_mem'] = h
        u2 = _rmsnorm(h, w['norm_mem'][i], f"norm_mem_fwd_{i}")
        qm = _mm(u2, full['mem_w_q'][i], 'nn', f"mem_q_{i}", m=s_len, n=mw, k=d, out_dtype=BF16)
        kvm = _mm(mem_n, full['mem_w_kv'][i], 'nn', f"mem_kv_{i}", m=ml, n=2 * mw, k=d)
        omem = _mem_attn_fwd(qm, kvm, f"mem_attn_fwd_{i}")
        h = _mm(omem, full['mem_w_o'][i], 'nt', f"mem_out_{i}", m=s_len, n=d, k=mw, res=h)
        sv.update(u2=u2, qm=qm, kvm=kvm, omem=omem, h_ffn=h)
        u3 = _rmsnorm(h, w['norm_ffn'][i], f"norm_ffn_fwd_{i}")
        if i + 1 < depth:
            gu, got = _mm(u3, full['ffn_w_in'][i], 'nt', f"ffn_in_{i}", m=s_len, n=2 * ff, k=d, out_dtype=BF16,
                          carry=_gather_carry(placed_bufs(mixer_items(i + 1))))
            record(mixer_items(i + 1), got)
        else:
            gu = _mm(u3, full['ffn_w_in'][i], 'nt', f"ffn_in_{i}", m=s_len, n=2 * ff, k=d, out_dtype=BF16)
        act = _swiglu_fwd(gu, f"ffn_act_fwd_{i}")
        h = _mm(act, full['ffn_w_out'][i], 'nn', f"ffn_out_{i}", m=s_len, n=d, k=ff, res=h)
        sv.update(u3=u3, gu=gu, act=act)
        saved.append(sv)

    loss_part, dh, dhb, dg_final = _loss_and_grad(h, w['norm_final'], tgt, "loss_and_final_norm")

    gbig = {n_: [None] * len(full[n_]) for n_ in big}
    gsm = {n_: [None] * depth for n_ in ('norm_mix', 'norm_mem', 'norm_ffn')}
    for n_ in ('ssd_conv_w', 'ssd_conv_b', 'ssd_norm', 'ssd_dt_bias', 'ssd_a_log', 'ssd_d'):
        gsm[n_] = [None] * n_a
    dmem_n = None
    dk_parts = [[] for _ in DIL_DILATIONS]
    dv_parts = [[] for _ in DIL_DILATIONS]
    gshard = {}

    def rs_prepare(its, tag):
        gl = []
        for n_, l in its:
            t = jnp.stack(gbig[n_]) if l is None else gbig[n_][l][None]
            gl.append(t.reshape(t.shape[0], 4, t.shape[1] // 4, t.shape[2]))
        theirs = _swap_halves(gl, f"grad_swap_halves_{tag}")
        pairs = [_pair_sum(t, o, f"grad_pair_sum_{n_}_{l}") for (n_, l), t, o in zip(its, gl, theirs)]
        return its, [pr[0] for pr in pairs], [pr[1] for pr in pairs], [t.shape[3] for t in gl], tag

    def rs_finish(pend, lands):
        its, _, _, widths, tag = pend
        fins = [_chip_sum(ld, nw_, f"grad_chip_sum_{n_}_{l}") for (n_, l), ld, nw_ in zip(its, lands, widths)]
        for it, t in zip(its, _join_halves(fins, f"grad_join_halves_{tag}")):
            gshard[it] = t

    pend_mixer = None
    for i in reversed(range(depth)):
        sv = saved[i]
        dact = _mm(dhb, full['ffn_w_out'][i], 'nt', f"ffn_out_bwd_x_{i}", m=s_len, n=ff, k=d, out_dtype=BF16)
        gbig['ffn_w_out'][i] = _mm(sv['act'], dhb, 'tn', f"ffn_out_bwd_w_{i}", m=ff, n=d, k=s_len)
        dgu = _swiglu_bwd(sv['gu'], dact, f"ffn_act_bwd_{i}")
        if pend_mixer is None:
            gbig['ffn_w_in'][i] = _mm(dgu, sv['u3'], 'tn', f"ffn_in_bwd_w_{i}", m=2 * ff, n=d, k=s_len)
        else:
            gbig['ffn_w_in'][i], got = _mm(dgu, sv['u3'], 'tn', f"ffn_in_bwd_w_{i}", m=2 * ff, n=d, k=s_len,
                                           carry=_scatter_carry(pend_mixer[1], pend_mixer[2]))
            rs_finish(pend_mixer, got)
        du = _mm(dgu, full['ffn_w_in'][i], 'nn', f"ffn_in_bwd_x_{i}", m=s_len, n=d, k=2 * ff)
        dh, dhb, gsm['norm_ffn'][i] = _rmsnorm_bwd(sv['h_ffn'], w['norm_ffn'][i], du, dh, f"norm_ffn_bwd_{i}")
        do = _mm(dhb, full['mem_w_o'][i], 'nn', f"mem_out_bwd_x_{i}", m=s_len, n=mw, k=d, out_dtype=BF16)
        gbig['mem_w_o'][i] = _mm(dhb, sv['omem'], 'tn', f"mem_out_bwd_w_{i}", m=d, n=mw, k=s_len)
        dqm, dkvm = _mem_attn_bwd(sv['qm'], sv['kvm'], do, f"mem_attn_bwd_{i}")
        gbig['mem_w_q'][i] = _mm(sv['u2'], dqm, 'tn', f"mem_q_bwd_w_{i}", m=d, n=mw, k=s_len)
        du = _mm(dqm, full['mem_w_q'][i], 'nt', f"mem_q_bwd_x_{i}", m=s_len, n=d, k=mw)
        gbig['mem_w_kv'][i] = _mm(mem_n, dkvm, 'tn', f"mem_kv_bwd_w_{i}", m=d, n=2 * mw, k=ml)
        dmem_n = _mm(dkvm, full['mem_w_kv'][i], 'nt', f"mem_kv_bwd_x_{i}", m=ml, n=d, k=2 * mw, res=dmem_n)
        dh, dhb, gsm['norm_mem'][i] = _rmsnorm_bwd(sv['h_mem'], w['norm_mem'][i], du, dh, f"norm_mem_bwd_{i}")
        pend_ffn = rs_prepare(ffn_items(i), f"ffn_{i}")
        ffn_scatter = _scatter_carry(pend_ffn[1], pend_ffn[2])
        if i >= n_a:
            j = i - n_a
            dom = _mm(dhb, full['dil_w_o'][j], 'nt', f"dil_out_bwd_x_{i}", m=s_len, n=dw, k=d)
            gbig['dil_w_o'][j] = _mm(sv['om'], dhb, 'tn', f"dil_out_bwd_w_{i}", m=dw, n=d, k=s_len)
            mb = _mix_bwd(sv['os'], sv['ls'], dom, f"dil_mix_bwd_{i}", dh=hd)
            dqs = []
            for g in range(len(DIL_DILATIONS)):
                if g == 0:
                    (dq3, dk3, dv3), got = _dil_attn_bwd(sv['qgs'][g], kgs[g], vgs[g], mb[g], sv['ls'][g], mb[3 + g],
                                                         f"dil_attn_bwd_{i}_{g}", dh=hd, carry=ffn_scatter)
                    rs_finish(pend_ffn, got)
                else:
                    dq3, dk3, dv3 = _dil_attn_bwd(sv['qgs'][g], kgs[g], vgs[g], mb[g], sv['ls'][g], mb[3 + g],
                                                  f"dil_attn_bwd_{i}_{g}", dh=hd)
                dqs.append([dq3])
                dk_parts[g].append(dk3)
                dv_parts[g].append(dv3)
            dq_raw = _from_residue(dqs, cos2, sin2, f"rope_q_bwd_{i}", dw=dw, rope_cols=3)
            gbig['dil_w_q'][j] = _mm(dq_raw, sv['u'], 'tn', f"dil_q_bwd_w_{i}", m=w3, n=d, k=s_len)
            du = _mm(dq_raw, full['dil_w_q'][j], 'nn', f"dil_q_bwd_x_{i}", m=s_len, n=d, k=w3)
            dh, dhb, gsm['norm_mix'][i] = _rmsnorm_bwd(sv['h_mix'], w['norm_mix'][i], du, dh, f"norm_mix_bwd_{i}")
            if j == 0:
                dkv = _from_residue(dk_parts + dv_parts, cos2, sin2, "rope_kv_bwd", dw=dw, rope_cols=3)
                gbig['w_kv_shared'][0] = _mm(dkv, kv_saved['kvn'], 'tn', "kv_proj_bwd_w", m=2 * w3, n=d, k=s_len)
                du = _mm(dkv, kv_saved['wkv'], 'nn', "kv_proj_bwd_x", m=s_len, n=d, k=2 * w3)
                dh, dhb, dg_kv = _rmsnorm_bwd(kv_saved['h'], w['kv_norm'], du, dh, "kv_norm_bwd")
        else:
            dyn = _mm(dhb, full['ssd_w_out'][i], 'nt', f"ssd_out_bwd_x_{i}", m=s_len, n=di, k=d)
            gbig['ssd_w_out'][i] = _mm(sv['yn'], dhb, 'tn', f"ssd_out_bwd_w_{i}", m=di, n=d, k=s_len)
            dy, dz, gsm['ssd_norm'][i] = _gated_norm_bwd(sv['y'], sv['z'], ssd_nw[i], dyn, f"ssd_norm_bwd_{i}")
            (dx, db_, dc_, ddtc, ddtr, dac, dar, ddl), got = _ssd_bwd(
                sv['xbc'], sv['dtc'], sv['dtr'], sv['a_row'], sv['a_col'], sv['dskip'], sv['hins'], dy,
                f"ssd_scan_bwd_{i}", di=di, p=p, carry=ffn_scatter)
            rs_finish(pend_ffn, got)
            dxbc = jnp.concatenate([dx, db_, dc_], axis=1)
            ddt = ddtc.transpose(1, 0, 2).reshape(s_len, nh) + ddtr.transpose(2, 0, 1).reshape(s_len, nh)
            gsm['ssd_a_log'][i] = (dac.reshape(nh) + dar.reshape(nh)) * a_neg[i]
            gsm['ssd_d'][i] = ddl.reshape(nh, p).sum(axis=1)
            ddt_raw, dbias = _softplus_bwd(sv['dt_raw'], w['ssd_dt_bias'][i], ddt, f"ssd_dt_bwd_{i}")
            gsm['ssd_dt_bias'][i] = dbias.reshape(nh)
            dpre, dcw, dcb = _conv_bwd_pre(sv['xbc_raw'], conv_w[i], conv_b[i], dxbc, f"ssd_conv_bwd_pre_{i}")
            gsm['ssd_conv_w'][i], gsm['ssd_conv_b'][i] = dcw, dcb.reshape(cd)
            dxbc_raw = _conv_bwd_in(dpre, conv_w[i], f"ssd_conv_bwd_in_{i}")
            gbig['ssd_w_in'][i] = jnp.concatenate([
                _mm(dz, sv['u'], 'tn', f"ssd_in_bwd_w_z_{i}", m=di, n=d, k=s_len),
                _mm(dxbc_raw, sv['u'], 'tn', f"ssd_in_bwd_w_xbc_{i}", m=cd, n=d, k=s_len),
                _mm(ddt_raw, sv['u'], 'tn', f"ssd_in_bwd_w_dt_{i}", m=nh, n=d, k=s_len)], axis=0)
            du = _mm(dz, sv['wt'], 'nn', f"ssd_in_bwd_x_z_{i}", m=s_len, n=d, k=di)
            du = _mm(dxbc_raw, sv['wt'], 'nn', f"ssd_in_bwd_x_xbc_{i}", m=s_len, n=d, k=cd, b_koff=di, res=du)
            du = _mm(ddt_raw, sv['wt_dt'], 'nn', f"ssd_in_bwd_x_dt_{i}", m=s_len, n=d, k=nh, res=du)
            dh, dhb, gsm['norm_mix'][i] = _rmsnorm_bwd(sv['h_mix'], w['norm_mix'][i], du, dh, f"norm_mix_bwd_{i}")
        pend_mixer = rs_prepare(mixer_items(i), f"mixer_{i}")
    rs_finish(pend_mixer, _scatter_partials(pend_mixer[1], pend_mixer[2], "grad_scatter_last"))
    grad_x = dh[None]
    _, dg_src = _rmsnorm_bwd_noacc(mem[0], w['mem_src_norm'], dmem_n, "mem_src_norm_bwd")

    grads_big = {}
    for n_ in big:
        t = gshard[(n_, None)] if n_ in stacked else jnp.concatenate([gshard[(n_, l)] for l in range(len(full[n_]))])
        if n_ in COL_SHARDED:
            t = jnp.swapaxes(t, 1, 2)
        grads_big[n_] = t.reshape(w[n_].shape)

    sm_names = ['norm_mix', 'norm_mem', 'norm_ffn', 'ssd_conv_w', 'ssd_conv_b', 'ssd_norm', 'ssd_dt_bias', 'ssd_a_log',
                'ssd_d']
    sm_arrs = [jnp.stack([t.reshape(t.shape[-1]) if n_.startswith('norm') else t for t in gsm[n_]]) for n_ in sm_names]
    sm_names += ['norm_final', 'kv_norm', 'mem_src_norm', 'loss']
    sm_arrs += [dg_final.reshape(d), dg_kv.reshape(d), dg_src.reshape(d), loss_part[0, :1]]
    summed = _unpack(_allreduce_small(_pack(sm_arrs, d), "reduce_small"), [t.shape for t in sm_arrs], d)
    gs = dict(zip(sm_names, summed))
    loss = gs.pop('loss').reshape(())
    grads = dict(grads_big)
    for n_, t in gs.items():
        if n_ in small_sharded:
            wd = w[n_].shape[-1]
            t = lax.dynamic_slice_in_dim(t, chip * wd, wd, axis=t.ndim - 1)
        grads[n_] = t.reshape(w[n_].shape)

    deltas, new_m, new_v = [], [], []
    for n_, m_, v_ in zip(WEIGHTS, mom, vel):
        dlt, nm, nv = _adamw(w[n_], grads[n_], m_, v_, f"adamw_{n_}")
        deltas.append(dlt)
        new_m.append(nm)
        new_v.append(nv)
    return (loss, grad_x, *[grads[n_] for n_ in WEIGHTS], *deltas, *new_m, *new_v)


def kernel(x, mem, positions, norm_mix, norm_mem, norm_ffn, norm_final, ssd_w_in, ssd_conv_w, ssd_conv_b, ssd_dt_bias, ssd_a_log, ssd_d, ssd_norm, ssd_w_out, kv_norm, w_kv_shared, dil_w_q, dil_w_o, mem_src_norm, mem_w_q, mem_w_kv, mem_w_o, ffn_w_in, ffn_w_out, loss_target, m_norm_mix, m_norm_mem, m_norm_ffn, m_norm_final, m_ssd_w_in, m_ssd_conv_w, m_ssd_conv_b, m_ssd_dt_bias, m_ssd_a_log, m_ssd_d, m_ssd_norm, m_ssd_w_out, m_kv_norm, m_w_kv_shared, m_dil_w_q, m_dil_w_o, m_mem_src_norm, m_mem_w_q, m_mem_w_kv, m_mem_w_o, m_ffn_w_in, m_ffn_w_out, v_norm_mix, v_norm_mem, v_norm_ffn, v_norm_final, v_ssd_w_in, v_ssd_conv_w, v_ssd_conv_b, v_ssd_dt_bias, v_ssd_a_log, v_ssd_d, v_ssd_norm, v_ssd_w_out, v_kv_norm, v_w_kv_shared, v_dil_w_q, v_dil_w_o, v_mem_src_norm, v_mem_w_q, v_mem_w_kv, v_mem_w_o, v_ffn_w_in, v_ffn_w_out):
    wts = (norm_mix, norm_mem, norm_ffn, norm_final, ssd_w_in, ssd_conv_w, ssd_conv_b, ssd_dt_bias, ssd_a_log, ssd_d, ssd_norm, ssd_w_out, kv_norm, w_kv_shared, dil_w_q, dil_w_o, mem_src_norm, mem_w_q, mem_w_kv, mem_w_o, ffn_w_in, ffn_w_out)
    mom = (m_norm_mix, m_norm_mem, m_norm_ffn, m_norm_final, m_ssd_w_in, m_ssd_conv_w, m_ssd_conv_b, m_ssd_dt_bias, m_ssd_a_log, m_ssd_d, m_ssd_norm, m_ssd_w_out, m_kv_norm, m_w_kv_shared, m_dil_w_q, m_dil_w_o, m_mem_src_norm, m_mem_w_q, m_mem_w_kv, m_mem_w_o, m_ffn_w_in, m_ffn_w_out)
    vel = (v_norm_mix, v_norm_mem, v_norm_ffn, v_norm_final, v_ssd_w_in, v_ssd_conv_w, v_ssd_conv_b, v_ssd_dt_bias, v_ssd_a_log, v_ssd_d, v_ssd_norm, v_ssd_w_out, v_kv_norm, v_w_kv_shared, v_dil_w_q, v_dil_w_o, v_mem_src_norm, v_mem_w_q, v_mem_w_kv, v_mem_w_o, v_ffn_w_in, v_ffn_w_out)
    return _step(x, mem, positions, wts, loss_target, mom, vel)
```

```python
import functools
import math

import jax
import jax.numpy as jnp
from jax import lax
from jax.experimental import pallas as pl
from jax.experimental.pallas import tpu as pltpu

F32 = jnp.float32
BF16 = jnp.bfloat16
MESH = pl.DeviceIdType.MESH
HIGHEST = lax.Precision.HIGHEST

NORM_EPS = 1e-6
SSD_GROUPS = 8
SSD_STATE = 128
SSD_CHUNK = 128
SSD_CONV = 4
DIL_DILATIONS = (1, 4, 16)
DIL_HEADS = 16
DIL_BLOCK = 128
ATTN_ROWS = 32
ROPE_THETA = 10000.0
MEM_HEADS = 4
ADAM_LR, ADAM_B1, ADAM_B2, ADAM_EPS, ADAM_WD, ADAM_STEP = 0.001, 0.9, 0.999, 1e-08, 0.01, 10

V7X_VMEM_LIMIT = 48 * 1024 * 1024
LANES = 128
SUBLANES_BF16 = 16

WEIGHTS = ['norm_mix', 'norm_mem', 'norm_ffn', 'norm_final', 'ssd_w_in', 'ssd_conv_w', 'ssd_conv_b',
           'ssd_dt_bias', 'ssd_a_log', 'ssd_d', 'ssd_norm', 'ssd_w_out', 'kv_norm', 'w_kv_shared', 'dil_w_q',
           'dil_w_o', 'mem_src_norm', 'mem_w_q', 'mem_w_kv', 'mem_w_o', 'ffn_w_in', 'ffn_w_out']
COL_SHARDED = ('ssd_w_in', 'w_kv_shared', 'dil_w_q', 'mem_w_o', 'ffn_w_in')
ROW_SHARDED = ('ssd_w_out', 'dil_w_o', 'mem_w_q', 'mem_w_kv', 'ffn_w_out')


def _dot(a, b, ca, cb, prec=None):
    return lax.dot_general(a, b, (((ca,), (cb,)), ((), ())), preferred_element_type=F32, precision=prec)


def _nn(a, b, prec=None):
    return _dot(a, b, 1, 0, prec)


def _nt(a, b, prec=None):
    return _dot(a, b, 1, 1, prec)


def _tn(a, b, prec=None):
    return _dot(a, b, 0, 0, prec)


def _tile(n, pref, unit=LANES):
    if n <= pref:
        return n
    t = (pref // unit) * unit
    while t >= unit:
        if n % t == 0:
            return t
        t -= unit
    return n


class _Carry:
    def __init__(self, bufs, outs, n_sems, start, finish):
        self.bufs, self.outs, self.n_sems, self.start, self.finish = list(bufs), list(outs), n_sems, start, finish


def _call(body, name, grid, in_specs, out_specs, out_shape, scratch=(), sem=None, carry=None):
    if carry is None:
        return pl.pallas_call(
            body, name=name, grid=grid, in_specs=in_specs, out_specs=out_specs, out_shape=out_shape,
            scratch_shapes=list(scratch),
            compiler_params=pltpu.CompilerParams(dimension_semantics=sem, vmem_limit_bytes=V7X_VMEM_LIMIT))
    single = not isinstance(out_specs, (list, tuple))
    o_specs = [out_specs] if single else list(out_specs)
    o_shape = [out_shape] if single else list(out_shape)
    n_in, n_out, n_cb, n_co, n_scr = len(in_specs), len(o_specs), len(carry.bufs), len(carry.outs), len(scratch)
    anyspec = pl.BlockSpec(memory_space=pl.ANY)

    def riding(*refs):
        ins, cbufs = refs[:n_in], list(refs[n_in:n_in + n_cb])
        outs = refs[n_in + n_cb:n_in + n_cb + n_out]
        for k, b in enumerate(carry.outs):
            cbufs[b] = refs[n_in + n_cb + n_out + k]
        rest = refs[n_in + n_cb + n_out + n_co:]
        send, recv = rest[n_scr], rest[n_scr + 1]
        ids = [pl.program_id(a) for a in range(len(grid))]
        first = functools.reduce(lambda p, q: p & q, [i == 0 for i in ids])
        last = functools.reduce(lambda p, q: p & q, [i == g - 1 for i, g in zip(ids, grid)])

        @pl.when(first)
        def _():
            carry.start(cbufs, send, recv)

        body(*ins, *outs, *rest[:n_scr])

        @pl.when(last)
        def _():
            carry.finish(cbufs, send, recv)

    call = pl.pallas_call(
        riding, name=name, grid=grid, in_specs=list(in_specs) + [anyspec] * n_cb, out_specs=o_specs + [anyspec] * n_co,
        out_shape=o_shape + [_sds(carry.bufs[b].shape, carry.bufs[b].dtype) for b in carry.outs],
        scratch_shapes=list(scratch) + [pltpu.SemaphoreType.DMA((carry.n_sems,)), pltpu.SemaphoreType.DMA((carry.n_sems,))],
        input_output_aliases={n_in + b: n_out + k for k, b in enumerate(carry.outs)},
        compiler_params=pltpu.CompilerParams(dimension_semantics=("arbitrary",) * len(grid), has_side_effects=True,
                                             vmem_limit_bytes=V7X_VMEM_LIMIT))

    def run(*operands):
        res = call(*operands, *carry.bufs)
        own = res[0] if single else list(res[:n_out])
        return own, list(res[n_out:])

    return run


def _sds(shape, dtype):
    return jax.ShapeDtypeStruct(tuple(shape), dtype)


def _silu(x):
    return x * jax.nn.sigmoid(x)


def _dsilu(x):
    s = jax.nn.sigmoid(x)
    return s * (1.0 + x * (1.0 - s))


def _mm(a, b, mode, name, *, m, n, k, out_dtype=F32, res=None, b_noff=0, b_koff=0, carry=None):
    has_res = res is not None
    m_unit = LANES if mode == 'tn' else SUBLANES_BF16
    n_lim = math.gcd(n, b_noff) if b_noff else n
    k_lim = math.gcd(k, b_koff) if b_koff else k
    wide = out_dtype == F32
    if k <= 2048:
        tk = k
        tm = _tile(m, 2048, m_unit)
        tn = _tile(n_lim, 512 if (wide or has_res) else 1024)
    else:
        tk = _tile(k_lim, 1024)
        tm = _tile(m, 1024, m_unit)
        tn = _tile(n_lim, 2048 if (wide and not has_res) else 1024)
    nk = k // tk
    use_acc = nk > 1 and not wide
    jo, ko = b_noff // tn, b_koff // tk
    if mode == 'nn':
        a_spec = pl.BlockSpec((tm, tk), lambda i, j, kk: (i, kk))
        b_spec = pl.BlockSpec((tk, tn), lambda i, j, kk: (kk + ko, j + jo))
        ca, cb = 1, 0
    elif mode == 'nt':
        a_spec = pl.BlockSpec((tm, tk), lambda i, j, kk: (i, kk))
        b_spec = pl.BlockSpec((tn, tk), lambda i, j, kk: (j + jo, kk + ko))
        ca, cb = 1, 1
    else:
        a_spec = pl.BlockSpec((tk, tm), lambda i, j, kk: (kk, i))
        b_spec = pl.BlockSpec((tk, tn), lambda i, j, kk: (kk + ko, j + jo))
        ca, cb = 0, 0
    o_spec = pl.BlockSpec((tm, tn), lambda i, j, kk: (i, j))

    def body(*refs):
        a_ref, b_ref = refs[0], refs[1]
        r_ref = refs[2] if has_res else None
        o_ref = refs[3] if has_res else refs[2]
        acc = refs[-1] if use_acc else None

        def prod():
            return _dot(a_ref[...].astype(BF16), b_ref[...].astype(BF16), ca, cb)

        def with_res(v):
            return v + r_ref[...].astype(F32) if has_res else v

        if nk == 1:
            o_ref[...] = with_res(prod()).astype(o_ref.dtype)
            return
        kk = pl.program_id(2)
        if use_acc:
            @pl.when(kk == 0)
            def _():
                acc[...] = prod()

            @pl.when((kk > 0) & (kk < nk - 1))
            def _():
                acc[...] += prod()

            @pl.when(kk == nk - 1)
            def _():
                o_ref[...] = with_res(acc[...] + prod()).astype(o_ref.dtype)
        else:
            @pl.when(kk == 0)
            def _():
                o_ref[...] = with_res(prod())

            @pl.when(kk > 0)
            def _():
                o_ref[...] += prod()

    ins = [a, b] + ([res] if has_res else [])
    specs = [a_spec, b_spec] + ([o_spec] if has_res else [])
    return _call(body, name, (m // tm, n // tn, nk), specs, o_spec, _sds((m, n), out_dtype),
                 scratch=[pltpu.VMEM((tm, tn), F32)] if use_acc else [], sem=("parallel", "parallel", "arbitrary"),
                 carry=carry)(*ins)


def _rowwise(fn, name, rows, consts, outs, tb, n_rows):
    n_r, n_c = len(rows), len(consts)
    in_specs = [pl.BlockSpec((tb, w), functools.partial(lambda i, cb: (i, cb), cb=cb)) for _, w, cb in rows]
    in_specs += [pl.BlockSpec(c.shape, functools.partial(lambda i, nd: (0,) * nd, nd=c.ndim)) for c in consts]
    out_specs, out_shape = [], []
    for kind, w, dt in outs:
        if kind == 'row':
            out_specs.append(pl.BlockSpec((tb, w), lambda i: (i, 0)))
            out_shape.append(_sds((n_rows, w), dt))
        else:
            out_specs.append(pl.BlockSpec(w, lambda i: (0, 0)))
            out_shape.append(_sds(w, dt))

    def body(*refs):
        ins = [r[...] for r in refs[:n_r + n_c]]
        orefs = refs[n_r + n_c:]
        vals = fn(*ins)
        i = pl.program_id(0)
        for (kind, _, _), o_ref, v in zip(outs, orefs, vals):
            if kind == 'row':
                o_ref[...] = v.astype(o_ref.dtype)
            else:
                @pl.when(i == 0)
                def _(o_ref=o_ref):
                    o_ref[...] = jnp.zeros_like(o_ref)

                o_ref[...] += v.astype(o_ref.dtype)

    res = _call(body, name, (n_rows // tb,), in_specs, out_specs, out_shape, sem=("arbitrary",))(
        *[r[0] for r in rows], *consts)
    return res


def _rms_fwd_fn(h, g):
    r = lax.rsqrt(jnp.mean(h * h, axis=-1, keepdims=True) + NORM_EPS)
    return (h * r * g,)


def _rms_bwd_vals(h, g, dy):
    r = lax.rsqrt(jnp.mean(h * h, axis=-1, keepdims=True) + NORM_EPS)
    t = dy * g
    dh = r * t - h * (r * r * r) * jnp.mean(h * t, axis=-1, keepdims=True)
    dg = jnp.sum(dy * h * r, axis=0, keepdims=True)
    return dh, dg


def _rmsnorm(h, g, name):
    n, d = h.shape
    return _rowwise(_rms_fwd_fn, name, [(h, d, 0)], [g.reshape(1, d)], [('row', d, BF16)], _tile(n, 512, 8), n)[0]


def _rmsnorm_bwd(h, g, du, dres, name):
    n, d = h.shape

    def fn(hv, duv, drv, gv):
        dh, dg = _rms_bwd_vals(hv, gv, duv.astype(F32))
        return dh + drv, dh + drv, dg

    return _rowwise(fn, name, [(h, d, 0), (du, d, 0), (dres, d, 0)], [g.reshape(1, d)],
                    [('row', d, F32), ('row', d, BF16), ('acc', (1, d), F32)], _tile(n, 256, 8), n)


def _rmsnorm_bwd_noacc(h, g, du, name):
    n, d = h.shape

    def fn(hv, duv, gv):
        return _rms_bwd_vals(hv, gv, duv.astype(F32))

    return _rowwise(fn, name, [(h, d, 0), (du, d, 0)], [g.reshape(1, d)],
                    [('row', d, F32), ('acc', (1, d), F32)], _tile(n, 256, 8), n)


def _swiglu_fwd(gu, name):
    n, f2 = gu.shape
    f = f2 // 2

    def fn(v):
        v = v.astype(F32)
        return (_silu(v[:, :f]) * v[:, f:],)

    return _rowwise(fn, name, [(gu, f2, 0)], [], [('row', f, BF16)], _tile(n, 256, 16), n)[0]


def _swiglu_bwd(gu, dact, name):
    n, f2 = gu.shape
    f = f2 // 2

    def fn(v, da):
        v = v.astype(F32)
        g, up = v[:, :f], v[:, f:]
        da = da.astype(F32)
        return (jnp.concatenate([da * up * _dsilu(g), da * _silu(g)], axis=1),)

    return _rowwise(fn, name, [(gu, f2, 0), (dact, f, 0)], [], [('row', f2, BF16)], _tile(n, 256, 16), n)[0]


def _group_sum(v, ng):
    w = v.shape[1] // ng
    return [jnp.sum(v[:, i * w:(i + 1) * w], axis=1, keepdims=True) for i in range(ng)]


def _gated_norm_fwd(y, z, nw, name):
    n, di = y.shape
    gw = di // SSD_GROUPS

    def fn(yv, zv, nwv):
        a = yv * _silu(zv)
        ms = _group_sum(a * a, SSD_GROUPS)
        out = jnp.concatenate([a[:, i * gw:(i + 1) * gw] * lax.rsqrt(ms[i] / gw + NORM_EPS)
                               for i in range(SSD_GROUPS)], axis=1)
        return (out * nwv,)

    return _rowwise(fn, name, [(y, di, 0), (z, di, 0)], [nw.reshape(1, di)], [('row', di, BF16)], _tile(n, 256, 8), n)[0]


def _gated_norm_bwd(y, z, nw, dout, name):
    n, di = y.shape
    gw = di // SSD_GROUPS

    def fn(yv, zv, dov, nwv):
        sz = _silu(zv)
        a = yv * sz
        t = dov * nwv
        ms = _group_sum(a * a, SSD_GROUPS)
        at = _group_sum(a * t, SSD_GROUPS)
        das, ars = [], []
        for i in range(SSD_GROUPS):
            r = lax.rsqrt(ms[i] / gw + NORM_EPS)
            sl = slice(i * gw, (i + 1) * gw)
            das.append(r * t[:, sl] - a[:, sl] * (r * r * r) * (at[i] / gw))
            ars.append(a[:, sl] * r)
        da = jnp.concatenate(das, axis=1)
        ar = jnp.concatenate(ars, axis=1)
        return da * sz, da * yv * _dsilu(zv), jnp.sum(dov * ar, axis=0, keepdims=True)

    return _rowwise(fn, name, [(y, di, 0), (z, di, 0), (dout, di, 0)], [nw.reshape(1, di)],
                    [('row', di, F32), ('row', di, BF16), ('acc', (1, di), F32)], _tile(n, 128, 8), n)


def _softplus_fwd(raw, bias, name):
    n, h = raw.shape

    def fn(v, b):
        t = v + b
        return (jnp.maximum(t, 0.0) + jnp.log(1.0 + jnp.exp(-jnp.abs(t))),)

    return _rowwise(fn, name, [(raw, h, 0)], [bias.reshape(1, h)], [('row', h, F32)], _tile(n, 1024, 8), n)[0]


def _softplus_bwd(raw, bias, ddt, name):
    n, h = raw.shape

    def fn(v, d, b):
        g = d * jax.nn.sigmoid(v + b)
        return g, jnp.sum(g, axis=0, keepdims=True)

    return _rowwise(fn, name, [(raw, h, 0), (ddt, h, 0)], [bias.reshape(1, h)],
                    [('row', h, BF16), ('acc', (1, h), F32)], _tile(n, 1024, 8), n)


def _loss_and_grad(h, g, tgt, name):
    n, d = h.shape

    def fn(hv, tv, gv):
        y = _rms_fwd_fn(hv, gv)[0]
        err = y - tv
        part = 0.5 * jnp.sum(jnp.sum(err * err, axis=1, keepdims=True), axis=0, keepdims=True) / d
        dh, dg = _rms_bwd_vals(hv, gv, err / d)
        return jnp.broadcast_to(part, (8, LANES)), dh, dh, dg

    return _rowwise(fn, name, [(h, d, 0), (tgt, d, 0)], [g.reshape(1, d)],
                    [('acc', (8, LANES), F32), ('row', d, F32), ('row', d, BF16), ('acc', (1, d), F32)],
                    _tile(n, 256, 8), n)


def _rope_tables(pos_col, dh, name):
    n = pos_col.shape[0]
    half = dh // 2
    inv = ROPE_THETA ** (-jnp.arange(half, dtype=F32) / half)
    inv2 = jnp.concatenate([inv, inv]).reshape(1, dh)
    sign = jnp.concatenate([-jnp.ones((half,), F32), jnp.ones((half,), F32)]).reshape(1, dh)

    def fn(p, iv, sg):
        ang = p.astype(F32) * iv
        return jnp.cos(ang), jnp.sin(ang) * sg

    return _rowwise(fn, name, [(pos_col, 1, 0)], [inv2, sign], [('row', dh, F32), ('row', dh, F32)], _tile(n, 1024, 8), n)


def _conv_taps(ext, w, tb):
    shifted = [ext[8:] if k == SSD_CONV - 1 else pltpu.roll(ext, SSD_CONV - 1 - k, 0)[8:] for k in range(SSD_CONV)]
    pre = shifted[0] * w[0:1]
    for k in range(1, SSD_CONV):
        pre = pre + shifted[k] * w[k:k + 1]
    return pre, shifted


def _conv_specs(n, c, tb, tc):
    hb = tb // 8
    blk = pl.BlockSpec((tb, tc), lambda j, i: (i, j))
    halo = pl.BlockSpec((8, tc), lambda j, i: (jnp.maximum(i * hb - 1, 0), j))
    wsp = pl.BlockSpec((SSD_CONV, tc), lambda j, i: (0, j))
    bsp = pl.BlockSpec((1, tc), lambda j, i: (0, j))
    return blk, halo, wsp, bsp


def _conv_fwd(u, w, b, name):
    n, c = u.shape
    tb, tc = _tile(n, 512, 8), _tile(c, 1536)
    blk, halo, wsp, bsp = _conv_specs(n, c, tb, tc)

    def body(u_ref, h_ref, w_ref, b_ref, o_ref):
        halo_v = jnp.where(pl.program_id(1) > 0, h_ref[...], 0.0)
        pre, _ = _conv_taps(jnp.concatenate([halo_v, u_ref[...]], axis=0), w_ref[...], tb)
        o_ref[...] = _silu(pre + b_ref[...])

    return _call(body, name, (c // tc, n // tb), [blk, halo, wsp, bsp], blk, _sds((n, c), F32),
                 sem=("parallel", "arbitrary"))(u, u, w, b.reshape(1, c))


def _conv_bwd_pre(u, w, b, dout, name):
    n, c = u.shape
    tb, tc = _tile(n, 512, 8), _tile(c, 1536)
    blk, halo, wsp, bsp = _conv_specs(n, c, tb, tc)

    def body(u_ref, h_ref, w_ref, b_ref, d_ref, dp_ref, dw_ref, db_ref):
        i = pl.program_id(1)
        halo_v = jnp.where(i > 0, h_ref[...], 0.0)
        pre, shifted = _conv_taps(jnp.concatenate([halo_v, u_ref[...]], axis=0), w_ref[...], tb)
        dp = d_ref[...] * _dsilu(pre + b_ref[...])
        dp_ref[...] = dp

        @pl.when(i == 0)
        def _():
            dw_ref[...] = jnp.zeros_like(dw_ref)
            db_ref[...] = jnp.zeros_like(db_ref)

        dw_ref[...] += jnp.concatenate([jnp.sum(dp * s, axis=0, keepdims=True) for s in shifted], axis=0)
        db_ref[...] += jnp.sum(dp, axis=0, keepdims=True)

    return _call(body, name, (c // tc, n // tb), [blk, halo, wsp, bsp, blk], [blk, wsp, bsp],
                 [_sds((n, c), F32), _sds((SSD_CONV, c), F32), _sds((1, c), F32)],
                 sem=("parallel", "arbitrary"))(u, u, w, b.reshape(1, c), dout)


def _conv_bwd_in(dpre, w, name):
    n, c = dpre.shape
    tb, tc = _tile(n, 512, 8), _tile(c, 1536)
    hb = tb // 8
    nb = n // tb
    blk = pl.BlockSpec((tb, tc), lambda j, i: (i, j))
    nxt = pl.BlockSpec((8, tc), lambda j, i: (jnp.minimum((i + 1) * hb, n // 8 - 1), j))
    wsp = pl.BlockSpec((SSD_CONV, tc), lambda j, i: (0, j))

    def body(d_ref, n_ref, w_ref, o_ref):
        nxt_v = jnp.where(pl.program_id(1) < nb - 1, n_ref[...], 0.0)
        ext = jnp.concatenate([d_ref[...], nxt_v], axis=0)
        wv = w_ref[...]
        acc = ext[:tb] * wv[SSD_CONV - 1:SSD_CONV]
        for k in range(SSD_CONV - 1):
            s = SSD_CONV - 1 - k
            acc = acc + pltpu.roll(ext, tb + 8 - s, 0)[:tb] * wv[k:k + 1]
        o_ref[...] = acc.astype(o_ref.dtype)

    return _call(body, name, (c // tc, nb), [blk, nxt, wsp], blk, _sds((n, c), BF16),
                 sem=("parallel", "arbitrary"))(dpre, dpre, w)


def _col(v, j):
    lane = lax.broadcasted_iota(jnp.int32, v.shape, 1)
    return jnp.sum(jnp.where(lane == j, v, 0.0), axis=1, keepdims=True)


def _row(v, j):
    sub = lax.broadcasted_iota(jnp.int32, v.shape, 0)
    return jnp.sum(jnp.where(sub == j, v, 0.0), axis=0, keepdims=True)


def _ssd_head_terms(dtc, dtr, a_row, a_col):
    ll = dtc.shape[0]
    r = lax.broadcasted_iota(jnp.int32, (ll, ll), 0)
    c = lax.broadcasted_iota(jnp.int32, (ll, ll), 1)
    tril = (r >= c)
    trilf = tril.astype(F32)
    triuf = (r <= c).astype(F32)
    cumc = _nn(trilf, dtc * a_row, HIGHEST)
    cumr = _nn(dtr * a_col, triuf, HIGHEST)
    return cumc, cumr, tril, trilf, triuf


def _ssd_specs(n, di, gn, jh, p, nc, rev):
    ll = SSD_CHUNK
    jp = jh * p

    def ci(c):
        return (nc - 1 - c) if rev else c

    xs = pl.BlockSpec((ll, jp), lambda g, c: (ci(c), g))
    bs = pl.BlockSpec((ll, SSD_STATE), lambda g, c: (ci(c), di // SSD_STATE + g))
    cs = pl.BlockSpec((ll, SSD_STATE), lambda g, c: (ci(c), (di + gn * SSD_STATE) // SSD_STATE + g))
    dtc = pl.BlockSpec((None, ll, jh), lambda g, c: (g, ci(c), 0))
    dtr = pl.BlockSpec((None, jh, ll), lambda g, c: (g, 0, ci(c)))
    arow = pl.BlockSpec((None, 1, jh), lambda g, c: (g, 0, 0))
    acol = pl.BlockSpec((None, jh, 1), lambda g, c: (g, 0, 0))
    dsk = pl.BlockSpec((None, 1, jp), lambda g, c: (g, 0, 0))
    hin = pl.BlockSpec((None, None, SSD_STATE, jp), lambda g, c: (g, ci(c), 0, 0))
    ys = pl.BlockSpec((ll, jp), lambda g, c: (ci(c), g))
    return xs, bs, cs, dtc, dtr, arow, acol, dsk, hin, ys


def _ssd_fwd(xbc, dtc, dtr, a_row, a_col, dskip, name, *, di, p, carry=None):
    n = xbc.shape[0]
    gn = SSD_GROUPS
    jh = dtc.shape[2]
    jp = jh * p
    ll = SSD_CHUNK
    nc = n // ll
    xs, bs, cs, dtcs, dtrs, arow, acol, dsk, hin, ys = _ssd_specs(n, di, gn, jh, p, nc, False)
    pair = 2 * p

    def body(x_ref, b_ref, c_ref, dtc_ref, dtr_ref, ar_ref, ac_ref, ds_ref, y_ref, hin_ref, h_scr):
        @pl.when(pl.program_id(1) == 0)
        def _():
            h_scr[...] = jnp.zeros_like(h_scr)

        dtcv, dtrv = dtc_ref[...], dtr_ref[...]
        cumc, cumr, tril, _, _ = _ssd_head_terms(dtcv, dtrv, ar_ref[...], ac_ref[...])
        tot = jnp.sum(dtcv * ar_ref[...], axis=0, keepdims=True)
        bb, cb_ = b_ref[...].astype(BF16), c_ref[...].astype(BF16)
        cbm = _nt(cb_, bb)
        hin_ref[...] = h_scr[...]
        lane = lax.broadcasted_iota(jnp.int32, (ll, pair), 1)
        lane1 = lax.broadcasted_iota(jnp.int32, (1, pair), 1)
        for pr in range(jh // 2):
            sl = slice(pr * pair, (pr + 1) * pair)
            xp = x_ref[:, sl]
            xpb = xp.astype(BF16)
            hp = h_scr[:, sl]
            ydiag = jnp.zeros((ll, pair), F32)
            e_p = jnp.zeros((ll, pair), F32)
            w_p = jnp.zeros((ll, pair), F32)
            cd_p = jnp.zeros((1, pair), F32)
            for q in range(2):
                j = 2 * pr + q
                mj = (lane >= p) if q else (lane < p)
                cc, cr = _col(cumc, j), _row(cumr, j)
                decay = jnp.exp(jnp.where(tril, cc - cr, -1e30))
                mm = cbm * decay * _row(dtrv, j)
                ydiag = ydiag + _nn(mm.astype(BF16), jnp.where(mj, xpb, jnp.zeros_like(xpb)))
                cl = _col(tot, j)
                e_p = jnp.where(mj, jnp.exp(cc), e_p)
                w_p = jnp.where(mj, jnp.exp(cl - cc) * _col(dtcv, j), w_p)
                cd_p = jnp.where((lane1 >= p) if q else (lane1 < p), jnp.exp(cl), cd_p)
            yoff = _nn(cb_, hp.astype(BF16)) * e_p
            y_ref[:, sl] = ydiag + yoff + xp * ds_ref[:, sl]
            h_scr[:, sl] = hp * cd_p + _tn(bb, (xp * w_p).astype(BF16))

    return _call(body, name, (gn, nc), [xs, bs, cs, dtcs, dtrs, arow, acol, dsk], [ys, hin],
                 [_sds((n, di), F32), _sds((gn, nc, SSD_STATE, jp), F32)],
                 scratch=[pltpu.VMEM((SSD_STATE, jp), F32)], sem=("parallel", "arbitrary"), carry=carry)(
        xbc, xbc, xbc, dtc, dtr, a_row, a_col, dskip)


def _ssd_bwd(xbc, dtc, dtr, a_row, a_col, dskip, hins, dy, name, *, di, p, carry=None):
    n = xbc.shape[0]
    gn = SSD_GROUPS
    jh = dtc.shape[2]
    jp = jh * p
    ll = SSD_CHUNK
    nc = n // ll
    xs, bs, cs, dtcs, dtrs, arow, acol, dsk, hin, ys = _ssd_specs(n, di, gn, jh, p, nc, True)
    pair = 2 * p
    bc_out = pl.BlockSpec((ll, SSD_STATE), lambda g, c: (nc - 1 - c, g))

    def body(x_ref, b_ref, c_ref, dtc_ref, dtr_ref, ar_ref, ac_ref, ds_ref, hin_ref, dy_ref,
             dx_ref, db_ref, dc_ref, ddtc_ref, ddtr_ref, dac_ref, dar_ref, dd_ref, dh_scr):
        first = pl.program_id(1) == 0

        @pl.when(first)
        def _():
            dh_scr[...] = jnp.zeros_like(dh_scr)
            dac_ref[...] = jnp.zeros_like(dac_ref)
            dar_ref[...] = jnp.zeros_like(dar_ref)
            dd_ref[...] = jnp.zeros_like(dd_ref)

        dtcv, dtrv = dtc_ref[...], dtr_ref[...]
        a_r, a_c = ar_ref[...], ac_ref[...]
        cumc, cumr, tril, trilf, triuf = _ssd_head_terms(dtcv, dtrv, a_r, a_c)
        tot = jnp.sum(dtcv * a_r, axis=0, keepdims=True)
        bb, cb_ = b_ref[...].astype(BF16), c_ref[...].astype(BF16)
        cbm = _nt(cb_, bb)
        lane = lax.broadcasted_iota(jnp.int32, (ll, pair), 1)
        lane1 = lax.broadcasted_iota(jnp.int32, (1, pair), 1)
        lane_j = lax.broadcasted_iota(jnp.int32, (ll, jh), 1)
        sub_l = lax.broadcasted_iota(jnp.int32, (ll, jh), 0)
        sub_j = lax.broadcasted_iota(jnp.int32, (jh, ll), 0)
        dcb = jnp.zeros((ll, ll), F32)
        db_acc = jnp.zeros((ll, SSD_STATE), F32)
        dc_acc = jnp.zeros((ll, SSD_STATE), F32)
        dcum_c = jnp.zeros((ll, jh), F32)
        dcum_r = jnp.zeros((jh, ll), F32)
        ddt_c = jnp.zeros((ll, jh), F32)
        ddt_r = jnp.zeros((jh, ll), F32)
        for pr in range(jh // 2):
            sl = slice(pr * pair, (pr + 1) * pair)
            xp = x_ref[:, sl]
            xpb = xp.astype(BF16)
            dyp = dy_ref[:, sl]
            hp = hin_ref[:, sl]
            hpb = hp.astype(BF16)
            dhp = dh_scr[:, sl]
            dhpb = dhp.astype(BF16)
            ch = _nn(cb_, hpb)
            gp = _nn(bb, dhpb)
            e_p = jnp.zeros((ll, pair), F32)
            w_p = jnp.zeros((ll, pair), F32)
            cd_p = jnp.zeros((1, pair), F32)
            dxp = dyp * ds_ref[:, sl]
            heads = []
            for q in range(2):
                j = 2 * pr + q
                mj = (lane >= p) if q else (lane < p)
                cc, cr = _col(cumc, j), _row(cumr, j)
                cl = _col(tot, j)
                ej = jnp.exp(cc)
                wdec = jnp.exp(cl - cc)
                wj = wdec * _col(dtcv, j)
                e_p = jnp.where(mj, ej, e_p)
                w_p = jnp.where(mj, wj, w_p)
                cdj = jnp.exp(cl)
                cd_p = jnp.where((lane1 >= p) if q else (lane1 < p), cdj, cd_p)
                heads.append((j, mj, cc, cr, cl, wdec, wj, cdj))
            dye = dyp * e_p
            dyeb = dye.astype(BF16)
            dc_acc = dc_acc + _nt(dyeb, hpb)
            dh_new = dhp * cd_p + _tn(cb_, dyeb)
            dxp = dxp + gp * w_p
            db_acc = db_acc + _nt((xp * w_p).astype(BF16), dhpb)
            t_off = dye * ch
            t_w = gp * xp
            t_cd = jnp.sum(dhp * hp, axis=0, keepdims=True)
            for (j, mj, cc, cr, cl, wdec, wj, cdj) in heads:
                dyj = jnp.where(mj, dyp, 0.0).astype(BF16)
                decay = jnp.exp(jnp.where(tril, cc - cr, -1e30))
                dtrow = _row(dtrv, j)
                mm = cbm * decay * dtrow
                dm = _nt(dyj, xpb)
                dxp = dxp + _tn(mm.astype(BF16), dyj)
                dcb = dcb + dm * decay * dtrow
                ddt_rj = jnp.sum(dm * cbm * decay, axis=0, keepdims=True)
                dseg = dm * mm
                dcum_cj = jnp.sum(dseg, axis=1, keepdims=True) + jnp.sum(jnp.where(mj, t_off, 0.0), axis=1, keepdims=True)
                dcum_rj = -jnp.sum(dseg, axis=0, keepdims=True)
                dwj = jnp.sum(jnp.where(mj, t_w, 0.0), axis=1, keepdims=True)
                ddt_cj = dwj * wdec
                qj = dwj * wj
                dcum_cj = dcum_cj - qj
                m1 = (lane1 >= p) if (j % 2) else (lane1 < p)
                dcl = jnp.sum(jnp.where(m1, t_cd, 0.0), axis=1, keepdims=True) * cdj + jnp.sum(qj, axis=0, keepdims=True)
                dcum_c = dcum_c + jnp.where(lane_j == j, dcum_cj, 0.0) + jnp.where((lane_j == j) & (sub_l == ll - 1), dcl, 0.0)
                dcum_r = dcum_r + jnp.where(sub_j == j, dcum_rj, 0.0)
                ddt_c = ddt_c + jnp.where(lane_j == j, ddt_cj, 0.0)
                ddt_r = ddt_r + jnp.where(sub_j == j, ddt_rj, 0.0)
            dx_ref[:, sl] = dxp
            dd_ref[:, sl] += jnp.sum(dyp * xp, axis=0, keepdims=True)
            dh_scr[:, sl] = dh_new
        dcbb = dcb.astype(BF16)
        dc_ref[...] = dc_acc + _nn(dcbb, bb)
        db_ref[...] = db_acc + _tn(dcbb, cb_)
        dda_c = _nn(triuf, dcum_c, HIGHEST)
        dda_r = _nn(dcum_r, trilf, HIGHEST)
        ddtc_ref[...] = ddt_c + dda_c * a_r
        ddtr_ref[...] = ddt_r + dda_r * a_c
        dac_ref[...] += jnp.sum(dda_c * dtcv, axis=0, keepdims=True)
        dar_ref[...] += jnp.sum(dda_r * dtrv, axis=1, keepdims=True)

    gs = SSD_GROUPS * SSD_STATE
    return _call(body, name, (gn, nc), [xs, bs, cs, dtcs, dtrs, arow, acol, dsk, hin, ys],
                 [ys, bc_out, bc_out, dtcs, dtrs, arow, acol, dsk],
                 [_sds((n, di), F32), _sds((n, gs), F32), _sds((n, gs), F32), _sds(dtc.shape, F32), _sds(dtr.shape, F32),
                  _sds((gn, 1, jh), F32), _sds((gn, jh, 1), F32), _sds((gn, 1, jp), F32)],
                 scratch=[pltpu.VMEM((SSD_STATE, jp), F32)], sem=("parallel", "arbitrary"), carry=carry)(
        xbc, xbc, xbc, dtc, dtr, a_row, a_col, dskip, hins, dy)


def _band_masks():
    r = lax.broadcasted_iota(jnp.int32, (DIL_BLOCK, DIL_BLOCK), 0)
    c = lax.broadcasted_iota(jnp.int32, (DIL_BLOCK, DIL_BLOCK), 1)
    return c >= r, c <= r


def _dil_attn_fwd(q3, k3, v3, name, *, dh, carry=None):
    dil, n, dw = q3.shape
    nb = n // DIL_BLOCK
    scale = dh ** -0.5
    tb = DIL_BLOCK
    cur = pl.BlockSpec((None, tb, dw), lambda r, jb: (r, jb, 0))
    prev = pl.BlockSpec((None, tb, dw), lambda r, jb: (r, jnp.maximum(jb - 1, 0), 0))
    o_spec = cur
    l_spec = pl.BlockSpec((None, tb, LANES), lambda r, jb: (r, jb, 0))

    def body(q_ref, kc_ref, kp_ref, vc_ref, vp_ref, o_ref, l_ref, s_scr, p_scr, d_scr):
        mp, mc = _band_masks()
        mp = mp & (pl.program_id(1) > 0)
        for h in range(DIL_HEADS):
            sl = slice(h * dh, (h + 1) * dh)
            qh = q_ref[:, sl]
            s_scr[h, :, :tb] = jnp.where(mp, _nt(qh, kp_ref[:, sl]) * scale, -jnp.inf)
            s_scr[h, :, tb:] = jnp.where(mc, _nt(qh, kc_ref[:, sl]) * scale, -jnp.inf)
        lane = lax.broadcasted_iota(jnp.int32, (ATTN_ROWS, LANES), 1)
        for h in range(DIL_HEADS):
            for rc in range(tb // ATTN_ROWS):
                rows = slice(rc * ATTN_ROWS, (rc + 1) * ATTN_ROWS)
                s = s_scr[h, rows, :]
                mx = jnp.max(jnp.maximum(s[:, :tb], s[:, tb:]), axis=1, keepdims=True)
                pm = jnp.exp(s - mx)
                den = jnp.sum(pm[:, :tb] + pm[:, tb:], axis=1, keepdims=True)
                p_scr[h, rows, :] = pm.astype(BF16)
                d_scr[h, rows, :] = den
                lse = mx + jnp.log(den)
                l_ref[rows, :] = jnp.where(lane == h, lse, 0.0 if h == 0 else l_ref[rows, :])
        for h in range(DIL_HEADS):
            sl = slice(h * dh, (h + 1) * dh)
            o = _nn(p_scr[h, :, :tb], vp_ref[:, sl]) + _nn(p_scr[h, :, tb:], vc_ref[:, sl])
            o_ref[:, sl] = o / d_scr[h]

    return _call(body, name, (dil, nb), [cur, cur, prev, cur, prev], [o_spec, l_spec],
                 [_sds((dil, n, dw), F32), _sds((dil, n, LANES), F32)],
                 scratch=[pltpu.VMEM((DIL_HEADS, tb, 2 * tb), F32), pltpu.VMEM((DIL_HEADS, tb, 2 * tb), BF16),
                          pltpu.VMEM((DIL_HEADS, tb, 1), F32)],
                 sem=("parallel", "arbitrary"), carry=carry)(q3, k3, k3, v3, v3)


def _dil_attn_bwd(q3, k3, v3, do3, lse3, dl3, name, *, dh, carry=None):
    dil, n, dw = q3.shape
    nb = n // DIL_BLOCK
    scale = dh ** -0.5
    tb = DIL_BLOCK

    def nx(jb):
        return jnp.minimum(jb + 1, nb - 1)

    q_c = pl.BlockSpec((None, tb, dw), lambda r, jb: (r, jb, 0))
    q_n = pl.BlockSpec((None, tb, dw), lambda r, jb: (r, nx(jb), 0))
    k_p = pl.BlockSpec((None, tb, dw), lambda r, jb: (r, jnp.maximum(jb - 1, 0), 0))
    o_c, o_n = q_c, q_n
    l_c = pl.BlockSpec((None, tb, LANES), lambda r, jb: (r, jb, 0))
    l_n = pl.BlockSpec((None, tb, LANES), lambda r, jb: (r, nx(jb), 0))

    def body(qc_ref, qn_ref, kc_ref, kp_ref, vc_ref, vp_ref, doc_ref, don_ref, lc_ref, ln_ref, dlc_ref, dln_ref,
             dq_ref, dk_ref, dv_ref, s_scr, dp_scr, p_scr, ds_scr):
        jb = pl.program_id(1)
        has_prev = jb > 0
        has_next = jb < nb - 1
        for h in range(DIL_HEADS):
            sl = slice(h * dh, (h + 1) * dh)
            qc, qn, kc, kp = qc_ref[:, sl], qn_ref[:, sl], kc_ref[:, sl], kp_ref[:, sl]
            vc, vp, doc, don = vc_ref[:, sl], vp_ref[:, sl], doc_ref[:, sl], don_ref[:, sl]
            for pi, (qh, kh, vh, doh) in enumerate(((qc, kp, vp, doc), (qc, kc, vc, doc), (qn, kc, vc, don))):
                s_scr[h, pi] = _nt(qh, kh) * scale
                dp_scr[h, pi] = _nt(doh, vh)
        r = lax.broadcasted_iota(jnp.int32, (ATTN_ROWS, tb), 0)
        c = lax.broadcasted_iota(jnp.int32, (ATTN_ROWS, tb), 1)
        for h in range(DIL_HEADS):
            for rc in range(tb // ATTN_ROWS):
                rows = slice(rc * ATTN_ROWS, (rc + 1) * ATTN_ROWS)
                m_prev, m_cur = c >= r + rc * ATTN_ROWS, c <= r + rc * ATTN_ROWS
                lch, dlch = _col(lc_ref[rows, :], h), _col(dlc_ref[rows, :], h)
                lnh, dlnh = _col(ln_ref[rows, :], h), _col(dln_ref[rows, :], h)
                for pi, (lse_h, dl_h, mask) in enumerate(((lch, dlch, m_prev & has_prev), (lch, dlch, m_cur),
                                                          (lnh, dlnh, m_prev & has_next))):
                    pm = jnp.where(mask, jnp.exp(s_scr[h, pi, rows, :] - lse_h), 0.0)
                    ds_scr[h, pi, rows, :] = (pm * (dp_scr[h, pi, rows, :] - dl_h) * scale).astype(BF16)
                    if pi > 0:
                        p_scr[h, pi - 1, rows, :] = pm.astype(BF16)
        for h in range(DIL_HEADS):
            sl = slice(h * dh, (h + 1) * dh)
            dq_ref[:, sl] = _nn(ds_scr[h, 0], kp_ref[:, sl]) + _nn(ds_scr[h, 1], kc_ref[:, sl])
            dk_ref[:, sl] = _tn(ds_scr[h, 1], qc_ref[:, sl]) + _tn(ds_scr[h, 2], qn_ref[:, sl])
            dv_ref[:, sl] = _tn(p_scr[h, 0], doc_ref[:, sl]) + _tn(p_scr[h, 1], don_ref[:, sl])

    out = _sds((dil, n, dw), F32)
    return _call(body, name, (dil, nb), [q_c, q_n, q_c, k_p, q_c, k_p, o_c, o_n, l_c, l_n, l_c, l_n],
                 [o_c, o_c, o_c], [out, out, out],
                 scratch=[pltpu.VMEM((DIL_HEADS, 3, tb, tb), F32), pltpu.VMEM((DIL_HEADS, 3, tb, tb), F32),
                          pltpu.VMEM((DIL_HEADS, 2, tb, tb), BF16), pltpu.VMEM((DIL_HEADS, 3, tb, tb), BF16)],
                 sem=("parallel", "arbitrary"), carry=carry)(
        q3, q3, k3, k3, v3, v3, do3, do3, lse3, lse3, dl3, dl3)


def _head_expand(dw, dh):
    r = lax.broadcasted_iota(jnp.int32, (LANES, dw), 0)
    c = lax.broadcasted_iota(jnp.int32, (LANES, dw), 1)
    return ((c // dh) == r).astype(F32)


def _mix_weights(l0, l1, l2):
    mx = jnp.maximum(jnp.maximum(l0, l1), l2)
    e = [jnp.exp(l0 - mx), jnp.exp(l1 - mx), jnp.exp(l2 - mx)]
    den = e[0] + e[1] + e[2]
    return [v / den for v in e]


MIX_ROWS = 256


def _tile_scratch(tb, w):
    return pltpu.VMEM((w // LANES, tb, LANES), F32)


def _read_tiles(scr):
    return jnp.concatenate([scr[ct] for ct in range(scr.shape[0])], axis=1)


def _write_tiles(scr, val):
    for ct in range(scr.shape[0]):
        scr[ct] = val[:, ct * LANES:(ct + 1) * LANES]


def _to_token_order(scr, blk_ref, dil):
    rows = blk_ref.shape[1]
    for ct in range(scr.shape[0]):
        cs = slice(ct * LANES, (ct + 1) * LANES)
        if dil == 1:
            scr[ct] = blk_ref[0, :, cs].astype(F32)
        else:
            for r in range(dil):
                scr.at[ct][pl.ds(r, rows, stride=dil), :] = blk_ref[r, :, cs].astype(F32)


def _to_residue_order(out_ref, scr, dil):
    rows = out_ref.shape[1]
    for ct in range(scr.shape[0]):
        cs = slice(ct * LANES, (ct + 1) * LANES)
        if dil == 1:
            out_ref[0, :, cs] = scr[ct].astype(out_ref.dtype)
        else:
            for r in range(dil):
                out_ref[r, :, cs] = scr.at[ct][pl.ds(r, rows, stride=dil), :].astype(out_ref.dtype)


def _residue_spec(dil, tb, w):
    return pl.BlockSpec((dil, tb // dil, w), lambda i: (0, i, 0))


def _mix_fwd(os_, ls_, name, *, dh):
    dw = os_[0].shape[2]
    n = os_[0].shape[0] * os_[0].shape[1]
    tb = MIX_ROWS

    def body(o0, o1, o2, l0, l1, l2, out_ref, so0, so1, so2, sl0, sl1, sl2):
        for src, scr, dil in zip((o0, o1, o2, l0, l1, l2), (so0, so1, so2, sl0, sl1, sl2), DIL_DILATIONS * 2):
            _to_token_order(scr, src, dil)
        ex = _head_expand(dw, dh)
        ws = _mix_weights(_read_tiles(sl0), _read_tiles(sl1), _read_tiles(sl2))
        out_ref[...] = sum(_nn(wv, ex, HIGHEST) * _read_tiles(s) for wv, s in zip(ws, (so0, so1, so2))).astype(
            out_ref.dtype)

    specs = [_residue_spec(dl, tb, dw) for dl in DIL_DILATIONS] + [_residue_spec(dl, tb, LANES) for dl in DIL_DILATIONS]
    return _call(body, name, (n // tb,), specs, pl.BlockSpec((tb, dw), lambda i: (i, 0)), _sds((n, dw), BF16),
                 scratch=[_tile_scratch(tb, dw)] * 3 + [_tile_scratch(tb, LANES)] * 3, sem=("parallel",))(
        *os_, *ls_)


def _mix_bwd(os_, ls_, do, name, *, dh):
    dw = os_[0].shape[2]
    n = os_[0].shape[0] * os_[0].shape[1]
    tb = MIX_ROWS

    def body(o0, o1, o2, l0, l1, l2, do_ref, d0, d1, d2, r0, r1, r2, so0, so1, so2, sl0, sl1, sl2):
        for src, scr, dil in zip((o0, o1, o2, l0, l1, l2), (so0, so1, so2, sl0, sl1, sl2), DIL_DILATIONS * 2):
            _to_token_order(scr, src, dil)
        ex = _head_expand(dw, dh)
        dov = do_ref[...]
        ws = _mix_weights(_read_tiles(sl0), _read_tiles(sl1), _read_tiles(sl2))
        dws = [_nt(dov * _read_tiles(s), ex, HIGHEST) for s in (so0, so1, so2)]
        sdw = ws[0] * dws[0] + ws[1] * dws[1] + ws[2] * dws[2]
        for wv, so, sl_, d_ref, r_ref, dil in zip(ws, (so0, so1, so2), (sl0, sl1, sl2), (d0, d1, d2), (r0, r1, r2),
                                                  DIL_DILATIONS):
            _write_tiles(so, _nn(wv, ex, HIGHEST) * dov)
            _write_tiles(sl_, wv * sdw)
            _to_residue_order(d_ref, so, dil)
            _to_residue_order(r_ref, sl_, dil)

    specs = [_residue_spec(dl, tb, dw) for dl in DIL_DILATIONS] + [_residue_spec(dl, tb, LANES) for dl in DIL_DILATIONS]
    return _call(body, name, (n // tb,), specs + [pl.BlockSpec((tb, dw), lambda i: (i, 0))], specs,
                 [_sds(o.shape, BF16) for o in os_] + [_sds(l.shape, F32) for l in ls_],
                 scratch=[_tile_scratch(tb, dw)] * 3 + [_tile_scratch(tb, LANES)] * 3, sem=("parallel",))(
        *os_, *ls_, do)


def _to_residue(x, cos2, sin2, name, *, dw, rope):
    n = x.shape[0]
    dh = cos2.shape[1]
    tb = MIX_ROWS

    def body(x_ref, c_ref, s_ref, o0, o1, o2, scr):
        c, s = c_ref[...], s_ref[...]
        for g, (o_ref, dil) in enumerate(zip((o0, o1, o2), DIL_DILATIONS)):
            for hh in range(dw // dh):
                v = x_ref[:, g * dw + hh * dh:g * dw + (hh + 1) * dh]
                if rope:
                    v = v * c + pltpu.roll(v, dh // 2, 1) * s
                lo = (hh * dh) % LANES
                scr[(hh * dh) // LANES, :, lo:lo + dh] = v
            _to_residue_order(o_ref, scr, dil)

    row = pl.BlockSpec((tb, dh), lambda i: (i, 0))
    return _call(body, name, (n // tb,), [pl.BlockSpec((tb, 3 * dw), lambda i: (i, 0)), row, row],
                 [_residue_spec(dl, tb, dw) for dl in DIL_DILATIONS],
                 [_sds((dl, n // dl, dw), BF16) for dl in DIL_DILATIONS],
                 scratch=[_tile_scratch(tb, dw)], sem=("parallel",))(x, cos2, sin2)


FROM_RESIDUE_ROWS = 128


def _from_residue(cols, cos2, sin2, name, *, dw, rope_cols):
    dils = DIL_DILATIONS * (len(cols) // len(DIL_DILATIONS))
    n = cols[0][0].shape[0] * cols[0][0].shape[1]
    dh = cos2.shape[1]
    tb = FROM_RESIDUE_ROWS
    counts = [len(cl) for cl in cols]

    def body(*refs):
        n_in = sum(counts)
        srcs, (c_ref, s_ref, o_ref, scr) = refs[:n_in], refs[n_in:]
        c, s = c_ref[...], s_ref[...]
        k = 0
        for cb, (cnt, dil) in enumerate(zip(counts, dils)):
            rows = tb // dil
            for ct in range(dw // LANES):
                cs = slice(ct * LANES, (ct + 1) * LANES)
                for r in range(dil):
                    acc = srcs[k][r, :, cs]
                    for extra in srcs[k + 1:k + cnt]:
                        acc = acc + extra[r, :, cs]
                    if dil == 1:
                        scr[ct] = acc
                    else:
                        scr.at[ct][pl.ds(r, rows, stride=dil), :] = acc
            k += cnt
            for hh in range(dw // dh):
                lo = (hh * dh) % LANES
                v = scr[(hh * dh) // LANES, :, lo:lo + dh]
                if cb < rope_cols:
                    v = v * c - pltpu.roll(v, dh // 2, 1) * s
                o_ref[:, cb * dw + hh * dh:cb * dw + (hh + 1) * dh] = v.astype(o_ref.dtype)

    row = pl.BlockSpec((tb, dh), lambda i: (i, 0))
    specs = [_residue_spec(dil, tb, dw) for cnt, dil in zip(counts, dils) for _ in range(cnt)]
    return _call(body, name, (n // tb,), specs + [row, row], pl.BlockSpec((tb, len(cols) * dw), lambda i: (i, 0)),
                 _sds((n, len(cols) * dw), BF16), scratch=[_tile_scratch(tb, dw)], sem=("parallel",))(
        *[a for cl in cols for a in cl], cos2, sin2)


def _mem_attn_fwd(q, kv, name):
    n, mw = q.shape
    ml = kv.shape[0]
    dh = mw // MEM_HEADS
    scale = dh ** -0.5
    tb = _tile(n, 512, 8)

    def body(q_ref, kv_ref, o_ref):
        for h in range(MEM_HEADS):
            sl = slice(h * dh, (h + 1) * dh)
            s = _nt(q_ref[:, sl].astype(BF16), kv_ref[:, sl].astype(BF16)) * scale
            pm = jnp.exp(s - jnp.max(s, axis=1, keepdims=True))
            pm = pm / jnp.sum(pm, axis=1, keepdims=True)
            o_ref[:, sl] = _nn(pm.astype(BF16), kv_ref[:, mw + h * dh:mw + (h + 1) * dh].astype(BF16)).astype(o_ref.dtype)

    return _call(body, name, (n // tb,), [pl.BlockSpec((tb, mw), lambda i: (i, 0)), pl.BlockSpec((ml, 2 * mw), lambda i: (0, 0))],
                 pl.BlockSpec((tb, mw), lambda i: (i, 0)), _sds((n, mw), BF16), sem=("parallel",))(q, kv)


def _mem_attn_bwd(q, kv, do, name):
    n, mw = q.shape
    ml = kv.shape[0]
    dh = mw // MEM_HEADS
    scale = dh ** -0.5
    tb = _tile(n, 512, 8)

    def body(q_ref, kv_ref, do_ref, dq_ref, dkv_ref):
        @pl.when(pl.program_id(0) == 0)
        def _():
            dkv_ref[...] = jnp.zeros_like(dkv_ref)

        for h in range(MEM_HEADS):
            sl = slice(h * dh, (h + 1) * dh)
            vsl = slice(mw + h * dh, mw + (h + 1) * dh)
            qh, kh, vh = q_ref[:, sl].astype(BF16), kv_ref[:, sl].astype(BF16), kv_ref[:, vsl].astype(BF16)
            doh = do_ref[:, sl].astype(BF16)
            s = _nt(qh, kh) * scale
            pm = jnp.exp(s - jnp.max(s, axis=1, keepdims=True))
            pm = pm / jnp.sum(pm, axis=1, keepdims=True)
            dp = _nt(doh, vh)
            ds = (pm * (dp - jnp.sum(pm * dp, axis=1, keepdims=True)) * scale).astype(BF16)
            dq_ref[:, sl] = _nn(ds, kh).astype(dq_ref.dtype)
            dkv_ref[:, sl] += _tn(ds, qh)
            dkv_ref[:, vsl] += _tn(pm.astype(BF16), doh)

    row = pl.BlockSpec((tb, mw), lambda i: (i, 0))
    kvs = pl.BlockSpec((ml, 2 * mw), lambda i: (0, 0))
    return _call(body, name, (n // tb,), [row, kvs, row], [row, kvs], [_sds((n, mw), BF16), _sds((ml, 2 * mw), F32)],
                 sem=("arbitrary",))(q, kv, do)


def _adamw(w, g, m, v, name):
    shape = w.shape
    c = shape[-1]
    r = max(1, math.prod(shape[:-1]))
    w2, g2, m2, v2 = (t.reshape(r, c) for t in (w, g, m, v))
    tb = _tile(r, max(8, (1 << 19) // max(c, 1) // 8 * 8), 8)
    bc1 = 1.0 - ADAM_B1 ** ADAM_STEP
    bc2 = 1.0 - ADAM_B2 ** ADAM_STEP

    def body(w_ref, g_ref, m_ref, v_ref, d_ref, nm_ref, nv_ref):
        gv = g_ref[...]
        nm = ADAM_B1 * m_ref[...] + (1.0 - ADAM_B1) * gv
        nv = ADAM_B2 * v_ref[...] + (1.0 - ADAM_B2) * (gv * gv)
        d_ref[...] = -ADAM_LR * ((nm / bc1) / (jnp.sqrt(nv / bc2) + ADAM_EPS) + ADAM_WD * w_ref[...])
        nm_ref[...] = nm
        nv_ref[...] = nv

    spec = pl.BlockSpec((tb, c), lambda i: (i, 0))
    o = _sds((r, c), F32)
    d, nm, nv = _call(body, name, (r // tb,), [spec] * 4, [spec] * 3, [o, o, o], sem=("parallel",))(w2, g2, m2, v2)
    return d.reshape(shape), nm.reshape(shape), nv.reshape(shape)


def _place():
    x, y, c = lax.axis_index("x"), lax.axis_index("y"), lax.axis_index("c")
    chips = [(1 - x, y), (x, 1 - y), (1 - x, 1 - y)]
    return x, y, c, chips


def _cols(ref, c):
    hw = ref.shape[-1] // 2
    return ref.at[(slice(None),) * (len(ref.shape) - 1) + (pl.ds(c * hw, hw),)]


def _slot(ref, s):
    return ref.at[:, s]


def _my_chip():
    return 2 * lax.axis_index("x") + lax.axis_index("y")


def _place_shard(t, name):
    nl, r, n = t.shape
    tb = _tile(r, 512, SUBLANES_BF16)

    def body(t_ref, o_ref):
        o_ref[...] = t_ref[...].astype(BF16)

    return _call(body, name, (nl, r // tb), [pl.BlockSpec((None, tb, n), lambda l, i: (l, i, 0))],
                 pl.BlockSpec((None, None, tb, n), lambda l, i: (l, _my_chip(), i, 0)),
                 _sds((nl, 4, r, n), BF16), sem=("parallel", "parallel"))(t)


def _aliased_comm_call(body, name, bufs, n_sems):
    na = len(bufs)
    anyspec = pl.BlockSpec(memory_space=pl.ANY)
    return pl.pallas_call(
        body, name=name, in_specs=[anyspec] * na, out_specs=[anyspec] * na,
        out_shape=[_sds(b.shape, b.dtype) for b in bufs], input_output_aliases={a: a for a in range(na)},
        scratch_shapes=[pltpu.SemaphoreType.DMA((n_sems,)), pltpu.SemaphoreType.DMA((n_sems,))],
        compiler_params=pltpu.CompilerParams(has_side_effects=True))(*bufs)


def _same_block_copy(blk, send, recv, k, to):
    return pltpu.make_async_remote_copy(src_ref=blk, dst_ref=blk, send_sem=send.at[k], recv_sem=recv.at[k],
                                        device_id=to, device_id_type=MESH)


def _gather_start(bufs, send, recv):
    x, y, c, chips = _place()
    for a, buf in enumerate(bufs):
        for j, (px, py) in enumerate(chips):
            _same_block_copy(_cols(_slot(buf, 2 * x + y), c), send, recv, 6 * a + j, (px, py, c)).start()


def _gather_finish(bufs, send, recv):
    x, y, c, chips = _place()
    sib = (x, y, 1 - c)
    for a, buf in enumerate(bufs):
        for j, (px, py) in enumerate(chips):
            blk = _cols(_slot(buf, 2 * px + py), c)
            _same_block_copy(blk, send, recv, 6 * a + j, (px, py, c)).wait_recv()
            _same_block_copy(blk, send, recv, 6 * a + 3 + j, sib).start()
    for a, buf in enumerate(bufs):
        for j, (px, py) in enumerate(chips):
            _same_block_copy(_cols(_slot(buf, 2 * px + py), 1 - c), send, recv, 6 * a + 3 + j, sib).wait_recv()
    for a, buf in enumerate(bufs):
        for j, (px, py) in enumerate(chips):
            _same_block_copy(_cols(_slot(buf, 2 * x + y), c), send, recv, 6 * a + j, (px, py, c)).wait_send()
            _same_block_copy(_cols(_slot(buf, 2 * px + py), c), send, recv, 6 * a + 3 + j, sib).wait_send()


def _gather_carry(bufs):
    return _Carry(bufs, range(len(bufs)), 6 * len(bufs), _gather_start, _gather_finish)


def _gather_weights(bufs, name):
    na = len(bufs)

    def body(*refs):
        outs = refs[na:2 * na]
        send, recv = refs[2 * na:]
        _gather_start(outs, send, recv)
        _gather_finish(outs, send, recv)

    return _aliased_comm_call(body, name, bufs, 6 * na)


def _swap_halves(grads, name):
    na = len(grads)

    def body(*refs):
        srcs, outs = refs[:na], refs[na:2 * na]
        send, recv = refs[2 * na:]
        x, y, c, _ = _place()
        sib = (x, y, 1 - c)
        cps = [pltpu.make_async_remote_copy(src_ref=_cols(srcs[a], 1 - c), dst_ref=outs[a], send_sem=send.at[a],
                                            recv_sem=recv.at[a], device_id=sib, device_id_type=MESH) for a in range(na)]
        for cp in cps:
            cp.start()
        for cp in cps:
            cp.wait()

    anyspec = pl.BlockSpec(memory_space=pl.ANY)
    return pl.pallas_call(
        body, name=name, in_specs=[anyspec] * na, out_specs=[anyspec] * na,
        out_shape=[_sds(g.shape[:-1] + (g.shape[-1] // 2,), g.dtype) for g in grads],
        scratch_shapes=[pltpu.SemaphoreType.DMA((na,)), pltpu.SemaphoreType.DMA((na,))],
        compiler_params=pltpu.CompilerParams(has_side_effects=True))(*grads)


def _pair_sum(g, theirs, name):
    nl, _, r, n = g.shape
    hw = n // 2
    tb = _tile(r, 256, SUBLANES_BF16)

    def body(g_ref, t_ref, p_ref, own_ref):
        p_ref[...] = (g_ref[...] + t_ref[...]).astype(BF16)
        own_ref[...] = p_ref[_my_chip()]

    blk = pl.BlockSpec((None, 4, tb, hw), lambda l, i: (l, 0, i, 0))
    out = _sds((nl, 4, r, hw), BF16)
    return _call(body, name, (nl, r // tb),
                 [pl.BlockSpec((None, 4, tb, hw), lambda l, i: (l, 0, i, lax.axis_index("c"))), blk],
                 [blk, pl.BlockSpec((None, None, tb, hw), lambda l, i: (l, _my_chip(), i, 0))],
                 [out, out], sem=("parallel", "parallel"))(g, theirs)


def _scatter_start(bufs, send, recv):
    na = len(bufs) // 2
    x, y, c, chips = _place()
    for a in range(na):
        for j, (px, py) in enumerate(chips):
            pltpu.make_async_remote_copy(src_ref=_slot(bufs[a], 2 * px + py), dst_ref=_slot(bufs[na + a], 2 * x + y),
                                         send_sem=send.at[3 * a + j], recv_sem=recv.at[3 * a + j],
                                         device_id=(px, py, c), device_id_type=MESH).start()


def _scatter_finish(bufs, send, recv):
    na = len(bufs) // 2
    x, y, c, chips = _place()
    for a in range(na):
        for j, (px, py) in enumerate(chips):
            _same_block_copy(_slot(bufs[na + a], 2 * px + py), send, recv, 3 * a + j, (px, py, c)).wait_recv()
    for a in range(na):
        for j, (px, py) in enumerate(chips):
            _same_block_copy(_slot(bufs[a], 2 * px + py), send, recv, 3 * a + j, (px, py, c)).wait_send()


def _scatter_carry(parts, lands):
    na = len(parts)
    return _Carry(list(parts) + list(lands), range(na, 2 * na), 3 * na, _scatter_start, _scatter_finish)


def _scatter_partials(parts, lands, name):
    na = len(parts)

    def body(*refs):
        bufs = list(refs[:na]) + list(refs[2 * na:3 * na])
        send, recv = refs[3 * na:]
        _scatter_start(bufs, send, recv)
        _scatter_finish(bufs, send, recv)

    anyspec = pl.BlockSpec(memory_space=pl.ANY)
    return pl.pallas_call(
        body, name=name, in_specs=[anyspec] * (2 * na), out_specs=[anyspec] * na,
        out_shape=[_sds(b.shape, b.dtype) for b in lands],
        input_output_aliases={na + a: a for a in range(na)},
        scratch_shapes=[pltpu.SemaphoreType.DMA((3 * na,)), pltpu.SemaphoreType.DMA((3 * na,))],
        compiler_params=pltpu.CompilerParams(has_side_effects=True))(*parts, *lands)


def _chip_sum(land, n, name):
    nl, _, r, hw = land.shape
    tb = _tile(r, 512, SUBLANES_BF16)

    def body(i_ref, o_ref):
        o_ref[...] = ((i_ref[0].astype(F32) + i_ref[1].astype(F32)) + i_ref[2].astype(F32)) + i_ref[3].astype(F32)

    return _call(body, name, (nl, r // tb), [pl.BlockSpec((None, 4, tb, hw), lambda l, i: (l, 0, i, 0))],
                 pl.BlockSpec((None, tb, hw), lambda l, i: (l, i, lax.axis_index("c"))),
                 _sds((nl, r, n), F32), sem=("parallel", "parallel"))(land)


def _join_halves(bufs, name):
    na = len(bufs)

    def body(*refs):
        outs = refs[na:2 * na]
        send, recv = refs[2 * na:]
        x, y, c, _ = _place()
        sib = (x, y, 1 - c)
        cps = []
        for a in range(na):
            mine = _cols(outs[a], c)
            cp = pltpu.make_async_remote_copy(src_ref=mine, dst_ref=mine, send_sem=send.at[a], recv_sem=recv.at[a],
                                              device_id=sib, device_id_type=MESH)
            cp.start()
            cps.append(cp)
        for a in range(na):
            oth = _cols(outs[a], 1 - c)
            pltpu.make_async_remote_copy(src_ref=oth, dst_ref=oth, send_sem=send.at[a], recv_sem=recv.at[a],
                                         device_id=sib, device_id_type=MESH).wait_recv()
        for cp in cps:
            cp.wait_send()

    return _aliased_comm_call(body, name, bufs, na)


def _allreduce_small(v, name):
    rows, cols = v.shape

    def body(v_ref, o_ref, buf, send, recv):
        x, y, c, _ = _place()
        me = 4 * x + 2 * y + c
        buf[me] = v_ref[...]
        cps = []
        for d in range(1, 8):
            to = (x ^ (d >> 2), y ^ ((d >> 1) & 1), c ^ (d & 1))
            cp = pltpu.make_async_remote_copy(src_ref=v_ref, dst_ref=buf.at[me], send_sem=send.at[d - 1],
                                              recv_sem=recv.at[d - 1], device_id=to, device_id_type=MESH)
            cp.start()
            cps.append(cp)
        for d in range(1, 8):
            frm = 4 * (x ^ (d >> 2)) + 2 * (y ^ ((d >> 1) & 1)) + (c ^ (d & 1))
            got = buf.at[frm]
            pltpu.make_async_remote_copy(src_ref=got, dst_ref=got, send_sem=send.at[d - 1], recv_sem=recv.at[d - 1],
                                         device_id=(x, y, c), device_id_type=MESH).wait_recv()
        for cp in cps:
            cp.wait_send()
        acc = buf[0]
        for d in range(1, 8):
            acc = acc + buf[d]
        o_ref[...] = acc

    vm = pl.BlockSpec(memory_space=pltpu.VMEM)
    return pl.pallas_call(
        body, name=name, in_specs=[vm], out_specs=vm, out_shape=_sds((rows, cols), F32),
        scratch_shapes=[pltpu.VMEM((8, rows, cols), F32), pltpu.SemaphoreType.DMA((7,)), pltpu.SemaphoreType.DMA((7,))],
        compiler_params=pltpu.CompilerParams(has_side_effects=True, vmem_limit_bytes=V7X_VMEM_LIMIT))(v)


def _pack(arrs, pw):
    rows = []
    for a in arrs:
        f = a.astype(F32).reshape(-1)
        pad = (-f.shape[0]) % pw
        rows.append(jnp.pad(f, (0, pad)).reshape(-1, pw))
    p = jnp.concatenate(rows, axis=0)
    return jnp.pad(p, ((0, (-p.shape[0]) % 8), (0, 0)))


def _unpack(p, shapes, pw):
    out, r0 = [], 0
    for s in shapes:
        size = math.prod(s) if s else 1
        nr = -(-size // pw)
        out.append(p[r0:r0 + nr].reshape(-1)[:size].reshape(s))
        r0 += nr
    return out


def _step(x, mem, positions, wts, loss_target, mom, vel):
    w = dict(zip(WEIGHTS, wts))
    s_len, d = x.shape[1], x.shape[2]
    ml = mem.shape[1]
    depth = w['norm_mix'].shape[0]
    n_a = w['ssd_w_in'].shape[0]
    di = 4 * w['ssd_w_out'].shape[1]
    nh = w['ssd_dt_bias'].shape[1]
    p = di // nh
    gn = SSD_GROUPS
    jh = nh // gn
    cd = 4 * w['ssd_conv_w'].shape[2]
    w3 = 4 * w['dil_w_q'].shape[2]
    dw = w3 // 3
    hd = dw // DIL_HEADS
    mw = w['mem_w_q'].shape[2]
    ff = 4 * w['ffn_w_out'].shape[1]
    cx, cy, cc = lax.axis_index("x"), lax.axis_index("y"), lax.axis_index("c")
    chip = 2 * cx + cy


    big = list(COL_SHARDED + ROW_SHARDED)
    stacked = ('mem_w_q', 'mem_w_kv', 'mem_w_o')

    def mixer_items(i):
        if i < n_a:
            return [('ssd_w_in', i), ('ssd_w_out', i)]
        return ([('w_kv_shared', 0)] if i == n_a else []) + [('dil_w_q', i - n_a), ('dil_w_o', i - n_a)]

    def ffn_items(i):
        return [('ffn_w_in', i), ('ffn_w_out', i)] + ([(n_, None) for n_ in stacked] if i == 0 else [])

    def shard_of(name, l):
        t = w[name] if w[name].ndim == 3 else w[name][None]
        if l is not None:
            t = t[l:l + 1]
        return jnp.swapaxes(t, 1, 2).astype(BF16) if name in COL_SHARDED else t

    full = {n_: [None] * (w[n_].shape[0] if w[n_].ndim == 3 else 1) for n_ in big}

    def placed_bufs(its):
        return [_place_shard(shard_of(n_, l), f"place_{n_}_{l}") for n_, l in its]

    def record(its, bufs):
        for (n_, l), t in zip(its, bufs):
            fl = t.reshape(t.shape[0], 4 * t.shape[2], t.shape[3])
            if l is None:
                full[n_] = [fl[k] for k in range(fl.shape[0])]
            else:
                full[n_][l] = fl[0]

    record(mixer_items(0), _gather_weights(placed_bufs(mixer_items(0)), "gather_weights_first"))

    small_sharded = ('ssd_conv_w', 'ssd_conv_b', 'ssd_norm')

    def placed(name):
        t = w[name]
        wd = t.shape[-1]
        z = jnp.zeros(t.shape[:-1] + (4 * wd,), F32)
        z = lax.dynamic_update_slice_in_dim(z, t, chip * wd, axis=t.ndim - 1)
        return z * (cc == 0).astype(F32)

    sm_shapes = [w[n_].shape[:-1] + (4 * w[n_].shape[-1],) for n_ in small_sharded]
    conv_w, conv_b, ssd_nw = _unpack(_allreduce_small(_pack([placed(n_) for n_ in small_sharded], d), "gather_small"),
                                     sm_shapes, d)

    h = x[0]
    tgt = loss_target[0]
    mem_n = _rmsnorm(mem[0], w['mem_src_norm'], "mem_src_norm_fwd")
    cos2, sin2 = _rope_tables(positions[0].reshape(s_len, 1), hd, "rope_tables")
    a_neg = -jnp.exp(w['ssd_a_log'])
    saved = []
    kv_saved = None
    kgs = vgs = None
    for i in range(depth):
        sv = {'h_mix': h}
        u = _rmsnorm(h, w['norm_mix'][i], f"norm_mix_fwd_{i}")
        sv['u'] = u
        if i < n_a:
            wt = full['ssd_w_in'][i]
            wt_dt = wt[di + cd:]
            z = _mm(u, wt, 'nt', f"ssd_in_z_{i}", m=s_len, n=di, k=d)
            xbc_raw = _mm(u, wt, 'nt', f"ssd_in_xbc_{i}", m=s_len, n=cd, k=d, b_noff=di)
            dt_raw = _mm(u, wt_dt, 'nt', f"ssd_in_dt_{i}", m=s_len, n=nh, k=d)
            xbc = _conv_fwd(xbc_raw, conv_w[i], conv_b[i], f"ssd_conv_fwd_{i}")
            dt = _softplus_fwd(dt_raw, w['ssd_dt_bias'][i], f"ssd_dt_fwd_{i}")
            dtc = dt.reshape(s_len, gn, jh).transpose(1, 0, 2)
            dtr = dt.reshape(s_len, gn, jh).transpose(1, 2, 0)
            a_row = a_neg[i].reshape(gn, 1, jh)
            a_col = a_neg[i].reshape(gn, jh, 1)
            dskip = jnp.repeat(w['ssd_d'][i], p).reshape(gn, 1, jh * p)
            (y, hins), got = _ssd_fwd(xbc, dtc, dtr, a_row, a_col, dskip, f"ssd_scan_fwd_{i}", di=di, p=p,
                                      carry=_gather_carry(placed_bufs(ffn_items(i))))
            record(ffn_items(i), got)
            yn = _gated_norm_fwd(y, z, ssd_nw[i], f"ssd_norm_fwd_{i}")
            h = _mm(yn, full['ssd_w_out'][i], 'nn', f"ssd_out_{i}", m=s_len, n=d, k=di, res=h)
            sv.update(wt=wt, wt_dt=wt_dt, z=z, xbc_raw=xbc_raw, xbc=xbc, dt_raw=dt_raw, dtc=dtc, dtr=dtr, a_row=a_row,
                      a_col=a_col, dskip=dskip, y=y, hins=hins, yn=yn)
        else:
            j = i - n_a
            if j == 0:
                kvn = _rmsnorm(h, w['kv_norm'], "kv_norm_fwd")
                wkv = full['w_kv_shared'][0]
                k_raw = _mm(kvn, wkv, 'nt', "kv_proj_k", m=s_len, n=w3, k=d)
                v_raw = _mm(kvn, wkv, 'nt', "kv_proj_v", m=s_len, n=w3, k=d, b_noff=w3)
                kgs = _to_residue(k_raw, cos2, sin2, "rope_k", dw=dw, rope=True)
                vgs = _to_residue(v_raw, cos2, sin2, "residue_v", dw=dw, rope=False)
                kv_saved = {'h': h, 'kvn': kvn, 'wkv': wkv}
            q_raw = _mm(u, full['dil_w_q'][j], 'nt', f"dil_q_{i}", m=s_len, n=w3, k=d)
            qgs = _to_residue(q_raw, cos2, sin2, f"rope_q_{i}", dw=dw, rope=True)
            os_, ls_ = [], []
            for g in range(len(DIL_DILATIONS)):
                if g == 0:
                    (o3, l3), got = _dil_attn_fwd(qgs[g], kgs[g], vgs[g], f"dil_attn_fwd_{i}_{g}", dh=hd,
                                                  carry=_gather_carry(placed_bufs(ffn_items(i))))
                    record(ffn_items(i), got)
                else:
                    o3, l3 = _dil_attn_fwd(qgs[g], kgs[g], vgs[g], f"dil_attn_fwd_{i}_{g}", dh=hd)
                os_.append(o3)
                ls_.append(l3)
            om = _mix_fwd(os_, ls_, f"dil_mix_fwd_{i}", dh=hd)
            h = _mm(om, full['dil_w_o'][j], 'nn', f"dil_out_{i}", m=s_len, n=d, k=dw, res=h)
            sv.update(qgs=qgs, os=os_, ls=ls_, om=om)
        sv['h_mem'] = h
        u2 = _rmsnorm(h, w['norm_mem'][i], f"norm_mem_fwd_{i}")
        qm = _mm(u2, full['mem_w_q'][i], 'nn', f"mem_q_{i}", m=s_len, n=mw, k=d, out_dtype=BF16)
        kvm = _mm(mem_n, full['mem_w_kv'][i], 'nn', f"mem_kv_{i}", m=ml, n=2 * mw, k=d)
        omem = _mem_attn_fwd(qm, kvm, f"mem_attn_fwd_{i}")
        h = _mm(omem, full['mem_w_o'][i], 'nt', f"mem_out_{i}", m=s_len, n=d, k=mw, res=h)
        sv.update(u2=u2, qm=qm, kvm=kvm, omem=omem, h_ffn=h)
        u3 = _rmsnorm(h, w['norm_ffn'][i], f"norm_ffn_fwd_{i}")
        if i + 1 < depth:
            gu, got = _mm(u3, full['ffn_w_in'][i], 'nt', f"ffn_in_{i}", m=s_len, n=2 * ff, k=d, out_dtype=BF16,
                          carry=_gather_carry(placed_bufs(mixer_items(i + 1))))
            record(mixer_items(i + 1), got)
        else:
            gu = _mm(u3, full['ffn_w_in'][i], 'nt', f"ffn_in_{i}", m=s_len, n=2 * ff, k=d, out_dtype=BF16)
        act = _swiglu_fwd(gu, f"ffn_act_fwd_{i}")
        h = _mm(act, full['ffn_w_out'][i], 'nn', f"ffn_out_{i}", m=s_len, n=d, k=ff, res=h)
        sv.update(u3=u3, gu=gu, act=act)
        saved.append(sv)

    loss_part, dh, dhb, dg_final = _loss_and_grad(h, w['norm_final'], tgt, "loss_and_final_norm")

    gbig = {n_: [None] * len(full[n_]) for n_ in big}
    gsm = {n_: [None] * depth for n_ in ('norm_mix', 'norm_mem', 'norm_ffn')}
    for n_ in ('ssd_conv_w', 'ssd_conv_b', 'ssd_norm', 'ssd_dt_bias', 'ssd_a_log', 'ssd_d'):
        gsm[n_] = [None] * n_a
    dmem_n = None
    dk_parts = [[] for _ in DIL_DILATIONS]
    dv_parts = [[] for _ in DIL_DILATIONS]
    gshard = {}

    def rs_prepare(its, tag):
        gl = []
        for n_, l in its:
            t = jnp.stack(gbig[n_]) if l is None else gbig[n_][l][None]
            gl.append(t.reshape(t.shape[0], 4, t.shape[1] // 4, t.shape[2]))
        theirs = _swap_halves(gl, f"grad_swap_halves_{tag}")
        pairs = [_pair_sum(t, o, f"grad_pair_sum_{n_}_{l}") for (n_, l), t, o in zip(its, gl, theirs)]
        return its, [pr[0] for pr in pairs], [pr[1] for pr in pairs], [t.shape[3] for t in gl], tag

    def rs_finish(pend, lands):
        its, _, _, widths, tag = pend
        fins = [_chip_sum(ld, nw_, f"grad_chip_sum_{n_}_{l}") for (n_, l), ld, nw_ in zip(its, lands, widths)]
        for it, t in zip(its, _join_halves(fins, f"grad_join_halves_{tag}")):
            gshard[it] = t

    pend_mixer = None
    for i in reversed(range(depth)):
        sv = saved[i]
        dact = _mm(dhb, full['ffn_w_out'][i], 'nt', f"ffn_out_bwd_x_{i}", m=s_len, n=ff, k=d, out_dtype=BF16)
        gbig['ffn_w_out'][i] = _mm(sv['act'], dhb, 'tn', f"ffn_out_bwd_w_{i}", m=ff, n=d, k=s_len)
        dgu = _swiglu_bwd(sv['gu'], dact, f"ffn_act_bwd_{i}")
        if pend_mixer is None:
            gbig['ffn_w_in'][i] = _mm(dgu, sv['u3'], 'tn', f"ffn_in_bwd_w_{i}", m=2 * ff, n=d, k=s_len)
        else:
            gbig['ffn_w_in'][i], got = _mm(dgu, sv['u3'], 'tn', f"ffn_in_bwd_w_{i}", m=2 * ff, n=d, k=s_len,
                                           carry=_scatter_carry(pend_mixer[1], pend_mixer[2]))
            rs_finish(pend_mixer, got)
        du = _mm(dgu, full['ffn_w_in'][i], 'nn', f"ffn_in_bwd_x_{i}", m=s_len, n=d, k=2 * ff)
        dh, dhb, gsm['norm_ffn'][i] = _rmsnorm_bwd(sv['h_ffn'], w['norm_ffn'][i], du, dh, f"norm_ffn_bwd_{i}")
        do = _mm(dhb, full['mem_w_o'][i], 'nn', f"mem_out_bwd_x_{i}", m=s_len, n=mw, k=d, out_dtype=BF16)
        gbig['mem_w_o'][i] = _mm(dhb, sv['omem'], 'tn', f"mem_out_bwd_w_{i}", m=d, n=mw, k=s_len)
        dqm, dkvm = _mem_attn_bwd(sv['qm'], sv['kvm'], do, f"mem_attn_bwd_{i}")
        gbig['mem_w_q'][i] = _mm(sv['u2'], dqm, 'tn', f"mem_q_bwd_w_{i}", m=d, n=mw, k=s_len)
        du = _mm(dqm, full['mem_w_q'][i], 'nt', f"mem_q_bwd_x_{i}", m=s_len, n=d, k=mw)
        gbig['mem_w_kv'][i] = _mm(mem_n, dkvm, 'tn', f"mem_kv_bwd_w_{i}", m=d, n=2 * mw, k=ml)
        dmem_n = _mm(dkvm, full['mem_w_kv'][i], 'nt', f"mem_kv_bwd_x_{i}", m=ml, n=d, k=2 * mw, res=dmem_n)
        dh, dhb, gsm['norm_mem'][i] = _rmsnorm_bwd(sv['h_mem'], w['norm_mem'][i], du, dh, f"norm_mem_bwd_{i}")
        pend_ffn = rs_prepare(ffn_items(i), f"ffn_{i}")
        ffn_scatter = _scatter_carry(pend_ffn[1], pend_ffn[2])
        if i >= n_a:
            j = i - n_a
            dom = _mm(dhb, full['dil_w_o'][j], 'nt', f"dil_out_bwd_x_{i}", m=s_len, n=dw, k=d)
            gbig['dil_w_o'][j] = _mm(sv['om'], dhb, 'tn', f"dil_out_bwd_w_{i}", m=dw, n=d, k=s_len)
            mb = _mix_bwd(sv['os'], sv['ls'], dom, f"dil_mix_bwd_{i}", dh=hd)
            dqs = []
            for g in range(len(DIL_DILATIONS)):
                if g == 0:
                    (dq3, dk3, dv3), got = _dil_attn_bwd(sv['qgs'][g], kgs[g], vgs[g], mb[g], sv['ls'][g], mb[3 + g],
                                                         f"dil_attn_bwd_{i}_{g}", dh=hd, carry=ffn_scatter)
                    rs_finish(pend_ffn, got)
                else:
                    dq3, dk3, dv3 = _dil_attn_bwd(sv['qgs'][g], kgs[g], vgs[g], mb[g], sv['ls'][g], mb[3 + g],
                                                  f"dil_attn_bwd_{i}_{g}", dh=hd)
                dqs.append([dq3])
                dk_parts[g].append(dk3)
                dv_parts[g].append(dv3)
            dq_raw = _from_residue(dqs, cos2, sin2, f"rope_q_bwd_{i}", dw=dw, rope_cols=3)
            gbig['dil_w_q'][j] = _mm(dq_raw, sv['u'], 'tn', f"dil_q_bwd_w_{i}", m=w3, n=d, k=s_len)
            du = _mm(dq_raw, full['dil_w_q'][j], 'nn', f"dil_q_bwd_x_{i}", m=s_len, n=d, k=w3)
            dh, dhb, gsm['norm_mix'][i] = _rmsnorm_bwd(sv['h_mix'], w['norm_mix'][i], du, dh, f"norm_mix_bwd_{i}")
            if j == 0:
                dkv = _from_residue(dk_parts + dv_parts, cos2, sin2, "rope_kv_bwd", dw=dw, rope_cols=3)
                gbig['w_kv_shared'][0] = _mm(dkv, kv_saved['kvn'], 'tn', "kv_proj_bwd_w", m=2 * w3, n=d, k=s_len)
                du = _mm(dkv, kv_saved['wkv'], 'nn', "kv_proj_bwd_x", m=s_len, n=d, k=2 * w3)
                dh, dhb, dg_kv = _rmsnorm_bwd(kv_saved['h'], w['kv_norm'], du, dh, "kv_norm_bwd")
        else:
            dyn = _mm(dhb, full['ssd_w_out'][i], 'nt', f"ssd_out_bwd_x_{i}", m=s_len, n=di, k=d)
            gbig['ssd_w_out'][i] = _mm(sv['yn'], dhb, 'tn', f"ssd_out_bwd_w_{i}", m=di, n=d, k=s_len)
            dy, dz, gsm['ssd_norm'][i] = _gated_norm_bwd(sv['y'], sv['z'], ssd_nw[i], dyn, f"ssd_norm_bwd_{i}")
            (dx, db_, dc_, ddtc, ddtr, dac, dar, ddl), got = _ssd_bwd(
                sv['xbc'], sv['dtc'], sv['dtr'], sv['a_row'], sv['a_col'], sv['dskip'], sv['hins'], dy,
                f"ssd_scan_bwd_{i}", di=di, p=p, carry=ffn_scatter)
            rs_finish(pend_ffn, got)
            dxbc = jnp.concatenate([dx, db_, dc_], axis=1)
            ddt = ddtc.transpose(1, 0, 2).reshape(s_len, nh) + ddtr.transpose(2, 0, 1).reshape(s_len, nh)
            gsm['ssd_a_log'][i] = (dac.reshape(nh) + dar.reshape(nh)) * a_neg[i]
            gsm['ssd_d'][i] = ddl.reshape(nh, p).sum(axis=1)
            ddt_raw, dbias = _softplus_bwd(sv['dt_raw'], w['ssd_dt_bias'][i], ddt, f"ssd_dt_bwd_{i}")
            gsm['ssd_dt_bias'][i] = dbias.reshape(nh)
            dpre, dcw, dcb = _conv_bwd_pre(sv['xbc_raw'], conv_w[i], conv_b[i], dxbc, f"ssd_conv_bwd_pre_{i}")
            gsm['ssd_conv_w'][i], gsm['ssd_conv_b'][i] = dcw, dcb.reshape(cd)
            dxbc_raw = _conv_bwd_in(dpre, conv_w[i], f"ssd_conv_bwd_in_{i}")
            gbig['ssd_w_in'][i] = jnp.concatenate([
                _mm(dz, sv['u'], 'tn', f"ssd_in_bwd_w_z_{i}", m=di, n=d, k=s_len),
                _mm(dxbc_raw, sv['u'], 'tn', f"ssd_in_bwd_w_xbc_{i}", m=cd, n=d, k=s_len),
                _mm(ddt_raw, sv['u'], 'tn', f"ssd_in_bwd_w_dt_{i}", m=nh, n=d, k=s_len)], axis=0)
            du = _mm(dz, sv['wt'], 'nn', f"ssd_in_bwd_x_z_{i}", m=s_len, n=d, k=di)
            du = _mm(dxbc_raw, sv['wt'], 'nn', f"ssd_in_bwd_x_xbc_{i}", m=s_len, n=d, k=cd, b_koff=di, res=du)
            du = _mm(ddt_raw, sv['wt_dt'], 'nn', f"ssd_in_bwd_x_dt_{i}", m=s_len, n=d, k=nh, res=du)
            dh, dhb, gsm['norm_mix'][i] = _rmsnorm_bwd(sv['h_mix'], w['norm_mix'][i], du, dh, f"norm_mix_bwd_{i}")
        pend_mixer = rs_prepare(mixer_items(i), f"mixer_{i}")
    rs_finish(pend_mixer, _scatter_partials(pend_mixer[1], pend_mixer[2], "grad_scatter_last"))
    grad_x = dh[None]
    _, dg_src = _rmsnorm_bwd_noacc(mem[0], w['mem_src_norm'], dmem_n, "mem_src_norm_bwd")

    grads_big = {}
    for n_ in big:
        t = gshard[(n_, None)] if n_ in stacked else jnp.concatenate([gshard[(n_, l)] for l in range(len(full[n_]))])
        if n_ in COL_SHARDED:
            t = jnp.swapaxes(t, 1, 2)
        grads_big[n_] = t.reshape(w[n_].shape)

    sm_names = ['norm_mix', 'norm_mem', 'norm_ffn', 'ssd_conv_w', 'ssd_conv_b', 'ssd_norm', 'ssd_dt_bias', 'ssd_a_log',
                'ssd_d']
    sm_arrs = [jnp.stack([t.reshape(t.shape[-1]) if n_.startswith('norm') else t for t in gsm[n_]]) for n_ in sm_names]
    sm_names += ['norm_final', 'kv_norm', 'mem_src_norm', 'loss']
    sm_arrs += [dg_final.reshape(d), dg_kv.reshape(d), dg_src.reshape(d), loss_part[0, :1]]
    summed = _unpack(_allreduce_small(_pack(sm_arrs, d), "reduce_small"), [t.shape for t in sm_arrs], d)
    gs = dict(zip(sm_names, summed))
    loss = gs.pop('loss').reshape(())
    grads = dict(grads_big)
    for n_, t in gs.items():
        if n_ in small_sharded:
            wd = w[n_].shape[-1]
            t = lax.dynamic_slice_in_dim(t, chip * wd, wd, axis=t.ndim - 1)
        grads[n_] = t.reshape(w[n_].shape)

    deltas, new_m, new_v = [], [], []
    for n_, m_, v_ in zip(WEIGHTS, mom, vel):
        dlt, nm, nv = _adamw(w[n_], grads[n_], m_, v_, f"adamw_{n_}")
        deltas.append(dlt)
        new_m.append(nm)
        new_v.append(nv)
    return (loss, grad_x, *[grads[n_] for n_ in WEIGHTS], *deltas, *new_m, *new_v)


def kernel(x, mem, positions, norm_mix, norm_mem, norm_ffn, norm_final, ssd_w_in, ssd_conv_w, ssd_conv_b, ssd_dt_bias, ssd_a_log, ssd_d, ssd_norm, ssd_w_out, kv_norm, w_kv_shared, dil_w_q, dil_w_o, mem_src_norm, mem_w_q, mem_w_kv, mem_w_o, ffn_w_in, ffn_w_out, loss_target, m_norm_mix, m_norm_mem, m_norm_ffn, m_norm_final, m_ssd_w_in, m_ssd_conv_w, m_ssd_conv_b, m_ssd_dt_bias, m_ssd_a_log, m_ssd_d, m_ssd_norm, m_ssd_w_out, m_kv_norm, m_w_kv_shared, m_dil_w_q, m_dil_w_o, m_mem_src_norm, m_mem_w_q, m_mem_w_kv, m_mem_w_o, m_ffn_w_in, m_ffn_w_out, v_norm_mix, v_norm_mem, v_norm_ffn, v_norm_final, v_ssd_w_in, v_ssd_conv_w, v_ssd_conv_b, v_ssd_dt_bias, v_ssd_a_log, v_ssd_d, v_ssd_norm, v_ssd_w_out, v_kv_norm, v_w_kv_shared, v_dil_w_q, v_dil_w_o, v_mem_src_norm, v_mem_w_q, v_mem_w_kv, v_mem_w_o, v_ffn_w_in, v_ffn_w_out):
    wts = (norm_mix, norm_mem, norm_ffn, norm_final, ssd_w_in, ssd_conv_w, ssd_conv_b, ssd_dt_bias, ssd_a_log, ssd_d, ssd_norm, ssd_w_out, kv_norm, w_kv_shared, dil_w_q, dil_w_o, mem_src_norm, mem_w_q, mem_w_kv, mem_w_o, ffn_w_in, ffn_w_out)
    mom = (m_norm_mix, m_norm_mem, m_norm_ffn, m_norm_final, m_ssd_w_in, m_ssd_conv_w, m_ssd_conv_b, m_ssd_dt_bias, m_ssd_a_log, m_ssd_d, m_ssd_norm, m_ssd_w_out, m_kv_norm, m_w_kv_shared, m_dil_w_q, m_dil_w_o, m_mem_src_norm, m_mem_w_q, m_mem_w_kv, m_mem_w_o, m_ffn_w_in, m_ffn_w_out)
    vel = (v_norm_mix, v_norm_mem, v_norm_ffn, v_norm_final, v_ssd_w_in, v_ssd_conv_w, v_ssd_conv_b, v_ssd_dt_bias, v_ssd_a_log, v_ssd_d, v_ssd_norm, v_ssd_w_out, v_kv_norm, v_w_kv_shared, v_dil_w_q, v_dil_w_o, v_mem_src_norm, v_mem_w_q, v_mem_w_kv, v_mem_w_o, v_ffn_w_in, v_ffn_w_out)
    return _step(x, mem, positions, wts, loss_target, mom, vel)
```

```python
import functools
import math

import jax
import jax.numpy as jnp
from jax import lax
from jax.experimental import pallas as pl
from jax.experimental.pallas import tpu as pltpu

F32 = jnp.float32
BF16 = jnp.bfloat16
MESH = pl.DeviceIdType.MESH
HIGHEST = lax.Precision.HIGHEST

NORM_EPS = 1e-6
SSD_GROUPS = 8
SSD_STATE = 128
SSD_CHUNK = 128
SSD_CONV = 4
DIL_DILATIONS = (1, 4, 16)
DIL_HEADS = 16
DIL_BLOCK = 128
ATTN_ROWS = 32
ROPE_THETA = 10000.0
MEM_HEADS = 4
ADAM_LR, ADAM_B1, ADAM_B2, ADAM_EPS, ADAM_WD, ADAM_STEP = 0.001, 0.9, 0.999, 1e-08, 0.01, 10

V7X_VMEM_LIMIT = 48 * 1024 * 1024
LANES = 128
SUBLANES_BF16 = 16

WEIGHTS = ['norm_mix', 'norm_mem', 'norm_ffn', 'norm_final', 'ssd_w_in', 'ssd_conv_w', 'ssd_conv_b',
           'ssd_dt_bias', 'ssd_a_log', 'ssd_d', 'ssd_norm', 'ssd_w_out', 'kv_norm', 'w_kv_shared', 'dil_w_q',
           'dil_w_o', 'mem_src_norm', 'mem_w_q', 'mem_w_kv', 'mem_w_o', 'ffn_w_in', 'ffn_w_out']
COL_SHARDED = ('ssd_w_in', 'w_kv_shared', 'dil_w_q', 'mem_w_o', 'ffn_w_in')
ROW_SHARDED = ('ssd_w_out', 'dil_w_o', 'mem_w_q', 'mem_w_kv', 'ffn_w_out')


def _dot(a, b, ca, cb, prec=None):
    return lax.dot_general(a, b, (((ca,), (cb,)), ((), ())), preferred_element_type=F32, precision=prec)


def _nn(a, b, prec=None):
    return _dot(a, b, 1, 0, prec)


def _nt(a, b, prec=None):
    return _dot(a, b, 1, 1, prec)


def _tn(a, b, prec=None):
    return _dot(a, b, 0, 0, prec)


def _tile(n, pref, unit=LANES):
    if n <= pref:
        return n
    t = (pref // unit) * unit
    while t >= unit:
        if n % t == 0:
            return t
        t -= unit
    return n


class _Carry:
    def __init__(self, bufs, outs, n_sems, start, finish):
        self.bufs, self.outs, self.n_sems, self.start, self.finish = list(bufs), list(outs), n_sems, start, finish


def _call(body, name, grid, in_specs, out_specs, out_shape, scratch=(), sem=None, carry=None):
    if carry is None:
        return pl.pallas_call(
            body, name=name, grid=grid, in_specs=in_specs, out_specs=out_specs, out_shape=out_shape,
            scratch_shapes=list(scratch),
            compiler_params=pltpu.CompilerParams(dimension_semantics=sem, vmem_limit_bytes=V7X_VMEM_LIMIT))
    single = not isinstance(out_specs, (list, tuple))
    o_specs = [out_specs] if single else list(out_specs)
    o_shape = [out_shape] if single else list(out_shape)
    n_in, n_out, n_cb, n_co, n_scr = len(in_specs), len(o_specs), len(carry.bufs), len(carry.outs), len(scratch)
    anyspec = pl.BlockSpec(memory_space=pl.ANY)

    def riding(*refs):
        ins, cbufs = refs[:n_in], list(refs[n_in:n_in + n_cb])
        outs = refs[n_in + n_cb:n_in + n_cb + n_out]
        for k, b in enumerate(carry.outs):
            cbufs[b] = refs[n_in + n_cb + n_out + k]
        rest = refs[n_in + n_cb + n_out + n_co:]
        send, recv = rest[n_scr], rest[n_scr + 1]
        ids = [pl.program_id(a) for a in range(len(grid))]
        first = functools.reduce(lambda p, q: p & q, [i == 0 for i in ids])
        last = functools.reduce(lambda p, q: p & q, [i == g - 1 for i, g in zip(ids, grid)])

        @pl.when(first)
        def _():
            carry.start(cbufs, send, recv)

        body(*ins, *outs, *rest[:n_scr])

        @pl.when(last)
        def _():
            carry.finish(cbufs, send, recv)

    call = pl.pallas_call(
        riding, name=name, grid=grid, in_specs=list(in_specs) + [anyspec] * n_cb, out_specs=o_specs + [anyspec] * n_co,
        out_shape=o_shape + [_sds(carry.bufs[b].shape, carry.bufs[b].dtype) for b in carry.outs],
        scratch_shapes=list(scratch) + [pltpu.SemaphoreType.DMA((carry.n_sems,)), pltpu.SemaphoreType.DMA((carry.n_sems,))],
        input_output_aliases={n_in + b: n_out + k for k, b in enumerate(carry.outs)},
        compiler_params=pltpu.CompilerParams(dimension_semantics=("arbitrary",) * len(grid), has_side_effects=True,
                                             vmem_limit_bytes=V7X_VMEM_LIMIT))

    def run(*operands):
        res = call(*operands, *carry.bufs)
        own = res[0] if single else list(res[:n_out])
        return own, list(res[n_out:])

    return run


def _sds(shape, dtype):
    return jax.ShapeDtypeStruct(tuple(shape), dtype)


def _silu(x):
    return x * jax.nn.sigmoid(x)


def _dsilu(x):
    s = jax.nn.sigmoid(x)
    return s * (1.0 + x * (1.0 - s))


def _mm(a, b, mode, name, *, m, n, k, out_dtype=F32, res=None, b_noff=0, b_koff=0, carry=None):
    has_res = res is not None
    m_unit = LANES if mode == 'tn' else SUBLANES_BF16
    n_lim = math.gcd(n, b_noff) if b_noff else n
    k_lim = math.gcd(k, b_koff) if b_koff else k
    wide = out_dtype == F32
    if k <= 2048:
        tk = k
        tm = _tile(m, 2048, m_unit)
        tn = _tile(n_lim, 512 if (wide or has_res) else 1024)
    else:
        tk = _tile(k_lim, 1024)
        tm = _tile(m, 1024, m_unit)
        tn = _tile(n_lim, 2048 if (wide and not has_res) else 1024)
    nk = k // tk
    use_acc = nk > 1 and not wide
    jo, ko = b_noff // tn, b_koff // tk
    if mode == 'nn':
        a_spec = pl.BlockSpec((tm, tk), lambda i, j, kk: (i, kk))
        b_spec = pl.BlockSpec((tk, tn), lambda i, j, kk: (kk + ko, j + jo))
        ca, cb = 1, 0
    elif mode == 'nt':
        a_spec = pl.BlockSpec((tm, tk), lambda i, j, kk: (i, kk))
        b_spec = pl.BlockSpec((tn, tk), lambda i, j, kk: (j + jo, kk + ko))
        ca, cb = 1, 1
    else:
        a_spec = pl.BlockSpec((tk, tm), lambda i, j, kk: (kk, i))
        b_spec = pl.BlockSpec((tk, tn), lambda i, j, kk: (kk + ko, j + jo))
        ca, cb = 0, 0
    o_spec = pl.BlockSpec((tm, tn), lambda i, j, kk: (i, j))

    def body(*refs):
        a_ref, b_ref = refs[0], refs[1]
        r_ref = refs[2] if has_res else None
        o_ref = refs[3] if has_res else refs[2]
        acc = refs[-1] if use_acc else None

        def prod():
            return _dot(a_ref[...].astype(BF16), b_ref[...].astype(BF16), ca, cb)

        def with_res(v):
            return v + r_ref[...].astype(F32) if has_res else v

        if nk == 1:
            o_ref[...] = with_res(prod()).astype(o_ref.dtype)
            return
        kk = pl.program_id(2)
        if use_acc:
            @pl.when(kk == 0)
            def _():
                acc[...] = prod()

            @pl.when((kk > 0) & (kk < nk - 1))
            def _():
                acc[...] += prod()

            @pl.when(kk == nk - 1)
            def _():
                o_ref[...] = with_res(acc[...] + prod()).astype(o_ref.dtype)
        else:
            @pl.when(kk == 0)
            def _():
                o_ref[...] = with_res(prod())

            @pl.when(kk > 0)
            def _():
                o_ref[...] += prod()

    ins = [a, b] + ([res] if has_res else [])
    specs = [a_spec, b_spec] + ([o_spec] if has_res else [])
    return _call(body, name, (m // tm, n // tn, nk), specs, o_spec, _sds((m, n), out_dtype),
                 scratch=[pltpu.VMEM((tm, tn), F32)] if use_acc else [], sem=("parallel", "parallel", "arbitrary"),
                 carry=carry)(*ins)


def _rowwise(fn, name, rows, consts, outs, tb, n_rows):
    n_r, n_c = len(rows), len(consts)
    in_specs = [pl.BlockSpec((tb, w), functools.partial(lambda i, cb: (i, cb), cb=cb)) for _, w, cb in rows]
    in_specs += [pl.BlockSpec(c.shape, functools.partial(lambda i, nd: (0,) * nd, nd=c.ndim)) for c in consts]
    out_specs, out_shape = [], []
    for kind, w, dt in outs:
        if kind == 'row':
            out_specs.append(pl.BlockSpec((tb, w), lambda i: (i, 0)))
            out_shape.append(_sds((n_rows, w), dt))
        else:
            out_specs.append(pl.BlockSpec(w, lambda i: (0, 0)))
            out_shape.append(_sds(w, dt))

    def body(*refs):
        ins = [r[...] for r in refs[:n_r + n_c]]
        orefs = refs[n_r + n_c:]
        vals = fn(*ins)
        i = pl.program_id(0)
        for (kind, _, _), o_ref, v in zip(outs, orefs, vals):
            if kind == 'row':
                o_ref[...] = v.astype(o_ref.dtype)
            else:
                @pl.when(i == 0)
                def _(o_ref=o_ref):
                    o_ref[...] = jnp.zeros_like(o_ref)

                o_ref[...] += v.astype(o_ref.dtype)

    res = _call(body, name, (n_rows // tb,), in_specs, out_specs, out_shape, sem=("arbitrary",))(
        *[r[0] for r in rows], *consts)
    return res


def _rms_fwd_fn(h, g):
    r = lax.rsqrt(jnp.mean(h * h, axis=-1, keepdims=True) + NORM_EPS)
    return (h * r * g,)


def _rms_bwd_vals(h, g, dy):
    r = lax.rsqrt(jnp.mean(h * h, axis=-1, keepdims=True) + NORM_EPS)
    t = dy * g
    dh = r * t - h * (r * r * r) * jnp.mean(h * t, axis=-1, keepdims=True)
    dg = jnp.sum(dy * h * r, axis=0, keepdims=True)
    return dh, dg


def _rmsnorm(h, g, name):
    n, d = h.shape
    return _rowwise(_rms_fwd_fn, name, [(h, d, 0)], [g.reshape(1, d)], [('row', d, BF16)], _tile(n, 512, 8), n)[0]


def _rmsnorm_bwd(h, g, du, dres, name):
    n, d = h.shape

    def fn(hv, duv, drv, gv):
        dh, dg = _rms_bwd_vals(hv, gv, duv.astype(F32))
        return dh + drv, dh + drv, dg

    return _rowwise(fn, name, [(h, d, 0), (du, d, 0), (dres, d, 0)], [g.reshape(1, d)],
                    [('row', d, F32), ('row', d, BF16), ('acc', (1, d), F32)], _tile(n, 256, 8), n)


def _rmsnorm_bwd_noacc(h, g, du, name):
    n, d = h.shape

    def fn(hv, duv, gv):
        return _rms_bwd_vals(hv, gv, duv.astype(F32))

    return _rowwise(fn, name, [(h, d, 0), (du, d, 0)], [g.reshape(1, d)],
                    [('row', d, F32), ('acc', (1, d), F32)], _tile(n, 256, 8), n)


def _swiglu_fwd(gu, name):
    n, f2 = gu.shape
    f = f2 // 2

    def fn(v):
        v = v.astype(F32)
        return (_silu(v[:, :f]) * v[:, f:],)

    return _rowwise(fn, name, [(gu, f2, 0)], [], [('row', f, BF16)], _tile(n, 256, 16), n)[0]


def _swiglu_bwd(gu, dact, name):
    n, f2 = gu.shape
    f = f2 // 2

    def fn(v, da):
        v = v.astype(F32)
        g, up = v[:, :f], v[:, f:]
        da = da.astype(F32)
        return (jnp.concatenate([da * up * _dsilu(g), da * _silu(g)], axis=1),)

    return _rowwise(fn, name, [(gu, f2, 0), (dact, f, 0)], [], [('row', f2, BF16)], _tile(n, 256, 16), n)[0]


def _group_sum(v, ng):
    w = v.shape[1] // ng
    return [jnp.sum(v[:, i * w:(i + 1) * w], axis=1, keepdims=True) for i in range(ng)]


def _gated_norm_fwd(y, z, nw, name):
    n, di = y.shape
    gw = di // SSD_GROUPS

    def fn(yv, zv, nwv):
        a = yv * _silu(zv)
        ms = _group_sum(a * a, SSD_GROUPS)
        out = jnp.concatenate([a[:, i * gw:(i + 1) * gw] * lax.rsqrt(ms[i] / gw + NORM_EPS)
                               for i in range(SSD_GROUPS)], axis=1)
        return (out * nwv,)

    return _rowwise(fn, name, [(y, di, 0), (z, di, 0)], [nw.reshape(1, di)], [('row', di, BF16)], _tile(n, 256, 8), n)[0]


def _gated_norm_bwd(y, z, nw, dout, name):
    n, di = y.shape
    gw = di // SSD_GROUPS

    def fn(yv, zv, dov, nwv):
        sz = _silu(zv)
        a = yv * sz
        t = dov * nwv
        ms = _group_sum(a * a, SSD_GROUPS)
        at = _group_sum(a * t, SSD_GROUPS)
        das, ars = [], []
        for i in range(SSD_GROUPS):
            r = lax.rsqrt(ms[i] / gw + NORM_EPS)
            sl = slice(i * gw, (i + 1) * gw)
            das.append(r * t[:, sl] - a[:, sl] * (r * r * r) * (at[i] / gw))
            ars.append(a[:, sl] * r)
        da = jnp.concatenate(das, axis=1)
        ar = jnp.concatenate(ars, axis=1)
        return da * sz, da * yv * _dsilu(zv), jnp.sum(dov * ar, axis=0, keepdims=True)

    return _rowwise(fn, name, [(y, di, 0), (z, di, 0), (dout, di, 0)], [nw.reshape(1, di)],
                    [('row', di, F32), ('row', di, BF16), ('acc', (1, di), F32)], _tile(n, 128, 8), n)


def _softplus_fwd(raw, bias, name):
    n, h = raw.shape

    def fn(v, b):
        t = v + b
        return (jnp.maximum(t, 0.0) + jnp.log(1.0 + jnp.exp(-jnp.abs(t))),)

    return _rowwise(fn, name, [(raw, h, 0)], [bias.reshape(1, h)], [('row', h, F32)], _tile(n, 1024, 8), n)[0]


def _softplus_bwd(raw, bias, ddt, name):
    n, h = raw.shape

    def fn(v, d, b):
        g = d * jax.nn.sigmoid(v + b)
        return g, jnp.sum(g, axis=0, keepdims=True)

    return _rowwise(fn, name, [(raw, h, 0), (ddt, h, 0)], [bias.reshape(1, h)],
                    [('row', h, BF16), ('acc', (1, h), F32)], _tile(n, 1024, 8), n)


def _loss_and_grad(h, g, tgt, name):
    n, d = h.shape

    def fn(hv, tv, gv):
        y = _rms_fwd_fn(hv, gv)[0]
        err = y - tv
        part = 0.5 * jnp.sum(jnp.sum(err * err, axis=1, keepdims=True), axis=0, keepdims=True) / d
        dh, dg = _rms_bwd_vals(hv, gv, err / d)
        return jnp.broadcast_to(part, (8, LANES)), dh, dh, dg

    return _rowwise(fn, name, [(h, d, 0), (tgt, d, 0)], [g.reshape(1, d)],
                    [('acc', (8, LANES), F32), ('row', d, F32), ('row', d, BF16), ('acc', (1, d), F32)],
                    _tile(n, 256, 8), n)


def _rope_tables(pos_col, dh, name):
    n = pos_col.shape[0]
    half = dh // 2
    inv = ROPE_THETA ** (-jnp.arange(half, dtype=F32) / half)
    inv2 = jnp.concatenate([inv, inv]).reshape(1, dh)
    sign = jnp.concatenate([-jnp.ones((half,), F32), jnp.ones((half,), F32)]).reshape(1, dh)

    def fn(p, iv, sg):
        ang = p.astype(F32) * iv
        return jnp.cos(ang), jnp.sin(ang) * sg

    return _rowwise(fn, name, [(pos_col, 1, 0)], [inv2, sign], [('row', dh, F32), ('row', dh, F32)], _tile(n, 1024, 8), n)


def _conv_taps(ext, w, tb):
    shifted = [ext[8:] if k == SSD_CONV - 1 else pltpu.roll(ext, SSD_CONV - 1 - k, 0)[8:] for k in range(SSD_CONV)]
    pre = shifted[0] * w[0:1]
    for k in range(1, SSD_CONV):
        pre = pre + shifted[k] * w[k:k + 1]
    return pre, shifted


def _conv_specs(n, c, tb, tc):
    hb = tb // 8
    blk = pl.BlockSpec((tb, tc), lambda j, i: (i, j))
    halo = pl.BlockSpec((8, tc), lambda j, i: (jnp.maximum(i * hb - 1, 0), j))
    wsp = pl.BlockSpec((SSD_CONV, tc), lambda j, i: (0, j))
    bsp = pl.BlockSpec((1, tc), lambda j, i: (0, j))
    return blk, halo, wsp, bsp


def _conv_fwd(u, w, b, name):
    n, c = u.shape
    tb, tc = _tile(n, 512, 8), _tile(c, 1536)
    blk, halo, wsp, bsp = _conv_specs(n, c, tb, tc)

    def body(u_ref, h_ref, w_ref, b_ref, o_ref):
        halo_v = jnp.where(pl.program_id(1) > 0, h_ref[...], 0.0)
        pre, _ = _conv_taps(jnp.concatenate([halo_v, u_ref[...]], axis=0), w_ref[...], tb)
        o_ref[...] = _silu(pre + b_ref[...])

    return _call(body, name, (c // tc, n // tb), [blk, halo, wsp, bsp], blk, _sds((n, c), F32),
                 sem=("parallel", "arbitrary"))(u, u, w, b.reshape(1, c))


def _conv_bwd_pre(u, w, b, dout, name):
    n, c = u.shape
    tb, tc = _tile(n, 512, 8), _tile(c, 1536)
    blk, halo, wsp, bsp = _conv_specs(n, c, tb, tc)

    def body(u_ref, h_ref, w_ref, b_ref, d_ref, dp_ref, dw_ref, db_ref):
        i = pl.program_id(1)
        halo_v = jnp.where(i > 0, h_ref[...], 0.0)
        pre, shifted = _conv_taps(jnp.concatenate([halo_v, u_ref[...]], axis=0), w_ref[...], tb)
        dp = d_ref[...] * _dsilu(pre + b_ref[...])
        dp_ref[...] = dp

        @pl.when(i == 0)
        def _():
            dw_ref[...] = jnp.zeros_like(dw_ref)
            db_ref[...] = jnp.zeros_like(db_ref)

        dw_ref[...] += jnp.concatenate([jnp.sum(dp * s, axis=0, keepdims=True) for s in shifted], axis=0)
        db_ref[...] += jnp.sum(dp, axis=0, keepdims=True)

    return _call(body, name, (c // tc, n // tb), [blk, halo, wsp, bsp, blk], [blk, wsp, bsp],
                 [_sds((n, c), F32), _sds((SSD_CONV, c), F32), _sds((1, c), F32)],
                 sem=("parallel", "arbitrary"))(u, u, w, b.reshape(1, c), dout)


def _conv_bwd_in(dpre, w, name):
    n, c = dpre.shape
    tb, tc = _tile(n, 512, 8), _tile(c, 1536)
    hb = tb // 8
    nb = n // tb
    blk = pl.BlockSpec((tb, tc), lambda j, i: (i, j))
    nxt = pl.BlockSpec((8, tc), lambda j, i: (jnp.minimum((i + 1) * hb, n // 8 - 1), j))
    wsp = pl.BlockSpec((SSD_CONV, tc), lambda j, i: (0, j))

    def body(d_ref, n_ref, w_ref, o_ref):
        nxt_v = jnp.where(pl.program_id(1) < nb - 1, n_ref[...], 0.0)
        ext = jnp.concatenate([d_ref[...], nxt_v], axis=0)
        wv = w_ref[...]
        acc = ext[:tb] * wv[SSD_CONV - 1:SSD_CONV]
        for k in range(SSD_CONV - 1):
            s = SSD_CONV - 1 - k
            acc = acc + pltpu.roll(ext, tb + 8 - s, 0)[:tb] * wv[k:k + 1]
        o_ref[...] = acc.astype(o_ref.dtype)

    return _call(body, name, (c // tc, nb), [blk, nxt, wsp], blk, _sds((n, c), BF16),
                 sem=("parallel", "arbitrary"))(dpre, dpre, w)


def _col(v, j):
    lane = lax.broadcasted_iota(jnp.int32, v.shape, 1)
    return jnp.sum(jnp.where(lane == j, v, 0.0), axis=1, keepdims=True)


def _row(v, j):
    sub = lax.broadcasted_iota(jnp.int32, v.shape, 0)
    return jnp.sum(jnp.where(sub == j, v, 0.0), axis=0, keepdims=True)


def _ssd_head_terms(dtc, dtr, a_row, a_col):
    ll = dtc.shape[0]
    r = lax.broadcasted_iota(jnp.int32, (ll, ll), 0)
    c = lax.broadcasted_iota(jnp.int32, (ll, ll), 1)
    tril = (r >= c)
    trilf = tril.astype(F32)
    triuf = (r <= c).astype(F32)
    cumc = _nn(trilf, dtc * a_row, HIGHEST)
    cumr = _nn(dtr * a_col, triuf, HIGHEST)
    return cumc, cumr, tril, trilf, triuf


def _ssd_specs(n, di, gn, jh, p, nc, rev):
    ll = SSD_CHUNK
    jp = jh * p

    def ci(c):
        return (nc - 1 - c) if rev else c

    xs = pl.BlockSpec((ll, jp), lambda g, c: (ci(c), g))
    bs = pl.BlockSpec((ll, SSD_STATE), lambda g, c: (ci(c), di // SSD_STATE + g))
    cs = pl.BlockSpec((ll, SSD_STATE), lambda g, c: (ci(c), (di + gn * SSD_STATE) // SSD_STATE + g))
    dtc = pl.BlockSpec((None, ll, jh), lambda g, c: (g, ci(c), 0))
    dtr = pl.BlockSpec((None, jh, ll), lambda g, c: (g, 0, ci(c)))
    arow = pl.BlockSpec((None, 1, jh), lambda g, c: (g, 0, 0))
    acol = pl.BlockSpec((None, jh, 1), lambda g, c: (g, 0, 0))
    dsk = pl.BlockSpec((None, 1, jp), lambda g, c: (g, 0, 0))
    hin = pl.BlockSpec((None, None, SSD_STATE, jp), lambda g, c: (g, ci(c), 0, 0))
    ys = pl.BlockSpec((ll, jp), lambda g, c: (ci(c), g))
    return xs, bs, cs, dtc, dtr, arow, acol, dsk, hin, ys


def _ssd_fwd(xbc, dtc, dtr, a_row, a_col, dskip, name, *, di, p, carry=None):
    n = xbc.shape[0]
    gn = SSD_GROUPS
    jh = dtc.shape[2]
    jp = jh * p
    ll = SSD_CHUNK
    nc = n // ll
    xs, bs, cs, dtcs, dtrs, arow, acol, dsk, hin, ys = _ssd_specs(n, di, gn, jh, p, nc, False)
    pair = 2 * p

    def body(x_ref, b_ref, c_ref, dtc_ref, dtr_ref, ar_ref, ac_ref, ds_ref, y_ref, hin_ref, h_scr):
        @pl.when(pl.program_id(1) == 0)
        def _():
            h_scr[...] = jnp.zeros_like(h_scr)

        dtcv, dtrv = dtc_ref[...], dtr_ref[...]
        cumc, cumr, tril, _, _ = _ssd_head_terms(dtcv, dtrv, ar_ref[...], ac_ref[...])
        tot = jnp.sum(dtcv * ar_ref[...], axis=0, keepdims=True)
        bb, cb_ = b_ref[...].astype(BF16), c_ref[...].astype(BF16)
        cbm = _nt(cb_, bb)
        hin_ref[...] = h_scr[...]
        lane = lax.broadcasted_iota(jnp.int32, (ll, pair), 1)
        lane1 = lax.broadcasted_iota(jnp.int32, (1, pair), 1)
        for pr in range(jh // 2):
            sl = slice(pr * pair, (pr + 1) * pair)
            xp = x_ref[:, sl]
            xpb = xp.astype(BF16)
            hp = h_scr[:, sl]
            ydiag = jnp.zeros((ll, pair), F32)
            e_p = jnp.zeros((ll, pair), F32)
            w_p = jnp.zeros((ll, pair), F32)
            cd_p = jnp.zeros((1, pair), F32)
            for q in range(2):
                j = 2 * pr + q
                mj = (lane >= p) if q else (lane < p)
                cc, cr = _col(cumc, j), _row(cumr, j)
                decay = jnp.exp(jnp.where(tril, cc - cr, -1e30))
                mm = cbm * decay * _row(dtrv, j)
                ydiag = ydiag + _nn(mm.astype(BF16), jnp.where(mj, xpb, jnp.zeros_like(xpb)))
                cl = _col(tot, j)
                e_p = jnp.where(mj, jnp.exp(cc), e_p)
                w_p = jnp.where(mj, jnp.exp(cl - cc) * _col(dtcv, j), w_p)
                cd_p = jnp.where((lane1 >= p) if q else (lane1 < p), jnp.exp(cl), cd_p)
            yoff = _nn(cb_, hp.astype(BF16)) * e_p
            y_ref[:, sl] = ydiag + yoff + xp * ds_ref[:, sl]
            h_scr[:, sl] = hp * cd_p + _tn(bb, (xp * w_p).astype(BF16))

    return _call(body, name, (gn, nc), [xs, bs, cs, dtcs, dtrs, arow, acol, dsk], [ys, hin],
                 [_sds((n, di), F32), _sds((gn, nc, SSD_STATE, jp), F32)],
                 scratch=[pltpu.VMEM((SSD_STATE, jp), F32)], sem=("parallel", "arbitrary"), carry=carry)(
        xbc, xbc, xbc, dtc, dtr, a_row, a_col, dskip)


def _ssd_bwd(xbc, dtc, dtr, a_row, a_col, dskip, hins, dy, name, *, di, p, carry=None):
    n = xbc.shape[0]
    gn = SSD_GROUPS
    jh = dtc.shape[2]
    jp = jh * p
    ll = SSD_CHUNK
    nc = n // ll
    xs, bs, cs, dtcs, dtrs, arow, acol, dsk, hin, ys = _ssd_specs(n, di, gn, jh, p, nc, True)
    pair = 2 * p
    bc_out = pl.BlockSpec((ll, SSD_STATE), lambda g, c: (nc - 1 - c, g))

    def body(x_ref, b_ref, c_ref, dtc_ref, dtr_ref, ar_ref, ac_ref, ds_ref, hin_ref, dy_ref,
             dx_ref, db_ref, dc_ref, ddtc_ref, ddtr_ref, dac_ref, dar_ref, dd_ref, dh_scr):
        first = pl.program_id(1) == 0

        @pl.when(first)
        def _():
            dh_scr[...] = jnp.zeros_like(dh_scr)
            dac_ref[...] = jnp.zeros_like(dac_ref)
            dar_ref[...] = jnp.zeros_like(dar_ref)
            dd_ref[...] = jnp.zeros_like(dd_ref)

        dtcv, dtrv = dtc_ref[...], dtr_ref[...]
        a_r, a_c = ar_ref[...], ac_ref[...]
        cumc, cumr, tril, trilf, triuf = _ssd_head_terms(dtcv, dtrv, a_r, a_c)
        tot = jnp.sum(dtcv * a_r, axis=0, keepdims=True)
        bb, cb_ = b_ref[...].astype(BF16), c_ref[...].astype(BF16)
        cbm = _nt(cb_, bb)
        lane = lax.broadcasted_iota(jnp.int32, (ll, pair), 1)
        lane1 = lax.broadcasted_iota(jnp.int32, (1, pair), 1)
        lane_j = lax.broadcasted_iota(jnp.int32, (ll, jh), 1)
        sub_l = lax.broadcasted_iota(jnp.int32, (ll, jh), 0)
        sub_j = lax.broadcasted_iota(jnp.int32, (jh, ll), 0)
        dcb = jnp.zeros((ll, ll), F32)
        db_acc = jnp.zeros((ll, SSD_STATE), F32)
        dc_acc = jnp.zeros((ll, SSD_STATE), F32)
        dcum_c = jnp.zeros((ll, jh), F32)
        dcum_r = jnp.zeros((jh, ll), F32)
        ddt_c = jnp.zeros((ll, jh), F32)
        ddt_r = jnp.zeros((jh, ll), F32)
        for pr in range(jh // 2):
            sl = slice(pr * pair, (pr + 1) * pair)
            xp = x_ref[:, sl]
            xpb = xp.astype(BF16)
            dyp = dy_ref[:, sl]
            hp = hin_ref[:, sl]
            hpb = hp.astype(BF16)
            dhp = dh_scr[:, sl]
            dhpb = dhp.astype(BF16)
            ch = _nn(cb_, hpb)
            gp = _nn(bb, dhpb)
            e_p = jnp.zeros((ll, pair), F32)
            w_p = jnp.zeros((ll, pair), F32)
            cd_p = jnp.zeros((1, pair), F32)
            dxp = dyp * ds_ref[:, sl]
            heads = []
            for q in range(2):
                j = 2 * pr + q
                mj = (lane >= p) if q else (lane < p)
                cc, cr = _col(cumc, j), _row(cumr, j)
                cl = _col(tot, j)
                ej = jnp.exp(cc)
                wdec = jnp.exp(cl - cc)
                wj = wdec * _col(dtcv, j)
                e_p = jnp.where(mj, ej, e_p)
                w_p = jnp.where(mj, wj, w_p)
                cdj = jnp.exp(cl)
                cd_p = jnp.where((lane1 >= p) if q else (lane1 < p), cdj, cd_p)
                heads.append((j, mj, cc, cr, cl, wdec, wj, cdj))
            dye = dyp * e_p
            dyeb = dye.astype(BF16)
            dc_acc = dc_acc + _nt(dyeb, hpb)
            dh_new = dhp * cd_p + _tn(cb_, dyeb)
            dxp = dxp + gp * w_p
            db_acc = db_acc + _nt((xp * w_p).astype(BF16), dhpb)
            t_off = dye * ch
            t_w = gp * xp
            t_cd = jnp.sum(dhp * hp, axis=0, keepdims=True)
            for (j, mj, cc, cr, cl, wdec, wj, cdj) in heads:
                dyj = jnp.where(mj, dyp, 0.0).astype(BF16)
                decay = jnp.exp(jnp.where(tril, cc - cr, -1e30))
                dtrow = _row(dtrv, j)
                mm = cbm * decay * dtrow
                dm = _nt(dyj, xpb)
                dxp = dxp + _tn(mm.astype(BF16), dyj)
                dcb = dcb + dm * decay * dtrow
                ddt_rj = jnp.sum(dm * cbm * decay, axis=0, keepdims=True)
                dseg = dm * mm
                dcum_cj = jnp.sum(dseg, axis=1, keepdims=True) + jnp.sum(jnp.where(mj, t_off, 0.0), axis=1, keepdims=True)
                dcum_rj = -jnp.sum(dseg, axis=0, keepdims=True)
                dwj = jnp.sum(jnp.where(mj, t_w, 0.0), axis=1, keepdims=True)
                ddt_cj = dwj * wdec
                qj = dwj * wj
                dcum_cj = dcum_cj - qj
                m1 = (lane1 >= p) if (j % 2) else (lane1 < p)
                dcl = jnp.sum(jnp.where(m1, t_cd, 0.0), axis=1, keepdims=True) * cdj + jnp.sum(qj, axis=0, keepdims=True)
                dcum_c = dcum_c + jnp.where(lane_j == j, dcum_cj, 0.0) + jnp.where((lane_j == j) & (sub_l == ll - 1), dcl, 0.0)
                dcum_r = dcum_r + jnp.where(sub_j == j, dcum_rj, 0.0)
                ddt_c = ddt_c + jnp.where(lane_j == j, ddt_cj, 0.0)
                ddt_r = ddt_r + jnp.where(sub_j == j, ddt_rj, 0.0)
            dx_ref[:, sl] = dxp
            dd_ref[:, sl] += jnp.sum(dyp * xp, axis=0, keepdims=True)
            dh_scr[:, sl] = dh_new
        dcbb = dcb.astype(BF16)
        dc_ref[...] = dc_acc + _nn(dcbb, bb)
        db_ref[...] = db_acc + _tn(dcbb, cb_)
        dda_c = _nn(triuf, dcum_c, HIGHEST)
        dda_r = _nn(dcum_r, trilf, HIGHEST)
        ddtc_ref[...] = ddt_c + dda_c * a_r
        ddtr_ref[...] = ddt_r + dda_r * a_c
        dac_ref[...] += jnp.sum(dda_c * dtcv, axis=0, keepdims=True)
        dar_ref[...] += jnp.sum(dda_r * dtrv, axis=1, keepdims=True)

    gs = SSD_GROUPS * SSD_STATE
    return _call(body, name, (gn, nc), [xs, bs, cs, dtcs, dtrs, arow, acol, dsk, hin, ys],
                 [ys, bc_out, bc_out, dtcs, dtrs, arow, acol, dsk],
                 [_sds((n, di), F32), _sds((n, gs), F32), _sds((n, gs), F32), _sds(dtc.shape, F32), _sds(dtr.shape, F32),
                  _sds((gn, 1, jh), F32), _sds((gn, jh, 1), F32), _sds((gn, 1, jp), F32)],
                 scratch=[pltpu.VMEM((SSD_STATE, jp), F32)], sem=("parallel", "arbitrary"), carry=carry)(
        xbc, xbc, xbc, dtc, dtr, a_row, a_col, dskip, hins, dy)


def _band_masks():
    r = lax.broadcasted_iota(jnp.int32, (DIL_BLOCK, DIL_BLOCK), 0)
    c = lax.broadcasted_iota(jnp.int32, (DIL_BLOCK, DIL_BLOCK), 1)
    return c >= r, c <= r


def _dil_attn_fwd(q3, k3, v3, name, *, dh, carry=None):
    dil, n, dw = q3.shape
    nb = n // DIL_BLOCK
    scale = dh ** -0.5
    tb = DIL_BLOCK
    cur = pl.BlockSpec((None, tb, dw), lambda r, jb: (r, jb, 0))
    prev = pl.BlockSpec((None, tb, dw), lambda r, jb: (r, jnp.maximum(jb - 1, 0), 0))
    o_spec = cur
    l_spec = pl.BlockSpec((None, tb, LANES), lambda r, jb: (r, jb, 0))

    def body(q_ref, kc_ref, kp_ref, vc_ref, vp_ref, o_ref, l_ref, s_scr, p_scr, d_scr):
        mp, mc = _band_masks()
        mp = mp & (pl.program_id(1) > 0)
        for h in range(DIL_HEADS):
            sl = slice(h * dh, (h + 1) * dh)
            qh = q_ref[:, sl]
            s_scr[h, :, :tb] = jnp.where(mp, _nt(qh, kp_ref[:, sl]) * scale, -jnp.inf)
            s_scr[h, :, tb:] = jnp.where(mc, _nt(qh, kc_ref[:, sl]) * scale, -jnp.inf)
        lane = lax.broadcasted_iota(jnp.int32, (ATTN_ROWS, LANES), 1)
        for h in range(DIL_HEADS):
            for rc in range(tb // ATTN_ROWS):
                rows = slice(rc * ATTN_ROWS, (rc + 1) * ATTN_ROWS)
                s = s_scr[h, rows, :]
                mx = jnp.max(jnp.maximum(s[:, :tb], s[:, tb:]), axis=1, keepdims=True)
                pm = jnp.exp(s - mx)
                den = jnp.sum(pm[:, :tb] + pm[:, tb:], axis=1, keepdims=True)
                p_scr[h, rows, :] = pm.astype(BF16)
                d_scr[h, rows, :] = den
                lse = mx + jnp.log(den)
                l_ref[rows, :] = jnp.where(lane == h, lse, 0.0 if h == 0 else l_ref[rows, :])
        for h in range(DIL_HEADS):
            sl = slice(h * dh, (h + 1) * dh)
            o = _nn(p_scr[h, :, :tb], vp_ref[:, sl]) + _nn(p_scr[h, :, tb:], vc_ref[:, sl])
            o_ref[:, sl] = o / d_scr[h]

    return _call(body, name, (dil, nb), [cur, cur, prev, cur, prev], [o_spec, l_spec],
                 [_sds((dil, n, dw), F32), _sds((dil, n, LANES), F32)],
                 scratch=[pltpu.VMEM((DIL_HEADS, tb, 2 * tb), F32), pltpu.VMEM((DIL_HEADS, tb, 2 * tb), BF16),
                          pltpu.VMEM((DIL_HEADS, tb, 1), F32)],
                 sem=("parallel", "arbitrary"), carry=carry)(q3, k3, k3, v3, v3)


def _dil_attn_bwd(q3, k3, v3, do3, lse3, dl3, name, *, dh, carry=None):
    dil, n, dw = q3.shape
    nb = n // DIL_BLOCK
    scale = dh ** -0.5
    tb = DIL_BLOCK

    def nx(jb):
        return jnp.minimum(jb + 1, nb - 1)

    q_c = pl.BlockSpec((None, tb, dw), lambda r, jb: (r, jb, 0))
    q_n = pl.BlockSpec((None, tb, dw), lambda r, jb: (r, nx(jb), 0))
    k_p = pl.BlockSpec((None, tb, dw), lambda r, jb: (r, jnp.maximum(jb - 1, 0), 0))
    o_c, o_n = q_c, q_n
    l_c = pl.BlockSpec((None, tb, LANES), lambda r, jb: (r, jb, 0))
    l_n = pl.BlockSpec((None, tb, LANES), lambda r, jb: (r, nx(jb), 0))

    def body(qc_ref, qn_ref, kc_ref, kp_ref, vc_ref, vp_ref, doc_ref, don_ref, lc_ref, ln_ref, dlc_ref, dln_ref,
             dq_ref, dk_ref, dv_ref, s_scr, dp_scr, p_scr, ds_scr):
        jb = pl.program_id(1)
        has_prev = jb > 0
        has_next = jb < nb - 1
        for h in range(DIL_HEADS):
            sl = slice(h * dh, (h + 1) * dh)
            qc, qn, kc, kp = qc_ref[:, sl], qn_ref[:, sl], kc_ref[:, sl], kp_ref[:, sl]
            vc, vp, doc, don = vc_ref[:, sl], vp_ref[:, sl], doc_ref[:, sl], don_ref[:, sl]
            for pi, (qh, kh, vh, doh) in enumerate(((qc, kp, vp, doc), (qc, kc, vc, doc), (qn, kc, vc, don))):
                s_scr[h, pi] = _nt(qh, kh) * scale
                dp_scr[h, pi] = _nt(doh, vh)
        r = lax.broadcasted_iota(jnp.int32, (ATTN_ROWS, tb), 0)
        c = lax.broadcasted_iota(jnp.int32, (ATTN_ROWS, tb), 1)
        for h in range(DIL_HEADS):
            for rc in range(tb // ATTN_ROWS):
                rows = slice(rc * ATTN_ROWS, (rc + 1) * ATTN_ROWS)
                m_prev, m_cur = c >= r + rc * ATTN_ROWS, c <= r + rc * ATTN_ROWS
                lch, dlch = _col(lc_ref[rows, :], h), _col(dlc_ref[rows, :], h)
                lnh, dlnh = _col(ln_ref[rows, :], h), _col(dln_ref[rows, :], h)
                for pi, (lse_h, dl_h, mask) in enumerate(((lch, dlch, m_prev & has_prev), (lch, dlch, m_cur),
                                                          (lnh, dlnh, m_prev & has_next))):
                    pm = jnp.where(mask, jnp.exp(s_scr[h, pi, rows, :] - lse_h), 0.0)
                    ds_scr[h, pi, rows, :] = (pm * (dp_scr[h, pi, rows, :] - dl_h) * scale).astype(BF16)
                    if pi > 0:
                        p_scr[h, pi - 1, rows, :] = pm.astype(BF16)
        for h in range(DIL_HEADS):
            sl = slice(h * dh, (h + 1) * dh)
            dq_ref[:, sl] = _nn(ds_scr[h, 0], kp_ref[:, sl]) + _nn(ds_scr[h, 1], kc_ref[:, sl])
            dk_ref[:, sl] = _tn(ds_scr[h, 1], qc_ref[:, sl]) + _tn(ds_scr[h, 2], qn_ref[:, sl])
            dv_ref[:, sl] = _tn(p_scr[h, 0], doc_ref[:, sl]) + _tn(p_scr[h, 1], don_ref[:, sl])

    out = _sds((dil, n, dw), F32)
    return _call(body, name, (dil, nb), [q_c, q_n, q_c, k_p, q_c, k_p, o_c, o_n, l_c, l_n, l_c, l_n],
                 [o_c, o_c, o_c], [out, out, out],
                 scratch=[pltpu.VMEM((DIL_HEADS, 3, tb, tb), F32), pltpu.VMEM((DIL_HEADS, 3, tb, tb), F32),
                          pltpu.VMEM((DIL_HEADS, 2, tb, tb), BF16), pltpu.VMEM((DIL_HEADS, 3, tb, tb), BF16)],
                 sem=("parallel", "arbitrary"), carry=carry)(
        q3, q3, k3, k3, v3, v3, do3, do3, lse3, lse3, dl3, dl3)


def _head_expand(dw, dh):
    r = lax.broadcasted_iota(jnp.int32, (LANES, dw), 0)
    c = lax.broadcasted_iota(jnp.int32, (LANES, dw), 1)
    return ((c // dh) == r).astype(F32)


def _mix_weights(l0, l1, l2):
    mx = jnp.maximum(jnp.maximum(l0, l1), l2)
    e = [jnp.exp(l0 - mx), jnp.exp(l1 - mx), jnp.exp(l2 - mx)]
    den = e[0] + e[1] + e[2]
    return [v / den for v in e]


MIX_ROWS = 256


def _tile_scratch(tb, w):
    return pltpu.VMEM((w // LANES, tb, LANES), F32)


def _read_tiles(scr):
    return jnp.concatenate([scr[ct] for ct in range(scr.shape[0])], axis=1)


def _write_tiles(scr, val):
    for ct in range(scr.shape[0]):
        scr[ct] = val[:, ct * LANES:(ct + 1) * LANES]


def _to_token_order(scr, blk_ref, dil):
    rows = blk_ref.shape[1]
    for ct in range(scr.shape[0]):
        cs = slice(ct * LANES, (ct + 1) * LANES)
        if dil == 1:
            scr[ct] = blk_ref[0, :, cs].astype(F32)
        else:
            for r in range(dil):
                scr.at[ct][pl.ds(r, rows, stride=dil), :] = blk_ref[r, :, cs].astype(F32)


def _to_residue_order(out_ref, scr, dil):
    rows = out_ref.shape[1]
    for ct in range(scr.shape[0]):
        cs = slice(ct * LANES, (ct + 1) * LANES)
        if dil == 1:
            out_ref[0, :, cs] = scr[ct].astype(out_ref.dtype)
        else:
            for r in range(dil):
                out_ref[r, :, cs] = scr.at[ct][pl.ds(r, rows, stride=dil), :].astype(out_ref.dtype)


def _residue_spec(dil, tb, w):
    return pl.BlockSpec((dil, tb // dil, w), lambda i: (0, i, 0))


def _mix_fwd(os_, ls_, name, *, dh):
    dw = os_[0].shape[2]
    n = os_[0].shape[0] * os_[0].shape[1]
    tb = MIX_ROWS

    def body(o0, o1, o2, l0, l1, l2, out_ref, so0, so1, so2, sl0, sl1, sl2):
        for src, scr, dil in zip((o0, o1, o2, l0, l1, l2), (so0, so1, so2, sl0, sl1, sl2), DIL_DILATIONS * 2):
            _to_token_order(scr, src, dil)
        ex = _head_expand(dw, dh)
        ws = _mix_weights(_read_tiles(sl0), _read_tiles(sl1), _read_tiles(sl2))
        out_ref[...] = sum(_nn(wv, ex, HIGHEST) * _read_tiles(s) for wv, s in zip(ws, (so0, so1, so2))).astype(
            out_ref.dtype)

    specs = [_residue_spec(dl, tb, dw) for dl in DIL_DILATIONS] + [_residue_spec(dl, tb, LANES) for dl in DIL_DILATIONS]
    return _call(body, name, (n // tb,), specs, pl.BlockSpec((tb, dw), lambda i: (i, 0)), _sds((n, dw), BF16),
                 scratch=[_tile_scratch(tb, dw)] * 3 + [_tile_scratch(tb, LANES)] * 3, sem=("parallel",))(
        *os_, *ls_)


def _mix_bwd(os_, ls_, do, name, *, dh):
    dw = os_[0].shape[2]
    n = os_[0].shape[0] * os_[0].shape[1]
    tb = MIX_ROWS

    def body(o0, o1, o2, l0, l1, l2, do_ref, d0, d1, d2, r0, r1, r2, so0, so1, so2, sl0, sl1, sl2):
        for src, scr, dil in zip((o0, o1, o2, l0, l1, l2), (so0, so1, so2, sl0, sl1, sl2), DIL_DILATIONS * 2):
            _to_token_order(scr, src, dil)
        ex = _head_expand(dw, dh)
        dov = do_ref[...]
        ws = _mix_weights(_read_tiles(sl0), _read_tiles(sl1), _read_tiles(sl2))
        dws = [_nt(dov * _read_tiles(s), ex, HIGHEST) for s in (so0, so1, so2)]
        sdw = ws[0] * dws[0] + ws[1] * dws[1] + ws[2] * dws[2]
        for wv, so, sl_, d_ref, r_ref, dil in zip(ws, (so0, so1, so2), (sl0, sl1, sl2), (d0, d1, d2), (r0, r1, r2),
                                                  DIL_DILATIONS):
            _write_tiles(so, _nn(wv, ex, HIGHEST) * dov)
            _write_tiles(sl_, wv * sdw)
            _to_residue_order(d_ref, so, dil)
            _to_residue_order(r_ref, sl_, dil)

    specs = [_residue_spec(dl, tb, dw) for dl in DIL_DILATIONS] + [_residue_spec(dl, tb, LANES) for dl in DIL_DILATIONS]
    return _call(body, name, (n // tb,), specs + [pl.BlockSpec((tb, dw), lambda i: (i, 0))], specs,
                 [_sds(o.shape, BF16) for o in os_] + [_sds(l.shape, F32) for l in ls_],
                 scratch=[_tile_scratch(tb, dw)] * 3 + [_tile_scratch(tb, LANES)] * 3, sem=("parallel",))(
        *os_, *ls_, do)


def _to_residue(x, cos2, sin2, name, *, dw, rope):
    n = x.shape[0]
    dh = cos2.shape[1]
    tb = MIX_ROWS

    def body(x_ref, c_ref, s_ref, o0, o1, o2, scr):
        c, s = c_ref[...], s_ref[...]
        for g, (o_ref, dil) in enumerate(zip((o0, o1, o2), DIL_DILATIONS)):
            for hh in range(dw // dh):
                v = x_ref[:, g * dw + hh * dh:g * dw + (hh + 1) * dh]
                if rope:
                    v = v * c + pltpu.roll(v, dh // 2, 1) * s
                lo = (hh * dh) % LANES
                scr[(hh * dh) // LANES, :, lo:lo + dh] = v
            _to_residue_order(o_ref, scr, dil)

    row = pl.BlockSpec((tb, dh), lambda i: (i, 0))
    return _call(body, name, (n // tb,), [pl.BlockSpec((tb, 3 * dw), lambda i: (i, 0)), row, row],
                 [_residue_spec(dl, tb, dw) for dl in DIL_DILATIONS],
                 [_sds((dl, n // dl, dw), BF16) for dl in DIL_DILATIONS],
                 scratch=[_tile_scratch(tb, dw)], sem=("parallel",))(x, cos2, sin2)


FROM_RESIDUE_ROWS = 128


def _from_residue(cols, cos2, sin2, name, *, dw, rope_cols):
    dils = DIL_DILATIONS * (len(cols) // len(DIL_DILATIONS))
    n = cols[0][0].shape[0] * cols[0][0].shape[1]
    dh = cos2.shape[1]
    tb = FROM_RESIDUE_ROWS
    counts = [len(cl) for cl in cols]

    def body(*refs):
        n_in = sum(counts)
        srcs, (c_ref, s_ref, o_ref, scr) = refs[:n_in], refs[n_in:]
        c, s = c_ref[...], s_ref[...]
        k = 0
        for cb, (cnt, dil) in enumerate(zip(counts, dils)):
            rows = tb // dil
            for ct in range(dw // LANES):
                cs = slice(ct * LANES, (ct + 1) * LANES)
                for r in range(dil):
                    acc = srcs[k][r, :, cs]
                    for extra in srcs[k + 1:k + cnt]:
                        acc = acc + extra[r, :, cs]
                    if dil == 1:
                        scr[ct] = acc
                    else:
                        scr.at[ct][pl.ds(r, rows, stride=dil), :] = acc
            k += cnt
            for hh in range(dw // dh):
                lo = (hh * dh) % LANES
                v = scr[(hh * dh) // LANES, :, lo:lo + dh]
                if cb < rope_cols:
                    v = v * c - pltpu.roll(v, dh // 2, 1) * s
                o_ref[:, cb * dw + hh * dh:cb * dw + (hh + 1) * dh] = v.astype(o_ref.dtype)

    row = pl.BlockSpec((tb, dh), lambda i: (i, 0))
    specs = [_residue_spec(dil, tb, dw) for cnt, dil in zip(counts, dils) for _ in range(cnt)]
    return _call(body, name, (n // tb,), specs + [row, row], pl.BlockSpec((tb, len(cols) * dw), lambda i: (i, 0)),
                 _sds((n, len(cols) * dw), BF16), scratch=[_tile_scratch(tb, dw)], sem=("parallel",))(
        *[a for cl in cols for a in cl], cos2, sin2)


def _mem_attn_fwd(q, kv, name):
    n, mw = q.shape
    ml = kv.shape[0]
    dh = mw // MEM_HEADS
    scale = dh ** -0.5
    tb = _tile(n, 512, 8)

    def body(q_ref, kv_ref, o_ref):
        for h in range(MEM_HEADS):
            sl = slice(h * dh, (h + 1) * dh)
            s = _nt(q_ref[:, sl].astype(BF16), kv_ref[:, sl].astype(BF16)) * scale
            pm = jnp.exp(s - jnp.max(s, axis=1, keepdims=True))
            pm = pm / jnp.sum(pm, axis=1, keepdims=True)
            o_ref[:, sl] = _nn(pm.astype(BF16), kv_ref[:, mw + h * dh:mw + (h + 1) * dh].astype(BF16)).astype(o_ref.dtype)

    return _call(body, name, (n // tb,), [pl.BlockSpec((tb, mw), lambda i: (i, 0)), pl.BlockSpec((ml, 2 * mw), lambda i: (0, 0))],
                 pl.BlockSpec((tb, mw), lambda i: (i, 0)), _sds((n, mw), BF16), sem=("parallel",))(q, kv)


def _mem_attn_bwd(q, kv, do, name):
    n, mw = q.shape
    ml = kv.shape[0]
    dh = mw // MEM_HEADS
    scale = dh ** -0.5
    tb = _tile(n, 512, 8)

    def body(q_ref, kv_ref, do_ref, dq_ref, dkv_ref):
        @pl.when(pl.program_id(0) == 0)
        def _():
            dkv_ref[...] = jnp.zeros_like(dkv_ref)

        for h in range(MEM_HEADS):
            sl = slice(h * dh, (h + 1) * dh)
            vsl = slice(mw + h * dh, mw + (h + 1) * dh)
            qh, kh, vh = q_ref[:, sl].astype(BF16), kv_ref[:, sl].astype(BF16), kv_ref[:, vsl].astype(BF16)
            doh = do_ref[:, sl].astype(BF16)
            s = _nt(qh, kh) * scale
            pm = jnp.exp(s - jnp.max(s, axis=1, keepdims=True))
            pm = pm / jnp.sum(pm, axis=1, keepdims=True)
            dp = _nt(doh, vh)
            ds = (pm * (dp - jnp.sum(pm * dp, axis=1, keepdims=True)) * scale).astype(BF16)
            dq_ref[:, sl] = _nn(ds, kh).astype(dq_ref.dtype)
            dkv_ref[:, sl] += _tn(ds, qh)
            dkv_ref[:, vsl] += _tn(pm.astype(BF16), doh)

    row = pl.BlockSpec((tb, mw), lambda i: (i, 0))
    kvs = pl.BlockSpec((ml, 2 * mw), lambda i: (0, 0))
    return _call(body, name, (n // tb,), [row, kvs, row], [row, kvs], [_sds((n, mw), BF16), _sds((ml, 2 * mw), F32)],
                 sem=("arbitrary",))(q, kv, do)


def _adamw(w, g, m, v, name):
    shape = w.shape
    c = shape[-1]
    r = max(1, math.prod(shape[:-1]))
    w2, g2, m2, v2 = (t.reshape(r, c) for t in (w, g, m, v))
    tb = _tile(r, max(8, (1 << 19) // max(c, 1) // 8 * 8), 8)
    bc1 = 1.0 - ADAM_B1 ** ADAM_STEP
    bc2 = 1.0 - ADAM_B2 ** ADAM_STEP

    def body(w_ref, g_ref, m_ref, v_ref, d_ref, nm_ref, nv_ref):
        gv = g_ref[...]
        nm = ADAM_B1 * m_ref[...] + (1.0 - ADAM_B1) * gv
        nv = ADAM_B2 * v_ref[...] + (1.0 - ADAM_B2) * (gv * gv)
        d_ref[...] = -ADAM_LR * ((nm / bc1) / (jnp.sqrt(nv / bc2) + ADAM_EPS) + ADAM_WD * w_ref[...])
        nm_ref[...] = nm
        nv_ref[...] = nv

    spec = pl.BlockSpec((tb, c), lambda i: (i, 0))
    o = _sds((r, c), F32)
    d, nm, nv = _call(body, name, (r // tb,), [spec] * 4, [spec] * 3, [o, o, o], sem=("parallel",))(w2, g2, m2, v2)
    return d.reshape(shape), nm.reshape(shape), nv.reshape(shape)


def _place():
    x, y, c = lax.axis_index("x"), lax.axis_index("y"), lax.axis_index("c")
    chips = [(1 - x, y), (x, 1 - y), (1 - x, 1 - y)]
    return x, y, c, chips


def _cols(ref, c):
    hw = ref.shape[-1] // 2
    return ref.at[(slice(None),) * (len(ref.shape) - 1) + (pl.ds(c * hw, hw),)]


def _slot(ref, s):
    return ref.at[:, s]


def _my_chip():
    return 2 * lax.axis_index("x") + lax.axis_index("y")


def _place_shard(t, name):
    nl, r, n = t.shape
    tb = _tile(r, 512, SUBLANES_BF16)

    def body(t_ref, o_ref):
        o_ref[...] = t_ref[...].astype(BF16)

    return _call(body, name, (nl, r // tb), [pl.BlockSpec((None, tb, n), lambda l, i: (l, i, 0))],
                 pl.BlockSpec((None, None, tb, n), lambda l, i: (l, _my_chip(), i, 0)),
                 _sds((nl, 4, r, n), BF16), sem=("parallel", "parallel"))(t)


def _aliased_comm_call(body, name, bufs, n_sems):
    na = len(bufs)
    anyspec = pl.BlockSpec(memory_space=pl.ANY)
    return pl.pallas_call(
        body, name=name, in_specs=[anyspec] * na, out_specs=[anyspec] * na,
        out_shape=[_sds(b.shape, b.dtype) for b in bufs], input_output_aliases={a: a for a in range(na)},
        scratch_shapes=[pltpu.SemaphoreType.DMA((n_sems,)), pltpu.SemaphoreType.DMA((n_sems,))],
        compiler_params=pltpu.CompilerParams(has_side_effects=True))(*bufs)


def _same_block_copy(blk, send, recv, k, to):
    return pltpu.make_async_remote_copy(src_ref=blk, dst_ref=blk, send_sem=send.at[k], recv_sem=recv.at[k],
                                        device_id=to, device_id_type=MESH)


def _gather_start(bufs, send, recv):
    x, y, c, chips = _place()
    for a, buf in enumerate(bufs):
        for j, (px, py) in enumerate(chips):
            _same_block_copy(_cols(_slot(buf, 2 * x + y), c), send, recv, 6 * a + j, (px, py, c)).start()


def _gather_finish(bufs, send, recv):
    x, y, c, chips = _place()
    sib = (x, y, 1 - c)
    for a, buf in enumerate(bufs):
        for j, (px, py) in enumerate(chips):
            blk = _cols(_slot(buf, 2 * px + py), c)
            _same_block_copy(blk, send, recv, 6 * a + j, (px, py, c)).wait_recv()
            _same_block_copy(blk, send, recv, 6 * a + 3 + j, sib).start()
    for a, buf in enumerate(bufs):
        for j, (px, py) in enumerate(chips):
            _same_block_copy(_cols(_slot(buf, 2 * px + py), 1 - c), send, recv, 6 * a + 3 + j, sib).wait_recv()
    for a, buf in enumerate(bufs):
        for j, (px, py) in enumerate(chips):
            _same_block_copy(_cols(_slot(buf, 2 * x + y), c), send, recv, 6 * a + j, (px, py, c)).wait_send()
            _same_block_copy(_cols(_slot(buf, 2 * px + py), c), send, recv, 6 * a + 3 + j, sib).wait_send()


def _gather_carry(bufs):
    return _Carry(bufs, range(len(bufs)), 6 * len(bufs), _gather_start, _gather_finish)


def _gather_weights(bufs, name):
    na = len(bufs)

    def body(*refs):
        outs = refs[na:2 * na]
        send, recv = refs[2 * na:]
        _gather_start(outs, send, recv)
        _gather_finish(outs, send, recv)

    return _aliased_comm_call(body, name, bufs, 6 * na)


def _swap_copy(bufs, send, recv, a, c):
    na = len(bufs) // 2
    x, y = lax.axis_index("x"), lax.axis_index("y")
    return pltpu.make_async_remote_copy(src_ref=_cols(bufs[a], 1 - c), dst_ref=bufs[na + a], send_sem=send.at[a],
                                        recv_sem=recv.at[a], device_id=(x, y, 1 - c), device_id_type=MESH)


def _swap_start(bufs, send, recv):
    c = lax.axis_index("c")
    for a in range(len(bufs) // 2):
        _swap_copy(bufs, send, recv, a, c).start()


def _swap_finish(bufs, send, recv):
    c = lax.axis_index("c")
    for a in range(len(bufs) // 2):
        _swap_copy(bufs, send, recv, a, c).wait()


def _swap_carry(grads):
    lands = [lax.empty(g.shape[:-1] + (g.shape[-1] // 2,), g.dtype) for g in grads]
    return _Carry(list(grads) + lands, range(len(grads), 2 * len(grads)), len(grads), _swap_start, _swap_finish)


def _swap_halves(grads, name):
    na = len(grads)

    def body(*refs):
        srcs, outs = refs[:na], refs[na:2 * na]
        send, recv = refs[2 * na:]
        x, y, c, _ = _place()
        sib = (x, y, 1 - c)
        cps = [pltpu.make_async_remote_copy(src_ref=_cols(srcs[a], 1 - c), dst_ref=outs[a], send_sem=send.at[a],
                                            recv_sem=recv.at[a], device_id=sib, device_id_type=MESH) for a in range(na)]
        for cp in cps:
            cp.start()
        for cp in cps:
            cp.wait()

    anyspec = pl.BlockSpec(memory_space=pl.ANY)
    return pl.pallas_call(
        body, name=name, in_specs=[anyspec] * na, out_specs=[anyspec] * na,
        out_shape=[_sds(g.shape[:-1] + (g.shape[-1] // 2,), g.dtype) for g in grads],
        scratch_shapes=[pltpu.SemaphoreType.DMA((na,)), pltpu.SemaphoreType.DMA((na,))],
        compiler_params=pltpu.CompilerParams(has_side_effects=True))(*grads)


def _pair_sum(g, theirs, name):
    nl, _, r, n = g.shape
    hw = n // 2
    tb = _tile(r, 256, SUBLANES_BF16)

    def body(g_ref, t_ref, p_ref, own_ref):
        p_ref[...] = (g_ref[...] + t_ref[...]).astype(BF16)
        own_ref[...] = p_ref[_my_chip()]

    blk = pl.BlockSpec((None, 4, tb, hw), lambda l, i: (l, 0, i, 0))
    out = _sds((nl, 4, r, hw), BF16)
    return _call(body, name, (nl, r // tb),
                 [pl.BlockSpec((None, 4, tb, hw), lambda l, i: (l, 0, i, lax.axis_index("c"))), blk],
                 [blk, pl.BlockSpec((None, None, tb, hw), lambda l, i: (l, _my_chip(), i, 0))],
                 [out, out], sem=("parallel", "parallel"))(g, theirs)


def _scatter_start(bufs, send, recv):
    na = len(bufs) // 2
    x, y, c, chips = _place()
    for a in range(na):
        for j, (px, py) in enumerate(chips):
            pltpu.make_async_remote_copy(src_ref=_slot(bufs[a], 2 * px + py), dst_ref=_slot(bufs[na + a], 2 * x + y),
                                         send_sem=send.at[3 * a + j], recv_sem=recv.at[3 * a + j],
                                         device_id=(px, py, c), device_id_type=MESH).start()


def _scatter_finish(bufs, send, recv):
    na = len(bufs) // 2
    x, y, c, chips = _place()
    for a in range(na):
        for j, (px, py) in enumerate(chips):
            _same_block_copy(_slot(bufs[na + a], 2 * px + py), send, recv, 3 * a + j, (px, py, c)).wait_recv()
    for a in range(na):
        for j, (px, py) in enumerate(chips):
            _same_block_copy(_slot(bufs[a], 2 * px + py), send, recv, 3 * a + j, (px, py, c)).wait_send()


def _scatter_carry(parts, lands):
    na = len(parts)
    return _Carry(list(parts) + list(lands), range(na, 2 * na), 3 * na, _scatter_start, _scatter_finish)


def _scatter_partials(parts, lands, name):
    na = len(parts)

    def body(*refs):
        bufs = list(refs[:na]) + list(refs[2 * na:3 * na])
        send, recv = refs[3 * na:]
        _scatter_start(bufs, send, recv)
        _scatter_finish(bufs, send, recv)

    anyspec = pl.BlockSpec(memory_space=pl.ANY)
    return pl.pallas_call(
        body, name=name, in_specs=[anyspec] * (2 * na), out_specs=[anyspec] * na,
        out_shape=[_sds(b.shape, b.dtype) for b in lands],
        input_output_aliases={na + a: a for a in range(na)},
        scratch_shapes=[pltpu.SemaphoreType.DMA((3 * na,)), pltpu.SemaphoreType.DMA((3 * na,))],
        compiler_params=pltpu.CompilerParams(has_side_effects=True))(*parts, *lands)


def _chip_sum(land, n, name):
    nl, _, r, hw = land.shape
    tb = _tile(r, 512, SUBLANES_BF16)

    def body(i_ref, o_ref):
        o_ref[...] = ((i_ref[0].astype(F32) + i_ref[1].astype(F32)) + i_ref[2].astype(F32)) + i_ref[3].astype(F32)

    return _call(body, name, (nl, r // tb), [pl.BlockSpec((None, 4, tb, hw), lambda l, i: (l, 0, i, 0))],
                 pl.BlockSpec((None, tb, hw), lambda l, i: (l, i, lax.axis_index("c"))),
                 _sds((nl, r, n), F32), sem=("parallel", "parallel"))(land)


def _join_halves(bufs, name):
    na = len(bufs)

    def body(*refs):
        outs = refs[na:2 * na]
        send, recv = refs[2 * na:]
        x, y, c, _ = _place()
        sib = (x, y, 1 - c)
        cps = []
        for a in range(na):
            mine = _cols(outs[a], c)
            cp = pltpu.make_async_remote_copy(src_ref=mine, dst_ref=mine, send_sem=send.at[a], recv_sem=recv.at[a],
                                              device_id=sib, device_id_type=MESH)
            cp.start()
            cps.append(cp)
        for a in range(na):
            oth = _cols(outs[a], 1 - c)
            pltpu.make_async_remote_copy(src_ref=oth, dst_ref=oth, send_sem=send.at[a], recv_sem=recv.at[a],
                                         device_id=sib, device_id_type=MESH).wait_recv()
        for cp in cps:
            cp.wait_send()

    return _aliased_comm_call(body, name, bufs, na)


def _allreduce_small(v, name):
    rows, cols = v.shape

    def body(v_ref, o_ref, buf, send, recv):
        x, y, c, _ = _place()
        me = 4 * x + 2 * y + c
        buf[me] = v_ref[...]
        cps = []
        for d in range(1, 8):
            to = (x ^ (d >> 2), y ^ ((d >> 1) & 1), c ^ (d & 1))
            cp = pltpu.make_async_remote_copy(src_ref=v_ref, dst_ref=buf.at[me], send_sem=send.at[d - 1],
                                              recv_sem=recv.at[d - 1], device_id=to, device_id_type=MESH)
            cp.start()
            cps.append(cp)
        for d in range(1, 8):
            frm = 4 * (x ^ (d >> 2)) + 2 * (y ^ ((d >> 1) & 1)) + (c ^ (d & 1))
            got = buf.at[frm]
            pltpu.make_async_remote_copy(src_ref=got, dst_ref=got, send_sem=send.at[d - 1], recv_sem=recv.at[d - 1],
                                         device_id=(x, y, c), device_id_type=MESH).wait_recv()
        for cp in cps:
            cp.wait_send()
        acc = buf[0]
        for d in range(1, 8):
            acc = acc + buf[d]
        o_ref[...] = acc

    vm = pl.BlockSpec(memory_space=pltpu.VMEM)
    return pl.pallas_call(
        body, name=name, in_specs=[vm], out_specs=vm, out_shape=_sds((rows, cols), F32),
        scratch_shapes=[pltpu.VMEM((8, rows, cols), F32), pltpu.SemaphoreType.DMA((7,)), pltpu.SemaphoreType.DMA((7,))],
        compiler_params=pltpu.CompilerParams(has_side_effects=True, vmem_limit_bytes=V7X_VMEM_LIMIT))(v)


def _pack(arrs, pw):
    rows = []
    for a in arrs:
        f = a.astype(F32).reshape(-1)
        pad = (-f.shape[0]) % pw
        rows.append(jnp.pad(f, (0, pad)).reshape(-1, pw))
    p = jnp.concatenate(rows, axis=0)
    return jnp.pad(p, ((0, (-p.shape[0]) % 8), (0, 0)))


def _unpack(p, shapes, pw):
    out, r0 = [], 0
    for s in shapes:
        size = math.prod(s) if s else 1
        nr = -(-size // pw)
        out.append(p[r0:r0 + nr].reshape(-1)[:size].reshape(s))
        r0 += nr
    return out


def _step(x, mem, positions, wts, loss_target, mom, vel):
    w = dict(zip(WEIGHTS, wts))
    s_len, d = x.shape[1], x.shape[2]
    ml = mem.shape[1]
    depth = w['norm_mix'].shape[0]
    n_a = w['ssd_w_in'].shape[0]
    di = 4 * w['ssd_w_out'].shape[1]
    nh = w['ssd_dt_bias'].shape[1]
    p = di // nh
    gn = SSD_GROUPS
    jh = nh // gn
    cd = 4 * w['ssd_conv_w'].shape[2]
    w3 = 4 * w['dil_w_q'].shape[2]
    dw = w3 // 3
    hd = dw // DIL_HEADS
    mw = w['mem_w_q'].shape[2]
    ff = 4 * w['ffn_w_out'].shape[1]
    cx, cy, cc = lax.axis_index("x"), lax.axis_index("y"), lax.axis_index("c")
    chip = 2 * cx + cy


    big = list(COL_SHARDED + ROW_SHARDED)
    stacked = ('mem_w_q', 'mem_w_kv', 'mem_w_o')

    def mixer_items(i):
        if i < n_a:
            return [('ssd_w_in', i), ('ssd_w_out', i)]
        return ([('w_kv_shared', 0)] if i == n_a else []) + [('dil_w_q', i - n_a), ('dil_w_o', i - n_a)]

    def ffn_items(i):
        return [('ffn_w_in', i), ('ffn_w_out', i)] + ([(n_, None) for n_ in stacked] if i == 0 else [])

    def shard_of(name, l):
        t = w[name] if w[name].ndim == 3 else w[name][None]
        if l is not None:
            t = t[l:l + 1]
        return jnp.swapaxes(t, 1, 2).astype(BF16) if name in COL_SHARDED else t

    full = {n_: [None] * (w[n_].shape[0] if w[n_].ndim == 3 else 1) for n_ in big}

    def placed_bufs(its):
        return [_place_shard(shard_of(n_, l), f"place_{n_}_{l}") for n_, l in its]

    def record(its, bufs):
        for (n_, l), t in zip(its, bufs):
            fl = t.reshape(t.shape[0], 4 * t.shape[2], t.shape[3])
            if l is None:
                full[n_] = [fl[k] for k in range(fl.shape[0])]
            else:
                full[n_][l] = fl[0]

    record(mixer_items(0), _gather_weights(placed_bufs(mixer_items(0)), "gather_weights_first"))

    small_sharded = ('ssd_conv_w', 'ssd_conv_b', 'ssd_norm')

    def placed(name):
        t = w[name]
        wd = t.shape[-1]
        z = jnp.zeros(t.shape[:-1] + (4 * wd,), F32)
        z = lax.dynamic_update_slice_in_dim(z, t, chip * wd, axis=t.ndim - 1)
        return z * (cc == 0).astype(F32)

    sm_shapes = [w[n_].shape[:-1] + (4 * w[n_].shape[-1],) for n_ in small_sharded]
    conv_w, conv_b, ssd_nw = _unpack(_allreduce_small(_pack([placed(n_) for n_ in small_sharded], d), "gather_small"),
                                     sm_shapes, d)

    h = x[0]
    tgt = loss_target[0]
    mem_n = _rmsnorm(mem[0], w['mem_src_norm'], "mem_src_norm_fwd")
    cos2, sin2 = _rope_tables(positions[0].reshape(s_len, 1), hd, "rope_tables")
    a_neg = -jnp.exp(w['ssd_a_log'])
    saved = []
    kv_saved = None
    kgs = vgs = None
    for i in range(depth):
        sv = {'h_mix': h}
        u = _rmsnorm(h, w['norm_mix'][i], f"norm_mix_fwd_{i}")
        sv['u'] = u
        if i < n_a:
            wt = full['ssd_w_in'][i]
            wt_dt = wt[di + cd:]
            z = _mm(u, wt, 'nt', f"ssd_in_z_{i}", m=s_len, n=di, k=d)
            xbc_raw = _mm(u, wt, 'nt', f"ssd_in_xbc_{i}", m=s_len, n=cd, k=d, b_noff=di)
            dt_raw = _mm(u, wt_dt, 'nt', f"ssd_in_dt_{i}", m=s_len, n=nh, k=d)
            xbc = _conv_fwd(xbc_raw, conv_w[i], conv_b[i], f"ssd_conv_fwd_{i}")
            dt = _softplus_fwd(dt_raw, w['ssd_dt_bias'][i], f"ssd_dt_fwd_{i}")
            dtc = dt.reshape(s_len, gn, jh).transpose(1, 0, 2)
            dtr = dt.reshape(s_len, gn, jh).transpose(1, 2, 0)
            a_row = a_neg[i].reshape(gn, 1, jh)
            a_col = a_neg[i].reshape(gn, jh, 1)
            dskip = jnp.repeat(w['ssd_d'][i], p).reshape(gn, 1, jh * p)
            (y, hins), got = _ssd_fwd(xbc, dtc, dtr, a_row, a_col, dskip, f"ssd_scan_fwd_{i}", di=di, p=p,
                                      carry=_gather_carry(placed_bufs(ffn_items(i))))
            record(ffn_items(i), got)
            yn = _gated_norm_fwd(y, z, ssd_nw[i], f"ssd_norm_fwd_{i}")
            h = _mm(yn, full['ssd_w_out'][i], 'nn', f"ssd_out_{i}", m=s_len, n=d, k=di, res=h)
            sv.update(wt=wt, wt_dt=wt_dt, z=z, xbc_raw=xbc_raw, xbc=xbc, dt_raw=dt_raw, dtc=dtc, dtr=dtr, a_row=a_row,
                      a_col=a_col, dskip=dskip, y=y, hins=hins, yn=yn)
        else:
            j = i - n_a
            if j == 0:
                kvn = _rmsnorm(h, w['kv_norm'], "kv_norm_fwd")
                wkv = full['w_kv_shared'][0]
                k_raw = _mm(kvn, wkv, 'nt', "kv_proj_k", m=s_len, n=w3, k=d)
                v_raw = _mm(kvn, wkv, 'nt', "kv_proj_v", m=s_len, n=w3, k=d, b_noff=w3)
                kgs = _to_residue(k_raw, cos2, sin2, "rope_k", dw=dw, rope=True)
                vgs = _to_residue(v_raw, cos2, sin2, "residue_v", dw=dw, rope=False)
                kv_saved = {'h': h, 'kvn': kvn, 'wkv': wkv}
            q_raw = _mm(u, full['dil_w_q'][j], 'nt', f"dil_q_{i}", m=s_len, n=w3, k=d)
            qgs = _to_residue(q_raw, cos2, sin2, f"rope_q_{i}", dw=dw, rope=True)
            os_, ls_ = [], []
            for g in range(len(DIL_DILATIONS)):
                if g == 0:
                    (o3, l3), got = _dil_attn_fwd(qgs[g], kgs[g], vgs[g], f"dil_attn_fwd_{i}_{g}", dh=hd,
                                                  carry=_gather_carry(placed_bufs(ffn_items(i))))
                    record(ffn_items(i), got)
                else:
                    o3, l3 = _dil_attn_fwd(qgs[g], kgs[g], vgs[g], f"dil_attn_fwd_{i}_{g}", dh=hd)
                os_.append(o3)
                ls_.append(l3)
            om = _mix_fwd(os_, ls_, f"dil_mix_fwd_{i}", dh=hd)
            h = _mm(om, full['dil_w_o'][j], 'nn', f"dil_out_{i}", m=s_len, n=d, k=dw, res=h)
            sv.update(qgs=qgs, os=os_, ls=ls_, om=om)
        sv['h_mem'] = h
        u2 = _rmsnorm(h, w['norm_mem'][i], f"norm_mem_fwd_{i}")
        qm = _mm(u2, full['mem_w_q'][i], 'nn', f"mem_q_{i}", m=s_len, n=mw, k=d, out_dtype=BF16)
        kvm = _mm(mem_n, full['mem_w_kv'][i], 'nn', f"mem_kv_{i}", m=ml, n=2 * mw, k=d)
        omem = _mem_attn_fwd(qm, kvm, f"mem_attn_fwd_{i}")
        h = _mm(omem, full['mem_w_o'][i], 'nt', f"mem_out_{i}", m=s_len, n=d, k=mw, res=h)
        sv.update(u2=u2, qm=qm, kvm=kvm, omem=omem, h_ffn=h)
        u3 = _rmsnorm(h, w['norm_ffn'][i], f"norm_ffn_fwd_{i}")
        if i + 1 < depth:
            gu, got = _mm(u3, full['ffn_w_in'][i], 'nt', f"ffn_in_{i}", m=s_len, n=2 * ff, k=d, out_dtype=BF16,
                          carry=_gather_carry(placed_bufs(mixer_items(i + 1))))
            record(mixer_items(i + 1), got)
        else:
            gu = _mm(u3, full['ffn_w_in'][i], 'nt', f"ffn_in_{i}", m=s_len, n=2 * ff, k=d, out_dtype=BF16)
        act = _swiglu_fwd(gu, f"ffn_act_fwd_{i}")
        h = _mm(act, full['ffn_w_out'][i], 'nn', f"ffn_out_{i}", m=s_len, n=d, k=ff, res=h)
        sv.update(u3=u3, gu=gu, act=act)
        saved.append(sv)

    loss_part, dh, dhb, dg_final = _loss_and_grad(h, w['norm_final'], tgt, "loss_and_final_norm")

    gbig = {n_: [None] * len(full[n_]) for n_ in big}
    gsm = {n_: [None] * depth for n_ in ('norm_mix', 'norm_mem', 'norm_ffn')}
    for n_ in ('ssd_conv_w', 'ssd_conv_b', 'ssd_norm', 'ssd_dt_bias', 'ssd_a_log', 'ssd_d'):
        gsm[n_] = [None] * n_a
    dmem_n = None
    dk_parts = [[] for _ in DIL_DILATIONS]
    dv_parts = [[] for _ in DIL_DILATIONS]
    gshard = {}

    def rs_grads(its):
        gl = []
        for n_, l in its:
            t = jnp.stack(gbig[n_]) if l is None else gbig[n_][l][None]
            gl.append(t.reshape(t.shape[0], 4, t.shape[1] // 4, t.shape[2]))
        return gl

    def rs_pairs(its, gl, theirs, tag):
        pairs = [_pair_sum(t, o, f"grad_pair_sum_{n_}_{l}") for (n_, l), t, o in zip(its, gl, theirs)]
        return its, [pr[0] for pr in pairs], [pr[1] for pr in pairs], [t.shape[3] for t in gl], tag

    def rs_finish(pend, lands):
        its, _, _, widths, tag = pend
        fins = [_chip_sum(ld, nw_, f"grad_chip_sum_{n_}_{l}") for (n_, l), ld, nw_ in zip(its, lands, widths)]
        for it, t in zip(its, _join_halves(fins, f"grad_join_halves_{tag}")):
            gshard[it] = t

    pend_mixer = None
    for i in reversed(range(depth)):
        sv = saved[i]
        dact = _mm(dhb, full['ffn_w_out'][i], 'nt', f"ffn_out_bwd_x_{i}", m=s_len, n=ff, k=d, out_dtype=BF16)
        gbig['ffn_w_out'][i] = _mm(sv['act'], dhb, 'tn', f"ffn_out_bwd_w_{i}", m=ff, n=d, k=s_len)
        dgu = _swiglu_bwd(sv['gu'], dact, f"ffn_act_bwd_{i}")
        if pend_mixer is None:
            gbig['ffn_w_in'][i] = _mm(dgu, sv['u3'], 'tn', f"ffn_in_bwd_w_{i}", m=2 * ff, n=d, k=s_len)
        else:
            gbig['ffn_w_in'][i], got = _mm(dgu, sv['u3'], 'tn', f"ffn_in_bwd_w_{i}", m=2 * ff, n=d, k=s_len,
                                           carry=_scatter_carry(pend_mixer[1], pend_mixer[2]))
            rs_finish(pend_mixer, got)
        gl_ffn = rs_grads(ffn_items(i)[:2])
        du, theirs_ffn = _mm(dgu, full['ffn_w_in'][i], 'nn', f"ffn_in_bwd_x_{i}", m=s_len, n=d, k=2 * ff,
                             carry=_swap_carry(gl_ffn))
        dh, dhb, gsm['norm_ffn'][i] = _rmsnorm_bwd(sv['h_ffn'], w['norm_ffn'][i], du, dh, f"norm_ffn_bwd_{i}")
        do = _mm(dhb, full['mem_w_o'][i], 'nn', f"mem_out_bwd_x_{i}", m=s_len, n=mw, k=d, out_dtype=BF16)
        gbig['mem_w_o'][i] = _mm(dhb, sv['omem'], 'tn', f"mem_out_bwd_w_{i}", m=d, n=mw, k=s_len)
        dqm, dkvm = _mem_attn_bwd(sv['qm'], sv['kvm'], do, f"mem_attn_bwd_{i}")
        gbig['mem_w_q'][i] = _mm(sv['u2'], dqm, 'tn', f"mem_q_bwd_w_{i}", m=d, n=mw, k=s_len)
        du = _mm(dqm, full['mem_w_q'][i], 'nt', f"mem_q_bwd_x_{i}", m=s_len, n=d, k=mw)
        gbig['mem_w_kv'][i] = _mm(mem_n, dkvm, 'tn', f"mem_kv_bwd_w_{i}", m=d, n=2 * mw, k=ml)
        dmem_n = _mm(dkvm, full['mem_w_kv'][i], 'nt', f"mem_kv_bwd_x_{i}", m=ml, n=d, k=2 * mw, res=dmem_n)
        dh, dhb, gsm['norm_mem'][i] = _rmsnorm_bwd(sv['h_mem'], w['norm_mem'][i], du, dh, f"norm_mem_bwd_{i}")
        if len(ffn_items(i)) > 2:
            gl_mem = rs_grads(ffn_items(i)[2:])
            gl_ffn, theirs_ffn = gl_ffn + gl_mem, theirs_ffn + list(_swap_halves(gl_mem, "grad_swap_halves_mem"))
        pend_ffn = rs_pairs(ffn_items(i), gl_ffn, theirs_ffn, f"ffn_{i}")
        ffn_scatter = _scatter_carry(pend_ffn[1], pend_ffn[2])
        if i >= n_a:
            j = i - n_a
            dom = _mm(dhb, full['dil_w_o'][j], 'nt', f"dil_out_bwd_x_{i}", m=s_len, n=dw, k=d)
            gbig['dil_w_o'][j] = _mm(sv['om'], dhb, 'tn', f"dil_out_bwd_w_{i}", m=dw, n=d, k=s_len)
            mb = _mix_bwd(sv['os'], sv['ls'], dom, f"dil_mix_bwd_{i}", dh=hd)
            dqs = []
            for g in range(len(DIL_DILATIONS)):
                if g == 0:
                    (dq3, dk3, dv3), got = _dil_attn_bwd(sv['qgs'][g], kgs[g], vgs[g], mb[g], sv['ls'][g], mb[3 + g],
                                                         f"dil_attn_bwd_{i}_{g}", dh=hd, carry=ffn_scatter)
                    rs_finish(pend_ffn, got)
                else:
                    dq3, dk3, dv3 = _dil_attn_bwd(sv['qgs'][g], kgs[g], vgs[g], mb[g], sv['ls'][g], mb[3 + g],
                                                  f"dil_attn_bwd_{i}_{g}", dh=hd)
                dqs.append([dq3])
                dk_parts[g].append(dk3)
                dv_parts[g].append(dv3)
            dq_raw = _from_residue(dqs, cos2, sin2, f"rope_q_bwd_{i}", dw=dw, rope_cols=3)
            gbig['dil_w_q'][j] = _mm(dq_raw, sv['u'], 'tn', f"dil_q_bwd_w_{i}", m=w3, n=d, k=s_len)
            if j > 0:
                gl_mix = rs_grads(mixer_items(i))
                du, theirs_mix = _mm(dq_raw, full['dil_w_q'][j], 'nn', f"dil_q_bwd_x_{i}", m=s_len, n=d, k=w3,
                                     carry=_swap_carry(gl_mix))
            else:
                du = _mm(dq_raw, full['dil_w_q'][j], 'nn', f"dil_q_bwd_x_{i}", m=s_len, n=d, k=w3)
            dh, dhb, gsm['norm_mix'][i] = _rmsnorm_bwd(sv['h_mix'], w['norm_mix'][i], du, dh, f"norm_mix_bwd_{i}")
            if j == 0:
                dkv = _from_residue(dk_parts + dv_parts, cos2, sin2, "rope_kv_bwd", dw=dw, rope_cols=3)
                gbig['w_kv_shared'][0] = _mm(dkv, kv_saved['kvn'], 'tn', "kv_proj_bwd_w", m=2 * w3, n=d, k=s_len)
                gl_mix = rs_grads(mixer_items(i))
                du, theirs_mix = _mm(dkv, kv_saved['wkv'], 'nn', "kv_proj_bwd_x", m=s_len, n=d, k=2 * w3,
                                     carry=_swap_carry(gl_mix))
                dh, dhb, dg_kv = _rmsnorm_bwd(kv_saved['h'], w['kv_norm'], du, dh, "kv_norm_bwd")
        else:
            dyn = _mm(dhb, full['ssd_w_out'][i], 'nt', f"ssd_out_bwd_x_{i}", m=s_len, n=di, k=d)
            gbig['ssd_w_out'][i] = _mm(sv['yn'], dhb, 'tn', f"ssd_out_bwd_w_{i}", m=di, n=d, k=s_len)
            dy, dz, gsm['ssd_norm'][i] = _gated_norm_bwd(sv['y'], sv['z'], ssd_nw[i], dyn, f"ssd_norm_bwd_{i}")
            (dx, db_, dc_, ddtc, ddtr, dac, dar, ddl), got = _ssd_bwd(
                sv['xbc'], sv['dtc'], sv['dtr'], sv['a_row'], sv['a_col'], sv['dskip'], sv['hins'], dy,
                f"ssd_scan_bwd_{i}", di=di, p=p, carry=ffn_scatter)
            rs_finish(pend_ffn, got)
            dxbc = jnp.concatenate([dx, db_, dc_], axis=1)
            ddt = ddtc.transpose(1, 0, 2).reshape(s_len, nh) + ddtr.transpose(2, 0, 1).reshape(s_len, nh)
            gsm['ssd_a_log'][i] = (dac.reshape(nh) + dar.reshape(nh)) * a_neg[i]
            gsm['ssd_d'][i] = ddl.reshape(nh, p).sum(axis=1)
            ddt_raw, dbias = _softplus_bwd(sv['dt_raw'], w['ssd_dt_bias'][i], ddt, f"ssd_dt_bwd_{i}")
            gsm['ssd_dt_bias'][i] = dbias.reshape(nh)
            dpre, dcw, dcb = _conv_bwd_pre(sv['xbc_raw'], conv_w[i], conv_b[i], dxbc, f"ssd_conv_bwd_pre_{i}")
            gsm['ssd_conv_w'][i], gsm['ssd_conv_b'][i] = dcw, dcb.reshape(cd)
            dxbc_raw = _conv_bwd_in(dpre, conv_w[i], f"ssd_conv_bwd_in_{i}")
            gbig['ssd_w_in'][i] = jnp.concatenate([
                _mm(dz, sv['u'], 'tn', f"ssd_in_bwd_w_z_{i}", m=di, n=d, k=s_len),
                _mm(dxbc_raw, sv['u'], 'tn', f"ssd_in_bwd_w_xbc_{i}", m=cd, n=d, k=s_len),
                _mm(ddt_raw, sv['u'], 'tn', f"ssd_in_bwd_w_dt_{i}", m=nh, n=d, k=s_len)], axis=0)
            gl_mix = rs_grads(mixer_items(i))
            du, theirs_mix = _mm(dz, sv['wt'], 'nn', f"ssd_in_bwd_x_z_{i}", m=s_len, n=d, k=di, carry=_swap_carry(gl_mix))
            du = _mm(dxbc_raw, sv['wt'], 'nn', f"ssd_in_bwd_x_xbc_{i}", m=s_len, n=d, k=cd, b_koff=di, res=du)
            du = _mm(ddt_raw, sv['wt_dt'], 'nn', f"ssd_in_bwd_x_dt_{i}", m=s_len, n=d, k=nh, res=du)
            dh, dhb, gsm['norm_mix'][i] = _rmsnorm_bwd(sv['h_mix'], w['norm_mix'][i], du, dh, f"norm_mix_bwd_{i}")
        pend_mixer = rs_pairs(mixer_items(i), gl_mix, theirs_mix, f"mixer_{i}")
    rs_finish(pend_mixer, _scatter_partials(pend_mixer[1], pend_mixer[2], "grad_scatter_last"))
    grad_x = dh[None]
    _, dg_src = _rmsnorm_bwd_noacc(mem[0], w['mem_src_norm'], dmem_n, "mem_src_norm_bwd")

    grads_big = {}
    for n_ in big:
        t = gshard[(n_, None)] if n_ in stacked else jnp.concatenate([gshard[(n_, l)] for l in range(len(full[n_]))])
        if n_ in COL_SHARDED:
            t = jnp.swapaxes(t, 1, 2)
        grads_big[n_] = t.reshape(w[n_].shape)

    sm_names = ['norm_mix', 'norm_mem', 'norm_ffn', 'ssd_conv_w', 'ssd_conv_b', 'ssd_norm', 'ssd_dt_bias', 'ssd_a_log',
                'ssd_d']
    sm_arrs = [jnp.stack([t.reshape(t.shape[-1]) if n_.startswith('norm') else t for t in gsm[n_]]) for n_ in sm_names]
    sm_names += ['norm_final', 'kv_norm', 'mem_src_norm', 'loss']
    sm_arrs += [dg_final.reshape(d), dg_kv.reshape(d), dg_src.reshape(d), loss_part[0, :1]]
    summed = _unpack(_allreduce_small(_pack(sm_arrs, d), "reduce_small"), [t.shape for t in sm_arrs], d)
    gs = dict(zip(sm_names, summed))
    loss = gs.pop('loss').reshape(())
    grads = dict(grads_big)
    for n_, t in gs.items():
        if n_ in small_sharded:
            wd = w[n_].shape[-1]
            t = lax.dynamic_slice_in_dim(t, chip * wd, wd, axis=t.ndim - 1)
        grads[n_] = t.reshape(w[n_].shape)

    deltas, new_m, new_v = [], [], []
    for n_, m_, v_ in zip(WEIGHTS, mom, vel):
        dlt, nm, nv = _adamw(w[n_], grads[n_], m_, v_, f"adamw_{n_}")
        deltas.append(dlt)
        new_m.append(nm)
        new_v.append(nv)
    return (loss, grad_x, *[grads[n_] for n_ in WEIGHTS], *deltas, *new_m, *new_v)


def kernel(x, mem, positions, norm_mix, norm_mem, norm_ffn, norm_final, ssd_w_in, ssd_conv_w, ssd_conv_b, ssd_dt_bias, ssd_a_log, ssd_d, ssd_norm, ssd_w_out, kv_norm, w_kv_shared, dil_w_q, dil_w_o, mem_src_norm, mem_w_q, mem_w_kv, mem_w_o, ffn_w_in, ffn_w_out, loss_target, m_norm_mix, m_norm_mem, m_norm_ffn, m_norm_final, m_ssd_w_in, m_ssd_conv_w, m_ssd_conv_b, m_ssd_dt_bias, m_ssd_a_log, m_ssd_d, m_ssd_norm, m_ssd_w_out, m_kv_norm, m_w_kv_shared, m_dil_w_q, m_dil_w_o, m_mem_src_norm, m_mem_w_q, m_mem_w_kv, m_mem_w_o, m_ffn_w_in, m_ffn_w_out, v_norm_mix, v_norm_mem, v_norm_ffn, v_norm_final, v_ssd_w_in, v_ssd_conv_w, v_ssd_conv_b, v_ssd_dt_bias, v_ssd_a_log, v_ssd_d, v_ssd_norm, v_ssd_w_out, v_kv_norm, v_w_kv_shared, v_dil_w_q, v_dil_w_o, v_mem_src_norm, v_mem_w_q, v_mem_w_kv, v_mem_w_o, v_ffn_w_in, v_ffn_w_out):
    wts = (norm_mix, norm_mem, norm_ffn, norm_final, ssd_w_in, ssd_conv_w, ssd_conv_b, ssd_dt_bias, ssd_a_log, ssd_d, ssd_norm, ssd_w_out, kv_norm, w_kv_shared, dil_w_q, dil_w_o, mem_src_norm, mem_w_q, mem_w_kv, mem_w_o, ffn_w_in, ffn_w_out)
    mom = (m_norm_mix, m_norm_mem, m_norm_ffn, m_norm_final, m_ssd_w_in, m_ssd_conv_w, m_ssd_conv_b, m_ssd_dt_bias, m_ssd_a_log, m_ssd_d, m_ssd_norm, m_ssd_w_out, m_kv_norm, m_w_kv_shared, m_dil_w_q, m_dil_w_o, m_mem_src_norm, m_mem_w_q, m_mem_w_kv, m_mem_w_o, m_ffn_w_in, m_ffn_w_out)
    vel = (v_norm_mix, v_norm_mem, v_norm_ffn, v_norm_final, v_ssd_w_in, v_ssd_conv_w, v_ssd_conv_b, v_ssd_dt_bias, v_ssd_a_log, v_ssd_d, v_ssd_norm, v_ssd_w_out, v_kv_norm, v_w_kv_shared, v_dil_w_q, v_dil_w_o, v_mem_src_norm, v_mem_w_q, v_mem_w_kv, v_mem_w_o, v_ffn_w_in, v_ffn_w_out)
    return _step(x, mem, positions, wts, loss_target, mom, vel)
```

```python
import functools
import math

import jax
import jax.numpy as jnp
from jax import lax
from jax.experimental import pallas as pl
from jax.experimental.pallas import tpu as pltpu

F32 = jnp.float32
BF16 = jnp.bfloat16
MESH = pl.DeviceIdType.MESH
HIGHEST = lax.Precision.HIGHEST

NORM_EPS = 1e-6
SSD_GROUPS = 8
SSD_STATE = 128
SSD_CHUNK = 128
SSD_CONV = 4
DIL_DILATIONS = (1, 4, 16)
DIL_HEADS = 16
DIL_BLOCK = 128
ATTN_ROWS = 32
ROPE_THETA = 10000.0
MEM_HEADS = 4
ADAM_LR, ADAM_B1, ADAM_B2, ADAM_EPS, ADAM_WD, ADAM_STEP = 0.001, 0.9, 0.999, 1e-08, 0.01, 10

V7X_VMEM_LIMIT = 48 * 1024 * 1024
LANES = 128
SUBLANES_BF16 = 16

WEIGHTS = ['norm_mix', 'norm_mem', 'norm_ffn', 'norm_final', 'ssd_w_in', 'ssd_conv_w', 'ssd_conv_b',
           'ssd_dt_bias', 'ssd_a_log', 'ssd_d', 'ssd_norm', 'ssd_w_out', 'kv_norm', 'w_kv_shared', 'dil_w_q',
           'dil_w_o', 'mem_src_norm', 'mem_w_q', 'mem_w_kv', 'mem_w_o', 'ffn_w_in', 'ffn_w_out']
COL_SHARDED = ('ssd_w_in', 'w_kv_shared', 'dil_w_q', 'mem_w_o', 'ffn_w_in')
ROW_SHARDED = ('ssd_w_out', 'dil_w_o', 'mem_w_q', 'mem_w_kv', 'ffn_w_out')


def _dot(a, b, ca, cb, prec=None):
    return lax.dot_general(a, b, (((ca,), (cb,)), ((), ())), preferred_element_type=F32, precision=prec)


def _nn(a, b, prec=None):
    return _dot(a, b, 1, 0, prec)


def _nt(a, b, prec=None):
    return _dot(a, b, 1, 1, prec)


def _tn(a, b, prec=None):
    return _dot(a, b, 0, 0, prec)


def _tile(n, pref, unit=LANES):
    if n <= pref:
        return n
    t = (pref // unit) * unit
    while t >= unit:
        if n % t == 0:
            return t
        t -= unit
    return n


class _Carry:
    def __init__(self, bufs, outs, n_sems, start, finish):
        self.bufs, self.outs, self.n_sems, self.start, self.finish = list(bufs), list(outs), n_sems, start, finish


def _call(body, name, grid, in_specs, out_specs, out_shape, scratch=(), sem=None, carry=None):
    if carry is None:
        return pl.pallas_call(
            body, name=name, grid=grid, in_specs=in_specs, out_specs=out_specs, out_shape=out_shape,
            scratch_shapes=list(scratch),
            compiler_params=pltpu.CompilerParams(dimension_semantics=sem, vmem_limit_bytes=V7X_VMEM_LIMIT))
    single = not isinstance(out_specs, (list, tuple))
    o_specs = [out_specs] if single else list(out_specs)
    o_shape = [out_shape] if single else list(out_shape)
    n_in, n_out, n_cb, n_co, n_scr = len(in_specs), len(o_specs), len(carry.bufs), len(carry.outs), len(scratch)
    anyspec = pl.BlockSpec(memory_space=pl.ANY)

    def riding(*refs):
        ins, cbufs = refs[:n_in], list(refs[n_in:n_in + n_cb])
        outs = refs[n_in + n_cb:n_in + n_cb + n_out]
        for k, b in enumerate(carry.outs):
            cbufs[b] = refs[n_in + n_cb + n_out + k]
        rest = refs[n_in + n_cb + n_out + n_co:]
        send, recv = rest[n_scr], rest[n_scr + 1]
        ids = [pl.program_id(a) for a in range(len(grid))]
        first = functools.reduce(lambda p, q: p & q, [i == 0 for i in ids])
        last = functools.reduce(lambda p, q: p & q, [i == g - 1 for i, g in zip(ids, grid)])

        @pl.when(first)
        def _():
            carry.start(cbufs, send, recv)

        body(*ins, *outs, *rest[:n_scr])

        @pl.when(last)
        def _():
            carry.finish(cbufs, send, recv)

    call = pl.pallas_call(
        riding, name=name, grid=grid, in_specs=list(in_specs) + [anyspec] * n_cb, out_specs=o_specs + [anyspec] * n_co,
        out_shape=o_shape + [_sds(carry.bufs[b].shape, carry.bufs[b].dtype) for b in carry.outs],
        scratch_shapes=list(scratch) + [pltpu.SemaphoreType.DMA((carry.n_sems,)), pltpu.SemaphoreType.DMA((carry.n_sems,))],
        input_output_aliases={n_in + b: n_out + k for k, b in enumerate(carry.outs)},
        compiler_params=pltpu.CompilerParams(dimension_semantics=("arbitrary",) * len(grid), has_side_effects=True,
                                             vmem_limit_bytes=V7X_VMEM_LIMIT))

    def run(*operands):
        res = call(*operands, *carry.bufs)
        own = res[0] if single else list(res[:n_out])
        return own, list(res[n_out:])

    return run


def _sds(shape, dtype):
    return jax.ShapeDtypeStruct(tuple(shape), dtype)


def _silu(x):
    return x * jax.nn.sigmoid(x)


def _dsilu(x):
    s = jax.nn.sigmoid(x)
    return s * (1.0 + x * (1.0 - s))


def _mm(a, b, mode, name, *, m, n, k, out_dtype=F32, res=None, b_noff=0, b_koff=0, carry=None):
    has_res = res is not None
    m_unit = LANES if mode == 'tn' else SUBLANES_BF16
    n_lim = math.gcd(n, b_noff) if b_noff else n
    k_lim = math.gcd(k, b_koff) if b_koff else k
    wide = out_dtype == F32
    if k <= 2048:
        tk = k
        tm = _tile(m, 2048, m_unit)
        tn = _tile(n_lim, 512 if (wide or has_res) else 1024)
    else:
        tk = _tile(k_lim, 1024)
        tm = _tile(m, 1024, m_unit)
        tn = _tile(n_lim, 2048 if (wide and not has_res) else 1024)
    nk = k // tk
    use_acc = nk > 1 and not wide
    jo, ko = b_noff // tn, b_koff // tk
    if mode == 'nn':
        a_spec = pl.BlockSpec((tm, tk), lambda i, j, kk: (i, kk))
        b_spec = pl.BlockSpec((tk, tn), lambda i, j, kk: (kk + ko, j + jo))
        ca, cb = 1, 0
    elif mode == 'nt':
        a_spec = pl.BlockSpec((tm, tk), lambda i, j, kk: (i, kk))
        b_spec = pl.BlockSpec((tn, tk), lambda i, j, kk: (j + jo, kk + ko))
        ca, cb = 1, 1
    else:
        a_spec = pl.BlockSpec((tk, tm), lambda i, j, kk: (kk, i))
        b_spec = pl.BlockSpec((tk, tn), lambda i, j, kk: (kk + ko, j + jo))
        ca, cb = 0, 0
    o_spec = pl.BlockSpec((tm, tn), lambda i, j, kk: (i, j))

    def body(*refs):
        a_ref, b_ref = refs[0], refs[1]
        r_ref = refs[2] if has_res else None
        o_ref = refs[3] if has_res else refs[2]
        acc = refs[-1] if use_acc else None

        def prod():
            return _dot(a_ref[...].astype(BF16), b_ref[...].astype(BF16), ca, cb)

        def with_res(v):
            return v + r_ref[...].astype(F32) if has_res else v

        if nk == 1:
            o_ref[...] = with_res(prod()).astype(o_ref.dtype)
            return
        kk = pl.program_id(2)
        if use_acc:
            @pl.when(kk == 0)
            def _():
                acc[...] = prod()

            @pl.when((kk > 0) & (kk < nk - 1))
            def _():
                acc[...] += prod()

            @pl.when(kk == nk - 1)
            def _():
                o_ref[...] = with_res(acc[...] + prod()).astype(o_ref.dtype)
        else:
            @pl.when(kk == 0)
            def _():
                o_ref[...] = with_res(prod())

            @pl.when(kk > 0)
            def _():
                o_ref[...] += prod()

    ins = [a, b] + ([res] if has_res else [])
    specs = [a_spec, b_spec] + ([o_spec] if has_res else [])
    return _call(body, name, (m // tm, n // tn, nk), specs, o_spec, _sds((m, n), out_dtype),
                 scratch=[pltpu.VMEM((tm, tn), F32)] if use_acc else [], sem=("parallel", "parallel", "arbitrary"),
                 carry=carry)(*ins)


def _rowwise(fn, name, rows, consts, outs, tb, n_rows):
    n_r, n_c = len(rows), len(consts)
    in_specs = [pl.BlockSpec((tb, w), functools.partial(lambda i, cb: (i, cb), cb=cb)) for _, w, cb in rows]
    in_specs += [pl.BlockSpec(c.shape, functools.partial(lambda i, nd: (0,) * nd, nd=c.ndim)) for c in consts]
    out_specs, out_shape = [], []
    for kind, w, dt in outs:
        if kind == 'row':
            out_specs.append(pl.BlockSpec((tb, w), lambda i: (i, 0)))
            out_shape.append(_sds((n_rows, w), dt))
        else:
            out_specs.append(pl.BlockSpec(w, lambda i: (0, 0)))
            out_shape.append(_sds(w, dt))

    def body(*refs):
        ins = [r[...] for r in refs[:n_r + n_c]]
        orefs = refs[n_r + n_c:]
        vals = fn(*ins)
        i = pl.program_id(0)
        for (kind, _, _), o_ref, v in zip(outs, orefs, vals):
            if kind == 'row':
                o_ref[...] = v.astype(o_ref.dtype)
            else:
                @pl.when(i == 0)
                def _(o_ref=o_ref):
                    o_ref[...] = jnp.zeros_like(o_ref)

                o_ref[...] += v.astype(o_ref.dtype)

    res = _call(body, name, (n_rows // tb,), in_specs, out_specs, out_shape, sem=("arbitrary",))(
        *[r[0] for r in rows], *consts)
    return res


def _rms_fwd_fn(h, g):
    r = lax.rsqrt(jnp.mean(h * h, axis=-1, keepdims=True) + NORM_EPS)
    return (h * r * g,)


def _rms_bwd_vals(h, g, dy):
    r = lax.rsqrt(jnp.mean(h * h, axis=-1, keepdims=True) + NORM_EPS)
    t = dy * g
    dh = r * t - h * (r * r * r) * jnp.mean(h * t, axis=-1, keepdims=True)
    dg = jnp.sum(dy * h * r, axis=0, keepdims=True)
    return dh, dg


def _rmsnorm(h, g, name):
    n, d = h.shape
    return _rowwise(_rms_fwd_fn, name, [(h, d, 0)], [g.reshape(1, d)], [('row', d, BF16)], _tile(n, 512, 8), n)[0]


def _rmsnorm_bwd(h, g, du, dres, name):
    n, d = h.shape

    def fn(hv, duv, drv, gv):
        dh, dg = _rms_bwd_vals(hv, gv, duv.astype(F32))
        return dh + drv, dh + drv, dg

    return _rowwise(fn, name, [(h, d, 0), (du, d, 0), (dres, d, 0)], [g.reshape(1, d)],
                    [('row', d, F32), ('row', d, BF16), ('acc', (1, d), F32)], _tile(n, 256, 8), n)


def _rmsnorm_bwd_noacc(h, g, du, name):
    n, d = h.shape

    def fn(hv, duv, gv):
        return _rms_bwd_vals(hv, gv, duv.astype(F32))

    return _rowwise(fn, name, [(h, d, 0), (du, d, 0)], [g.reshape(1, d)],
                    [('row', d, F32), ('acc', (1, d), F32)], _tile(n, 256, 8), n)


def _swiglu_fwd(gu, name):
    n, f2 = gu.shape
    f = f2 // 2

    def fn(v):
        v = v.astype(F32)
        return (_silu(v[:, :f]) * v[:, f:],)

    return _rowwise(fn, name, [(gu, f2, 0)], [], [('row', f, BF16)], _tile(n, 256, 16), n)[0]


def _swiglu_bwd(gu, dact, name):
    n, f2 = gu.shape
    f = f2 // 2

    def fn(v, da):
        v = v.astype(F32)
        g, up = v[:, :f], v[:, f:]
        da = da.astype(F32)
        return (jnp.concatenate([da * up * _dsilu(g), da * _silu(g)], axis=1),)

    return _rowwise(fn, name, [(gu, f2, 0), (dact, f, 0)], [], [('row', f2, BF16)], _tile(n, 256, 16), n)[0]


def _group_sum(v, ng):
    w = v.shape[1] // ng
    return [jnp.sum(v[:, i * w:(i + 1) * w], axis=1, keepdims=True) for i in range(ng)]


def _gated_norm_fwd(y, z, nw, name):
    n, di = y.shape
    gw = di // SSD_GROUPS

    def fn(yv, zv, nwv):
        a = yv * _silu(zv)
        ms = _group_sum(a * a, SSD_GROUPS)
        out = jnp.concatenate([a[:, i * gw:(i + 1) * gw] * lax.rsqrt(ms[i] / gw + NORM_EPS)
                               for i in range(SSD_GROUPS)], axis=1)
        return (out * nwv,)

    return _rowwise(fn, name, [(y, di, 0), (z, di, 0)], [nw.reshape(1, di)], [('row', di, BF16)], _tile(n, 256, 8), n)[0]


def _gated_norm_bwd(y, z, nw, dout, name):
    n, di = y.shape
    gw = di // SSD_GROUPS

    def fn(yv, zv, dov, nwv):
        sz = _silu(zv)
        a = yv * sz
        t = dov * nwv
        ms = _group_sum(a * a, SSD_GROUPS)
        at = _group_sum(a * t, SSD_GROUPS)
        das, ars = [], []
        for i in range(SSD_GROUPS):
            r = lax.rsqrt(ms[i] / gw + NORM_EPS)
            sl = slice(i * gw, (i + 1) * gw)
            das.append(r * t[:, sl] - a[:, sl] * (r * r * r) * (at[i] / gw))
            ars.append(a[:, sl] * r)
        da = jnp.concatenate(das, axis=1)
        ar = jnp.concatenate(ars, axis=1)
        return da * sz, da * yv * _dsilu(zv), jnp.sum(dov * ar, axis=0, keepdims=True)

    return _rowwise(fn, name, [(y, di, 0), (z, di, 0), (dout, di, 0)], [nw.reshape(1, di)],
                    [('row', di, F32), ('row', di, BF16), ('acc', (1, di), F32)], _tile(n, 128, 8), n)


def _softplus_fwd(raw, bias, name):
    n, h = raw.shape

    def fn(v, b):
        t = v + b
        return (jnp.maximum(t, 0.0) + jnp.log(1.0 + jnp.exp(-jnp.abs(t))),)

    return _rowwise(fn, name, [(raw, h, 0)], [bias.reshape(1, h)], [('row', h, F32)], _tile(n, 1024, 8), n)[0]


def _softplus_bwd(raw, bias, ddt, name):
    n, h = raw.shape

    def fn(v, d, b):
        g = d * jax.nn.sigmoid(v + b)
        return g, jnp.sum(g, axis=0, keepdims=True)

    return _rowwise(fn, name, [(raw, h, 0), (ddt, h, 0)], [bias.reshape(1, h)],
                    [('row', h, BF16), ('acc', (1, h), F32)], _tile(n, 1024, 8), n)


def _loss_and_grad(h, g, tgt, name):
    n, d = h.shape

    def fn(hv, tv, gv):
        y = _rms_fwd_fn(hv, gv)[0]
        err = y - tv
        part = 0.5 * jnp.sum(jnp.sum(err * err, axis=1, keepdims=True), axis=0, keepdims=True) / d
        dh, dg = _rms_bwd_vals(hv, gv, err / d)
        return jnp.broadcast_to(part, (8, LANES)), dh, dh, dg

    return _rowwise(fn, name, [(h, d, 0), (tgt, d, 0)], [g.reshape(1, d)],
                    [('acc', (8, LANES), F32), ('row', d, F32), ('row', d, BF16), ('acc', (1, d), F32)],
                    _tile(n, 256, 8), n)


def _rope_tables(pos_col, dh, name):
    n = pos_col.shape[0]
    half = dh // 2
    inv = ROPE_THETA ** (-jnp.arange(half, dtype=F32) / half)
    inv2 = jnp.concatenate([inv, inv]).reshape(1, dh)
    sign = jnp.concatenate([-jnp.ones((half,), F32), jnp.ones((half,), F32)]).reshape(1, dh)

    def fn(p, iv, sg):
        ang = p.astype(F32) * iv
        return jnp.cos(ang), jnp.sin(ang) * sg

    return _rowwise(fn, name, [(pos_col, 1, 0)], [inv2, sign], [('row', dh, F32), ('row', dh, F32)], _tile(n, 1024, 8), n)


def _conv_taps(ext, w, tb):
    shifted = [ext[8:] if k == SSD_CONV - 1 else pltpu.roll(ext, SSD_CONV - 1 - k, 0)[8:] for k in range(SSD_CONV)]
    pre = shifted[0] * w[0:1]
    for k in range(1, SSD_CONV):
        pre = pre + shifted[k] * w[k:k + 1]
    return pre, shifted


def _conv_specs(n, c, tb, tc):
    hb = tb // 8
    blk = pl.BlockSpec((tb, tc), lambda j, i: (i, j))
    halo = pl.BlockSpec((8, tc), lambda j, i: (jnp.maximum(i * hb - 1, 0), j))
    wsp = pl.BlockSpec((SSD_CONV, tc), lambda j, i: (0, j))
    bsp = pl.BlockSpec((1, tc), lambda j, i: (0, j))
    return blk, halo, wsp, bsp


def _conv_fwd(u, w, b, name):
    n, c = u.shape
    tb, tc = _tile(n, 512, 8), _tile(c, 1536)
    blk, halo, wsp, bsp = _conv_specs(n, c, tb, tc)

    def body(u_ref, h_ref, w_ref, b_ref, o_ref):
        halo_v = jnp.where(pl.program_id(1) > 0, h_ref[...], 0.0)
        pre, _ = _conv_taps(jnp.concatenate([halo_v, u_ref[...]], axis=0), w_ref[...], tb)
        o_ref[...] = _silu(pre + b_ref[...])

    return _call(body, name, (c // tc, n // tb), [blk, halo, wsp, bsp], blk, _sds((n, c), F32),
                 sem=("parallel", "arbitrary"))(u, u, w, b.reshape(1, c))


def _conv_bwd_pre(u, w, b, dout, name):
    n, c = u.shape
    tb, tc = _tile(n, 512, 8), _tile(c, 1536)
    blk, halo, wsp, bsp = _conv_specs(n, c, tb, tc)

    def body(u_ref, h_ref, w_ref, b_ref, d_ref, dp_ref, dw_ref, db_ref):
        i = pl.program_id(1)
        halo_v = jnp.where(i > 0, h_ref[...], 0.0)
        pre, shifted = _conv_taps(jnp.concatenate([halo_v, u_ref[...]], axis=0), w_ref[...], tb)
        dp = d_ref[...] * _dsilu(pre + b_ref[...])
        dp_ref[...] = dp

        @pl.when(i == 0)
        def _():
            dw_ref[...] = jnp.zeros_like(dw_ref)
            db_ref[...] = jnp.zeros_like(db_ref)

        dw_ref[...] += jnp.concatenate([jnp.sum(dp * s, axis=0, keepdims=True) for s in shifted], axis=0)
        db_ref[...] += jnp.sum(dp, axis=0, keepdims=True)

    return _call(body, name, (c // tc, n // tb), [blk, halo, wsp, bsp, blk], [blk, wsp, bsp],
                 [_sds((n, c), F32), _sds((SSD_CONV, c), F32), _sds((1, c), F32)],
                 sem=("parallel", "arbitrary"))(u, u, w, b.reshape(1, c), dout)


def _conv_bwd_in(dpre, w, name):
    n, c = dpre.shape
    tb, tc = _tile(n, 512, 8), _tile(c, 1536)
    hb = tb // 8
    nb = n // tb
    blk = pl.BlockSpec((tb, tc), lambda j, i: (i, j))
    nxt = pl.BlockSpec((8, tc), lambda j, i: (jnp.minimum((i + 1) * hb, n // 8 - 1), j))
    wsp = pl.BlockSpec((SSD_CONV, tc), lambda j, i: (0, j))

    def body(d_ref, n_ref, w_ref, o_ref):
        nxt_v = jnp.where(pl.program_id(1) < nb - 1, n_ref[...], 0.0)
        ext = jnp.concatenate([d_ref[...], nxt_v], axis=0)
        wv = w_ref[...]
        acc = ext[:tb] * wv[SSD_CONV - 1:SSD_CONV]
        for k in range(SSD_CONV - 1):
            s = SSD_CONV - 1 - k
            acc = acc + pltpu.roll(ext, tb + 8 - s, 0)[:tb] * wv[k:k + 1]
        o_ref[...] = acc.astype(o_ref.dtype)

    return _call(body, name, (c // tc, nb), [blk, nxt, wsp], blk, _sds((n, c), BF16),
                 sem=("parallel", "arbitrary"))(dpre, dpre, w)


def _col(v, j):
    lane = lax.broadcasted_iota(jnp.int32, v.shape, 1)
    return jnp.sum(jnp.where(lane == j, v, 0.0), axis=1, keepdims=True)


def _row(v, j):
    sub = lax.broadcasted_iota(jnp.int32, v.shape, 0)
    return jnp.sum(jnp.where(sub == j, v, 0.0), axis=0, keepdims=True)


def _tril_mask(ll):
    return lax.broadcasted_iota(jnp.int32, (ll, ll), 0) >= lax.broadcasted_iota(jnp.int32, (ll, ll), 1)


def _ssd_cumsum(dt, a_neg, name):
    n, nh = dt.shape

    def fn(dtv, a):
        return (_nn(_tril_mask(SSD_CHUNK).astype(F32), dtv * a, HIGHEST),)

    return _rowwise(fn, name, [(dt, nh, 0)], [a_neg.reshape(1, nh)], [('row', nh, F32)], SSD_CHUNK, n)[0]


def _ssd_specs(n, di, gn, jh, p, nc, rev):
    ll = SSD_CHUNK
    jp = jh * p

    def ci(c):
        return (nc - 1 - c) if rev else c

    xs = pl.BlockSpec((ll, jp), lambda g, c: (ci(c), g))
    bs = pl.BlockSpec((ll, SSD_STATE), lambda g, c: (ci(c), di // SSD_STATE + g))
    cs = pl.BlockSpec((ll, SSD_STATE), lambda g, c: (ci(c), (di + gn * SSD_STATE) // SSD_STATE + g))
    dtc = pl.BlockSpec((None, ll, jh), lambda g, c: (g, ci(c), 0))
    dtr = pl.BlockSpec((None, jh, ll), lambda g, c: (g, 0, ci(c)))
    arow = pl.BlockSpec((None, 1, jh), lambda g, c: (g, 0, 0))
    acol = pl.BlockSpec((None, jh, 1), lambda g, c: (g, 0, 0))
    dsk = pl.BlockSpec((None, 1, jp), lambda g, c: (g, 0, 0))
    hin = pl.BlockSpec((None, None, SSD_STATE, jp), lambda g, c: (g, ci(c), 0, 0))
    ys = pl.BlockSpec((ll, jp), lambda g, c: (ci(c), g))
    return xs, bs, cs, dtc, dtr, arow, acol, dsk, hin, ys


def _ssd_fwd(xbc, dtc, dtr, cumc, cumr, a_row, a_col, dskip, name, *, di, p, carry=None):
    n = xbc.shape[0]
    gn = SSD_GROUPS
    jh = dtc.shape[2]
    jp = jh * p
    ll = SSD_CHUNK
    nc = n // ll
    xs, bs, cs, dtcs, dtrs, arow, acol, dsk, hin, ys = _ssd_specs(n, di, gn, jh, p, nc, False)
    pair = 2 * p

    def body(x_ref, b_ref, c_ref, dtc_ref, dtr_ref, cumc_ref, cumr_ref, ar_ref, ac_ref, ds_ref, y_ref, hin_ref, h_scr):
        @pl.when(pl.program_id(1) == 0)
        def _():
            h_scr[...] = jnp.zeros_like(h_scr)

        dtcv, dtrv = dtc_ref[...], dtr_ref[...]
        cumc, cumr = cumc_ref[...], cumr_ref[...]
        tril = _tril_mask(ll)
        tot = jnp.sum(dtcv * ar_ref[...], axis=0, keepdims=True)
        bb, cb_ = b_ref[...].astype(BF16), c_ref[...].astype(BF16)
        cbm = _nt(cb_, bb)
        hin_ref[...] = h_scr[...]
        lane = lax.broadcasted_iota(jnp.int32, (ll, pair), 1)
        lane1 = lax.broadcasted_iota(jnp.int32, (1, pair), 1)
        for pr in range(jh // 2):
            sl = slice(pr * pair, (pr + 1) * pair)
            xp = x_ref[:, sl]
            xpb = xp.astype(BF16)
            hp = h_scr[:, sl]
            ydiag = jnp.zeros((ll, pair), F32)
            e_p = jnp.zeros((ll, pair), F32)
            w_p = jnp.zeros((ll, pair), F32)
            cd_p = jnp.zeros((1, pair), F32)
            for q in range(2):
                j = 2 * pr + q
                mj = (lane >= p) if q else (lane < p)
                cc, cr = _col(cumc, j), _row(cumr, j)
                decay = jnp.exp(jnp.where(tril, cc - cr, -1e30))
                mm = cbm * decay * _row(dtrv, j)
                ydiag = ydiag + _nn(mm.astype(BF16), jnp.where(mj, xpb, jnp.zeros_like(xpb)))
                cl = _col(tot, j)
                e_p = jnp.where(mj, jnp.exp(cc), e_p)
                w_p = jnp.where(mj, jnp.exp(cl - cc) * _col(dtcv, j), w_p)
                cd_p = jnp.where((lane1 >= p) if q else (lane1 < p), jnp.exp(cl), cd_p)
            yoff = _nn(cb_, hp.astype(BF16)) * e_p
            y_ref[:, sl] = ydiag + yoff + xp * ds_ref[:, sl]
            h_scr[:, sl] = hp * cd_p + _tn(bb, (xp * w_p).astype(BF16))

    return _call(body, name, (gn, nc), [xs, bs, cs, dtcs, dtrs, dtcs, dtrs, arow, acol, dsk], [ys, hin],
                 [_sds((n, di), F32), _sds((gn, nc, SSD_STATE, jp), F32)],
                 scratch=[pltpu.VMEM((SSD_STATE, jp), F32)], sem=("parallel", "arbitrary"), carry=carry)(
        xbc, xbc, xbc, dtc, dtr, cumc, cumr, a_row, a_col, dskip)


def _ssd_bwd(xbc, dtc, dtr, cumc, cumr, a_row, a_col, dskip, hins, dy, name, *, di, p, carry=None):
    n = xbc.shape[0]
    gn = SSD_GROUPS
    jh = dtc.shape[2]
    jp = jh * p
    ll = SSD_CHUNK
    nc = n // ll
    xs, bs, cs, dtcs, dtrs, arow, acol, dsk, hin, ys = _ssd_specs(n, di, gn, jh, p, nc, True)
    pair = 2 * p
    bc_out = pl.BlockSpec((ll, SSD_STATE), lambda g, c: (nc - 1 - c, g))

    def body(x_ref, b_ref, c_ref, dtc_ref, dtr_ref, cumc_ref, cumr_ref, ar_ref, ac_ref, ds_ref, hin_ref, dy_ref,
             dx_ref, db_ref, dc_ref, ddtc_ref, ddtr_ref, dac_ref, dar_ref, dd_ref, dh_scr):
        first = pl.program_id(1) == 0

        @pl.when(first)
        def _():
            dh_scr[...] = jnp.zeros_like(dh_scr)
            dac_ref[...] = jnp.zeros_like(dac_ref)
            dar_ref[...] = jnp.zeros_like(dar_ref)
            dd_ref[...] = jnp.zeros_like(dd_ref)

        dtcv, dtrv = dtc_ref[...], dtr_ref[...]
        a_r, a_c = ar_ref[...], ac_ref[...]
        cumc, cumr = cumc_ref[...], cumr_ref[...]
        tril = _tril_mask(ll)
        trilf = tril.astype(F32)
        triuf = (lax.broadcasted_iota(jnp.int32, (ll, ll), 0) <= lax.broadcasted_iota(jnp.int32, (ll, ll), 1)).astype(F32)
        tot = jnp.sum(dtcv * a_r, axis=0, keepdims=True)
        bb, cb_ = b_ref[...].astype(BF16), c_ref[...].astype(BF16)
        cbm = _nt(cb_, bb)
        lane = lax.broadcasted_iota(jnp.int32, (ll, pair), 1)
        lane1 = lax.broadcasted_iota(jnp.int32, (1, pair), 1)
        lane_j = lax.broadcasted_iota(jnp.int32, (ll, jh), 1)
        sub_l = lax.broadcasted_iota(jnp.int32, (ll, jh), 0)
        sub_j = lax.broadcasted_iota(jnp.int32, (jh, ll), 0)
        dcb = jnp.zeros((ll, ll), F32)
        db_acc = jnp.zeros((ll, SSD_STATE), F32)
        dc_acc = jnp.zeros((ll, SSD_STATE), F32)
        dcum_c = jnp.zeros((ll, jh), F32)
        dcum_r = jnp.zeros((jh, ll), F32)
        ddt_c = jnp.zeros((ll, jh), F32)
        ddt_r = jnp.zeros((jh, ll), F32)
        for pr in range(jh // 2):
            sl = slice(pr * pair, (pr + 1) * pair)
            xp = x_ref[:, sl]
            xpb = xp.astype(BF16)
            dyp = dy_ref[:, sl]
            hp = hin_ref[:, sl]
            hpb = hp.astype(BF16)
            dhp = dh_scr[:, sl]
            dhpb = dhp.astype(BF16)
            ch = _nn(cb_, hpb)
            gp = _nn(bb, dhpb)
            e_p = jnp.zeros((ll, pair), F32)
            w_p = jnp.zeros((ll, pair), F32)
            cd_p = jnp.zeros((1, pair), F32)
            dxp = dyp * ds_ref[:, sl]
            heads = []
            for q in range(2):
                j = 2 * pr + q
                mj = (lane >= p) if q else (lane < p)
                cc, cr = _col(cumc, j), _row(cumr, j)
                cl = _col(tot, j)
                ej = jnp.exp(cc)
                wdec = jnp.exp(cl - cc)
                wj = wdec * _col(dtcv, j)
                e_p = jnp.where(mj, ej, e_p)
                w_p = jnp.where(mj, wj, w_p)
                cdj = jnp.exp(cl)
                cd_p = jnp.where((lane1 >= p) if q else (lane1 < p), cdj, cd_p)
                heads.append((j, mj, cc, cr, cl, wdec, wj, cdj))
            dye = dyp * e_p
            dyeb = dye.astype(BF16)
            dc_acc = dc_acc + _nt(dyeb, hpb)
            dh_new = dhp * cd_p + _tn(cb_, dyeb)
            dxp = dxp + gp * w_p
            db_acc = db_acc + _nt((xp * w_p).astype(BF16), dhpb)
            t_off = dye * ch
            t_w = gp * xp
            t_cd = jnp.sum(dhp * hp, axis=0, keepdims=True)
            for (j, mj, cc, cr, cl, wdec, wj, cdj) in heads:
                dyj = jnp.where(mj, dyp, 0.0).astype(BF16)
                decay = jnp.exp(jnp.where(tril, cc - cr, -1e30))
                dtrow = _row(dtrv, j)
                mm = cbm * decay * dtrow
                dm = _nt(dyj, xpb)
                dxp = dxp + _tn(mm.astype(BF16), dyj)
                dcb = dcb + dm * decay * dtrow
                ddt_rj = jnp.sum(dm * cbm * decay, axis=0, keepdims=True)
                dseg = dm * mm
                dcum_cj = jnp.sum(dseg, axis=1, keepdims=True) + jnp.sum(jnp.where(mj, t_off, 0.0), axis=1, keepdims=True)
                dcum_rj = -jnp.sum(dseg, axis=0, keepdims=True)
                dwj = jnp.sum(jnp.where(mj, t_w, 0.0), axis=1, keepdims=True)
                ddt_cj = dwj * wdec
                qj = dwj * wj
                dcum_cj = dcum_cj - qj
                m1 = (lane1 >= p) if (j % 2) else (lane1 < p)
                dcl = jnp.sum(jnp.where(m1, t_cd, 0.0), axis=1, keepdims=True) * cdj + jnp.sum(qj, axis=0, keepdims=True)
                dcum_c = dcum_c + jnp.where(lane_j == j, dcum_cj, 0.0) + jnp.where((lane_j == j) & (sub_l == ll - 1), dcl, 0.0)
                dcum_r = dcum_r + jnp.where(sub_j == j, dcum_rj, 0.0)
                ddt_c = ddt_c + jnp.where(lane_j == j, ddt_cj, 0.0)
                ddt_r = ddt_r + jnp.where(sub_j == j, ddt_rj, 0.0)
            dx_ref[:, sl] = dxp
            dd_ref[:, sl] += jnp.sum(dyp * xp, axis=0, keepdims=True)
            dh_scr[:, sl] = dh_new
        dcbb = dcb.astype(BF16)
        dc_ref[...] = dc_acc + _nn(dcbb, bb)
        db_ref[...] = db_acc + _tn(dcbb, cb_)
        dda_c = _nn(triuf, dcum_c, HIGHEST)
        dda_r = _nn(dcum_r, trilf, HIGHEST)
        ddtc_ref[...] = ddt_c + dda_c * a_r
        ddtr_ref[...] = ddt_r + dda_r * a_c
        dac_ref[...] += jnp.sum(dda_c * dtcv, axis=0, keepdims=True)
        dar_ref[...] += jnp.sum(dda_r * dtrv, axis=1, keepdims=True)

    gs = SSD_GROUPS * SSD_STATE
    return _call(body, name, (gn, nc), [xs, bs, cs, dtcs, dtrs, dtcs, dtrs, arow, acol, dsk, hin, ys],
                 [ys, bc_out, bc_out, dtcs, dtrs, arow, acol, dsk],
                 [_sds((n, di), F32), _sds((n, gs), F32), _sds((n, gs), F32), _sds(dtc.shape, F32), _sds(dtr.shape, F32),
                  _sds((gn, 1, jh), F32), _sds((gn, jh, 1), F32), _sds((gn, 1, jp), F32)],
                 scratch=[pltpu.VMEM((SSD_STATE, jp), F32)], sem=("parallel", "arbitrary"), carry=carry)(
        xbc, xbc, xbc, dtc, dtr, cumc, cumr, a_row, a_col, dskip, hins, dy)


def _band_masks():
    r = lax.broadcasted_iota(jnp.int32, (DIL_BLOCK, DIL_BLOCK), 0)
    c = lax.broadcasted_iota(jnp.int32, (DIL_BLOCK, DIL_BLOCK), 1)
    return c >= r, c <= r


def _dil_attn_fwd(q3, k3, v3, name, *, dh, carry=None):
    dil, n, dw = q3.shape
    nb = n // DIL_BLOCK
    scale = dh ** -0.5
    tb = DIL_BLOCK
    cur = pl.BlockSpec((None, tb, dw), lambda r, jb: (r, jb, 0))
    prev = pl.BlockSpec((None, tb, dw), lambda r, jb: (r, jnp.maximum(jb - 1, 0), 0))
    o_spec = cur
    l_spec = pl.BlockSpec((None, tb, LANES), lambda r, jb: (r, jb, 0))

    def body(q_ref, kc_ref, kp_ref, vc_ref, vp_ref, o_ref, l_ref, s_scr, p_scr, d_scr):
        mp, mc = _band_masks()
        mp = mp & (pl.program_id(1) > 0)
        for h in range(DIL_HEADS):
            sl = slice(h * dh, (h + 1) * dh)
            qh = q_ref[:, sl]
            s_scr[h, :, :tb] = jnp.where(mp, _nt(qh, kp_ref[:, sl]) * scale, -jnp.inf)
            s_scr[h, :, tb:] = jnp.where(mc, _nt(qh, kc_ref[:, sl]) * scale, -jnp.inf)
        lane = lax.broadcasted_iota(jnp.int32, (ATTN_ROWS, LANES), 1)
        for h in range(DIL_HEADS):
            for rc in range(tb // ATTN_ROWS):
                rows = slice(rc * ATTN_ROWS, (rc + 1) * ATTN_ROWS)
                s = s_scr[h, rows, :]
                mx = jnp.max(jnp.maximum(s[:, :tb], s[:, tb:]), axis=1, keepdims=True)
                pm = jnp.exp(s - mx)
                den = jnp.sum(pm[:, :tb] + pm[:, tb:], axis=1, keepdims=True)
                p_scr[h, rows, :] = pm.astype(BF16)
                d_scr[h, rows, :] = den
                lse = mx + jnp.log(den)
                l_ref[rows, :] = jnp.where(lane == h, lse, 0.0 if h == 0 else l_ref[rows, :])
        for h in range(DIL_HEADS):
            sl = slice(h * dh, (h + 1) * dh)
            o = _nn(p_scr[h, :, :tb], vp_ref[:, sl]) + _nn(p_scr[h, :, tb:], vc_ref[:, sl])
            o_ref[:, sl] = o / d_scr[h]

    return _call(body, name, (dil, nb), [cur, cur, prev, cur, prev], [o_spec, l_spec],
                 [_sds((dil, n, dw), F32), _sds((dil, n, LANES), F32)],
                 scratch=[pltpu.VMEM((DIL_HEADS, tb, 2 * tb), F32), pltpu.VMEM((DIL_HEADS, tb, 2 * tb), BF16),
                          pltpu.VMEM((DIL_HEADS, tb, 1), F32)],
                 sem=("parallel", "arbitrary"), carry=carry)(q3, k3, k3, v3, v3)


def _dil_attn_bwd(q3, k3, v3, do3, lse3, dl3, name, *, dh, carry=None):
    dil, n, dw = q3.shape
    nb = n // DIL_BLOCK
    scale = dh ** -0.5
    tb = DIL_BLOCK

    def nx(jb):
        return jnp.minimum(jb + 1, nb - 1)

    q_c = pl.BlockSpec((None, tb, dw), lambda r, jb: (r, jb, 0))
    q_n = pl.BlockSpec((None, tb, dw), lambda r, jb: (r, nx(jb), 0))
    k_p = pl.BlockSpec((None, tb, dw), lambda r, jb: (r, jnp.maximum(jb - 1, 0), 0))
    o_c, o_n = q_c, q_n
    l_c = pl.BlockSpec((None, tb, LANES), lambda r, jb: (r, jb, 0))
    l_n = pl.BlockSpec((None, tb, LANES), lambda r, jb: (r, nx(jb), 0))

    def body(qc_ref, qn_ref, kc_ref, kp_ref, vc_ref, vp_ref, doc_ref, don_ref, lc_ref, ln_ref, dlc_ref, dln_ref,
             dq_ref, dk_ref, dv_ref, s_scr, dp_scr, p_scr, ds_scr):
        jb = pl.program_id(1)
        has_prev = jb > 0
        has_next = jb < nb - 1
        for h in range(DIL_HEADS):
            sl = slice(h * dh, (h + 1) * dh)
            qc, qn, kc, kp = qc_ref[:, sl], qn_ref[:, sl], kc_ref[:, sl], kp_ref[:, sl]
            vc, vp, doc, don = vc_ref[:, sl], vp_ref[:, sl], doc_ref[:, sl], don_ref[:, sl]
            for pi, (qh, kh, vh, doh) in enumerate(((qc, kp, vp, doc), (qc, kc, vc, doc), (qn, kc, vc, don))):
                s_scr[h, pi] = _nt(qh, kh) * scale
                dp_scr[h, pi] = _nt(doh, vh)
        r = lax.broadcasted_iota(jnp.int32, (ATTN_ROWS, tb), 0)
        c = lax.broadcasted_iota(jnp.int32, (ATTN_ROWS, tb), 1)
        for h in range(DIL_HEADS):
            for rc in range(tb // ATTN_ROWS):
                rows = slice(rc * ATTN_ROWS, (rc + 1) * ATTN_ROWS)
                m_prev, m_cur = c >= r + rc * ATTN_ROWS, c <= r + rc * ATTN_ROWS
                lch, dlch = _col(lc_ref[rows, :], h), _col(dlc_ref[rows, :], h)
                lnh, dlnh = _col(ln_ref[rows, :], h), _col(dln_ref[rows, :], h)
                for pi, (lse_h, dl_h, mask) in enumerate(((lch, dlch, m_prev & has_prev), (lch, dlch, m_cur),
                                                          (lnh, dlnh, m_prev & has_next))):
                    pm = jnp.where(mask, jnp.exp(s_scr[h, pi, rows, :] - lse_h), 0.0)
                    ds_scr[h, pi, rows, :] = (pm * (dp_scr[h, pi, rows, :] - dl_h) * scale).astype(BF16)
                    if pi > 0:
                        p_scr[h, pi - 1, rows, :] = pm.astype(BF16)
        for h in range(DIL_HEADS):
            sl = slice(h * dh, (h + 1) * dh)
            dq_ref[:, sl] = _nn(ds_scr[h, 0], kp_ref[:, sl]) + _nn(ds_scr[h, 1], kc_ref[:, sl])
            dk_ref[:, sl] = _tn(ds_scr[h, 1], qc_ref[:, sl]) + _tn(ds_scr[h, 2], qn_ref[:, sl])
            dv_ref[:, sl] = _tn(p_scr[h, 0], doc_ref[:, sl]) + _tn(p_scr[h, 1], don_ref[:, sl])

    out = _sds((dil, n, dw), F32)
    return _call(body, name, (dil, nb), [q_c, q_n, q_c, k_p, q_c, k_p, o_c, o_n, l_c, l_n, l_c, l_n],
                 [o_c, o_c, o_c], [out, out, out],
                 scratch=[pltpu.VMEM((DIL_HEADS, 3, tb, tb), F32), pltpu.VMEM((DIL_HEADS, 3, tb, tb), F32),
                          pltpu.VMEM((DIL_HEADS, 2, tb, tb), BF16), pltpu.VMEM((DIL_HEADS, 3, tb, tb), BF16)],
                 sem=("parallel", "arbitrary"), carry=carry)(
        q3, q3, k3, k3, v3, v3, do3, do3, lse3, lse3, dl3, dl3)


def _head_expand(dw, dh):
    r = lax.broadcasted_iota(jnp.int32, (LANES, dw), 0)
    c = lax.broadcasted_iota(jnp.int32, (LANES, dw), 1)
    return ((c // dh) == r).astype(F32)


def _mix_weights(l0, l1, l2):
    mx = jnp.maximum(jnp.maximum(l0, l1), l2)
    e = [jnp.exp(l0 - mx), jnp.exp(l1 - mx), jnp.exp(l2 - mx)]
    den = e[0] + e[1] + e[2]
    return [v / den for v in e]


MIX_ROWS = 256


def _tile_scratch(tb, w):
    return pltpu.VMEM((w // LANES, tb, LANES), F32)


def _read_tiles(scr):
    return jnp.concatenate([scr[ct] for ct in range(scr.shape[0])], axis=1)


def _write_tiles(scr, val):
    for ct in range(scr.shape[0]):
        scr[ct] = val[:, ct * LANES:(ct + 1) * LANES]


def _to_token_order(scr, blk_ref, dil):
    rows = blk_ref.shape[1]
    for ct in range(scr.shape[0]):
        cs = slice(ct * LANES, (ct + 1) * LANES)
        if dil == 1:
            scr[ct] = blk_ref[0, :, cs].astype(F32)
        else:
            for r in range(dil):
                scr.at[ct][pl.ds(r, rows, stride=dil), :] = blk_ref[r, :, cs].astype(F32)


def _to_residue_order(out_ref, scr, dil):
    rows = out_ref.shape[1]
    for ct in range(scr.shape[0]):
        cs = slice(ct * LANES, (ct + 1) * LANES)
        if dil == 1:
            out_ref[0, :, cs] = scr[ct].astype(out_ref.dtype)
        else:
            for r in range(dil):
                out_ref[r, :, cs] = scr.at[ct][pl.ds(r, rows, stride=dil), :].astype(out_ref.dtype)


def _residue_spec(dil, tb, w):
    return pl.BlockSpec((dil, tb // dil, w), lambda i: (0, i, 0))


def _mix_fwd(os_, ls_, name, *, dh):
    dw = os_[0].shape[2]
    n = os_[0].shape[0] * os_[0].shape[1]
    tb = MIX_ROWS

    def body(o0, o1, o2, l0, l1, l2, out_ref, so0, so1, so2, sl0, sl1, sl2):
        for src, scr, dil in zip((o0, o1, o2, l0, l1, l2), (so0, so1, so2, sl0, sl1, sl2), DIL_DILATIONS * 2):
            _to_token_order(scr, src, dil)
        ex = _head_expand(dw, dh)
        ws = _mix_weights(_read_tiles(sl0), _read_tiles(sl1), _read_tiles(sl2))
        out_ref[...] = sum(_nn(wv, ex, HIGHEST) * _read_tiles(s) for wv, s in zip(ws, (so0, so1, so2))).astype(
            out_ref.dtype)

    specs = [_residue_spec(dl, tb, dw) for dl in DIL_DILATIONS] + [_residue_spec(dl, tb, LANES) for dl in DIL_DILATIONS]
    return _call(body, name, (n // tb,), specs, pl.BlockSpec((tb, dw), lambda i: (i, 0)), _sds((n, dw), BF16),
                 scratch=[_tile_scratch(tb, dw)] * 3 + [_tile_scratch(tb, LANES)] * 3, sem=("parallel",))(
        *os_, *ls_)


def _mix_bwd(os_, ls_, do, name, *, dh):
    dw = os_[0].shape[2]
    n = os_[0].shape[0] * os_[0].shape[1]
    tb = MIX_ROWS

    def body(o0, o1, o2, l0, l1, l2, do_ref, d0, d1, d2, r0, r1, r2, so0, so1, so2, sl0, sl1, sl2):
        for src, scr, dil in zip((o0, o1, o2, l0, l1, l2), (so0, so1, so2, sl0, sl1, sl2), DIL_DILATIONS * 2):
            _to_token_order(scr, src, dil)
        ex = _head_expand(dw, dh)
        dov = do_ref[...]
        ws = _mix_weights(_read_tiles(sl0), _read_tiles(sl1), _read_tiles(sl2))
        dws = [_nt(dov * _read_tiles(s), ex, HIGHEST) for s in (so0, so1, so2)]
        sdw = ws[0] * dws[0] + ws[1] * dws[1] + ws[2] * dws[2]
        for wv, so, sl_, d_ref, r_ref, dil in zip(ws, (so0, so1, so2), (sl0, sl1, sl2), (d0, d1, d2), (r0, r1, r2),
                                                  DIL_DILATIONS):
            _write_tiles(so, _nn(wv, ex, HIGHEST) * dov)
            _write_tiles(sl_, wv * sdw)
            _to_residue_order(d_ref, so, dil)
            _to_residue_order(r_ref, sl_, dil)

    specs = [_residue_spec(dl, tb, dw) for dl in DIL_DILATIONS] + [_residue_spec(dl, tb, LANES) for dl in DIL_DILATIONS]
    return _call(body, name, (n // tb,), specs + [pl.BlockSpec((tb, dw), lambda i: (i, 0))], specs,
                 [_sds(o.shape, BF16) for o in os_] + [_sds(l.shape, F32) for l in ls_],
                 scratch=[_tile_scratch(tb, dw)] * 3 + [_tile_scratch(tb, LANES)] * 3, sem=("parallel",))(
        *os_, *ls_, do)


def _to_residue(x, cos2, sin2, name, *, dw, rope):
    n = x.shape[0]
    dh = cos2.shape[1]
    tb = MIX_ROWS

    def body(x_ref, c_ref, s_ref, o0, o1, o2, scr):
        c, s = c_ref[...], s_ref[...]
        for g, (o_ref, dil) in enumerate(zip((o0, o1, o2), DIL_DILATIONS)):
            for hh in range(dw // dh):
                v = x_ref[:, g * dw + hh * dh:g * dw + (hh + 1) * dh]
                if rope:
                    v = v * c + pltpu.roll(v, dh // 2, 1) * s
                lo = (hh * dh) % LANES
                scr[(hh * dh) // LANES, :, lo:lo + dh] = v
            _to_residue_order(o_ref, scr, dil)

    row = pl.BlockSpec((tb, dh), lambda i: (i, 0))
    return _call(body, name, (n // tb,), [pl.BlockSpec((tb, 3 * dw), lambda i: (i, 0)), row, row],
                 [_residue_spec(dl, tb, dw) for dl in DIL_DILATIONS],
                 [_sds((dl, n // dl, dw), BF16) for dl in DIL_DILATIONS],
                 scratch=[_tile_scratch(tb, dw)], sem=("parallel",))(x, cos2, sin2)


FROM_RESIDUE_ROWS = 128


def _from_residue(cols, cos2, sin2, name, *, dw, rope_cols):
    dils = DIL_DILATIONS * (len(cols) // len(DIL_DILATIONS))
    n = cols[0][0].shape[0] * cols[0][0].shape[1]
    dh = cos2.shape[1]
    tb = FROM_RESIDUE_ROWS
    counts = [len(cl) for cl in cols]

    def body(*refs):
        n_in = sum(counts)
        srcs, (c_ref, s_ref, o_ref, scr) = refs[:n_in], refs[n_in:]
        c, s = c_ref[...], s_ref[...]
        k = 0
        for cb, (cnt, dil) in enumerate(zip(counts, dils)):
            rows = tb // dil
            for ct in range(dw // LANES):
                cs = slice(ct * LANES, (ct + 1) * LANES)
                for r in range(dil):
                    acc = srcs[k][r, :, cs]
                    for extra in srcs[k + 1:k + cnt]:
                        acc = acc + extra[r, :, cs]
                    if dil == 1:
                        scr[ct] = acc
                    else:
                        scr.at[ct][pl.ds(r, rows, stride=dil), :] = acc
            k += cnt
            for hh in range(dw // dh):
                lo = (hh * dh) % LANES
                v = scr[(hh * dh) // LANES, :, lo:lo + dh]
                if cb < rope_cols:
                    v = v * c - pltpu.roll(v, dh // 2, 1) * s
                o_ref[:, cb * dw + hh * dh:cb * dw + (hh + 1) * dh] = v.astype(o_ref.dtype)

    row = pl.BlockSpec((tb, dh), lambda i: (i, 0))
    specs = [_residue_spec(dil, tb, dw) for cnt, dil in zip(counts, dils) for _ in range(cnt)]
    return _call(body, name, (n // tb,), specs + [row, row], pl.BlockSpec((tb, len(cols) * dw), lambda i: (i, 0)),
                 _sds((n, len(cols) * dw), BF16), scratch=[_tile_scratch(tb, dw)], sem=("parallel",))(
        *[a for cl in cols for a in cl], cos2, sin2)


def _mem_attn_fwd(q, kv, name):
    n, mw = q.shape
    ml = kv.shape[0]
    dh = mw // MEM_HEADS
    scale = dh ** -0.5
    tb = _tile(n, 512, 8)

    def body(q_ref, kv_ref, o_ref):
        for h in range(MEM_HEADS):
            sl = slice(h * dh, (h + 1) * dh)
            s = _nt(q_ref[:, sl].astype(BF16), kv_ref[:, sl].astype(BF16)) * scale
            pm = jnp.exp(s - jnp.max(s, axis=1, keepdims=True))
            pm = pm / jnp.sum(pm, axis=1, keepdims=True)
            o_ref[:, sl] = _nn(pm.astype(BF16), kv_ref[:, mw + h * dh:mw + (h + 1) * dh].astype(BF16)).astype(o_ref.dtype)

    return _call(body, name, (n // tb,), [pl.BlockSpec((tb, mw), lambda i: (i, 0)), pl.BlockSpec((ml, 2 * mw), lambda i: (0, 0))],
                 pl.BlockSpec((tb, mw), lambda i: (i, 0)), _sds((n, mw), BF16), sem=("parallel",))(q, kv)


def _mem_attn_bwd(q, kv, do, name):
    n, mw = q.shape
    ml = kv.shape[0]
    dh = mw // MEM_HEADS
    scale = dh ** -0.5
    tb = _tile(n, 512, 8)

    def body(q_ref, kv_ref, do_ref, dq_ref, dkv_ref):
        @pl.when(pl.program_id(0) == 0)
        def _():
            dkv_ref[...] = jnp.zeros_like(dkv_ref)

        for h in range(MEM_HEADS):
            sl = slice(h * dh, (h + 1) * dh)
            vsl = slice(mw + h * dh, mw + (h + 1) * dh)
            qh, kh, vh = q_ref[:, sl].astype(BF16), kv_ref[:, sl].astype(BF16), kv_ref[:, vsl].astype(BF16)
            doh = do_ref[:, sl].astype(BF16)
            s = _nt(qh, kh) * scale
            pm = jnp.exp(s - jnp.max(s, axis=1, keepdims=True))
            pm = pm / jnp.sum(pm, axis=1, keepdims=True)
            dp = _nt(doh, vh)
            ds = (pm * (dp - jnp.sum(pm * dp, axis=1, keepdims=True)) * scale).astype(BF16)
            dq_ref[:, sl] = _nn(ds, kh).astype(dq_ref.dtype)
            dkv_ref[:, sl] += _tn(ds, qh)
            dkv_ref[:, vsl] += _tn(pm.astype(BF16), doh)

    row = pl.BlockSpec((tb, mw), lambda i: (i, 0))
    kvs = pl.BlockSpec((ml, 2 * mw), lambda i: (0, 0))
    return _call(body, name, (n // tb,), [row, kvs, row], [row, kvs], [_sds((n, mw), BF16), _sds((ml, 2 * mw), F32)],
                 sem=("arbitrary",))(q, kv, do)


def _adamw(w, g, m, v, name):
    shape = w.shape
    c = shape[-1]
    r = max(1, math.prod(shape[:-1]))
    w2, g2, m2, v2 = (t.reshape(r, c) for t in (w, g, m, v))
    tb = _tile(r, max(8, (1 << 19) // max(c, 1) // 8 * 8), 8)
    bc1 = 1.0 - ADAM_B1 ** ADAM_STEP
    bc2 = 1.0 - ADAM_B2 ** ADAM_STEP

    def body(w_ref, g_ref, m_ref, v_ref, d_ref, nm_ref, nv_ref):
        gv = g_ref[...]
        nm = ADAM_B1 * m_ref[...] + (1.0 - ADAM_B1) * gv
        nv = ADAM_B2 * v_ref[...] + (1.0 - ADAM_B2) * (gv * gv)
        d_ref[...] = -ADAM_LR * ((nm / bc1) / (jnp.sqrt(nv / bc2) + ADAM_EPS) + ADAM_WD * w_ref[...])
        nm_ref[...] = nm
        nv_ref[...] = nv

    spec = pl.BlockSpec((tb, c), lambda i: (i, 0))
    o = _sds((r, c), F32)
    d, nm, nv = _call(body, name, (r // tb,), [spec] * 4, [spec] * 3, [o, o, o], sem=("parallel",))(w2, g2, m2, v2)
    return d.reshape(shape), nm.reshape(shape), nv.reshape(shape)


def _place():
    x, y, c = lax.axis_index("x"), lax.axis_index("y"), lax.axis_index("c")
    chips = [(1 - x, y), (x, 1 - y), (1 - x, 1 - y)]
    return x, y, c, chips


def _cols(ref, c):
    hw = ref.shape[-1] // 2
    return ref.at[(slice(None),) * (len(ref.shape) - 1) + (pl.ds(c * hw, hw),)]


def _slot(ref, s):
    return ref.at[:, s]


def _my_chip():
    return 2 * lax.axis_index("x") + lax.axis_index("y")


def _place_shard(t, name):
    nl, r, n = t.shape
    tb = _tile(r, 512, SUBLANES_BF16)

    def body(t_ref, o_ref):
        o_ref[...] = t_ref[...].astype(BF16)

    return _call(body, name, (nl, r // tb), [pl.BlockSpec((None, tb, n), lambda l, i: (l, i, 0))],
                 pl.BlockSpec((None, None, tb, n), lambda l, i: (l, _my_chip(), i, 0)),
                 _sds((nl, 4, r, n), BF16), sem=("parallel", "parallel"))(t)


def _aliased_comm_call(body, name, bufs, n_sems):
    na = len(bufs)
    anyspec = pl.BlockSpec(memory_space=pl.ANY)
    return pl.pallas_call(
        body, name=name, in_specs=[anyspec] * na, out_specs=[anyspec] * na,
        out_shape=[_sds(b.shape, b.dtype) for b in bufs], input_output_aliases={a: a for a in range(na)},
        scratch_shapes=[pltpu.SemaphoreType.DMA((n_sems,)), pltpu.SemaphoreType.DMA((n_sems,))],
        compiler_params=pltpu.CompilerParams(has_side_effects=True))(*bufs)


def _same_block_copy(blk, send, recv, k, to):
    return pltpu.make_async_remote_copy(src_ref=blk, dst_ref=blk, send_sem=send.at[k], recv_sem=recv.at[k],
                                        device_id=to, device_id_type=MESH)


def _gather_start(bufs, send, recv):
    x, y, c, chips = _place()
    for a, buf in enumerate(bufs):
        for j, (px, py) in enumerate(chips):
            _same_block_copy(_cols(_slot(buf, 2 * x + y), c), send, recv, 6 * a + j, (px, py, c)).start()


def _gather_finish(bufs, send, recv):
    x, y, c, chips = _place()
    sib = (x, y, 1 - c)
    for a, buf in enumerate(bufs):
        for j, (px, py) in enumerate(chips):
            blk = _cols(_slot(buf, 2 * px + py), c)
            _same_block_copy(blk, send, recv, 6 * a + j, (px, py, c)).wait_recv()
            _same_block_copy(blk, send, recv, 6 * a + 3 + j, sib).start()
    for a, buf in enumerate(bufs):
        for j, (px, py) in enumerate(chips):
            _same_block_copy(_cols(_slot(buf, 2 * px + py), 1 - c), send, recv, 6 * a + 3 + j, sib).wait_recv()
    for a, buf in enumerate(bufs):
        for j, (px, py) in enumerate(chips):
            _same_block_copy(_cols(_slot(buf, 2 * x + y), c), send, recv, 6 * a + j, (px, py, c)).wait_send()
            _same_block_copy(_cols(_slot(buf, 2 * px + py), c), send, recv, 6 * a + 3 + j, sib).wait_send()


def _gather_carry(bufs):
    return _Carry(bufs, range(len(bufs)), 6 * len(bufs), _gather_start, _gather_finish)


def _gather_weights(bufs, name):
    na = len(bufs)

    def body(*refs):
        outs = refs[na:2 * na]
        send, recv = refs[2 * na:]
        _gather_start(outs, send, recv)
        _gather_finish(outs, send, recv)

    return _aliased_comm_call(body, name, bufs, 6 * na)


def _swap_copy(bufs, send, recv, a, c):
    na = len(bufs) // 2
    x, y = lax.axis_index("x"), lax.axis_index("y")
    return pltpu.make_async_remote_copy(src_ref=_cols(bufs[a], 1 - c), dst_ref=bufs[na + a], send_sem=send.at[a],
                                        recv_sem=recv.at[a], device_id=(x, y, 1 - c), device_id_type=MESH)


def _swap_start(bufs, send, recv):
    c = lax.axis_index("c")
    for a in range(len(bufs) // 2):
        _swap_copy(bufs, send, recv, a, c).start()


def _swap_finish(bufs, send, recv):
    c = lax.axis_index("c")
    for a in range(len(bufs) // 2):
        _swap_copy(bufs, send, recv, a, c).wait()


def _swap_carry(grads):
    lands = [lax.empty(g.shape[:-1] + (g.shape[-1] // 2,), g.dtype) for g in grads]
    return _Carry(list(grads) + lands, range(len(grads), 2 * len(grads)), len(grads), _swap_start, _swap_finish)


def _swap_halves(grads, name):
    na = len(grads)

    def body(*refs):
        srcs, outs = refs[:na], refs[na:2 * na]
        send, recv = refs[2 * na:]
        x, y, c, _ = _place()
        sib = (x, y, 1 - c)
        cps = [pltpu.make_async_remote_copy(src_ref=_cols(srcs[a], 1 - c), dst_ref=outs[a], send_sem=send.at[a],
                                            recv_sem=recv.at[a], device_id=sib, device_id_type=MESH) for a in range(na)]
        for cp in cps:
            cp.start()
        for cp in cps:
            cp.wait()

    anyspec = pl.BlockSpec(memory_space=pl.ANY)
    return pl.pallas_call(
        body, name=name, in_specs=[anyspec] * na, out_specs=[anyspec] * na,
        out_shape=[_sds(g.shape[:-1] + (g.shape[-1] // 2,), g.dtype) for g in grads],
        scratch_shapes=[pltpu.SemaphoreType.DMA((na,)), pltpu.SemaphoreType.DMA((na,))],
        compiler_params=pltpu.CompilerParams(has_side_effects=True))(*grads)


def _pair_sum(g, theirs, name):
    nl, _, r, n = g.shape
    hw = n // 2
    tb = _tile(r, 256, SUBLANES_BF16)

    def body(g_ref, t_ref, p_ref, own_ref):
        p_ref[...] = (g_ref[...] + t_ref[...]).astype(BF16)
        own_ref[...] = p_ref[_my_chip()]

    blk = pl.BlockSpec((None, 4, tb, hw), lambda l, i: (l, 0, i, 0))
    out = _sds((nl, 4, r, hw), BF16)
    return _call(body, name, (nl, r // tb),
                 [pl.BlockSpec((None, 4, tb, hw), lambda l, i: (l, 0, i, lax.axis_index("c"))), blk],
                 [blk, pl.BlockSpec((None, None, tb, hw), lambda l, i: (l, _my_chip(), i, 0))],
                 [out, out], sem=("parallel", "parallel"))(g, theirs)


def _scatter_start(bufs, send, recv):
    na = len(bufs) // 2
    x, y, c, chips = _place()
    for a in range(na):
        for j, (px, py) in enumerate(chips):
            pltpu.make_async_remote_copy(src_ref=_slot(bufs[a], 2 * px + py), dst_ref=_slot(bufs[na + a], 2 * x + y),
                                         send_sem=send.at[3 * a + j], recv_sem=recv.at[3 * a + j],
                                         device_id=(px, py, c), device_id_type=MESH).start()


def _scatter_finish(bufs, send, recv):
    na = len(bufs) // 2
    x, y, c, chips = _place()
    for a in range(na):
        for j, (px, py) in enumerate(chips):
            _same_block_copy(_slot(bufs[na + a], 2 * px + py), send, recv, 3 * a + j, (px, py, c)).wait_recv()
    for a in range(na):
        for j, (px, py) in enumerate(chips):
            _same_block_copy(_slot(bufs[a], 2 * px + py), send, recv, 3 * a + j, (px, py, c)).wait_send()


def _scatter_carry(parts, lands):
    na = len(parts)
    return _Carry(list(parts) + list(lands), range(na, 2 * na), 3 * na, _scatter_start, _scatter_finish)


def _scatter_partials(parts, lands, name):
    na = len(parts)

    def body(*refs):
        bufs = list(refs[:na]) + list(refs[2 * na:3 * na])
        send, recv = refs[3 * na:]
        _scatter_start(bufs, send, recv)
        _scatter_finish(bufs, send, recv)

    anyspec = pl.BlockSpec(memory_space=pl.ANY)
    return pl.pallas_call(
        body, name=name, in_specs=[anyspec] * (2 * na), out_specs=[anyspec] * na,
        out_shape=[_sds(b.shape, b.dtype) for b in lands],
        input_output_aliases={na + a: a for a in range(na)},
        scratch_shapes=[pltpu.SemaphoreType.DMA((3 * na,)), pltpu.SemaphoreType.DMA((3 * na,))],
        compiler_params=pltpu.CompilerParams(has_side_effects=True))(*parts, *lands)


def _chip_sum(land, n, name):
    nl, _, r, hw = land.shape
    tb = _tile(r, 512, SUBLANES_BF16)

    def body(i_ref, o_ref):
        o_ref[...] = ((i_ref[0].astype(F32) + i_ref[1].astype(F32)) + i_ref[2].astype(F32)) + i_ref[3].astype(F32)

    return _call(body, name, (nl, r // tb), [pl.BlockSpec((None, 4, tb, hw), lambda l, i: (l, 0, i, 0))],
                 pl.BlockSpec((None, tb, hw), lambda l, i: (l, i, lax.axis_index("c"))),
                 _sds((nl, r, n), F32), sem=("parallel", "parallel"))(land)


def _join_halves(bufs, name):
    na = len(bufs)

    def body(*refs):
        outs = refs[na:2 * na]
        send, recv = refs[2 * na:]
        x, y, c, _ = _place()
        sib = (x, y, 1 - c)
        cps = []
        for a in range(na):
            mine = _cols(outs[a], c)
            cp = pltpu.make_async_remote_copy(src_ref=mine, dst_ref=mine, send_sem=send.at[a], recv_sem=recv.at[a],
                                              device_id=sib, device_id_type=MESH)
            cp.start()
            cps.append(cp)
        for a in range(na):
            oth = _cols(outs[a], 1 - c)
            pltpu.make_async_remote_copy(src_ref=oth, dst_ref=oth, send_sem=send.at[a], recv_sem=recv.at[a],
                                         device_id=sib, device_id_type=MESH).wait_recv()
        for cp in cps:
            cp.wait_send()

    return _aliased_comm_call(body, name, bufs, na)


def _allreduce_small(v, name):
    rows, cols = v.shape

    def body(v_ref, o_ref, buf, send, recv):
        x, y, c, _ = _place()
        me = 4 * x + 2 * y + c
        buf[me] = v_ref[...]
        cps = []
        for d in range(1, 8):
            to = (x ^ (d >> 2), y ^ ((d >> 1) & 1), c ^ (d & 1))
            cp = pltpu.make_async_remote_copy(src_ref=v_ref, dst_ref=buf.at[me], send_sem=send.at[d - 1],
                                              recv_sem=recv.at[d - 1], device_id=to, device_id_type=MESH)
            cp.start()
            cps.append(cp)
        for d in range(1, 8):
            frm = 4 * (x ^ (d >> 2)) + 2 * (y ^ ((d >> 1) & 1)) + (c ^ (d & 1))
            got = buf.at[frm]
            pltpu.make_async_remote_copy(src_ref=got, dst_ref=got, send_sem=send.at[d - 1], recv_sem=recv.at[d - 1],
                                         device_id=(x, y, c), device_id_type=MESH).wait_recv()
        for cp in cps:
            cp.wait_send()
        acc = buf[0]
        for d in range(1, 8):
            acc = acc + buf[d]
        o_ref[...] = acc

    vm = pl.BlockSpec(memory_space=pltpu.VMEM)
    return pl.pallas_call(
        body, name=name, in_specs=[vm], out_specs=vm, out_shape=_sds((rows, cols), F32),
        scratch_shapes=[pltpu.VMEM((8, rows, cols), F32), pltpu.SemaphoreType.DMA((7,)), pltpu.SemaphoreType.DMA((7,))],
        compiler_params=pltpu.CompilerParams(has_side_effects=True, vmem_limit_bytes=V7X_VMEM_LIMIT))(v)


def _pack(arrs, pw):
    rows = []
    for a in arrs:
        f = a.astype(F32).reshape(-1)
        pad = (-f.shape[0]) % pw
        rows.append(jnp.pad(f, (0, pad)).reshape(-1, pw))
    p = jnp.concatenate(rows, axis=0)
    return jnp.pad(p, ((0, (-p.shape[0]) % 8), (0, 0)))


def _unpack(p, shapes, pw):
    out, r0 = [], 0
    for s in shapes:
        size = math.prod(s) if s else 1
        nr = -(-size // pw)
        out.append(p[r0:r0 + nr].reshape(-1)[:size].reshape(s))
        r0 += nr
    return out


def _step(x, mem, positions, wts, loss_target, mom, vel):
    w = dict(zip(WEIGHTS, wts))
    s_len, d = x.shape[1], x.shape[2]
    ml = mem.shape[1]
    depth = w['norm_mix'].shape[0]
    n_a = w['ssd_w_in'].shape[0]
    di = 4 * w['ssd_w_out'].shape[1]
    nh = w['ssd_dt_bias'].shape[1]
    p = di // nh
    gn = SSD_GROUPS
    jh = nh // gn
    cd = 4 * w['ssd_conv_w'].shape[2]
    w3 = 4 * w['dil_w_q'].shape[2]
    dw = w3 // 3
    hd = dw // DIL_HEADS
    mw = w['mem_w_q'].shape[2]
    ff = 4 * w['ffn_w_out'].shape[1]
    cx, cy, cc = lax.axis_index("x"), lax.axis_index("y"), lax.axis_index("c")
    chip = 2 * cx + cy


    big = list(COL_SHARDED + ROW_SHARDED)
    stacked = ('mem_w_q', 'mem_w_kv', 'mem_w_o')

    def mixer_items(i):
        if i < n_a:
            return [('ssd_w_in', i), ('ssd_w_out', i)]
        return ([('w_kv_shared', 0)] if i == n_a else []) + [('dil_w_q', i - n_a), ('dil_w_o', i - n_a)]

    def ffn_items(i):
        return [('ffn_w_in', i), ('ffn_w_out', i)] + ([(n_, None) for n_ in stacked] if i == 0 else [])

    def shard_of(name, l):
        t = w[name] if w[name].ndim == 3 else w[name][None]
        if l is not None:
            t = t[l:l + 1]
        return jnp.swapaxes(t, 1, 2).astype(BF16) if name in COL_SHARDED else t

    full = {n_: [None] * (w[n_].shape[0] if w[n_].ndim == 3 else 1) for n_ in big}

    def placed_bufs(its):
        return [_place_shard(shard_of(n_, l), f"place_{n_}_{l}") for n_, l in its]

    def record(its, bufs):
        for (n_, l), t in zip(its, bufs):
            fl = t.reshape(t.shape[0], 4 * t.shape[2], t.shape[3])
            if l is None:
                full[n_] = [fl[k] for k in range(fl.shape[0])]
            else:
                full[n_][l] = fl[0]

    record(mixer_items(0), _gather_weights(placed_bufs(mixer_items(0)), "gather_weights_first"))

    small_sharded = ('ssd_conv_w', 'ssd_conv_b', 'ssd_norm')

    def placed(name):
        t = w[name]
        wd = t.shape[-1]
        z = jnp.zeros(t.shape[:-1] + (4 * wd,), F32)
        z = lax.dynamic_update_slice_in_dim(z, t, chip * wd, axis=t.ndim - 1)
        return z * (cc == 0).astype(F32)

    sm_shapes = [w[n_].shape[:-1] + (4 * w[n_].shape[-1],) for n_ in small_sharded]
    conv_w, conv_b, ssd_nw = _unpack(_allreduce_small(_pack([placed(n_) for n_ in small_sharded], d), "gather_small"),
                                     sm_shapes, d)

    h = x[0]
    tgt = loss_target[0]
    mem_n = _rmsnorm(mem[0], w['mem_src_norm'], "mem_src_norm_fwd")
    cos2, sin2 = _rope_tables(positions[0].reshape(s_len, 1), hd, "rope_tables")
    a_neg = -jnp.exp(w['ssd_a_log'])
    saved = []
    kv_saved = None
    kgs = vgs = None
    for i in range(depth):
        sv = {'h_mix': h}
        u = _rmsnorm(h, w['norm_mix'][i], f"norm_mix_fwd_{i}")
        sv['u'] = u
        if i < n_a:
            wt = full['ssd_w_in'][i]
            wt_dt = wt[di + cd:]
            z = _mm(u, wt, 'nt', f"ssd_in_z_{i}", m=s_len, n=di, k=d)
            xbc_raw = _mm(u, wt, 'nt', f"ssd_in_xbc_{i}", m=s_len, n=cd, k=d, b_noff=di)
            dt_raw = _mm(u, wt_dt, 'nt', f"ssd_in_dt_{i}", m=s_len, n=nh, k=d)
            xbc = _conv_fwd(xbc_raw, conv_w[i], conv_b[i], f"ssd_conv_fwd_{i}")
            dt = _softplus_fwd(dt_raw, w['ssd_dt_bias'][i], f"ssd_dt_fwd_{i}")
            dtc = dt.reshape(s_len, gn, jh).transpose(1, 0, 2)
            dtr = dt.reshape(s_len, gn, jh).transpose(1, 2, 0)
            cum = _ssd_cumsum(dt, a_neg[i], f"ssd_cumsum_{i}")
            cumc = cum.reshape(s_len, gn, jh).transpose(1, 0, 2)
            cumr = cum.reshape(s_len, gn, jh).transpose(1, 2, 0)
            a_row = a_neg[i].reshape(gn, 1, jh)
            a_col = a_neg[i].reshape(gn, jh, 1)
            dskip = jnp.repeat(w['ssd_d'][i], p).reshape(gn, 1, jh * p)
            (y, hins), got = _ssd_fwd(xbc, dtc, dtr, cumc, cumr, a_row, a_col, dskip, f"ssd_scan_fwd_{i}", di=di, p=p,
                                      carry=_gather_carry(placed_bufs(ffn_items(i))))
            record(ffn_items(i), got)
            yn = _gated_norm_fwd(y, z, ssd_nw[i], f"ssd_norm_fwd_{i}")
            h = _mm(yn, full['ssd_w_out'][i], 'nn', f"ssd_out_{i}", m=s_len, n=d, k=di, res=h)
            sv.update(wt=wt, wt_dt=wt_dt, z=z, xbc_raw=xbc_raw, xbc=xbc, dt_raw=dt_raw, dtc=dtc, dtr=dtr, cumc=cumc, cumr=cumr, a_row=a_row,
                      a_col=a_col, dskip=dskip, y=y, hins=hins, yn=yn)
        else:
            j = i - n_a
            if j == 0:
                kvn = _rmsnorm(h, w['kv_norm'], "kv_norm_fwd")
                wkv = full['w_kv_shared'][0]
                k_raw = _mm(kvn, wkv, 'nt', "kv_proj_k", m=s_len, n=w3, k=d)
                v_raw = _mm(kvn, wkv, 'nt', "kv_proj_v", m=s_len, n=w3, k=d, b_noff=w3)
                kgs = _to_residue(k_raw, cos2, sin2, "rope_k", dw=dw, rope=True)
                vgs = _to_residue(v_raw, cos2, sin2, "residue_v", dw=dw, rope=False)
                kv_saved = {'h': h, 'kvn': kvn, 'wkv': wkv}
            q_raw = _mm(u, full['dil_w_q'][j], 'nt', f"dil_q_{i}", m=s_len, n=w3, k=d)
            qgs = _to_residue(q_raw, cos2, sin2, f"rope_q_{i}", dw=dw, rope=True)
            os_, ls_ = [], []
            for g in range(len(DIL_DILATIONS)):
                if g == 0:
                    (o3, l3), got = _dil_attn_fwd(qgs[g], kgs[g], vgs[g], f"dil_attn_fwd_{i}_{g}", dh=hd,
                                                  carry=_gather_carry(placed_bufs(ffn_items(i))))
                    record(ffn_items(i), got)
                else:
                    o3, l3 = _dil_attn_fwd(qgs[g], kgs[g], vgs[g], f"dil_attn_fwd_{i}_{g}", dh=hd)
                os_.append(o3)
                ls_.append(l3)
            om = _mix_fwd(os_, ls_, f"dil_mix_fwd_{i}", dh=hd)
            h = _mm(om, full['dil_w_o'][j], 'nn', f"dil_out_{i}", m=s_len, n=d, k=dw, res=h)
            sv.update(qgs=qgs, os=os_, ls=ls_, om=om)
        sv['h_mem'] = h
        u2 = _rmsnorm(h, w['norm_mem'][i], f"norm_mem_fwd_{i}")
        qm = _mm(u2, full['mem_w_q'][i], 'nn', f"mem_q_{i}", m=s_len, n=mw, k=d, out_dtype=BF16)
        kvm = _mm(mem_n, full['mem_w_kv'][i], 'nn', f"mem_kv_{i}", m=ml, n=2 * mw, k=d)
        omem = _mem_attn_fwd(qm, kvm, f"mem_attn_fwd_{i}")
        h = _mm(omem, full['mem_w_o'][i], 'nt', f"mem_out_{i}", m=s_len, n=d, k=mw, res=h)
        sv.update(u2=u2, qm=qm, kvm=kvm, omem=omem, h_ffn=h)
        u3 = _rmsnorm(h, w['norm_ffn'][i], f"norm_ffn_fwd_{i}")
        if i + 1 < depth:
            gu, got = _mm(u3, full['ffn_w_in'][i], 'nt', f"ffn_in_{i}", m=s_len, n=2 * ff, k=d, out_dtype=BF16,
                          carry=_gather_carry(placed_bufs(mixer_items(i + 1))))
            record(mixer_items(i + 1), got)
        else:
            gu = _mm(u3, full['ffn_w_in'][i], 'nt', f"ffn_in_{i}", m=s_len, n=2 * ff, k=d, out_dtype=BF16)
        act = _swiglu_fwd(gu, f"ffn_act_fwd_{i}")
        h = _mm(act, full['ffn_w_out'][i], 'nn', f"ffn_out_{i}", m=s_len, n=d, k=ff, res=h)
        sv.update(u3=u3, gu=gu, act=act)
        saved.append(sv)

    loss_part, dh, dhb, dg_final = _loss_and_grad(h, w['norm_final'], tgt, "loss_and_final_norm")

    gbig = {n_: [None] * len(full[n_]) for n_ in big}
    gsm = {n_: [None] * depth for n_ in ('norm_mix', 'norm_mem', 'norm_ffn')}
    for n_ in ('ssd_conv_w', 'ssd_conv_b', 'ssd_norm', 'ssd_dt_bias', 'ssd_a_log', 'ssd_d'):
        gsm[n_] = [None] * n_a
    dmem_n = None
    dk_parts = [[] for _ in DIL_DILATIONS]
    dv_parts = [[] for _ in DIL_DILATIONS]
    gshard = {}

    def rs_grads(its):
        gl = []
        for n_, l in its:
            t = jnp.stack(gbig[n_]) if l is None else gbig[n_][l][None]
            gl.append(t.reshape(t.shape[0], 4, t.shape[1] // 4, t.shape[2]))
        return gl

    def rs_pairs(its, gl, theirs, tag):
        pairs = [_pair_sum(t, o, f"grad_pair_sum_{n_}_{l}") for (n_, l), t, o in zip(its, gl, theirs)]
        return its, [pr[0] for pr in pairs], [pr[1] for pr in pairs], [t.shape[3] for t in gl], tag

    def rs_finish(pend, lands):
        its, _, _, widths, tag = pend
        fins = [_chip_sum(ld, nw_, f"grad_chip_sum_{n_}_{l}") for (n_, l), ld, nw_ in zip(its, lands, widths)]
        for it, t in zip(its, _join_halves(fins, f"grad_join_halves_{tag}")):
            gshard[it] = t

    pend_mixer = None
    for i in reversed(range(depth)):
        sv = saved[i]
        dact = _mm(dhb, full['ffn_w_out'][i], 'nt', f"ffn_out_bwd_x_{i}", m=s_len, n=ff, k=d, out_dtype=BF16)
        gbig['ffn_w_out'][i] = _mm(sv['act'], dhb, 'tn', f"ffn_out_bwd_w_{i}", m=ff, n=d, k=s_len)
        dgu = _swiglu_bwd(sv['gu'], dact, f"ffn_act_bwd_{i}")
        if pend_mixer is None:
            gbig['ffn_w_in'][i] = _mm(dgu, sv['u3'], 'tn', f"ffn_in_bwd_w_{i}", m=2 * ff, n=d, k=s_len)
        else:
            gbig['ffn_w_in'][i], got = _mm(dgu, sv['u3'], 'tn', f"ffn_in_bwd_w_{i}", m=2 * ff, n=d, k=s_len,
                                           carry=_scatter_carry(pend_mixer[1], pend_mixer[2]))
            rs_finish(pend_mixer, got)
        gl_ffn = rs_grads(ffn_items(i)[:2])
        du, theirs_ffn = _mm(dgu, full['ffn_w_in'][i], 'nn', f"ffn_in_bwd_x_{i}", m=s_len, n=d, k=2 * ff,
                             carry=_swap_carry(gl_ffn))
        dh, dhb, gsm['norm_ffn'][i] = _rmsnorm_bwd(sv['h_ffn'], w['norm_ffn'][i], du, dh, f"norm_ffn_bwd_{i}")
        do = _mm(dhb, full['mem_w_o'][i], 'nn', f"mem_out_bwd_x_{i}", m=s_len, n=mw, k=d, out_dtype=BF16)
        gbig['mem_w_o'][i] = _mm(dhb, sv['omem'], 'tn', f"mem_out_bwd_w_{i}", m=d, n=mw, k=s_len)
        dqm, dkvm = _mem_attn_bwd(sv['qm'], sv['kvm'], do, f"mem_attn_bwd_{i}")
        gbig['mem_w_q'][i] = _mm(sv['u2'], dqm, 'tn', f"mem_q_bwd_w_{i}", m=d, n=mw, k=s_len)
        du = _mm(dqm, full['mem_w_q'][i], 'nt', f"mem_q_bwd_x_{i}", m=s_len, n=d, k=mw)
        gbig['mem_w_kv'][i] = _mm(mem_n, dkvm, 'tn', f"mem_kv_bwd_w_{i}", m=d, n=2 * mw, k=ml)
        dmem_n = _mm(dkvm, full['mem_w_kv'][i], 'nt', f"mem_kv_bwd_x_{i}", m=ml, n=d, k=2 * mw, res=dmem_n)
        dh, dhb, gsm['norm_mem'][i] = _rmsnorm_bwd(sv['h_mem'], w['norm_mem'][i], du, dh, f"norm_mem_bwd_{i}")
        if len(ffn_items(i)) > 2:
            gl_mem = rs_grads(ffn_items(i)[2:])
            gl_ffn, theirs_ffn = gl_ffn + gl_mem, theirs_ffn + list(_swap_halves(gl_mem, "grad_swap_halves_mem"))
        pend_ffn = rs_pairs(ffn_items(i), gl_ffn, theirs_ffn, f"ffn_{i}")
        ffn_scatter = _scatter_carry(pend_ffn[1], pend_ffn[2])
        if i >= n_a:
            j = i - n_a
            dom = _mm(dhb, full['dil_w_o'][j], 'nt', f"dil_out_bwd_x_{i}", m=s_len, n=dw, k=d)
            gbig['dil_w_o'][j] = _mm(sv['om'], dhb, 'tn', f"dil_out_bwd_w_{i}", m=dw, n=d, k=s_len)
            mb = _mix_bwd(sv['os'], sv['ls'], dom, f"dil_mix_bwd_{i}", dh=hd)
            dqs = []
            for g in range(len(DIL_DILATIONS)):
                if g == 0:
                    (dq3, dk3, dv3), got = _dil_attn_bwd(sv['qgs'][g], kgs[g], vgs[g], mb[g], sv['ls'][g], mb[3 + g],
                                                         f"dil_attn_bwd_{i}_{g}", dh=hd, carry=ffn_scatter)
                    rs_finish(pend_ffn, got)
                else:
                    dq3, dk3, dv3 = _dil_attn_bwd(sv['qgs'][g], kgs[g], vgs[g], mb[g], sv['ls'][g], mb[3 + g],
                                                  f"dil_attn_bwd_{i}_{g}", dh=hd)
                dqs.append([dq3])
                dk_parts[g].append(dk3)
                dv_parts[g].append(dv3)
            dq_raw = _from_residue(dqs, cos2, sin2, f"rope_q_bwd_{i}", dw=dw, rope_cols=3)
            gbig['dil_w_q'][j] = _mm(dq_raw, sv['u'], 'tn', f"dil_q_bwd_w_{i}", m=w3, n=d, k=s_len)
            if j > 0:
                gl_mix = rs_grads(mixer_items(i))
                du, theirs_mix = _mm(dq_raw, full['dil_w_q'][j], 'nn', f"dil_q_bwd_x_{i}", m=s_len, n=d, k=w3,
                                     carry=_swap_carry(gl_mix))
            else:
                du = _mm(dq_raw, full['dil_w_q'][j], 'nn', f"dil_q_bwd_x_{i}", m=s_len, n=d, k=w3)
            dh, dhb, gsm['norm_mix'][i] = _rmsnorm_bwd(sv['h_mix'], w['norm_mix'][i], du, dh, f"norm_mix_bwd_{i}")
            if j == 0:
                dkv = _from_residue(dk_parts + dv_parts, cos2, sin2, "rope_kv_bwd", dw=dw, rope_cols=3)
                gbig['w_kv_shared'][0] = _mm(dkv, kv_saved['kvn'], 'tn', "kv_proj_bwd_w", m=2 * w3, n=d, k=s_len)
                gl_mix = rs_grads(mixer_items(i))
                du, theirs_mix = _mm(dkv, kv_saved['wkv'], 'nn', "kv_proj_bwd_x", m=s_len, n=d, k=2 * w3,
                                     carry=_swap_carry(gl_mix))
                dh, dhb, dg_kv = _rmsnorm_bwd(kv_saved['h'], w['kv_norm'], du, dh, "kv_norm_bwd")
        else:
            dyn = _mm(dhb, full['ssd_w_out'][i], 'nt', f"ssd_out_bwd_x_{i}", m=s_len, n=di, k=d)
            gbig['ssd_w_out'][i] = _mm(sv['yn'], dhb, 'tn', f"ssd_out_bwd_w_{i}", m=di, n=d, k=s_len)
            dy, dz, gsm['ssd_norm'][i] = _gated_norm_bwd(sv['y'], sv['z'], ssd_nw[i], dyn, f"ssd_norm_bwd_{i}")
            (dx, db_, dc_, ddtc, ddtr, dac, dar, ddl), got = _ssd_bwd(
                sv['xbc'], sv['dtc'], sv['dtr'], sv['cumc'], sv['cumr'], sv['a_row'], sv['a_col'], sv['dskip'], sv['hins'], dy,
                f"ssd_scan_bwd_{i}", di=di, p=p, carry=ffn_scatter)
            rs_finish(pend_ffn, got)
            dxbc = jnp.concatenate([dx, db_, dc_], axis=1)
            ddt = ddtc.transpose(1, 0, 2).reshape(s_len, nh) + ddtr.transpose(2, 0, 1).reshape(s_len, nh)
            gsm['ssd_a_log'][i] = (dac.reshape(nh) + dar.reshape(nh)) * a_neg[i]
            gsm['ssd_d'][i] = ddl.reshape(nh, p).sum(axis=1)
            ddt_raw, dbias = _softplus_bwd(sv['dt_raw'], w['ssd_dt_bias'][i], ddt, f"ssd_dt_bwd_{i}")
            gsm['ssd_dt_bias'][i] = dbias.reshape(nh)
            dpre, dcw, dcb = _conv_bwd_pre(sv['xbc_raw'], conv_w[i], conv_b[i], dxbc, f"ssd_conv_bwd_pre_{i}")
            gsm['ssd_conv_w'][i], gsm['ssd_conv_b'][i] = dcw, dcb.reshape(cd)
            dxbc_raw = _conv_bwd_in(dpre, conv_w[i], f"ssd_conv_bwd_in_{i}")
            gbig['ssd_w_in'][i] = jnp.concatenate([
                _mm(dz, sv['u'], 'tn', f"ssd_in_bwd_w_z_{i}", m=di, n=d, k=s_len),
                _mm(dxbc_raw, sv['u'], 'tn', f"ssd_in_bwd_w_xbc_{i}", m=cd, n=d, k=s_len),
                _mm(ddt_raw, sv['u'], 'tn', f"ssd_in_bwd_w_dt_{i}", m=nh, n=d, k=s_len)], axis=0)
            gl_mix = rs_grads(mixer_items(i))
            du, theirs_mix = _mm(dz, sv['wt'], 'nn', f"ssd_in_bwd_x_z_{i}", m=s_len, n=d, k=di, carry=_swap_carry(gl_mix))
            du = _mm(dxbc_raw, sv['wt'], 'nn', f"ssd_in_bwd_x_xbc_{i}", m=s_len, n=d, k=cd, b_koff=di, res=du)
            du = _mm(ddt_raw, sv['wt_dt'], 'nn', f"ssd_in_bwd_x_dt_{i}", m=s_len, n=d, k=nh, res=du)
            dh, dhb, gsm['norm_mix'][i] = _rmsnorm_bwd(sv['h_mix'], w['norm_mix'][i], du, dh, f"norm_mix_bwd_{i}")
        pend_mixer = rs_pairs(mixer_items(i), gl_mix, theirs_mix, f"mixer_{i}")
    rs_finish(pend_mixer, _scatter_partials(pend_mixer[1], pend_mixer[2], "grad_scatter_last"))
    grad_x = dh[None]
    _, dg_src = _rmsnorm_bwd_noacc(mem[0], w['mem_src_norm'], dmem_n, "mem_src_norm_bwd")

    grads_big = {}
    for n_ in big:
        t = gshard[(n_, None)] if n_ in stacked else jnp.concatenate([gshard[(n_, l)] for l in range(len(full[n_]))])
        if n_ in COL_SHARDED:
            t = jnp.swapaxes(t, 1, 2)
        grads_big[n_] = t.reshape(w[n_].shape)

    sm_names = ['norm_mix', 'norm_mem', 'norm_ffn', 'ssd_conv_w', 'ssd_conv_b', 'ssd_norm', 'ssd_dt_bias', 'ssd_a_log',
                'ssd_d']
    sm_arrs = [jnp.stack([t.reshape(t.shape[-1]) if n_.startswith('norm') else t for t in gsm[n_]]) for n_ in sm_names]
    sm_names += ['norm_final', 'kv_norm', 'mem_src_norm', 'loss']
    sm_arrs += [dg_final.reshape(d), dg_kv.reshape(d), dg_src.reshape(d), loss_part[0, :1]]
    summed = _unpack(_allreduce_small(_pack(sm_arrs, d), "reduce_small"), [t.shape for t in sm_arrs], d)
    gs = dict(zip(sm_names, summed))
    loss = gs.pop('loss').reshape(())
    grads = dict(grads_big)
    for n_, t in gs.items():
        if n_ in small_sharded:
            wd = w[n_].shape[-1]
            t = lax.dynamic_slice_in_dim(t, chip * wd, wd, axis=t.ndim - 1)
        grads[n_] = t.reshape(w[n_].shape)

    deltas, new_m, new_v = [], [], []
    for n_, m_, v_ in zip(WEIGHTS, mom, vel):
        dlt, nm, nv = _adamw(w[n_], grads[n_], m_, v_, f"adamw_{n_}")
        deltas.append(dlt)
        new_m.append(nm)
        new_v.append(nv)
    return (loss, grad_x, *[grads[n_] for n_ in WEIGHTS], *deltas, *new_m, *new_v)


def kernel(x, mem, positions, norm_mix, norm_mem, norm_ffn, norm_final, ssd_w_in, ssd_conv_w, ssd_conv_b, ssd_dt_bias, ssd_a_log, ssd_d, ssd_norm, ssd_w_out, kv_norm, w_kv_shared, dil_w_q, dil_w_o, mem_src_norm, mem_w_q, mem_w_kv, mem_w_o, ffn_w_in, ffn_w_out, loss_target, m_norm_mix, m_norm_mem, m_norm_ffn, m_norm_final, m_ssd_w_in, m_ssd_conv_w, m_ssd_conv_b, m_ssd_dt_bias, m_ssd_a_log, m_ssd_d, m_ssd_norm, m_ssd_w_out, m_kv_norm, m_w_kv_shared, m_dil_w_q, m_dil_w_o, m_mem_src_norm, m_mem_w_q, m_mem_w_kv, m_mem_w_o, m_ffn_w_in, m_ffn_w_out, v_norm_mix, v_norm_mem, v_norm_ffn, v_norm_final, v_ssd_w_in, v_ssd_conv_w, v_ssd_conv_b, v_ssd_dt_bias, v_ssd_a_log, v_ssd_d, v_ssd_norm, v_ssd_w_out, v_kv_norm, v_w_kv_shared, v_dil_w_q, v_dil_w_o, v_mem_src_norm, v_mem_w_q, v_mem_w_kv, v_mem_w_o, v_ffn_w_in, v_ffn_w_out):
    wts = (norm_mix, norm_mem, norm_ffn, norm_final, ssd_w_in, ssd_conv_w, ssd_conv_b, ssd_dt_bias, ssd_a_log, ssd_d, ssd_norm, ssd_w_out, kv_norm, w_kv_shared, dil_w_q, dil_w_o, mem_src_norm, mem_w_q, mem_w_kv, mem_w_o, ffn_w_in, ffn_w_out)
    mom = (m_norm_mix, m_norm_mem, m_norm_ffn, m_norm_final, m_ssd_w_in, m_ssd_conv_w, m_ssd_conv_b, m_ssd_dt_bias, m_ssd_a_log, m_ssd_d, m_ssd_norm, m_ssd_w_out, m_kv_norm, m_w_kv_shared, m_dil_w_q, m_dil_w_o, m_mem_src_norm, m_mem_w_q, m_mem_w_kv, m_mem_w_o, m_ffn_w_in, m_ffn_w_out)
    vel = (v_norm_mix, v_norm_mem, v_norm_ffn, v_norm_final, v_ssd_w_in, v_ssd_conv_w, v_ssd_conv_b, v_ssd_dt_bias, v_ssd_a_log, v_ssd_d, v_ssd_norm, v_ssd_w_out, v_kv_norm, v_w_kv_shared, v_dil_w_q, v_dil_w_o, v_mem_src_norm, v_mem_w_q, v_mem_w_kv, v_mem_w_o, v_ffn_w_in, v_ffn_w_out)
    return _step(x, mem, positions, wts, loss_target, mom, vel)
```
